```python
import math
import jax, jax.numpy as jnp
from jax import lax
import numpy as np

D_MODEL = 1024
BATCH = 16
SEQ = 2048
DEPTH = 1

HEAD_DIM = 64
N_HEADS_A = D_MODEL // (2 * HEAD_DIM)
N_HEADS_B = D_MODEL // (2 * HEAD_DIM)
WIDTH_A = N_HEADS_A * HEAD_DIM
WIDTH_B = N_HEADS_B * HEAD_DIM
MIX_WIDTH = WIDTH_A + WIDTH_B
N_IDX_HEADS = 8
IDX_DIM = 64
TOPK_MAX = 256
D_FF = 2816
N_BUCKETS = 32
MAX_DISTANCE = 128
SPARSE_BLOCK = 32
SB_BLOCK = 128
N_SUBLAYERS = 3
N_MOD = 3 * N_SUBLAYERS
EPS = 1e-6

IN_SIZES = [WIDTH_A, WIDTH_A, WIDTH_A, N_IDX_HEADS * IDX_DIM, IDX_DIM, N_IDX_HEADS,
            WIDTH_B, WIDTH_B, WIDTH_B]
IN_COLS = sum(IN_SIZES)
IN_SPLITS = list(np.cumsum(IN_SIZES)[:-1])

kernel_name = "hybrid_dsa_stickbreaking_macaron_block"


def rmsnorm(x, g):
    xf = x.astype(jnp.float32)
    y = xf * lax.rsqrt(jnp.mean(xf * xf, axis=-1, keepdims=True) + EPS)
    return (y * g.astype(jnp.float32)).astype(x.dtype)


def modulate(x, shift, scale):
    return x * (1 + scale[:, None, :]) + shift[:, None, :]


def swiglu(x, w_gate, w_up, w_down):
    return (jax.nn.silu(x @ w_gate) * (x @ w_up)) @ w_down


def t5_bucket(n):
    n = jnp.maximum(n, 0)
    max_exact = N_BUCKETS // 2
    nf = jnp.maximum(n, 1).astype(jnp.float32)
    large = max_exact + (jnp.log(nf / max_exact) / math.log(MAX_DISTANCE / max_exact)
                         * (N_BUCKETS - max_exact)).astype(jnp.int32)
    large = jnp.minimum(large, N_BUCKETS - 1)
    return jnp.where(n < max_exact, n, large)


def dsa_sparse_attention(q, k, v, q_idx, k_idx, w_idx, rel_bias):
    B, S, H, D = q.shape
    topk = min(TOPK_MAX, S // 4)
    nb = S // SPARSE_BLOCK
    key_pos = jnp.arange(S)

    def blocks(a):
        return jnp.swapaxes(a.reshape(B, nb, SPARSE_BLOCK, *a.shape[2:]), 0, 1)

    def one(args):
        qb, qib, wb, t0 = args
        tpos = t0 + jnp.arange(SPARSE_BLOCK)
        sc = jax.nn.relu(jnp.einsum('bthd,bsd->bths', qib, k_idx).astype(jnp.float32))
        score = jnp.einsum('bths,bth->bts', sc, wb.astype(jnp.float32))
        causal = key_pos[None, :] <= tpos[:, None]
        score = jnp.where(causal[None], score, -jnp.inf)
        _, idx = lax.top_k(score, topk)
        kg = jax.vmap(lambda kk, ii: kk[ii])(k, idx)
        vg = jax.vmap(lambda vv, ii: vv[ii])(v, idx)
        dist = tpos[None, :, None] - idx
        valid = dist >= 0
        bias = jnp.swapaxes(rel_bias[t5_bucket(dist)], -1, -2)
        logits = (jnp.einsum('bthd,btkhd->bthk', qb, kg).astype(jnp.float32) * (D ** -0.5)
                  + bias.astype(jnp.float32))
        logits = jnp.where(valid[:, :, None, :], logits, -jnp.inf)
        p = jax.nn.softmax(logits, axis=-1)
        return jnp.einsum('bthk,btkhd->bthd', p.astype(v.dtype), vg)

    starts = jnp.arange(nb) * SPARSE_BLOCK
    out = lax.map(one, (blocks(q), blocks(q_idx), blocks(w_idx), starts))
    return jnp.swapaxes(out, 0, 1).reshape(B, S, H * D)


def stick_breaking_attention(q, k, v):
    B, S, H, D = q.shape
    nb = S // SB_BLOCK
    key_pos = jnp.arange(S)
    q_blocks = jnp.swapaxes(q.reshape(B, nb, SB_BLOCK, H, D), 0, 1)

    def one(args):
        qb, t0 = args
        tpos = t0 + jnp.arange(SB_BLOCK)
        z = jnp.einsum('bthd,bshd->bhts', qb, k).astype(jnp.float32) * (D ** -0.5)
        mask = (key_pos[None, :] < tpos[:, None])[None, None]
        u = jnp.where(mask, jax.nn.log_sigmoid(-z), 0.0)
        between = lax.cumsum(u, axis=3, reverse=True) - u
        a = jnp.where(mask, jnp.exp(jax.nn.log_sigmoid(z) + between), 0.0)
        return jnp.einsum('bhts,bshd->bthd', a.astype(v.dtype), v)

    starts = jnp.arange(nb) * SB_BLOCK
    out = lax.map(one, (q_blocks, starts))
    return jnp.swapaxes(out, 0, 1).reshape(B, S, H * D)


def setup_inputs(seed: int = 0) -> dict:
    key = jax.random.key(seed)
    ks = jax.random.split(key, 20)
    f32 = jnp.float32
    nrm = lambda k, shape, s: jax.random.normal(k, shape, f32) * s
    return {
        "x": nrm(ks[0], (BATCH, SEQ, D_MODEL), 1.0),
        "c": nrm(ks[1], (BATCH, D_MODEL), 1.0),
        "w_ada": nrm(ks[2], (DEPTH, D_MODEL, N_MOD * D_MODEL), 0.5 * D_MODEL ** -0.5),
        "b_ada": nrm(ks[3], (DEPTH, N_MOD * D_MODEL), 0.02),
        "g_pre": 1.0 + nrm(ks[4], (DEPTH, N_SUBLAYERS, D_MODEL), 0.05),
        "g_post": 1.0 + nrm(ks[5], (DEPTH, N_SUBLAYERS, D_MODEL), 0.05),
        "w_ffn_gate": nrm(ks[6], (DEPTH, 2, D_MODEL, D_FF), D_MODEL ** -0.5),
        "w_ffn_up": nrm(ks[7], (DEPTH, 2, D_MODEL, D_FF), D_MODEL ** -0.5),
        "w_ffn_down": nrm(ks[8], (DEPTH, 2, D_FF, D_MODEL), D_FF ** -0.5),
        "w_in": nrm(ks[9], (DEPTH, D_MODEL, IN_COLS), D_MODEL ** -0.5),
        "g_kidx": 1.0 + nrm(ks[10], (DEPTH, IDX_DIM), 0.05),
        "rel_bias": nrm(ks[11], (N_BUCKETS, N_HEADS_A), 0.5),
        "g_out_a": 1.0 + nrm(ks[12], (DEPTH, WIDTH_A), 0.05),
        "g_out_b": 1.0 + nrm(ks[13], (DEPTH, WIDTH_B), 0.05),
        "w_out": nrm(ks[14], (DEPTH, MIX_WIDTH, D_MODEL), MIX_WIDTH ** -0.5),
    }


def reference(x, c, w_ada, b_ada, g_pre, g_post, w_ffn_gate, w_ffn_up, w_ffn_down,
              w_in, g_kidx, rel_bias, g_out_a, g_out_b, w_out):
    B, S, _ = x.shape
    h = x
    for l in range(DEPTH):
        mods = (jax.nn.silu(c) @ w_ada[l] + b_ada[l]).reshape(B, N_MOD, D_MODEL)
        shift = lambda i: mods[:, 3 * i]
        scale = lambda i: mods[:, 3 * i + 1]
        gate = lambda i: mods[:, 3 * i + 2][:, None, :]

        n = modulate(rmsnorm(h, g_pre[l, 0]), shift(0), scale(0))
        f = swiglu(n, w_ffn_gate[l, 0], w_ffn_up[l, 0], w_ffn_down[l, 0])
        h = h + 0.5 * gate(0) * rmsnorm(f, g_post[l, 0])

        n = modulate(rmsnorm(h, g_pre[l, 1]), shift(1), scale(1))
        p = n @ w_in[l]
        qa, ka, va, qi, ki, wi, qb, kb, vb = jnp.split(p, IN_SPLITS, axis=-1)
        qa = qa.reshape(B, S, N_HEADS_A, HEAD_DIM)
        ka = ka.reshape(B, S, N_HEADS_A, HEAD_DIM)
        va = va.reshape(B, S, N_HEADS_A, HEAD_DIM)
        qi = qi.reshape(B, S, N_IDX_HEADS, IDX_DIM)
        ki = rmsnorm(ki, g_kidx[l])
        wi = wi * ((N_IDX_HEADS * IDX_DIM) ** -0.5)
        qb = qb.reshape(B, S, N_HEADS_B, HEAD_DIM)
        kb = kb.reshape(B, S, N_HEADS_B, HEAD_DIM)
        vb = vb.reshape(B, S, N_HEADS_B, HEAD_DIM)

        oa = rmsnorm(dsa_sparse_attention(qa, ka, va, qi, ki, wi, rel_bias), g_out_a[l])
        ob = rmsnorm(stick_breaking_attention(qb, kb, vb), g_out_b[l])
        o = jnp.concatenate([oa, ob], axis=-1) @ w_out[l]
        h = h + gate(1) * rmsnorm(o, g_post[l, 1])

        n = modulate(rmsnorm(h, g_pre[l, 2]), shift(2), scale(2))
        f = swiglu(n, w_ffn_gate[l, 1], w_ffn_up[l, 1], w_ffn_down[l, 1])
        h = h + 0.5 * gate(2) * rmsnorm(f, g_post[l, 2])
    return h
```

```python
import functools
import math

import numpy as np
import jax
import jax.numpy as jnp
from jax import lax
from jax.experimental import pallas as pl
from jax.experimental.pallas import tpu as pltpu

F32 = jnp.float32
BF16 = jnp.bfloat16
I32 = jnp.int32

EPS = 1e-6
HEAD_DIM = 64
N_HEADS = 8
GROUP_WIDTH = N_HEADS * HEAD_DIM
IDX_DIM = 64
TOPK_MAX = 256
N_BUCKETS = 32
MAX_DISTANCE = 128
FFN_CHUNK = 256
TOKEN_TILE = 512
DSA_TILE = 256
SB_TILE = 128
NEG_BIG = -1e30
SB_DEAD = -104.0
VMEM_LIMIT = 56 * 1024 * 1024
INT_MIN = np.int32(-2 ** 31)


def _dot(a, b):
    return jnp.dot(a, b, preferred_element_type=F32)


def _dot_nt(a, b):
    return lax.dot_general(a, b, (((1,), (1,)), ((), ())), preferred_element_type=F32)


def _rms(x, g):
    return x * lax.rsqrt(jnp.mean(x * x, axis=-1, keepdims=True) + EPS) * g


def _resident(shape):
    n = len(shape)
    return pl.BlockSpec(shape, lambda *_: (0,) * n, pipeline_mode=pl.Buffered(1))


def _params(n_grid):
    return pltpu.CompilerParams(dimension_semantics=("arbitrary",) * n_grid,
                                vmem_limit_bytes=VMEM_LIMIT)


def _ada_kernel(c_ref, w_ref, b_ref, o_ref):
    c = c_ref[...]
    a = c * jax.nn.sigmoid(c)
    o_ref[...] = jnp.dot(a, w_ref[...], preferred_element_type=F32,
                         precision=lax.Precision.HIGHEST) + b_ref[...]


def _ada(c, w, b):
    bsz, d = c.shape
    n = w.shape[1]
    tn = 1152 if n % 1152 == 0 else n
    return pl.pallas_call(
        _ada_kernel,
        grid=(n // tn,),
        in_specs=[pl.BlockSpec((bsz, d), lambda j: (0, 0)),
                  pl.BlockSpec((d, tn), lambda j: (0, j)),
                  pl.BlockSpec((1, tn), lambda j: (0, j))],
        out_specs=pl.BlockSpec((bsz, tn), lambda j: (0, j)),
        out_shape=jax.ShapeDtypeStruct((bsz, n), F32),
        compiler_params=_params(1),
        name="ada",
    )(c, w, b.reshape(1, n))


def _ffn_kernel(*refs, sub, mix):
    if mix:
        (h_ref, oa_ref, ob_ref, woa_ref, wob_ref, gpm_ref, mods_ref, gpre_ref, gpost_ref,
         wgu_ref, wd_ref, o_ref, n_scr, acc_scr) = refs
    else:
        (h_ref, mods_ref, gpre_ref, gpost_ref, wgu_ref, wd_ref, o_ref, n_scr, acc_scr) = refs
    x = h_ref[...]
    if mix:
        o = _dot(oa_ref[...], woa_ref[...]) + _dot(ob_ref[...], wob_ref[...])
        x = x + mods_ref[0, 5:6, :] * _rms(o, gpm_ref[...])
    shift = mods_ref[0, 3 * sub:3 * sub + 1, :]
    scale = mods_ref[0, 3 * sub + 1:3 * sub + 2, :]
    gate = mods_ref[0, 3 * sub + 2:3 * sub + 3, :]
    n = _rms(x, gpre_ref[...]) * (1.0 + scale) + shift
    n_scr[...] = n.astype(BF16)
    acc_scr[...] = jnp.zeros_like(acc_scr)
    n_chunks = wgu_ref.shape[0]
    fc = wd_ref.shape[1]

    def chunk(ci, carry):
        r = _dot(n_scr[...], wgu_ref[ci])
        g = r[:, :fc]
        u = r[:, fc:]
        a = (g * jax.nn.sigmoid(g) * u).astype(BF16)
        acc_scr[...] += _dot(a, wd_ref[ci])
        return carry

    lax.fori_loop(0, n_chunks, chunk, 0)
    f = acc_scr[...]
    o_ref[...] = x + 0.5 * gate * _rms(f, gpost_ref[...])


def _ffn(h, mods, g_pre, g_post, wgu, wd, sub, seq, mix=None):
    n_tok, d = h.shape
    tm = min(TOKEN_TILE, seq)
    tiles_per_batch = seq // tm
    n_chunks, _, fc2 = wgu.shape
    row = pl.BlockSpec((tm, d), lambda i: (i, 0))
    vec = pl.BlockSpec((1, d), lambda i: (0, 0))
    mod_spec = pl.BlockSpec((1, mods.shape[1], d), lambda i: (i // tiles_per_batch, 0, 0))
    args, specs = [h], [row]
    if mix is not None:
        oa, ob, woa, wob, g_post_mix = mix
        gw = oa.shape[1]
        half = pl.BlockSpec((tm, gw), lambda i: (i, 0))
        args += [oa, ob, woa, wob, g_post_mix]
        specs += [half, half, _resident(woa.shape), _resident(wob.shape), vec]
    args += [mods, g_pre, g_post, wgu, wd]
    specs += [mod_spec, vec, vec, _resident(wgu.shape), _resident(wd.shape)]
    return pl.pallas_call(
        functools.partial(_ffn_kernel, sub=sub, mix=mix is not None),
        grid=(n_tok // tm,),
        in_specs=specs,
        out_specs=row,
        out_shape=jax.ShapeDtypeStruct((n_tok, d), F32),
        scratch_shapes=[pltpu.VMEM((tm, d), BF16), pltpu.VMEM((tm, d), F32)],
        compiler_params=_params(1),
        name="ffn_mix" if mix is not None else "ffn",
    )(*args)


def _proj_kernel(h_ref, mods_ref, gpre_ref, gk_ref, wa_ref, wki_ref, wwi_ref, wb_ref,
                 qa_ref, ka_ref, va_ref, qi_ref, ki_ref, wi_ref, qb_ref, kb_ref, vb_ref, n_scr):
    x = h_ref[...]
    shift = mods_ref[0, 3:4, :]
    scale = mods_ref[0, 4:5, :]
    n_scr[...] = (_rms(x, gpre_ref[...]) * (1.0 + scale) + shift).astype(BF16)
    gw = GROUP_WIDTH
    qscale = HEAD_DIM ** -0.5

    def heads(w_ref, col, out_ref, mul):
        p = _dot(n_scr[...], w_ref[:, col * gw:(col + 1) * gw])
        if mul != 1.0:
            p = p * mul
        for hd in range(N_HEADS):
            out_ref[0, hd] = p[:, hd * HEAD_DIM:(hd + 1) * HEAD_DIM].astype(BF16)

    heads(wa_ref, 0, qa_ref, qscale)
    heads(wa_ref, 1, ka_ref, 1.0)
    heads(wa_ref, 2, va_ref, 1.0)
    heads(wa_ref, 3, qi_ref, 1.0)
    heads(wb_ref, 0, qb_ref, qscale)
    heads(wb_ref, 1, kb_ref, 1.0)
    heads(wb_ref, 2, vb_ref, 1.0)
    ki = _dot(n_scr[...], wki_ref[...])
    ki_ref[0] = _rms(ki, gk_ref[...]).astype(BF16)
    wi_ref[0] = _dot(n_scr[...], wwi_ref[...]) * ((N_HEADS * IDX_DIM) ** -0.5)


def _proj(h1, mods, g_pre, g_kidx, wa, wki, wwi, wb, bsz, seq):
    n_tok, d = h1.shape
    tm = min(TOKEN_TILE, seq)
    tpb = seq // tm
    row = pl.BlockSpec((tm, d), lambda i: (i, 0))
    vec = pl.BlockSpec((1, d), lambda i: (0, 0))
    mod_spec = pl.BlockSpec((1, mods.shape[1], d), lambda i: (i // tpb, 0, 0))
    hm = pl.BlockSpec((1, N_HEADS, tm, HEAD_DIM), lambda i: (i // tpb, 0, i % tpb, 0))
    hm_shape = jax.ShapeDtypeStruct((bsz, N_HEADS, seq, HEAD_DIM), BF16)
    ki_spec = pl.BlockSpec((1, tm, IDX_DIM), lambda i: (i // tpb, i % tpb, 0))
    wi_spec = pl.BlockSpec((1, tm, 128), lambda i: (i // tpb, i % tpb, 0))
    return pl.pallas_call(
        _proj_kernel,
        grid=(n_tok // tm,),
        in_specs=[row, mod_spec, vec, pl.BlockSpec((1, IDX_DIM), lambda i: (0, 0)),
                  _resident(wa.shape), _resident(wki.shape), _resident(wwi.shape),
                  _resident(wb.shape)],
        out_specs=[hm, hm, hm, hm, ki_spec, wi_spec, hm, hm, hm],
        out_shape=[hm_shape, hm_shape, hm_shape, hm_shape,
                   jax.ShapeDtypeStruct((bsz, seq, IDX_DIM), BF16),
                   jax.ShapeDtypeStruct((bsz, seq, 128), F32),
                   hm_shape, hm_shape, hm_shape],
        scratch_shapes=[pltpu.VMEM((tm, d), BF16)],
        compiler_params=_params(1),
        name="proj",
    )(h1, mods, g_pre, g_kidx, wa, wki, wwi, wb)


def _t5_bucket_np(n):
    n = np.maximum(n, 0)
    max_exact = N_BUCKETS // 2
    nf = np.maximum(n, 1).astype(np.float32)
    large = max_exact + (np.log(nf / np.float32(max_exact)) / np.float32(math.log(MAX_DISTANCE / max_exact))
                         * np.float32(N_BUCKETS - max_exact)).astype(np.int32)
    large = np.minimum(large, N_BUCKETS - 1)
    return np.where(n < max_exact, n, large).astype(np.int32)


def _bias_kernel(relb_ref, bucket_ref, o_ref):
    hd = pl.program_id(0)
    far = relb_ref[N_BUCKETS - 1, hd]
    for r in range(2):
        bk = bucket_ref[r]
        acc = jnp.zeros(bk.shape, F32)
        for k in range(N_BUCKETS - 1):
            acc = jnp.where(bk == k, relb_ref[k, hd] - far, acc)
        o_ref[0, r] = acc


def _bias_tiles(rel_bias, t):
    assert t >= MAX_DISTANCE
    r = np.arange(t)[:, None]
    c = np.arange(t)[None, :]
    buckets = np.stack([_t5_bucket_np(r - c), _t5_bucket_np(t + r - c)]).astype(np.int32)
    return pl.pallas_call(
        _bias_kernel,
        grid=(N_HEADS,),
        in_specs=[pl.BlockSpec(memory_space=pltpu.SMEM),
                  pl.BlockSpec((2, t, t), lambda hd: (0, 0, 0))],
        out_specs=pl.BlockSpec((1, 2, t, t), lambda hd: (hd, 0, 0, 0)),
        out_shape=jax.ShapeDtypeStruct((N_HEADS, 2, t, t), F32),
        compiler_params=_params(1),
        name="bias_tiles",
    )(rel_bias, jnp.asarray(buckets))


def _fold(x):
    w = x.shape[1] // 128
    acc = x[:, :128]
    for j in range(1, w):
        acc = acc + x[:, j * 128:(j + 1) * 128]
    return acc


def _fold_max(x):
    w = x.shape[1] // 128
    acc = x[:, :128]
    for j in range(1, w):
        acc = jnp.maximum(acc, x[:, j * 128:(j + 1) * 128])
    return acc


def _dsa_kernel(qi_ref, wi_ref, ki_ref, qa_ref, ka_ref, va_ref, bias_ref, g_ref, o_ref,
                key_scr, selm_scr, logit_scr, oh_scr, *, topk, idx_bits):
    t = DSA_TILE
    i = pl.program_id(1)
    nkb = i + 1
    row = lax.broadcasted_iota(I32, (t, t), 0)
    col = lax.broadcasted_iota(I32, (t, t), 1)
    causal = col <= row
    wi = wi_ref[0]

    def score_block(kb, diag):
        kblk = ki_ref[0, pl.ds(pl.multiple_of(kb * t, t), t), :]
        acc = jnp.zeros((t, t), F32)
        for hd in range(N_HEADS):
            sc = _dot_nt(qi_ref[0, hd], kblk)
            acc = acc + jnp.maximum(sc, 0.0) * wi[:, hd:hd + 1]
        acc = acc + 0.0
        if diag:
            acc = jnp.where(causal, acc, -jnp.inf)
        bits = pltpu.bitcast(acc, I32)
        key_scr[kb] = bits ^ ((bits >> 31) & jnp.int32(0x7FFFFFFF))

    def score_far(kb, c):
        score_block(kb, False)
        return c

    lax.fori_loop(0, i, score_far, 0)
    score_block(i, True)

    def count(indicator):
        def body(kb, acc):
            return acc + _fold(indicator(key_scr[kb], kb))
        acc = lax.fori_loop(0, nkb, body, jnp.zeros((t, 128), F32))
        return jnp.sum(acc, axis=1, keepdims=True)

    kf = float(topk)

    def search_step(j, ans):
        cand_off = ans | lax.shift_left(jnp.int32(1), 31 - j)
        cand = cand_off ^ INT_MIN
        cnt = count(lambda k, kb: jnp.where(k >= cand, 1.0, 0.0))
        return jnp.where(cnt >= kf, cand_off, ans)

    thr = lax.fori_loop(0, 32, search_step, jnp.zeros((t, 1), I32)) ^ INT_MIN

    def write_mask(mask_fn):
        def far(kb, c):
            selm_scr[kb] = mask_fn(key_scr[kb], kb)
            return c
        lax.fori_loop(0, i, far, 0)
        selm_scr[i] = jnp.where(causal, mask_fn(key_scr[i], i), NEG_BIG)

    write_mask(lambda k, kb: jnp.where(k >= thr, 0.0, NEG_BIG))

    need = kf - count(lambda k, kb: jnp.where(k > thr, 1.0, 0.0))
    n_tied = count(lambda k, kb: jnp.where(k == thr, 1.0, 0.0))
    excess = jnp.max(jnp.where(n_tied > need, 1.0, 0.0))

    @pl.when(excess > 0.0)
    def _():
        def tied_before(bound):
            return count(lambda k, kb: jnp.where(
                k == thr, jnp.where((col + kb * t) < bound, 1.0, 0.0), 0.0))

        def idx_step(j, jm):
            cand = jm | lax.shift_left(jnp.int32(1), idx_bits - 1 - j)
            return jnp.where(tied_before(cand) < need, cand, jm)

        jm = lax.fori_loop(0, idx_bits, idx_step, jnp.zeros((t, 1), I32))
        write_mask(lambda k, kb: jnp.where(
            k > thr, 0.0,
            jnp.where(k == thr, jnp.where((col + kb * t) <= jm, 0.0, NEG_BIG), NEG_BIG)))

    def head(hd, c):
        q = qa_ref[0, hd]

        def logits(kb, band):
            kblk = ka_ref[0, hd, pl.ds(pl.multiple_of(kb * t, t), t), :]
            lg = _dot_nt(q, kblk) + selm_scr[kb]
            if band is not None:
                lg = lg + bias_ref[hd, band]
            logit_scr[kb] = lg
            return _fold_max(lg)

        def far(kb, m):
            return jnp.maximum(m, logits(kb, None))

        m = lax.fori_loop(0, jnp.maximum(i - 1, 0), far, jnp.full((t, 128), NEG_BIG, F32))
        m = jnp.maximum(m, logits(i, 0))
        m = lax.cond(i >= 1, lambda: jnp.maximum(m, logits(i - 1, 1)), lambda: m)
        mrow = jnp.max(m, axis=1, keepdims=True)

        def pv(kb, carry):
            ssum, acc = carry
            p = jnp.exp(logit_scr[kb] - mrow)
            vblk = va_ref[0, hd, pl.ds(pl.multiple_of(kb * t, t), t), :]
            return ssum + _fold(p), acc + _dot(p.astype(BF16), vblk)

        ssum, acc = lax.fori_loop(0, nkb, pv, (jnp.zeros((t, 128), F32),
                                               jnp.zeros((t, HEAD_DIM), F32)))
        oh_scr[hd] = acc / jnp.sum(ssum, axis=1, keepdims=True)
        return c

    lax.fori_loop(0, N_HEADS, head, 0)
    o = jnp.concatenate([oh_scr[hd] for hd in range(N_HEADS)], axis=1)
    o_ref[0] = _rms(o, g_ref[...]).astype(BF16)


def _dsa(qi, wi, ki, qa, ka, va, bias, g_out, bsz, seq):
    t = DSA_TILE
    nq = seq // t
    topk = min(TOPK_MAX, seq // 4)
    q_hm = pl.BlockSpec((1, N_HEADS, t, HEAD_DIM), lambda b, i: (b, 0, i, 0))
    kv_hm = pl.BlockSpec((1, N_HEADS, seq, HEAD_DIM), lambda b, i: (b, 0, 0, 0))
    return pl.pallas_call(
        functools.partial(_dsa_kernel, topk=topk, idx_bits=(seq - 1).bit_length()),
        grid=(bsz, nq),
        in_specs=[q_hm,
                  pl.BlockSpec((1, t, 128), lambda b, i: (b, i, 0)),
                  pl.BlockSpec((1, seq, IDX_DIM), lambda b, i: (b, 0, 0)),
                  q_hm, kv_hm, kv_hm,
                  _resident(bias.shape),
                  pl.BlockSpec((1, GROUP_WIDTH), lambda b, i: (0, 0))],
        out_specs=pl.BlockSpec((1, t, GROUP_WIDTH), lambda b, i: (b, i, 0)),
        out_shape=jax.ShapeDtypeStruct((bsz, seq, GROUP_WIDTH), BF16),
        scratch_shapes=[pltpu.VMEM((nq, t, t), I32), pltpu.VMEM((nq, t, t), F32),
                        pltpu.VMEM((nq, t, t), F32), pltpu.VMEM((N_HEADS, t, HEAD_DIM), F32)],
        compiler_params=_params(2),
        name="dsa",
    )(qi, wi, ki, qa, ka, va, bias, g_out)


def _sb_kernel(q_ref, k_ref, v_ref, tri_ref, g_ref, o_ref, r_scr, acc_scr):
    t = SB_TILE
    i = pl.program_id(1)
    row = lax.broadcasted_iota(I32, (t, t), 0)
    col = lax.broadcasted_iota(I32, (t, t), 1)
    strict = col < row

    def block(kb, diag):
        rmax = None
        for hd in range(N_HEADS):
            rows = pl.ds(pl.multiple_of(kb * t, t), t)
            z = _dot_nt(q_ref[0, hd], k_ref[0, hd, rows, :])
            lg = -(jnp.maximum(z, 0.0) + jnp.log(1.0 + jnp.exp(-jnp.abs(z))))
            if diag:
                lg = jnp.where(strict, lg, 0.0)
            hi = lg.astype(BF16)
            lo = (lg - hi.astype(F32)).astype(BF16)
            cs = _dot(hi, tri_ref[...]) + _dot(lo, tri_ref[...])
            suffix = cs[:, :t]
            total = cs[:, t:]
            if diag:
                a = jnp.where(strict, jnp.exp(z + suffix), 0.0)
                r_new = total
                acc_scr[hd] = _dot(a.astype(BF16), v_ref[0, hd, rows, :])
            else:
                r_old = r_scr[hd]
                a = jnp.exp(z + suffix + r_old)
                r_new = r_old + total
                acc_scr[hd] += _dot(a.astype(BF16), v_ref[0, hd, rows, :])
            r_scr[hd] = r_new
            rmax = r_new if rmax is None else jnp.maximum(rmax, r_new)
        return jnp.max(rmax)

    rmax0 = block(i, True)

    def cond(c):
        kb, rmax = c
        return jnp.logical_and(kb >= 0, rmax > SB_DEAD)

    def body(c):
        kb, _ = c
        return kb - 1, block(kb, False)

    lax.while_loop(cond, body, (i - 1, rmax0))
    o = jnp.concatenate([acc_scr[hd] for hd in range(N_HEADS)], axis=1)
    o_ref[0] = _rms(o, g_ref[...]).astype(BF16)


def _sb(q, k, v, g_out, bsz, seq):
    t = SB_TILE
    nq = seq // t
    j = np.arange(t)[:, None]
    s = np.arange(t)[None, :]
    tri = np.concatenate([(j >= s), np.ones((t, t), bool)], axis=1).astype(np.float32)
    q_hm = pl.BlockSpec((1, N_HEADS, t, HEAD_DIM), lambda b, i: (b, 0, i, 0))
    kv_hm = pl.BlockSpec((1, N_HEADS, seq, HEAD_DIM), lambda b, i: (b, 0, 0, 0))
    return pl.pallas_call(
        _sb_kernel,
        grid=(bsz, nq),
        in_specs=[q_hm, kv_hm, kv_hm, _resident((t, 2 * t)),
                  pl.BlockSpec((1, GROUP_WIDTH), lambda b, i: (0, 0))],
        out_specs=pl.BlockSpec((1, t, GROUP_WIDTH), lambda b, i: (b, i, 0)),
        out_shape=jax.ShapeDtypeStruct((bsz, seq, GROUP_WIDTH), BF16),
        scratch_shapes=[pltpu.VMEM((N_HEADS, t, t), F32), pltpu.VMEM((N_HEADS, t, HEAD_DIM), F32)],
        compiler_params=_params(2),
        name="sb",
    )(q, k, v, jnp.asarray(tri, dtype=BF16), g_out)


def _pack_ffn(w_gate, w_up, w_down):
    d, f = w_gate.shape
    nc = f // FFN_CHUNK
    wg = w_gate.reshape(d, nc, FFN_CHUNK)
    wu = w_up.reshape(d, nc, FFN_CHUNK)
    wgu = jnp.concatenate([wg, wu], axis=2).transpose(1, 0, 2).astype(BF16)
    wd = w_down.reshape(nc, FFN_CHUNK, d).astype(BF16)
    return wgu, wd


def kernel(x, c, w_ada, b_ada, g_pre, g_post, w_ffn_gate, w_ffn_up, w_ffn_down, w_in, g_kidx,
           rel_bias, g_out_a, g_out_b, w_out):
    bsz, seq, d = x.shape
    depth = w_ada.shape[0]
    gw = GROUP_WIDTH
    h = x.reshape(bsz * seq, d)
    bias = _bias_tiles(rel_bias, DSA_TILE)
    for l in range(depth):
        mods = _ada(c, w_ada[l], b_ada[l]).reshape(bsz, -1, d)
        wgu1, wd1 = _pack_ffn(w_ffn_gate[l, 0], w_ffn_up[l, 0], w_ffn_down[l, 0])
        wgu2, wd2 = _pack_ffn(w_ffn_gate[l, 1], w_ffn_up[l, 1], w_ffn_down[l, 1])
        w = w_in[l]
        o_ki = 4 * gw
        o_wi = o_ki + IDX_DIM
        o_b = o_wi + N_HEADS
        wa = w[:, :o_ki].astype(BF16)
        wki = w[:, o_ki:o_wi].astype(BF16)
        wwi = jnp.pad(w[:, o_wi:o_b], ((0, 0), (0, 128 - N_HEADS))).astype(BF16)
        wb = w[:, o_b:].astype(BF16)
        woa = w_out[l, :gw].astype(BF16)
        wob = w_out[l, gw:].astype(BF16)
        vec = lambda g: g.reshape(1, -1)

        h1 = _ffn(h, mods, vec(g_pre[l, 0]), vec(g_post[l, 0]), wgu1, wd1, 0, seq)
        qa, ka, va, qi, ki, wi, qb, kb, vb = _proj(h1, mods, vec(g_pre[l, 1]), vec(g_kidx[l]),
                                                   wa, wki, wwi, wb, bsz, seq)
        oa = _dsa(qi, wi, ki, qa, ka, va, bias, vec(g_out_a[l]), bsz, seq)
        ob = _sb(qb, kb, vb, vec(g_out_b[l]), bsz, seq)
        h = _ffn(h1, mods, vec(g_pre[l, 2]), vec(g_post[l, 2]), wgu2, wd2, 2, seq,
                 mix=(oa.reshape(bsz * seq, gw), ob.reshape(bsz * seq, gw), woa, wob,
                      vec(g_post[l, 1])))
    return h.reshape(bsz, seq, d)
```

```python
import functools
import math

import numpy as np
import jax
import jax.numpy as jnp
from jax import lax
from jax.experimental import pallas as pl
from jax.experimental.pallas import tpu as pltpu

F32 = jnp.float32
BF16 = jnp.bfloat16
I32 = jnp.int32

EPS = 1e-6
HEAD_DIM = 64
N_HEADS = 8
GROUP_WIDTH = N_HEADS * HEAD_DIM
IDX_DIM = 64
TOPK_MAX = 256
N_BUCKETS = 32
MAX_DISTANCE = 128
SUBLANES = 8
FFN_CHUNK = 256
TOKEN_TILE = 512
DSA_TILE = 256
SCORE_ROWS = 64
SB_TILE = 128
NEG_BIG = -1e30
SB_DEAD = -104.0
VMEM_LIMIT = 56 * 1024 * 1024
INT_MIN = np.int32(-2 ** 31)


def _dot(a, b):
    return jnp.dot(a, b, preferred_element_type=F32)


def _dot_nt(a, b):
    return lax.dot_general(a, b, (((1,), (1,)), ((), ())), preferred_element_type=F32)


def _rms(x, g):
    return x * lax.rsqrt(jnp.mean(x * x, axis=-1, keepdims=True) + EPS) * g


def _resident(shape):
    n = len(shape)
    return pl.BlockSpec(shape, lambda *_: (0,) * n, pipeline_mode=pl.Buffered(1))


def _params(n_grid):
    return pltpu.CompilerParams(dimension_semantics=("arbitrary",) * n_grid,
                                vmem_limit_bytes=VMEM_LIMIT)


def _ada_kernel(c_ref, w_ref, b_ref, o_ref):
    c = c_ref[...]
    a = c * jax.nn.sigmoid(c)
    o_ref[...] = jnp.dot(a, w_ref[...], preferred_element_type=F32,
                         precision=lax.Precision.HIGHEST) + b_ref[...]


def _ada(c, w, b):
    bsz, d = c.shape
    n = w.shape[1]
    tn = 1152 if n % 1152 == 0 else n
    return pl.pallas_call(
        _ada_kernel,
        grid=(n // tn,),
        in_specs=[pl.BlockSpec((bsz, d), lambda j: (0, 0)),
                  pl.BlockSpec((d, tn), lambda j: (0, j)),
                  pl.BlockSpec((1, tn), lambda j: (0, j))],
        out_specs=pl.BlockSpec((bsz, tn), lambda j: (0, j)),
        out_shape=jax.ShapeDtypeStruct((bsz, n), F32),
        compiler_params=_params(1),
        name="ada",
    )(c, w, b.reshape(1, n))


def _ffn_kernel(*refs, sub, mix):
    if mix:
        (h_ref, oa_ref, ob_ref, woa_ref, wob_ref, gpm_ref, mods_ref, gpre_ref, gpost_ref,
         wgu_ref, wd_ref, o_ref, n_scr, acc_scr) = refs
    else:
        (h_ref, mods_ref, gpre_ref, gpost_ref, wgu_ref, wd_ref, o_ref, n_scr, acc_scr) = refs
    x = h_ref[...]
    if mix:
        o = _dot(oa_ref[...], woa_ref[...]) + _dot(ob_ref[...], wob_ref[...])
        x = x + mods_ref[0, 5:6, :] * _rms(o, gpm_ref[...])
    shift = mods_ref[0, 3 * sub:3 * sub + 1, :]
    scale = mods_ref[0, 3 * sub + 1:3 * sub + 2, :]
    gate = mods_ref[0, 3 * sub + 2:3 * sub + 3, :]
    n = _rms(x, gpre_ref[...]) * (1.0 + scale) + shift
    n_scr[...] = n.astype(BF16)
    acc_scr[...] = jnp.zeros_like(acc_scr)
    n_chunks = wgu_ref.shape[0]
    fc = wd_ref.shape[1]

    def chunk(ci, carry):
        r = _dot(n_scr[...], wgu_ref[ci])
        g = r[:, :fc]
        u = r[:, fc:]
        a = (g * jax.nn.sigmoid(g) * u).astype(BF16)
        acc_scr[...] += _dot(a, wd_ref[ci])
        return carry

    lax.fori_loop(0, n_chunks, chunk, 0)
    f = acc_scr[...]
    o_ref[...] = x + 0.5 * gate * _rms(f, gpost_ref[...])


def _ffn(h, mods, g_pre, g_post, wgu, wd, sub, seq, mix=None):
    n_tok, d = h.shape
    tm = min(TOKEN_TILE, seq)
    tiles_per_batch = seq // tm
    row = pl.BlockSpec((tm, d), lambda i: (i, 0))
    vec = pl.BlockSpec((1, d), lambda i: (0, 0))
    mod_spec = pl.BlockSpec((1, mods.shape[1], d), lambda i: (i // tiles_per_batch, 0, 0))
    args, specs = [h], [row]
    if mix is not None:
        oa, ob, woa, wob, g_post_mix = mix
        gw = oa.shape[1]
        half = pl.BlockSpec((tm, gw), lambda i: (i, 0))
        args += [oa, ob, woa, wob, g_post_mix]
        specs += [half, half, _resident(woa.shape), _resident(wob.shape), vec]
    args += [mods, g_pre, g_post, wgu, wd]
    specs += [mod_spec, vec, vec, _resident(wgu.shape), _resident(wd.shape)]
    return pl.pallas_call(
        functools.partial(_ffn_kernel, sub=sub, mix=mix is not None),
        grid=(n_tok // tm,),
        in_specs=specs,
        out_specs=row,
        out_shape=jax.ShapeDtypeStruct((n_tok, d), F32),
        scratch_shapes=[pltpu.VMEM((tm, d), BF16), pltpu.VMEM((tm, d), F32)],
        compiler_params=_params(1),
        name="ffn_mix" if mix is not None else "ffn",
    )(*args)


def _proj_kernel(h_ref, mods_ref, gpre_ref, gk_ref, wt_ref, wka_ref, wki_ref, wwit_ref, wb_ref,
                 qat_ref, ka_ref, vat_ref, qit_ref, ki_ref, wit_ref, qb_ref, kb_ref, vb_ref, n_scr):
    x = h_ref[...]
    shift = mods_ref[0, 3:4, :]
    scale = mods_ref[0, 4:5, :]
    n_scr[...] = (_rms(x, gpre_ref[...]) * (1.0 + scale) + shift).astype(BF16)
    gw = GROUP_WIDTH
    qscale = HEAD_DIM ** -0.5

    def heads(w, out_ref, mul):
        p = _dot(n_scr[...], w)
        if mul != 1.0:
            p = p * mul
        for hd in range(N_HEADS):
            out_ref[0, hd] = p[:, hd * HEAD_DIM:(hd + 1) * HEAD_DIM].astype(BF16)

    def feature_major(col, out_ref, mul):
        p = _dot_nt(wt_ref[col * gw:(col + 1) * gw, :], n_scr[...])
        if mul != 1.0:
            p = p * mul
        out_ref[0] = p.astype(BF16)

    feature_major(0, qat_ref, qscale)
    feature_major(1, vat_ref, 1.0)
    feature_major(2, qit_ref, 1.0)
    heads(wka_ref[...], ka_ref, 1.0)
    heads(wb_ref[:, :gw], qb_ref, qscale)
    heads(wb_ref[:, gw:2 * gw], kb_ref, 1.0)
    heads(wb_ref[:, 2 * gw:], vb_ref, 1.0)
    ki = _dot(n_scr[...], wki_ref[...])
    ki_ref[0] = _rms(ki, gk_ref[...]).astype(BF16)
    wit = _dot_nt(wwit_ref[...], n_scr[...])
    wit_ref[0] = wit[:N_HEADS, :] * ((N_HEADS * IDX_DIM) ** -0.5)


def _proj(h1, mods, g_pre, g_kidx, wt, wka, wki, wwit, wb, bsz, seq):
    n_tok, d = h1.shape
    tm = min(TOKEN_TILE, seq)
    tpb = seq // tm
    gw = GROUP_WIDTH
    row = pl.BlockSpec((tm, d), lambda i: (i, 0))
    vec = pl.BlockSpec((1, d), lambda i: (0, 0))
    mod_spec = pl.BlockSpec((1, mods.shape[1], d), lambda i: (i // tpb, 0, 0))
    hm = pl.BlockSpec((1, N_HEADS, tm, HEAD_DIM), lambda i: (i // tpb, 0, i % tpb, 0))
    hm_shape = jax.ShapeDtypeStruct((bsz, N_HEADS, seq, HEAD_DIM), BF16)
    fm = pl.BlockSpec((1, gw, tm), lambda i: (i // tpb, 0, i % tpb))
    fm_shape = jax.ShapeDtypeStruct((bsz, gw, seq), BF16)
    ki_spec = pl.BlockSpec((1, tm, IDX_DIM), lambda i: (i // tpb, i % tpb, 0))
    wit_spec = pl.BlockSpec((1, N_HEADS, tm), lambda i: (i // tpb, 0, i % tpb))
    return pl.pallas_call(
        _proj_kernel,
        grid=(n_tok // tm,),
        in_specs=[row, mod_spec, vec, pl.BlockSpec((1, IDX_DIM), lambda i: (0, 0)),
                  _resident(wt.shape), _resident(wka.shape), _resident(wki.shape),
                  _resident(wwit.shape), _resident(wb.shape)],
        out_specs=[fm, hm, fm, fm, ki_spec, wit_spec, hm, hm, hm],
        out_shape=[fm_shape, hm_shape, fm_shape, fm_shape,
                   jax.ShapeDtypeStruct((bsz, seq, IDX_DIM), BF16),
                   jax.ShapeDtypeStruct((bsz, N_HEADS, seq), F32),
                   hm_shape, hm_shape, hm_shape],
        scratch_shapes=[pltpu.VMEM((tm, d), BF16)],
        compiler_params=_params(1),
        name="proj",
    )(h1, mods, g_pre, g_kidx, wt, wka, wki, wwit, wb)


def _t5_bucket_np(n):
    n = np.maximum(n, 0)
    max_exact = N_BUCKETS // 2
    nf = np.maximum(n, 1).astype(np.float32)
    large = max_exact + (np.log(nf / np.float32(max_exact)) / np.float32(math.log(MAX_DISTANCE / max_exact))
                         * np.float32(N_BUCKETS - max_exact)).astype(np.int32)
    large = np.minimum(large, N_BUCKETS - 1)
    return np.where(n < max_exact, n, large).astype(np.int32)


def _bias_kernel(relb_ref, bucket_ref, o_ref):
    hd = pl.program_id(0)
    far = relb_ref[N_BUCKETS - 1, hd]
    for r in range(2):
        bk = bucket_ref[r]
        acc = jnp.zeros(bk.shape, F32)
        for k in range(N_BUCKETS - 1):
            acc = jnp.where(bk == k, relb_ref[k, hd] - far, acc)
        o_ref[0, r] = acc


def _bias_tiles(rel_bias, t):
    assert t >= MAX_DISTANCE
    s = np.arange(t)[:, None]
    q = np.arange(t)[None, :]
    buckets = np.stack([_t5_bucket_np(q - s), _t5_bucket_np(t + q - s)]).astype(np.int32)
    return pl.pallas_call(
        _bias_kernel,
        grid=(N_HEADS,),
        in_specs=[pl.BlockSpec(memory_space=pltpu.SMEM),
                  pl.BlockSpec((2, t, t), lambda hd: (0, 0, 0))],
        out_specs=pl.BlockSpec((1, 2, t, t), lambda hd: (hd, 0, 0, 0)),
        out_shape=jax.ShapeDtypeStruct((N_HEADS, 2, t, t), F32),
        compiler_params=_params(1),
        name="bias_tiles",
    )(rel_bias, jnp.asarray(buckets))


def _tree(op, xs):
    xs = list(xs)
    while len(xs) > 1:
        xs = [op(xs[j], xs[j + 1]) if j + 1 < len(xs) else xs[j] for j in range(0, len(xs), 2)]
    return xs[0]


def _row_groups(x):
    return [x[r:r + SUBLANES, :] for r in range(0, x.shape[0], SUBLANES)]


def _all_sublanes(op, x):
    for shift in (4, 2, 1):
        x = op(x, pltpu.roll(x, shift, 0))
    return x


def _dsa_kernel(qit_ref, wit_ref, ki_ref, qat_ref, ka_ref, vat_ref, bias_ref, gt_ref, o_ref,
                key_scr, selm_scr, logit_scr, out_scr, *, topk, idx_bits, n_q_tiles):
    t = DSA_TILE
    kf = float(topk)
    krow = lax.broadcasted_iota(I32, (t, t), 0)
    qcol = lax.broadcasted_iota(I32, (t, t), 1)
    causal = krow <= qcol
    krow8 = lax.broadcasted_iota(I32, (SUBLANES, t), 0)

    def tile_work(nkb):
        diag = nkb - 1

        wt = wit_ref[0]
        for kb in range(nkb):
            for r0 in range(0, t, SCORE_ROWS):
                kblk = ki_ref[0, kb * t + r0:kb * t + r0 + SCORE_ROWS, :]
                acc = None
                for hd in range(N_HEADS):
                    sc = _dot(kblk, qit_ref[0, hd * IDX_DIM:(hd + 1) * IDX_DIM, :])
                    term = jnp.maximum(sc, 0.0) * wt[hd:hd + 1, :]
                    acc = term if acc is None else acc + term
                acc = acc + 0.0
                if kb == diag:
                    acc = jnp.where(causal[r0:r0 + SCORE_ROWS, :], acc, -jnp.inf)
                bits = pltpu.bitcast(acc, I32)
                key_scr[kb, r0:r0 + SCORE_ROWS, :] = bits ^ ((bits >> 31) & jnp.int32(0x7FFFFFFF))

        def count_ge(cand):
            accs = [jnp.zeros((SUBLANES, t), F32) for _ in range(4)]
            n = 0
            for kb in range(nkb):
                for grp in _row_groups(key_scr[kb]):
                    accs[n % 4] = accs[n % 4] + jnp.where(grp >= cand, 1.0, 0.0)
                    n += 1
            return _all_sublanes(jnp.add, _tree(jnp.add, accs))

        def search_step(j, ans):
            cand_off = ans | lax.shift_left(jnp.int32(1), 31 - j)
            cnt = count_ge(cand_off ^ INT_MIN)
            return jnp.where(cnt >= kf, cand_off, ans)

        thr = lax.fori_loop(0, 32, search_step, jnp.zeros((SUBLANES, t), I32)) ^ INT_MIN

        n_ge = jnp.zeros((SUBLANES, t), F32)
        for kb in range(nkb):
            blk = key_scr[kb]
            for r, grp in enumerate(_row_groups(blk)):
                rows = slice(r * SUBLANES, (r + 1) * SUBLANES)
                sel = grp >= thr
                n_ge = n_ge + jnp.where(sel, 1.0, 0.0)
                m = jnp.where(sel, 0.0, NEG_BIG)
                if kb == diag:
                    m = jnp.where(causal[rows, :], m, NEG_BIG)
                selm_scr[kb, rows, :] = m
        n_ge = _all_sublanes(jnp.add, n_ge)
        excess = jnp.max(jnp.where(n_ge > kf, 1.0, 0.0))

        @pl.when(excess > 0.0)
        def _():
            def count(indicator):
                def body(g, acc):
                    kb = g // (t // SUBLANES)
                    r = g % (t // SUBLANES)
                    rows = pl.ds(pl.multiple_of(r * SUBLANES, SUBLANES), SUBLANES)
                    return acc + indicator(key_scr[kb, rows, :], krow8 + (kb * t + r * SUBLANES))
                acc = lax.fori_loop(0, nkb * (t // SUBLANES), body, jnp.zeros((SUBLANES, t), F32))
                return _all_sublanes(jnp.add, acc)

            need = kf - count(lambda k, s: jnp.where(k > thr, 1.0, 0.0))

            def idx_step(j, jm):
                cand = jm | lax.shift_left(jnp.int32(1), idx_bits - 1 - j)
                before = count(lambda k, s: jnp.where(k == thr, jnp.where(s < cand, 1.0, 0.0), 0.0))
                return jnp.where(before < need, cand, jm)

            jm = lax.fori_loop(0, idx_bits, idx_step, jnp.zeros((SUBLANES, t), I32))

            def rewrite(g, c):
                kb = g // (t // SUBLANES)
                r = g % (t // SUBLANES)
                rows = pl.ds(pl.multiple_of(r * SUBLANES, SUBLANES), SUBLANES)
                k = key_scr[kb, rows, :]
                s = krow8 + (kb * t + r * SUBLANES)
                keep = jnp.where(k > thr, 0.0,
                                 jnp.where(k == thr, jnp.where(s <= jm, 0.0, NEG_BIG), NEG_BIG))
                selm_scr[kb, rows, :] = jnp.where(selm_scr[kb, rows, :] < 0.0, NEG_BIG, keep)
                return c

            lax.fori_loop(0, nkb * (t // SUBLANES), rewrite, 0)

        def head(hd, c):
            hrows = pl.ds(pl.multiple_of(hd * HEAD_DIM, HEAD_DIM), HEAD_DIM)
            qt = qat_ref[0, hrows, :]
            maxes = []
            for kb in range(nkb):
                lg = _dot(ka_ref[0, hd, kb * t:(kb + 1) * t, :], qt) + selm_scr[kb]
                if kb >= nkb - 2:
                    lg = lg + bias_ref[hd, diag - kb]
                logit_scr[kb] = lg
                maxes.append(jnp.max(lg, axis=0, keepdims=True))
            m = _tree(jnp.maximum, maxes)
            ssum = None
            acc = None
            for kb in range(nkb):
                p = jnp.exp(logit_scr[kb] - m)
                ps = jnp.sum(p, axis=0, keepdims=True)
                pv = _dot(vat_ref[0, hrows, kb * t:(kb + 1) * t], p.astype(BF16))
                ssum = ps if ssum is None else ssum + ps
                acc = pv if acc is None else acc + pv
            out_scr[hrows, :] = acc * (1.0 / ssum)
            return c

        lax.fori_loop(0, N_HEADS, head, 0)

    i = pl.program_id(1)
    for k in range(n_q_tiles):
        pl.when(i == k)(functools.partial(tile_work, k + 1))

    o = out_scr[...]
    o = o * lax.rsqrt(jnp.mean(o * o, axis=0, keepdims=True) + EPS) * gt_ref[...]
    o_ref[0] = o.T.astype(BF16)


def _dsa(qit, wit, ki, qat, ka, vat, bias, g_out, bsz, seq):
    t = DSA_TILE
    nq = seq // t
    gw = GROUP_WIDTH
    topk = min(TOPK_MAX, seq // 4)
    q_fm = pl.BlockSpec((1, gw, t), lambda b, i: (b, 0, i))
    gt = jnp.broadcast_to(g_out.reshape(gw, 1), (gw, t))
    return pl.pallas_call(
        functools.partial(_dsa_kernel, topk=topk, idx_bits=(seq - 1).bit_length(), n_q_tiles=nq),
        grid=(bsz, nq),
        in_specs=[q_fm,
                  pl.BlockSpec((1, N_HEADS, t), lambda b, i: (b, 0, i)),
                  pl.BlockSpec((1, seq, IDX_DIM), lambda b, i: (b, 0, 0)),
                  q_fm,
                  pl.BlockSpec((1, N_HEADS, seq, HEAD_DIM), lambda b, i: (b, 0, 0, 0)),
                  pl.BlockSpec((1, gw, seq), lambda b, i: (b, 0, 0)),
                  _resident(bias.shape),
                  _resident((gw, t))],
        out_specs=pl.BlockSpec((1, t, gw), lambda b, i: (b, i, 0)),
        out_shape=jax.ShapeDtypeStruct((bsz, seq, gw), BF16),
        scratch_shapes=[pltpu.VMEM((nq, t, t), I32), pltpu.VMEM((nq, t, t), F32),
                        pltpu.VMEM((nq, t, t), F32), pltpu.VMEM((gw, t), F32)],
        compiler_params=_params(2),
        name="dsa",
    )(qit, wit, ki, qat, ka, vat, bias, gt)


def _sb_kernel(q_ref, k_ref, v_ref, tri_ref, g_ref, o_ref, r_scr, acc_scr):
    t = SB_TILE
    i = pl.program_id(1)
    row = lax.broadcasted_iota(I32, (t, t), 0)
    col = lax.broadcasted_iota(I32, (t, t), 1)
    strict = col < row

    def block(kb, diag):
        rmax = None
        for hd in range(N_HEADS):
            rows = pl.ds(pl.multiple_of(kb * t, t), t)
            z = _dot_nt(q_ref[0, hd], k_ref[0, hd, rows, :])
            lg = -(jnp.maximum(z, 0.0) + jnp.log(1.0 + jnp.exp(-jnp.abs(z))))
            if diag:
                lg = jnp.where(strict, lg, 0.0)
            hi = lg.astype(BF16)
            lo = (lg - hi.astype(F32)).astype(BF16)
            cs = _dot(hi, tri_ref[...]) + _dot(lo, tri_ref[...])
            suffix = cs[:, :t]
            total = cs[:, t:]
            if diag:
                a = jnp.where(strict, jnp.exp(z + suffix), 0.0)
                r_new = total
                acc_scr[hd] = _dot(a.astype(BF16), v_ref[0, hd, rows, :])
            else:
                r_old = r_scr[hd]
                a = jnp.exp(z + suffix + r_old)
                r_new = r_old + total
                acc_scr[hd] += _dot(a.astype(BF16), v_ref[0, hd, rows, :])
            r_scr[hd] = r_new
            rmax = r_new if rmax is None else jnp.maximum(rmax, r_new)
        return jnp.max(rmax)

    rmax0 = block(i, True)

    def cond(c):
        kb, rmax = c
        return jnp.logical_and(kb >= 0, rmax > SB_DEAD)

    def body(c):
        kb, _ = c
        return kb - 1, block(kb, False)

    lax.while_loop(cond, body, (i - 1, rmax0))
    o = jnp.concatenate([acc_scr[hd] for hd in range(N_HEADS)], axis=1)
    o_ref[0] = _rms(o, g_ref[...]).astype(BF16)


def _sb(q, k, v, g_out, bsz, seq):
    t = SB_TILE
    nq = seq // t
    j = np.arange(t)[:, None]
    s = np.arange(t)[None, :]
    tri = np.concatenate([(j >= s), np.ones((t, t), bool)], axis=1).astype(np.float32)
    q_hm = pl.BlockSpec((1, N_HEADS, t, HEAD_DIM), lambda b, i: (b, 0, i, 0))
    kv_hm = pl.BlockSpec((1, N_HEADS, seq, HEAD_DIM), lambda b, i: (b, 0, 0, 0))
    return pl.pallas_call(
        _sb_kernel,
        grid=(bsz, nq),
        in_specs=[q_hm, kv_hm, kv_hm, _resident((t, 2 * t)),
                  pl.BlockSpec((1, GROUP_WIDTH), lambda b, i: (0, 0))],
        out_specs=pl.BlockSpec((1, t, GROUP_WIDTH), lambda b, i: (b, i, 0)),
        out_shape=jax.ShapeDtypeStruct((bsz, seq, GROUP_WIDTH), BF16),
        scratch_shapes=[pltpu.VMEM((N_HEADS, t, t), F32), pltpu.VMEM((N_HEADS, t, HEAD_DIM), F32)],
        compiler_params=_params(2),
        name="sb",
    )(q, k, v, jnp.asarray(tri, dtype=BF16), g_out)


def _pack_ffn(w_gate, w_up, w_down):
    d, f = w_gate.shape
    nc = f // FFN_CHUNK
    wg = w_gate.reshape(d, nc, FFN_CHUNK)
    wu = w_up.reshape(d, nc, FFN_CHUNK)
    wgu = jnp.concatenate([wg, wu], axis=2).transpose(1, 0, 2).astype(BF16)
    wd = w_down.reshape(nc, FFN_CHUNK, d).astype(BF16)
    return wgu, wd


def kernel(x, c, w_ada, b_ada, g_pre, g_post, w_ffn_gate, w_ffn_up, w_ffn_down, w_in, g_kidx,
           rel_bias, g_out_a, g_out_b, w_out):
    bsz, seq, d = x.shape
    depth = w_ada.shape[0]
    gw = GROUP_WIDTH
    h = x.reshape(bsz * seq, d)
    bias = _bias_tiles(rel_bias, DSA_TILE)
    for l in range(depth):
        mods = _ada(c, w_ada[l], b_ada[l]).reshape(bsz, -1, d)
        wgu1, wd1 = _pack_ffn(w_ffn_gate[l, 0], w_ffn_up[l, 0], w_ffn_down[l, 0])
        wgu2, wd2 = _pack_ffn(w_ffn_gate[l, 1], w_ffn_up[l, 1], w_ffn_down[l, 1])
        w = w_in[l]
        o_ki = 4 * gw
        o_wi = o_ki + IDX_DIM
        o_b = o_wi + N_HEADS
        wt = jnp.concatenate([w[:, :gw], w[:, 2 * gw:3 * gw], w[:, 3 * gw:o_ki]], axis=1).T.astype(BF16)
        wka = w[:, gw:2 * gw].astype(BF16)
        wki = w[:, o_ki:o_wi].astype(BF16)
        wwit = jnp.pad(w[:, o_wi:o_b].T, ((0, 16 - N_HEADS), (0, 0))).astype(BF16)
        wb = w[:, o_b:].astype(BF16)
        woa = w_out[l, :gw].astype(BF16)
        wob = w_out[l, gw:].astype(BF16)
        vec = lambda g: g.reshape(1, -1)

        h1 = _ffn(h, mods, vec(g_pre[l, 0]), vec(g_post[l, 0]), wgu1, wd1, 0, seq)
        qat, ka, vat, qit, ki, wit, qb, kb, vb = _proj(h1, mods, vec(g_pre[l, 1]), vec(g_kidx[l]),
                                                       wt, wka, wki, wwit, wb, bsz, seq)
        oa = _dsa(qit, wit, ki, qat, ka, vat, bias, g_out_a[l], bsz, seq)
        ob = _sb(qb, kb, vb, vec(g_out_b[l]), bsz, seq)
        h = _ffn(h1, mods, vec(g_pre[l, 2]), vec(g_post[l, 2]), wgu2, wd2, 2, seq,
                 mix=(oa.reshape(bsz * seq, gw), ob.reshape(bsz * seq, gw), woa, wob,
                      vec(g_post[l, 1])))
    return h.reshape(bsz, seq, d)
```

```python
import functools
import math

import numpy as np
import jax
import jax.numpy as jnp
from jax import lax
from jax.experimental import pallas as pl
from jax.experimental.pallas import tpu as pltpu

F32 = jnp.float32
BF16 = jnp.bfloat16
I32 = jnp.int32

EPS = 1e-6
HEAD_DIM = 64
N_HEADS = 8
GROUP_WIDTH = N_HEADS * HEAD_DIM
IDX_DIM = 64
TOPK_MAX = 256
N_BUCKETS = 32
MAX_DISTANCE = 128
SUBLANES = 8
FFN_CHUNK = 256
TOKEN_TILE = 512
DSA_TILE = 256
SCORE_ROWS = 64
SB_TILE = 128
SB_WINDOW = 3
NEG_BIG = -1e30
SB_DEAD = -104.0
VMEM_LIMIT = 56 * 1024 * 1024
INT_MIN = np.int32(-2 ** 31)


def _dot(a, b):
    return jnp.dot(a, b, preferred_element_type=F32)


def _dot_nt(a, b):
    return lax.dot_general(a, b, (((1,), (1,)), ((), ())), preferred_element_type=F32)


def _rms(x, g):
    return x * lax.rsqrt(jnp.mean(x * x, axis=-1, keepdims=True) + EPS) * g


def _resident(shape):
    n = len(shape)
    return pl.BlockSpec(shape, lambda *_: (0,) * n, pipeline_mode=pl.Buffered(1))


def _params(n_grid):
    return pltpu.CompilerParams(dimension_semantics=("arbitrary",) * n_grid,
                                vmem_limit_bytes=VMEM_LIMIT)


def _ada_kernel(c_ref, w_ref, b_ref, o_ref):
    c = c_ref[...]
    a = c * jax.nn.sigmoid(c)
    o_ref[...] = jnp.dot(a, w_ref[...], preferred_element_type=F32,
                         precision=lax.Precision.HIGHEST) + b_ref[...]


def _ada(c, w, b):
    bsz, d = c.shape
    n = w.shape[1]
    tn = 1152 if n % 1152 == 0 else n
    return pl.pallas_call(
        _ada_kernel,
        grid=(n // tn,),
        in_specs=[pl.BlockSpec((bsz, d), lambda j: (0, 0)),
                  pl.BlockSpec((d, tn), lambda j: (0, j)),
                  pl.BlockSpec((1, tn), lambda j: (0, j))],
        out_specs=pl.BlockSpec((bsz, tn), lambda j: (0, j)),
        out_shape=jax.ShapeDtypeStruct((bsz, n), F32),
        compiler_params=_params(1),
        name="ada",
    )(c, w, b.reshape(1, n))


def _ffn_kernel(*refs, sub, mix):
    if mix:
        (h_ref, oa_ref, ob_ref, woa_ref, wob_ref, gpm_ref, mods_ref, gpre_ref, gpost_ref,
         wgu_ref, wd_ref, o_ref, n_scr, acc_scr) = refs
    else:
        (h_ref, mods_ref, gpre_ref, gpost_ref, wgu_ref, wd_ref, o_ref, n_scr, acc_scr) = refs
    x = h_ref[...]
    if mix:
        o = _dot(oa_ref[...], woa_ref[...]) + _dot(ob_ref[...], wob_ref[...])
        x = x + mods_ref[0, 5:6, :] * _rms(o, gpm_ref[...])
    shift = mods_ref[0, 3 * sub:3 * sub + 1, :]
    scale = mods_ref[0, 3 * sub + 1:3 * sub + 2, :]
    gate = mods_ref[0, 3 * sub + 2:3 * sub + 3, :]
    n = _rms(x, gpre_ref[...]) * (1.0 + scale) + shift
    n_scr[...] = n.astype(BF16)
    acc_scr[...] = jnp.zeros_like(acc_scr)
    n_chunks = wgu_ref.shape[0]
    fc = wd_ref.shape[1]

    def chunk(ci, carry):
        r = _dot(n_scr[...], wgu_ref[ci])
        g = r[:, :fc]
        u = r[:, fc:]
        a = (g * jax.nn.sigmoid(g) * u).astype(BF16)
        acc_scr[...] += _dot(a, wd_ref[ci])
        return carry

    lax.fori_loop(0, n_chunks, chunk, 0)
    f = acc_scr[...]
    o_ref[...] = x + 0.5 * gate * _rms(f, gpost_ref[...])


def _ffn(h, mods, g_pre, g_post, wgu, wd, sub, seq, mix=None):
    n_tok, d = h.shape
    tm = min(TOKEN_TILE, seq)
    tiles_per_batch = seq // tm
    row = pl.BlockSpec((tm, d), lambda i: (i, 0))
    vec = pl.BlockSpec((1, d), lambda i: (0, 0))
    mod_spec = pl.BlockSpec((1, mods.shape[1], d), lambda i: (i // tiles_per_batch, 0, 0))
    args, specs = [h], [row]
    if mix is not None:
        oa, ob, woa, wob, g_post_mix = mix
        gw = oa.shape[1]
        half = pl.BlockSpec((tm, gw), lambda i: (i, 0))
        args += [oa, ob, woa, wob, g_post_mix]
        specs += [half, half, _resident(woa.shape), _resident(wob.shape), vec]
    args += [mods, g_pre, g_post, wgu, wd]
    specs += [mod_spec, vec, vec, _resident(wgu.shape), _resident(wd.shape)]
    return pl.pallas_call(
        functools.partial(_ffn_kernel, sub=sub, mix=mix is not None),
        grid=(n_tok // tm,),
        in_specs=specs,
        out_specs=row,
        out_shape=jax.ShapeDtypeStruct((n_tok, d), F32),
        scratch_shapes=[pltpu.VMEM((tm, d), BF16), pltpu.VMEM((tm, d), F32)],
        compiler_params=_params(1),
        name="ffn_mix" if mix is not None else "ffn",
    )(*args)


def _proj_kernel(h_ref, mods_ref, gpre_ref, gk_ref, wt_ref, wka_ref, wki_ref, wwit_ref, wb_ref,
                 qat_ref, ka_ref, vat_ref, qit_ref, ki_ref, wit_ref, qb_ref, kb_ref, vb_ref, n_scr):
    x = h_ref[...]
    shift = mods_ref[0, 3:4, :]
    scale = mods_ref[0, 4:5, :]
    n_scr[...] = (_rms(x, gpre_ref[...]) * (1.0 + scale) + shift).astype(BF16)
    gw = GROUP_WIDTH
    qscale = HEAD_DIM ** -0.5

    def heads(w, out_ref, mul):
        p = _dot(n_scr[...], w)
        if mul != 1.0:
            p = p * mul
        for hd in range(N_HEADS):
            out_ref[0, hd] = p[:, hd * HEAD_DIM:(hd + 1) * HEAD_DIM].astype(BF16)

    def feature_major(col, out_ref, mul):
        p = _dot_nt(wt_ref[col * gw:(col + 1) * gw, :], n_scr[...])
        if mul != 1.0:
            p = p * mul
        out_ref[0] = p.astype(BF16)

    feature_major(0, qat_ref, qscale)
    feature_major(1, vat_ref, 1.0)
    feature_major(2, qit_ref, 1.0)
    heads(wka_ref[...], ka_ref, 1.0)
    heads(wb_ref[:, :gw], qb_ref, qscale)
    heads(wb_ref[:, gw:2 * gw], kb_ref, 1.0)
    heads(wb_ref[:, 2 * gw:], vb_ref, 1.0)
    ki = _dot(n_scr[...], wki_ref[...])
    ki_ref[0] = _rms(ki, gk_ref[...]).astype(BF16)
    wit = _dot_nt(wwit_ref[...], n_scr[...])
    wit_ref[0] = wit[:N_HEADS, :] * ((N_HEADS * IDX_DIM) ** -0.5)


def _proj(h1, mods, g_pre, g_kidx, wt, wka, wki, wwit, wb, bsz, seq):
    n_tok, d = h1.shape
    tm = min(TOKEN_TILE, seq)
    tpb = seq // tm
    gw = GROUP_WIDTH
    row = pl.BlockSpec((tm, d), lambda i: (i, 0))
    vec = pl.BlockSpec((1, d), lambda i: (0, 0))
    mod_spec = pl.BlockSpec((1, mods.shape[1], d), lambda i: (i // tpb, 0, 0))
    hm = pl.BlockSpec((1, N_HEADS, tm, HEAD_DIM), lambda i: (i // tpb, 0, i % tpb, 0))
    hm_shape = jax.ShapeDtypeStruct((bsz, N_HEADS, seq, HEAD_DIM), BF16)
    fm = pl.BlockSpec((1, gw, tm), lambda i: (i // tpb, 0, i % tpb))
    fm_shape = jax.ShapeDtypeStruct((bsz, gw, seq), BF16)
    ki_spec = pl.BlockSpec((1, tm, IDX_DIM), lambda i: (i // tpb, i % tpb, 0))
    wit_spec = pl.BlockSpec((1, N_HEADS, tm), lambda i: (i // tpb, 0, i % tpb))
    return pl.pallas_call(
        _proj_kernel,
        grid=(n_tok // tm,),
        in_specs=[row, mod_spec, vec, pl.BlockSpec((1, IDX_DIM), lambda i: (0, 0)),
                  _resident(wt.shape), _resident(wka.shape), _resident(wki.shape),
                  _resident(wwit.shape), _resident(wb.shape)],
        out_specs=[fm, hm, fm, fm, ki_spec, wit_spec, hm, hm, hm],
        out_shape=[fm_shape, hm_shape, fm_shape, fm_shape,
                   jax.ShapeDtypeStruct((bsz, seq, IDX_DIM), BF16),
                   jax.ShapeDtypeStruct((bsz, N_HEADS, seq), F32),
                   hm_shape, hm_shape, hm_shape],
        scratch_shapes=[pltpu.VMEM((tm, d), BF16)],
        compiler_params=_params(1),
        name="proj",
    )(h1, mods, g_pre, g_kidx, wt, wka, wki, wwit, wb)


def _t5_bucket_np(n):
    n = np.maximum(n, 0)
    max_exact = N_BUCKETS // 2
    nf = np.maximum(n, 1).astype(np.float32)
    large = max_exact + (np.log(nf / np.float32(max_exact)) / np.float32(math.log(MAX_DISTANCE / max_exact))
                         * np.float32(N_BUCKETS - max_exact)).astype(np.int32)
    large = np.minimum(large, N_BUCKETS - 1)
    return np.where(n < max_exact, n, large).astype(np.int32)


def _bias_kernel(relb_ref, bucket_ref, o_ref):
    hd = pl.program_id(0)
    far = relb_ref[N_BUCKETS - 1, hd]
    for r in range(2):
        bk = bucket_ref[r]
        acc = jnp.zeros(bk.shape, F32)
        for k in range(N_BUCKETS - 1):
            acc = jnp.where(bk == k, relb_ref[k, hd] - far, acc)
        o_ref[0, r] = acc


def _bias_tiles(rel_bias, t):
    assert t >= MAX_DISTANCE
    s = np.arange(t)[:, None]
    q = np.arange(t)[None, :]
    buckets = np.stack([_t5_bucket_np(q - s), _t5_bucket_np(t + q - s)]).astype(np.int32)
    return pl.pallas_call(
        _bias_kernel,
        grid=(N_HEADS,),
        in_specs=[pl.BlockSpec(memory_space=pltpu.SMEM),
                  pl.BlockSpec((2, t, t), lambda hd: (0, 0, 0))],
        out_specs=pl.BlockSpec((1, 2, t, t), lambda hd: (hd, 0, 0, 0)),
        out_shape=jax.ShapeDtypeStruct((N_HEADS, 2, t, t), F32),
        compiler_params=_params(1),
        name="bias_tiles",
    )(rel_bias, jnp.asarray(buckets))


def _tree(op, xs):
    xs = list(xs)
    while len(xs) > 1:
        xs = [op(xs[j], xs[j + 1]) if j + 1 < len(xs) else xs[j] for j in range(0, len(xs), 2)]
    return xs[0]


def _row_groups(x):
    return [x[r:r + SUBLANES, :] for r in range(0, x.shape[0], SUBLANES)]


def _all_sublanes(op, x):
    for shift in (4, 2, 1):
        x = op(x, pltpu.roll(x, shift, 0))
    return x


def _dsa_kernel(qit_ref, wit_ref, ki_ref, qat_ref, ka_ref, vat_ref, bias_ref, gt_ref, tri_ref, o_ref,
                key_scr, selm_scr, logit_scr, out_scr, *, topk, n_q_tiles):
    t = DSA_TILE
    kf = float(topk)
    krow = lax.broadcasted_iota(I32, (t, t), 0)
    qcol = lax.broadcasted_iota(I32, (t, t), 1)
    causal = krow <= qcol

    def tile_work(nkb):
        diag = nkb - 1

        wt = wit_ref[0]
        for kb in range(nkb):
            for r0 in range(0, t, SCORE_ROWS):
                kblk = ki_ref[0, kb * t + r0:kb * t + r0 + SCORE_ROWS, :]
                acc = None
                for hd in range(N_HEADS):
                    sc = _dot(kblk, qit_ref[0, hd * IDX_DIM:(hd + 1) * IDX_DIM, :])
                    term = jnp.maximum(sc, 0.0) * wt[hd:hd + 1, :]
                    acc = term if acc is None else acc + term
                acc = acc + 0.0
                if kb == diag:
                    acc = jnp.where(causal[r0:r0 + SCORE_ROWS, :], acc, -jnp.inf)
                bits = pltpu.bitcast(acc, I32)
                key_scr[kb, r0:r0 + SCORE_ROWS, :] = bits ^ ((bits >> 31) & jnp.int32(0x7FFFFFFF))

        def count_ge(cand):
            accs = [jnp.zeros((SUBLANES, t), F32) for _ in range(4)]
            n = 0
            for kb in range(nkb):
                for grp in _row_groups(key_scr[kb]):
                    accs[n % 4] = accs[n % 4] + jnp.where(grp >= cand, 1.0, 0.0)
                    n += 1
            return _all_sublanes(jnp.add, _tree(jnp.add, accs))

        def search_step(j, ans):
            cand_off = ans | lax.shift_left(jnp.int32(1), 31 - j)
            cnt = count_ge(cand_off ^ INT_MIN)
            return jnp.where(cnt >= kf, cand_off, ans)

        thr = lax.fori_loop(0, 32, search_step, jnp.zeros((SUBLANES, t), I32)) ^ INT_MIN

        n_ge = jnp.zeros((SUBLANES, t), F32)
        for kb in range(nkb):
            blk = key_scr[kb]
            for r, grp in enumerate(_row_groups(blk)):
                rows = slice(r * SUBLANES, (r + 1) * SUBLANES)
                sel = grp >= thr
                n_ge = n_ge + jnp.where(sel, 1.0, 0.0)
                m = jnp.where(sel, 0.0, NEG_BIG)
                if kb == diag:
                    m = jnp.where(causal[rows, :], m, NEG_BIG)
                selm_scr[kb, rows, :] = m
        n_ge = _all_sublanes(jnp.add, n_ge)
        excess = jnp.max(jnp.where(n_ge > kf, 1.0, 0.0))

        @pl.when(excess > 0.0)
        def _():
            thr1 = thr[0:1, :]

            def tied(kb):
                return jnp.where(key_scr[kb] == thr1, 1.0, 0.0)

            def count_tied(kb, acc):
                return acc + jnp.sum(tied(kb), axis=0, keepdims=True)

            n_tied = lax.fori_loop(0, nkb, count_tied, jnp.zeros((1, t), F32))
            need = kf - (n_ge[0:1, :] - n_tied)

            def rewrite(kb, before):
                e = tied(kb)
                rank = _dot(tri_ref[...], e.astype(BF16)) + before
                k = key_scr[kb]
                keep = jnp.where(k > thr1, 0.0,
                                 jnp.where(k == thr1, jnp.where(rank < need, 0.0, NEG_BIG), NEG_BIG))
                selm_scr[kb] = jnp.where(selm_scr[kb] < 0.0, NEG_BIG, keep)
                return before + jnp.sum(e, axis=0, keepdims=True)

            lax.fori_loop(0, nkb, rewrite, jnp.zeros((1, t), F32))

        def head(hd, c):
            hrows = pl.ds(pl.multiple_of(hd * HEAD_DIM, HEAD_DIM), HEAD_DIM)
            qt = qat_ref[0, hrows, :]
            maxes = []
            for kb in range(nkb):
                lg = _dot(ka_ref[0, hd, kb * t:(kb + 1) * t, :], qt) + selm_scr[kb]
                if kb >= nkb - 2:
                    lg = lg + bias_ref[hd, diag - kb]
                logit_scr[kb] = lg
                maxes.append(jnp.max(lg, axis=0, keepdims=True))
            m = _tree(jnp.maximum, maxes)
            ssum = None
            acc = None
            for kb in range(nkb):
                p = jnp.exp(logit_scr[kb] - m)
                ps = jnp.sum(p, axis=0, keepdims=True)
                pv = _dot(vat_ref[0, hrows, kb * t:(kb + 1) * t], p.astype(BF16))
                ssum = ps if ssum is None else ssum + ps
                acc = pv if acc is None else acc + pv
            out_scr[hrows, :] = acc * (1.0 / ssum)
            return c

        lax.fori_loop(0, N_HEADS, head, 0)

    i = pl.program_id(1)
    for k in range(n_q_tiles):
        pl.when(i == k)(functools.partial(tile_work, k + 1))

    o = out_scr[...]
    o = o * lax.rsqrt(jnp.mean(o * o, axis=0, keepdims=True) + EPS) * gt_ref[...]
    o_ref[0] = o.T.astype(BF16)


def _dsa(qit, wit, ki, qat, ka, vat, bias, g_out, bsz, seq):
    t = DSA_TILE
    nq = seq // t
    gw = GROUP_WIDTH
    topk = min(TOPK_MAX, seq // 4)
    q_fm = pl.BlockSpec((1, gw, t), lambda b, i: (b, 0, i))
    gt = jnp.broadcast_to(g_out.reshape(gw, 1), (gw, t))
    r = np.arange(t)
    lower = jnp.asarray(r[None, :] < r[:, None], dtype=BF16)
    return pl.pallas_call(
        functools.partial(_dsa_kernel, topk=topk, n_q_tiles=nq),
        grid=(bsz, nq),
        in_specs=[q_fm,
                  pl.BlockSpec((1, N_HEADS, t), lambda b, i: (b, 0, i)),
                  pl.BlockSpec((1, seq, IDX_DIM), lambda b, i: (b, 0, 0)),
                  q_fm,
                  pl.BlockSpec((1, N_HEADS, seq, HEAD_DIM), lambda b, i: (b, 0, 0, 0)),
                  pl.BlockSpec((1, gw, seq), lambda b, i: (b, 0, 0)),
                  _resident(bias.shape),
                  _resident((gw, t)),
                  _resident((t, t))],
        out_specs=pl.BlockSpec((1, t, gw), lambda b, i: (b, i, 0)),
        out_shape=jax.ShapeDtypeStruct((bsz, seq, gw), BF16),
        scratch_shapes=[pltpu.VMEM((nq, t, t), I32), pltpu.VMEM((nq, t, t), F32),
                        pltpu.VMEM((nq, t, t), F32), pltpu.VMEM((gw, t), F32)],
        compiler_params=_params(2),
        name="dsa",
    )(qit, wit, ki, qat, ka, vat, bias, gt, lower)


def _log1m_sigmoid(z):
    return -(jnp.maximum(z, 0.0) + jnp.log(1.0 + jnp.exp(-jnp.abs(z))))


def _split_bf16(x):
    hi = x.astype(BF16)
    lo = (x - hi.astype(F32)).astype(BF16)
    return jnp.concatenate([hi, lo], axis=1)


def _sb_kernel(q_ref, k_ref, v_ref, tri_ref, g_ref, o_ref, z_scr, lh_scr, cs_scr, r_scr, acc_scr):
    t = SB_TILE
    nw = SB_WINDOW
    i = pl.program_id(1)
    wb = jnp.maximum(i - (nw - 1), 0)
    wrows = pl.ds(pl.multiple_of(wb * t, t), nw * t)
    row = lax.broadcasted_iota(I32, (t, t), 0)
    col = lax.broadcasted_iota(I32, (t, t), 1)

    def causal(w):
        return (col - row) < (i - wb - w) * t

    for hd in range(N_HEADS):
        z = _dot_nt(q_ref[0, hd], k_ref[0, hd, wrows, :])
        z_scr[hd] = z
        for w in range(nw):
            lg = jnp.where(causal(w), _log1m_sigmoid(z[:, w * t:(w + 1) * t]), 0.0)
            lh_scr[(hd * nw + w) * t:(hd * nw + w + 1) * t, :] = _split_bf16(lg)
    cs_scr[...] = _dot(lh_scr[...], tri_ref[...])
    rmax = None
    for hd in range(N_HEADS):
        after = None
        a_blocks = [None] * nw
        for w in reversed(range(nw)):
            cs = cs_scr[(hd * nw + w) * t:(hd * nw + w + 1) * t, :]
            c = cs[:, :t] if after is None else cs[:, :t] + after
            a = jnp.where(causal(w), jnp.exp(z_scr[hd, :, w * t:(w + 1) * t] + c), 0.0)
            a_blocks[w] = a.astype(BF16)
            after = cs[:, t:] if after is None else after + cs[:, t:]
        acc_scr[hd] = _dot(jnp.concatenate(a_blocks, axis=1), v_ref[0, hd, wrows, :])
        r_scr[hd] = after
        rmax = after if rmax is None else jnp.maximum(rmax, after)

    def far_block(kb):
        rows = pl.ds(pl.multiple_of(kb * t, t), t)
        rmax = None
        for hd in range(N_HEADS):
            z = _dot_nt(q_ref[0, hd], k_ref[0, hd, rows, :])
            cs = _dot(_split_bf16(_log1m_sigmoid(z)), tri_ref[...])
            r_old = r_scr[hd]
            a = jnp.exp(z + cs[:, :t] + r_old)
            acc_scr[hd] += _dot(a.astype(BF16), v_ref[0, hd, rows, :])
            r_new = r_old + cs[:, t:]
            r_scr[hd] = r_new
            rmax = r_new if rmax is None else jnp.maximum(rmax, r_new)
        return jnp.max(rmax)

    def cond(c):
        kb, rmax = c
        return jnp.logical_and(kb >= 0, rmax > SB_DEAD)

    def body(c):
        kb, _ = c
        return kb - 1, far_block(kb)

    lax.while_loop(cond, body, (wb - 1, jnp.max(rmax)))
    o = jnp.concatenate([acc_scr[hd] for hd in range(N_HEADS)], axis=1)
    o_ref[0] = _rms(o, g_ref[...]).astype(BF16)


def _sb(q, k, v, g_out, bsz, seq):
    t = SB_TILE
    nq = seq // t
    assert seq >= SB_WINDOW * t
    j = np.arange(t)[:, None]
    s = np.arange(t)[None, :]
    tri = np.concatenate([(j >= s), np.ones((t, t), bool)], axis=1).astype(np.float32)
    tri = np.concatenate([tri, tri], axis=0)
    q_hm = pl.BlockSpec((1, N_HEADS, t, HEAD_DIM), lambda b, i: (b, 0, i, 0))
    kv_hm = pl.BlockSpec((1, N_HEADS, seq, HEAD_DIM), lambda b, i: (b, 0, 0, 0))
    n_rows = N_HEADS * SB_WINDOW * t
    return pl.pallas_call(
        _sb_kernel,
        grid=(bsz, nq),
        in_specs=[q_hm, kv_hm, kv_hm, _resident((2 * t, 2 * t)),
                  pl.BlockSpec((1, GROUP_WIDTH), lambda b, i: (0, 0))],
        out_specs=pl.BlockSpec((1, t, GROUP_WIDTH), lambda b, i: (b, i, 0)),
        out_shape=jax.ShapeDtypeStruct((bsz, seq, GROUP_WIDTH), BF16),
        scratch_shapes=[pltpu.VMEM((N_HEADS, t, SB_WINDOW * t), F32),
                        pltpu.VMEM((n_rows, 2 * t), BF16),
                        pltpu.VMEM((n_rows, 2 * t), F32),
                        pltpu.VMEM((N_HEADS, t, t), F32),
                        pltpu.VMEM((N_HEADS, t, HEAD_DIM), F32)],
        compiler_params=_params(2),
        name="sb",
    )(q, k, v, jnp.asarray(tri, dtype=BF16), g_out)


def _pack_ffn(w_gate, w_up, w_down):
    d, f = w_gate.shape
    nc = f // FFN_CHUNK
    wg = w_gate.reshape(d, nc, FFN_CHUNK)
    wu = w_up.reshape(d, nc, FFN_CHUNK)
    wgu = jnp.concatenate([wg, wu], axis=2).transpose(1, 0, 2).astype(BF16)
    wd = w_down.reshape(nc, FFN_CHUNK, d).astype(BF16)
    return wgu, wd


def kernel(x, c, w_ada, b_ada, g_pre, g_post, w_ffn_gate, w_ffn_up, w_ffn_down, w_in, g_kidx,
           rel_bias, g_out_a, g_out_b, w_out):
    bsz, seq, d = x.shape
    depth = w_ada.shape[0]
    gw = GROUP_WIDTH
    h = x.reshape(bsz * seq, d)
    bias = _bias_tiles(rel_bias, DSA_TILE)
    for l in range(depth):
        mods = _ada(c, w_ada[l], b_ada[l]).reshape(bsz, -1, d)
        wgu1, wd1 = _pack_ffn(w_ffn_gate[l, 0], w_ffn_up[l, 0], w_ffn_down[l, 0])
        wgu2, wd2 = _pack_ffn(w_ffn_gate[l, 1], w_ffn_up[l, 1], w_ffn_down[l, 1])
        w = w_in[l]
        o_ki = 4 * gw
        o_wi = o_ki + IDX_DIM
        o_b = o_wi + N_HEADS
        wt = jnp.concatenate([w[:, :gw], w[:, 2 * gw:3 * gw], w[:, 3 * gw:o_ki]], axis=1).T.astype(BF16)
        wka = w[:, gw:2 * gw].astype(BF16)
        wki = w[:, o_ki:o_wi].astype(BF16)
        wwit = jnp.pad(w[:, o_wi:o_b].T, ((0, 16 - N_HEADS), (0, 0))).astype(BF16)
        wb = w[:, o_b:].astype(BF16)
        woa = w_out[l, :gw].astype(BF16)
        wob = w_out[l, gw:].astype(BF16)
        vec = lambda g: g.reshape(1, -1)

        h1 = _ffn(h, mods, vec(g_pre[l, 0]), vec(g_post[l, 0]), wgu1, wd1, 0, seq)
        qat, ka, vat, qit, ki, wit, qb, kb, vb = _proj(h1, mods, vec(g_pre[l, 1]), vec(g_kidx[l]),
                                                       wt, wka, wki, wwit, wb, bsz, seq)
        oa = _dsa(qit, wit, ki, qat, ka, vat, bias, g_out_a[l], bsz, seq)
        ob = _sb(qb, kb, vb, vec(g_out_b[l]), bsz, seq)
        h = _ffn(h1, mods, vec(g_pre[l, 2]), vec(g_post[l, 2]), wgu2, wd2, 2, seq,
                 mix=(oa.reshape(bsz * seq, gw), ob.reshape(bsz * seq, gw), woa, wob,
                      vec(g_post[l, 1])))
    return h.reshape(bsz, seq, d)
```

```python
import functools
import math

import numpy as np
import jax
import jax.numpy as jnp
from jax import lax
from jax.experimental import pallas as pl
from jax.experimental.pallas import tpu as pltpu

F32 = jnp.float32
BF16 = jnp.bfloat16
I32 = jnp.int32

EPS = 1e-6
HEAD_DIM = 64
N_HEADS = 8
GROUP_WIDTH = N_HEADS * HEAD_DIM
IDX_DIM = 64
TOPK_MAX = 256
N_BUCKETS = 32
MAX_DISTANCE = 128
SUBLANES = 8
FFN_CHUNK = 256
TOKEN_TILE = 512
DSA_TILE = 256
SCORE_ROWS = 64
SB_TILE = 128
SB_WINDOW = 3
NEG_BIG = -1e30
SB_DEAD = -104.0
VMEM_LIMIT = 56 * 1024 * 1024
INT_MIN = np.int32(-2 ** 31)


def _dot(a, b):
    return jnp.dot(a, b, preferred_element_type=F32)


def _dot_nt(a, b):
    return lax.dot_general(a, b, (((1,), (1,)), ((), ())), preferred_element_type=F32)


def _rms(x, g):
    return x * lax.rsqrt(jnp.mean(x * x, axis=-1, keepdims=True) + EPS) * g


def _resident(shape):
    n = len(shape)
    return pl.BlockSpec(shape, lambda *_: (0,) * n, pipeline_mode=pl.Buffered(1))


def _params(n_grid):
    return pltpu.CompilerParams(dimension_semantics=("arbitrary",) * n_grid,
                                vmem_limit_bytes=VMEM_LIMIT)


def _ada_kernel(c_ref, w_ref, b_ref, o_ref):
    c = c_ref[...]
    a = c * jax.nn.sigmoid(c)
    w = w_ref[...]
    a_hi = a.astype(BF16)
    a_lo = (a - a_hi.astype(F32)).astype(BF16)
    w_hi = w.astype(BF16)
    w_lo = (w - w_hi.astype(F32)).astype(BF16)
    o_ref[...] = _dot(a_hi, w_hi) + (_dot(a_hi, w_lo) + _dot(a_lo, w_hi)) + b_ref[...]


def _ada(c, w, b):
    bsz, d = c.shape
    n = w.shape[1]
    tn = 1152 if n % 1152 == 0 else n
    return pl.pallas_call(
        _ada_kernel,
        grid=(n // tn,),
        in_specs=[pl.BlockSpec((bsz, d), lambda j: (0, 0)),
                  pl.BlockSpec((d, tn), lambda j: (0, j)),
                  pl.BlockSpec((1, tn), lambda j: (0, j))],
        out_specs=pl.BlockSpec((bsz, tn), lambda j: (0, j)),
        out_shape=jax.ShapeDtypeStruct((bsz, n), F32),
        compiler_params=_params(1),
        name="ada",
    )(c, w, b.reshape(1, n))


def _ffn_kernel(*refs, sub, mix):
    if mix:
        (h_ref, oa_ref, ob_ref, woa_ref, wob_ref, gpm_ref, mods_ref, gpre_ref, gpost_ref,
         wgu_ref, wd_ref, o_ref, n_scr, a_scr) = refs
    else:
        (h_ref, mods_ref, gpre_ref, gpost_ref, wgu_ref, wd_ref, o_ref, n_scr, a_scr) = refs
    x = h_ref[...]
    if mix:
        o = _dot(oa_ref[...], woa_ref[...]) + _dot(ob_ref[...], wob_ref[...])
        x = x + mods_ref[0, 5:6, :] * _rms(o, gpm_ref[...])
    shift = mods_ref[0, 3 * sub:3 * sub + 1, :]
    scale = mods_ref[0, 3 * sub + 1:3 * sub + 2, :]
    gate = mods_ref[0, 3 * sub + 2:3 * sub + 3, :]
    n = _rms(x, gpre_ref[...]) * (1.0 + scale) + shift
    n_scr[...] = n.astype(BF16)
    n_chunks = wgu_ref.shape[0]
    fc = wgu_ref.shape[2] // 2
    for ci in range(n_chunks):
        r = _dot(n_scr[...], wgu_ref[ci])
        g = r[:, :fc]
        u = r[:, fc:]
        a_scr[:, ci * fc:(ci + 1) * fc] = (g * jax.nn.sigmoid(g) * u).astype(BF16)
    f = _dot(a_scr[...], wd_ref[...])
    o_ref[...] = x + 0.5 * gate * _rms(f, gpost_ref[...])


def _ffn(h, mods, g_pre, g_post, wgu, wd, sub, seq, mix=None):
    n_tok, d = h.shape
    tm = min(TOKEN_TILE, seq)
    tiles_per_batch = seq // tm
    row = pl.BlockSpec((tm, d), lambda i: (i, 0))
    vec = pl.BlockSpec((1, d), lambda i: (0, 0))
    mod_spec = pl.BlockSpec((1, mods.shape[1], d), lambda i: (i // tiles_per_batch, 0, 0))
    args, specs = [h], [row]
    if mix is not None:
        oa, ob, woa, wob, g_post_mix = mix
        gw = oa.shape[1]
        half = pl.BlockSpec((tm, gw), lambda i: (i, 0))
        args += [oa, ob, woa, wob, g_post_mix]
        specs += [half, half, _resident(woa.shape), _resident(wob.shape), vec]
    args += [mods, g_pre, g_post, wgu, wd]
    specs += [mod_spec, vec, vec, _resident(wgu.shape), _resident(wd.shape)]
    return pl.pallas_call(
        functools.partial(_ffn_kernel, sub=sub, mix=mix is not None),
        grid=(n_tok // tm,),
        in_specs=specs,
        out_specs=row,
        out_shape=jax.ShapeDtypeStruct((n_tok, d), F32),
        scratch_shapes=[pltpu.VMEM((tm, d), BF16), pltpu.VMEM((tm, wd.shape[0]), BF16)],
        compiler_params=_params(1),
        name="ffn_mix" if mix is not None else "ffn",
    )(*args)


def _proj_kernel(h_ref, mods_ref, gpre_ref, gk_ref, wt_ref, wka_ref, wki_ref, wwit_ref, wb_ref,
                 qat_ref, ka_ref, vat_ref, qit_ref, ki_ref, wit_ref, qb_ref, kb_ref, vb_ref, n_scr):
    x = h_ref[...]
    shift = mods_ref[0, 3:4, :]
    scale = mods_ref[0, 4:5, :]
    n_scr[...] = (_rms(x, gpre_ref[...]) * (1.0 + scale) + shift).astype(BF16)
    gw = GROUP_WIDTH
    qscale = HEAD_DIM ** -0.5

    def heads(w, out_ref, mul):
        p = _dot(n_scr[...], w)
        if mul != 1.0:
            p = p * mul
        for hd in range(N_HEADS):
            out_ref[0, hd] = p[:, hd * HEAD_DIM:(hd + 1) * HEAD_DIM].astype(BF16)

    def feature_major(col, out_ref, mul):
        p = _dot_nt(wt_ref[col * gw:(col + 1) * gw, :], n_scr[...])
        if mul != 1.0:
            p = p * mul
        out_ref[0] = p.astype(BF16)

    feature_major(0, qat_ref, qscale)
    feature_major(1, vat_ref, 1.0)
    feature_major(2, qit_ref, 1.0)
    heads(wka_ref[...], ka_ref, 1.0)
    heads(wb_ref[:, :gw], qb_ref, qscale)
    heads(wb_ref[:, gw:2 * gw], kb_ref, 1.0)
    heads(wb_ref[:, 2 * gw:], vb_ref, 1.0)
    ki = _dot(n_scr[...], wki_ref[...])
    ki_ref[0] = _rms(ki, gk_ref[...]).astype(BF16)
    wit = _dot_nt(wwit_ref[...], n_scr[...])
    wit_ref[0] = wit[:N_HEADS, :] * ((N_HEADS * IDX_DIM) ** -0.5)


def _proj(h1, mods, g_pre, g_kidx, wt, wka, wki, wwit, wb, bsz, seq):
    n_tok, d = h1.shape
    tm = min(TOKEN_TILE, seq)
    tpb = seq // tm
    gw = GROUP_WIDTH
    row = pl.BlockSpec((tm, d), lambda i: (i, 0))
    vec = pl.BlockSpec((1, d), lambda i: (0, 0))
    mod_spec = pl.BlockSpec((1, mods.shape[1], d), lambda i: (i // tpb, 0, 0))
    hm = pl.BlockSpec((1, N_HEADS, tm, HEAD_DIM), lambda i: (i // tpb, 0, i % tpb, 0))
    hm_shape = jax.ShapeDtypeStruct((bsz, N_HEADS, seq, HEAD_DIM), BF16)
    fm = pl.BlockSpec((1, gw, tm), lambda i: (i // tpb, 0, i % tpb))
    fm_shape = jax.ShapeDtypeStruct((bsz, gw, seq), BF16)
    ki_spec = pl.BlockSpec((1, tm, IDX_DIM), lambda i: (i // tpb, i % tpb, 0))
    wit_spec = pl.BlockSpec((1, N_HEADS, tm), lambda i: (i // tpb, 0, i % tpb))
    return pl.pallas_call(
        _proj_kernel,
        grid=(n_tok // tm,),
        in_specs=[row, mod_spec, vec, pl.BlockSpec((1, IDX_DIM), lambda i: (0, 0)),
                  _resident(wt.shape), _resident(wka.shape), _resident(wki.shape),
                  _resident(wwit.shape), _resident(wb.shape)],
        out_specs=[fm, hm, fm, fm, ki_spec, wit_spec, hm, hm, hm],
        out_shape=[fm_shape, hm_shape, fm_shape, fm_shape,
                   jax.ShapeDtypeStruct((bsz, seq, IDX_DIM), BF16),
                   jax.ShapeDtypeStruct((bsz, N_HEADS, seq), F32),
                   hm_shape, hm_shape, hm_shape],
        scratch_shapes=[pltpu.VMEM((tm, d), BF16)],
        compiler_params=_params(1),
        name="proj",
    )(h1, mods, g_pre, g_kidx, wt, wka, wki, wwit, wb)


def _t5_bucket_np(n):
    n = np.maximum(n, 0)
    max_exact = N_BUCKETS // 2
    nf = np.maximum(n, 1).astype(np.float32)
    large = max_exact + (np.log(nf / np.float32(max_exact)) / np.float32(math.log(MAX_DISTANCE / max_exact))
                         * np.float32(N_BUCKETS - max_exact)).astype(np.int32)
    large = np.minimum(large, N_BUCKETS - 1)
    return np.where(n < max_exact, n, large).astype(np.int32)


def _bias_kernel(relb_ref, bucket_ref, o_ref):
    hd = pl.program_id(0)
    far = relb_ref[N_BUCKETS - 1, hd]
    for r in range(2):
        bk = bucket_ref[r]
        acc = jnp.zeros(bk.shape, F32)
        for k in range(N_BUCKETS - 1):
            acc = jnp.where(bk == k, relb_ref[k, hd] - far, acc)
        o_ref[0, r] = acc


def _bias_tiles(rel_bias, t):
    assert t >= MAX_DISTANCE
    s = np.arange(t)[:, None]
    q = np.arange(t)[None, :]
    buckets = np.stack([_t5_bucket_np(q - s), _t5_bucket_np(t + q - s)]).astype(np.int32)
    return pl.pallas_call(
        _bias_kernel,
        grid=(N_HEADS,),
        in_specs=[pl.BlockSpec(memory_space=pltpu.SMEM),
                  pl.BlockSpec((2, t, t), lambda hd: (0, 0, 0))],
        out_specs=pl.BlockSpec((1, 2, t, t), lambda hd: (hd, 0, 0, 0)),
        out_shape=jax.ShapeDtypeStruct((N_HEADS, 2, t, t), F32),
        compiler_params=_params(1),
        name="bias_tiles",
    )(rel_bias, jnp.asarray(buckets))


def _tree(op, xs):
    xs = list(xs)
    while len(xs) > 1:
        xs = [op(xs[j], xs[j + 1]) if j + 1 < len(xs) else xs[j] for j in range(0, len(xs), 2)]
    return xs[0]


def _row_groups(x):
    return [x[r:r + SUBLANES, :] for r in range(0, x.shape[0], SUBLANES)]


def _all_sublanes(op, x):
    for shift in (4, 2, 1):
        x = op(x, pltpu.roll(x, shift, 0))
    return x


def _dsa_kernel(qit_ref, wit_ref, ki_ref, qat_ref, ka_ref, vat_ref, bias_ref, gt_ref, tri_ref, o_ref,
                key_scr, selm_scr, logit_scr, out_scr, *, topk, n_q_tiles):
    t = DSA_TILE
    kf = float(topk)
    krow = lax.broadcasted_iota(I32, (t, t), 0)
    qcol = lax.broadcasted_iota(I32, (t, t), 1)
    causal = krow <= qcol

    def tile_work(nkb):
        diag = nkb - 1

        wt = wit_ref[0]
        for kb in range(nkb):
            for r0 in range(0, t, SCORE_ROWS):
                kblk = ki_ref[0, kb * t + r0:kb * t + r0 + SCORE_ROWS, :]
                acc = None
                for hd in range(N_HEADS):
                    sc = _dot(kblk, qit_ref[0, hd * IDX_DIM:(hd + 1) * IDX_DIM, :])
                    term = jnp.maximum(sc, 0.0) * wt[hd:hd + 1, :]
                    acc = term if acc is None else acc + term
                acc = acc + 0.0
                if kb == diag:
                    acc = jnp.where(causal[r0:r0 + SCORE_ROWS, :], acc, -jnp.inf)
                bits = pltpu.bitcast(acc, I32)
                key_scr[kb, r0:r0 + SCORE_ROWS, :] = bits ^ ((bits >> 31) & jnp.int32(0x7FFFFFFF))

        def count_ge(cand):
            accs = [jnp.zeros((SUBLANES, t), F32) for _ in range(4)]
            n = 0
            for kb in range(nkb):
                for grp in _row_groups(key_scr[kb]):
                    accs[n % 4] = accs[n % 4] + jnp.where(grp >= cand, 1.0, 0.0)
                    n += 1
            return _all_sublanes(jnp.add, _tree(jnp.add, accs))

        def search_step(j, ans):
            cand_off = ans | lax.shift_left(jnp.int32(1), 31 - j)
            cnt = count_ge(cand_off ^ INT_MIN)
            return jnp.where(cnt >= kf, cand_off, ans)

        thr = lax.fori_loop(0, 32, search_step, jnp.zeros((SUBLANES, t), I32)) ^ INT_MIN

        n_ge = jnp.zeros((SUBLANES, t), F32)
        for kb in range(nkb):
            blk = key_scr[kb]
            for r, grp in enumerate(_row_groups(blk)):
                rows = slice(r * SUBLANES, (r + 1) * SUBLANES)
                sel = grp >= thr
                n_ge = n_ge + jnp.where(sel, 1.0, 0.0)
                m = jnp.where(sel, 0.0, NEG_BIG)
                if kb == diag:
                    m = jnp.where(causal[rows, :], m, NEG_BIG)
                selm_scr[kb, rows, :] = m
        n_ge = _all_sublanes(jnp.add, n_ge)
        excess = jnp.max(jnp.where(n_ge > kf, 1.0, 0.0))

        @pl.when(excess > 0.0)
        def _():
            thr1 = thr[0:1, :]

            def tied(kb):
                return jnp.where(key_scr[kb] == thr1, 1.0, 0.0)

            def count_tied(kb, acc):
                return acc + jnp.sum(tied(kb), axis=0, keepdims=True)

            n_tied = lax.fori_loop(0, nkb, count_tied, jnp.zeros((1, t), F32))
            need = kf - (n_ge[0:1, :] - n_tied)

            def rewrite(kb, before):
                e = tied(kb)
                rank = _dot(tri_ref[...], e.astype(BF16)) + before
                k = key_scr[kb]
                keep = jnp.where(k > thr1, 0.0,
                                 jnp.where(k == thr1, jnp.where(rank < need, 0.0, NEG_BIG), NEG_BIG))
                selm_scr[kb] = jnp.where(selm_scr[kb] < 0.0, NEG_BIG, keep)
                return before + jnp.sum(e, axis=0, keepdims=True)

            lax.fori_loop(0, nkb, rewrite, jnp.zeros((1, t), F32))

        def head(hd, c):
            hrows = pl.ds(pl.multiple_of(hd * HEAD_DIM, HEAD_DIM), HEAD_DIM)
            qt = qat_ref[0, hrows, :]
            maxes = []
            for kb in range(nkb):
                lg = _dot(ka_ref[0, hd, kb * t:(kb + 1) * t, :], qt) + selm_scr[kb]
                if kb >= nkb - 2:
                    lg = lg + bias_ref[hd, diag - kb]
                logit_scr[kb] = lg
                maxes.append(jnp.max(lg, axis=0, keepdims=True))
            m = _tree(jnp.maximum, maxes)
            ssum = None
            acc = None
            for kb in range(nkb):
                p = jnp.exp(logit_scr[kb] - m)
                ps = jnp.sum(p, axis=0, keepdims=True)
                pv = _dot(vat_ref[0, hrows, kb * t:(kb + 1) * t], p.astype(BF16))
                ssum = ps if ssum is None else ssum + ps
                acc = pv if acc is None else acc + pv
            out_scr[hrows, :] = acc * (1.0 / ssum)
            return c

        lax.fori_loop(0, N_HEADS, head, 0, unroll=2)

    i = pl.program_id(1)
    for k in range(n_q_tiles):
        pl.when(i == k)(functools.partial(tile_work, k + 1))

    o = out_scr[...]
    o = o * lax.rsqrt(jnp.mean(o * o, axis=0, keepdims=True) + EPS) * gt_ref[...]
    o_ref[0] = o.T.astype(BF16)


def _dsa(qit, wit, ki, qat, ka, vat, bias, g_out, bsz, seq):
    t = DSA_TILE
    nq = seq // t
    gw = GROUP_WIDTH
    topk = min(TOPK_MAX, seq // 4)
    q_fm = pl.BlockSpec((1, gw, t), lambda b, i: (b, 0, i))
    gt = jnp.broadcast_to(g_out.reshape(gw, 1), (gw, t))
    r = np.arange(t)
    lower = jnp.asarray(r[None, :] < r[:, None], dtype=BF16)
    return pl.pallas_call(
        functools.partial(_dsa_kernel, topk=topk, n_q_tiles=nq),
        grid=(bsz, nq),
        in_specs=[q_fm,
                  pl.BlockSpec((1, N_HEADS, t), lambda b, i: (b, 0, i)),
                  pl.BlockSpec((1, seq, IDX_DIM), lambda b, i: (b, 0, 0)),
                  q_fm,
                  pl.BlockSpec((1, N_HEADS, seq, HEAD_DIM), lambda b, i: (b, 0, 0, 0)),
                  pl.BlockSpec((1, gw, seq), lambda b, i: (b, 0, 0)),
                  _resident(bias.shape),
                  _resident((gw, t)),
                  _resident((t, t))],
        out_specs=pl.BlockSpec((1, t, gw), lambda b, i: (b, i, 0)),
        out_shape=jax.ShapeDtypeStruct((bsz, seq, gw), BF16),
        scratch_shapes=[pltpu.VMEM((nq, t, t), I32), pltpu.VMEM((nq, t, t), F32),
                        pltpu.VMEM((nq, t, t), F32), pltpu.VMEM((gw, t), F32)],
        compiler_params=_params(2),
        name="dsa",
    )(qit, wit, ki, qat, ka, vat, bias, gt, lower)


def _log1m_sigmoid(z):
    return -(jnp.maximum(z, 0.0) + jnp.log(1.0 + jnp.exp(-jnp.abs(z))))


def _split_bf16(x):
    hi = x.astype(BF16)
    lo = (x - hi.astype(F32)).astype(BF16)
    return jnp.concatenate([hi, lo], axis=1)


def _sb_kernel(q_ref, k_ref, v_ref, tri_ref, g_ref, o_ref, z_scr, lh_scr, cs_scr, r_scr, acc_scr):
    t = SB_TILE
    nw = SB_WINDOW
    i = pl.program_id(1)
    wb = jnp.maximum(i - (nw - 1), 0)
    wrows = pl.ds(pl.multiple_of(wb * t, t), nw * t)
    row = lax.broadcasted_iota(I32, (t, t), 0)
    col = lax.broadcasted_iota(I32, (t, t), 1)

    def causal(w):
        return (col - row) < (i - wb - w) * t

    for hd in range(N_HEADS):
        z = _dot_nt(q_ref[0, hd], k_ref[0, hd, wrows, :])
        z_scr[hd] = z
        for w in range(nw):
            lg = jnp.where(causal(w), _log1m_sigmoid(z[:, w * t:(w + 1) * t]), 0.0)
            lh_scr[(hd * nw + w) * t:(hd * nw + w + 1) * t, :] = _split_bf16(lg)
    cs_scr[...] = _dot(lh_scr[...], tri_ref[...])
    rmax = None
    for hd in range(N_HEADS):
        after = None
        a_blocks = [None] * nw
        for w in reversed(range(nw)):
            cs = cs_scr[(hd * nw + w) * t:(hd * nw + w + 1) * t, :]
            c = cs[:, :t] if after is None else cs[:, :t] + after
            a = jnp.where(causal(w), jnp.exp(z_scr[hd, :, w * t:(w + 1) * t] + c), 0.0)
            a_blocks[w] = a.astype(BF16)
            after = cs[:, t:] if after is None else after + cs[:, t:]
        acc_scr[hd] = _dot(jnp.concatenate(a_blocks, axis=1), v_ref[0, hd, wrows, :])
        r_scr[hd] = after
        rmax = after if rmax is None else jnp.maximum(rmax, after)

    def far_block(kb):
        rows = pl.ds(pl.multiple_of(kb * t, t), t)
        rmax = None
        for hd in range(N_HEADS):
            z = _dot_nt(q_ref[0, hd], k_ref[0, hd, rows, :])
            cs = _dot(_split_bf16(_log1m_sigmoid(z)), tri_ref[...])
            r_old = r_scr[hd]
            a = jnp.exp(z + cs[:, :t] + r_old)
            acc_scr[hd] += _dot(a.astype(BF16), v_ref[0, hd, rows, :])
            r_new = r_old + cs[:, t:]
            r_scr[hd] = r_new
            rmax = r_new if rmax is None else jnp.maximum(rmax, r_new)
        return jnp.max(rmax)

    def cond(c):
        kb, rmax = c
        return jnp.logical_and(kb >= 0, rmax > SB_DEAD)

    def body(c):
        kb, _ = c
        return kb - 1, far_block(kb)

    lax.while_loop(cond, body, (wb - 1, jnp.max(rmax)))
    o = jnp.concatenate([acc_scr[hd] for hd in range(N_HEADS)], axis=1)
    o_ref[0] = _rms(o, g_ref[...]).astype(BF16)


def _sb(q, k, v, g_out, bsz, seq):
    t = SB_TILE
    nq = seq // t
    assert seq >= SB_WINDOW * t
    j = np.arange(t)[:, None]
    s = np.arange(t)[None, :]
    tri = np.concatenate([(j >= s), np.ones((t, t), bool)], axis=1).astype(np.float32)
    tri = np.concatenate([tri, tri], axis=0)
    q_hm = pl.BlockSpec((1, N_HEADS, t, HEAD_DIM), lambda b, i: (b, 0, i, 0))
    kv_hm = pl.BlockSpec((1, N_HEADS, seq, HEAD_DIM), lambda b, i: (b, 0, 0, 0))
    n_rows = N_HEADS * SB_WINDOW * t
    return pl.pallas_call(
        _sb_kernel,
        grid=(bsz, nq),
        in_specs=[q_hm, kv_hm, kv_hm, _resident((2 * t, 2 * t)),
                  pl.BlockSpec((1, GROUP_WIDTH), lambda b, i: (0, 0))],
        out_specs=pl.BlockSpec((1, t, GROUP_WIDTH), lambda b, i: (b, i, 0)),
        out_shape=jax.ShapeDtypeStruct((bsz, seq, GROUP_WIDTH), BF16),
        scratch_shapes=[pltpu.VMEM((N_HEADS, t, SB_WINDOW * t), F32),
                        pltpu.VMEM((n_rows, 2 * t), BF16),
                        pltpu.VMEM((n_rows, 2 * t), F32),
                        pltpu.VMEM((N_HEADS, t, t), F32),
                        pltpu.VMEM((N_HEADS, t, HEAD_DIM), F32)],
        compiler_params=_params(2),
        name="sb",
    )(q, k, v, jnp.asarray(tri, dtype=BF16), g_out)


def _pack_ffn(w_gate, w_up, w_down):
    d, f = w_gate.shape
    nc = f // FFN_CHUNK
    wg = w_gate.reshape(d, nc, FFN_CHUNK)
    wu = w_up.reshape(d, nc, FFN_CHUNK)
    wgu = jnp.concatenate([wg, wu], axis=2).transpose(1, 0, 2).astype(BF16)
    return wgu, w_down.astype(BF16)


def kernel(x, c, w_ada, b_ada, g_pre, g_post, w_ffn_gate, w_ffn_up, w_ffn_down, w_in, g_kidx,
           rel_bias, g_out_a, g_out_b, w_out):
    bsz, seq, d = x.shape
    depth = w_ada.shape[0]
    gw = GROUP_WIDTH
    h = x.reshape(bsz * seq, d)
    bias = _bias_tiles(rel_bias, DSA_TILE)
    for l in range(depth):
        mods = _ada(c, w_ada[l], b_ada[l]).reshape(bsz, -1, d)
        wgu1, wd1 = _pack_ffn(w_ffn_gate[l, 0], w_ffn_up[l, 0], w_ffn_down[l, 0])
        wgu2, wd2 = _pack_ffn(w_ffn_gate[l, 1], w_ffn_up[l, 1], w_ffn_down[l, 1])
        w = w_in[l]
        o_ki = 4 * gw
        o_wi = o_ki + IDX_DIM
        o_b = o_wi + N_HEADS
        wt = jnp.concatenate([w[:, :gw], w[:, 2 * gw:3 * gw], w[:, 3 * gw:o_ki]], axis=1).T.astype(BF16)
        wka = w[:, gw:2 * gw].astype(BF16)
        wki = w[:, o_ki:o_wi].astype(BF16)
        wwit = jnp.pad(w[:, o_wi:o_b].T, ((0, 16 - N_HEADS), (0, 0))).astype(BF16)
        wb = w[:, o_b:].astype(BF16)
        woa = w_out[l, :gw].astype(BF16)
        wob = w_out[l, gw:].astype(BF16)
        vec = lambda g: g.reshape(1, -1)

        h1 = _ffn(h, mods, vec(g_pre[l, 0]), vec(g_post[l, 0]), wgu1, wd1, 0, seq)
        qat, ka, vat, qit, ki, wit, qb, kb, vb = _proj(h1, mods, vec(g_pre[l, 1]), vec(g_kidx[l]),
                                                       wt, wka, wki, wwit, wb, bsz, seq)
        oa = _dsa(qit, wit, ki, qat, ka, vat, bias, g_out_a[l], bsz, seq)
        ob = _sb(qb, kb, vb, vec(g_out_b[l]), bsz, seq)
        h = _ffn(h1, mods, vec(g_pre[l, 2]), vec(g_post[l, 2]), wgu2, wd2, 2, seq,
                 mix=(oa.reshape(bsz * seq, gw), ob.reshape(bsz * seq, gw), woa, wob,
                      vec(g_post[l, 1])))
    return h.reshape(bsz, seq, d)
```

```python
import functools
import math

import numpy as np
import jax
import jax.numpy as jnp
from jax import lax
from jax.experimental import pallas as pl
from jax.experimental.pallas import tpu as pltpu

F32 = jnp.float32
BF16 = jnp.bfloat16
I32 = jnp.int32

EPS = 1e-6
HEAD_DIM = 64
N_HEADS = 8
GROUP_WIDTH = N_HEADS * HEAD_DIM
IDX_DIM = 64
TOPK_MAX = 256
N_BUCKETS = 32
MAX_DISTANCE = 128
SUBLANES = 8
FFN_CHUNK = 256
TOKEN_TILE = 512
DSA_TILE = 256
SCORE_ROWS = 64
SB_TILE = 128
SB_WINDOW = 3
LOG2E = math.log2(math.e)
NEG_BIG = -1e30
SB_DEAD = -104.0
VMEM_LIMIT = 56 * 1024 * 1024
INT_MIN = np.int32(-2 ** 31)


def _dot(a, b):
    return jnp.dot(a, b, preferred_element_type=F32)


def _dot_nt(a, b):
    return lax.dot_general(a, b, (((1,), (1,)), ((), ())), preferred_element_type=F32)


def _rms(x, g):
    return x * lax.rsqrt(jnp.mean(x * x, axis=-1, keepdims=True) + EPS) * g


def _resident(shape):
    n = len(shape)
    return pl.BlockSpec(shape, lambda *_: (0,) * n, pipeline_mode=pl.Buffered(1))


def _params(n_grid):
    return pltpu.CompilerParams(dimension_semantics=("arbitrary",) * n_grid,
                                vmem_limit_bytes=VMEM_LIMIT)


def _ada_kernel(c_ref, w_ref, b_ref, o_ref):
    c = c_ref[...]
    a = c * jax.nn.sigmoid(c)
    w = w_ref[...]
    a_hi = a.astype(BF16)
    a_lo = (a - a_hi.astype(F32)).astype(BF16)
    w_hi = w.astype(BF16)
    w_lo = (w - w_hi.astype(F32)).astype(BF16)
    o_ref[...] = _dot(a_hi, w_hi) + (_dot(a_hi, w_lo) + _dot(a_lo, w_hi)) + b_ref[...]


def _ada(c, w, b):
    bsz, d = c.shape
    n = w.shape[1]
    tn = 1152 if n % 1152 == 0 else n
    return pl.pallas_call(
        _ada_kernel,
        grid=(n // tn,),
        in_specs=[pl.BlockSpec((bsz, d), lambda j: (0, 0)),
                  pl.BlockSpec((d, tn), lambda j: (0, j)),
                  pl.BlockSpec((1, tn), lambda j: (0, j))],
        out_specs=pl.BlockSpec((bsz, tn), lambda j: (0, j)),
        out_shape=jax.ShapeDtypeStruct((bsz, n), F32),
        compiler_params=_params(1),
        name="ada",
    )(c, w, b.reshape(1, n))


def _ffn_kernel(*refs, sub, mix):
    if mix:
        (h_ref, oa_ref, ob_ref, woa_ref, wob_ref, gpm_ref, mods_ref, gpre_ref, gpost_ref,
         wg_ref, wu_ref, wd_ref, o_ref, n_scr, a_scr) = refs
    else:
        (h_ref, mods_ref, gpre_ref, gpost_ref, wg_ref, wu_ref, wd_ref, o_ref, n_scr, a_scr) = refs
    x = h_ref[...]
    if mix:
        o = _dot(oa_ref[...], woa_ref[...]) + _dot(ob_ref[...], wob_ref[...])
        x = x + mods_ref[0, 5:6, :] * _rms(o, gpm_ref[...])
    shift = mods_ref[0, 3 * sub:3 * sub + 1, :]
    scale = mods_ref[0, 3 * sub + 1:3 * sub + 2, :]
    gate = mods_ref[0, 3 * sub + 2:3 * sub + 3, :]
    n = _rms(x, gpre_ref[...]) * (1.0 + scale) + shift
    n_scr[...] = n.astype(BF16)
    fc = FFN_CHUNK
    for c0 in range(0, wg_ref.shape[1], fc):
        g = _dot(n_scr[...], wg_ref[:, c0:c0 + fc])
        u = _dot(n_scr[...], wu_ref[:, c0:c0 + fc])
        a_scr[:, c0:c0 + fc] = (g * jax.nn.sigmoid(g) * u).astype(BF16)
    f = _dot(a_scr[...], wd_ref[...])
    o_ref[...] = x + 0.5 * gate * _rms(f, gpost_ref[...])


def _ffn(h, mods, g_pre, g_post, wg, wu, wd, sub, seq, mix=None):
    n_tok, d = h.shape
    tm = min(TOKEN_TILE, seq)
    tiles_per_batch = seq // tm
    row = pl.BlockSpec((tm, d), lambda i: (i, 0))
    vec = pl.BlockSpec((1, d), lambda i: (0, 0))
    mod_spec = pl.BlockSpec((1, mods.shape[1], d), lambda i: (i // tiles_per_batch, 0, 0))
    args, specs = [h], [row]
    if mix is not None:
        oa, ob, woa, wob, g_post_mix = mix
        gw = oa.shape[1]
        half = pl.BlockSpec((tm, gw), lambda i: (i, 0))
        args += [oa, ob, woa, wob, g_post_mix]
        specs += [half, half, _resident(woa.shape), _resident(wob.shape), vec]
    args += [mods, g_pre, g_post, wg, wu, wd]
    specs += [mod_spec, vec, vec, _resident(wg.shape), _resident(wu.shape), _resident(wd.shape)]
    return pl.pallas_call(
        functools.partial(_ffn_kernel, sub=sub, mix=mix is not None),
        grid=(n_tok // tm,),
        in_specs=specs,
        out_specs=row,
        out_shape=jax.ShapeDtypeStruct((n_tok, d), F32),
        scratch_shapes=[pltpu.VMEM((tm, d), BF16), pltpu.VMEM((tm, wd.shape[0]), BF16)],
        compiler_params=_params(1),
        name="ffn_mix" if mix is not None else "ffn",
    )(*args)


def _proj_kernel(h_ref, mods_ref, gpre_ref, gk_ref, wt_ref, wka_ref, wki_ref, wwit_ref, wb_ref,
                 qat_ref, ka_ref, vat_ref, qit_ref, ki_ref, wit_ref, qb_ref, kb_ref, vb_ref, n_scr):
    x = h_ref[...]
    shift = mods_ref[0, 3:4, :]
    scale = mods_ref[0, 4:5, :]
    n_scr[...] = (_rms(x, gpre_ref[...]) * (1.0 + scale) + shift).astype(BF16)
    gw = GROUP_WIDTH
    qscale = HEAD_DIM ** -0.5

    def heads(w, out_ref, mul):
        p = _dot(n_scr[...], w)
        if mul != 1.0:
            p = p * mul
        for hd in range(N_HEADS):
            out_ref[0, hd] = p[:, hd * HEAD_DIM:(hd + 1) * HEAD_DIM].astype(BF16)

    def feature_major(col, out_ref, mul):
        p = _dot_nt(wt_ref[col * gw:(col + 1) * gw, :], n_scr[...])
        if mul != 1.0:
            p = p * mul
        out_ref[0] = p.astype(BF16)

    feature_major(0, qat_ref, qscale * LOG2E)
    feature_major(1, vat_ref, 1.0)
    feature_major(2, qit_ref, 1.0)
    heads(wka_ref[...], ka_ref, 1.0)
    heads(wb_ref[:, :gw], qb_ref, qscale)
    heads(wb_ref[:, gw:2 * gw], kb_ref, 1.0)
    heads(wb_ref[:, 2 * gw:], vb_ref, 1.0)
    ki = _dot(n_scr[...], wki_ref[...])
    ki_ref[0] = _rms(ki, gk_ref[...]).astype(BF16)
    wit = _dot_nt(wwit_ref[...], n_scr[...])
    wit_ref[0] = wit[:N_HEADS, :] * ((N_HEADS * IDX_DIM) ** -0.5)


def _proj(h1, mods, g_pre, g_kidx, wt, wka, wki, wwit, wb, bsz, seq):
    n_tok, d = h1.shape
    tm = min(TOKEN_TILE, seq)
    tpb = seq // tm
    gw = GROUP_WIDTH
    row = pl.BlockSpec((tm, d), lambda i: (i, 0))
    vec = pl.BlockSpec((1, d), lambda i: (0, 0))
    mod_spec = pl.BlockSpec((1, mods.shape[1], d), lambda i: (i // tpb, 0, 0))
    hm = pl.BlockSpec((1, N_HEADS, tm, HEAD_DIM), lambda i: (i // tpb, 0, i % tpb, 0))
    hm_shape = jax.ShapeDtypeStruct((bsz, N_HEADS, seq, HEAD_DIM), BF16)
    fm = pl.BlockSpec((1, gw, tm), lambda i: (i // tpb, 0, i % tpb))
    fm_shape = jax.ShapeDtypeStruct((bsz, gw, seq), BF16)
    ki_spec = pl.BlockSpec((1, tm, IDX_DIM), lambda i: (i // tpb, i % tpb, 0))
    wit_spec = pl.BlockSpec((1, N_HEADS, tm), lambda i: (i // tpb, 0, i % tpb))
    return pl.pallas_call(
        _proj_kernel,
        grid=(n_tok // tm,),
        in_specs=[row, mod_spec, vec, pl.BlockSpec((1, IDX_DIM), lambda i: (0, 0)),
                  _resident(wt.shape), _resident(wka.shape), _resident(wki.shape),
                  _resident(wwit.shape), _resident(wb.shape)],
        out_specs=[fm, hm, fm, fm, ki_spec, wit_spec, hm, hm, hm],
        out_shape=[fm_shape, hm_shape, fm_shape, fm_shape,
                   jax.ShapeDtypeStruct((bsz, seq, IDX_DIM), BF16),
                   jax.ShapeDtypeStruct((bsz, N_HEADS, seq), F32),
                   hm_shape, hm_shape, hm_shape],
        scratch_shapes=[pltpu.VMEM((tm, d), BF16)],
        compiler_params=_params(1),
        name="proj",
    )(h1, mods, g_pre, g_kidx, wt, wka, wki, wwit, wb)


def _t5_bucket_np(n):
    n = np.maximum(n, 0)
    max_exact = N_BUCKETS // 2
    nf = np.maximum(n, 1).astype(np.float32)
    large = max_exact + (np.log(nf / np.float32(max_exact)) / np.float32(math.log(MAX_DISTANCE / max_exact))
                         * np.float32(N_BUCKETS - max_exact)).astype(np.int32)
    large = np.minimum(large, N_BUCKETS - 1)
    return np.where(n < max_exact, n, large).astype(np.int32)


def _bias_kernel(relb_ref, bucket_ref, o_ref):
    hd = pl.program_id(0)
    far = relb_ref[N_BUCKETS - 1, hd]
    for r in range(2):
        bk = bucket_ref[r]
        acc = jnp.zeros(bk.shape, F32)
        for k in range(N_BUCKETS - 1):
            acc = jnp.where(bk == k, (relb_ref[k, hd] - far) * LOG2E, acc)
        o_ref[0, r] = acc


def _bias_tiles(rel_bias, t):
    assert t >= MAX_DISTANCE
    s = np.arange(t)[:, None]
    q = np.arange(t)[None, :]
    buckets = np.stack([_t5_bucket_np(q - s), _t5_bucket_np(t + q - s)]).astype(np.int32)
    return pl.pallas_call(
        _bias_kernel,
        grid=(N_HEADS,),
        in_specs=[pl.BlockSpec(memory_space=pltpu.SMEM),
                  pl.BlockSpec((2, t, t), lambda hd: (0, 0, 0))],
        out_specs=pl.BlockSpec((1, 2, t, t), lambda hd: (hd, 0, 0, 0)),
        out_shape=jax.ShapeDtypeStruct((N_HEADS, 2, t, t), F32),
        compiler_params=_params(1),
        name="bias_tiles",
    )(rel_bias, jnp.asarray(buckets))


def _tree(op, xs):
    xs = list(xs)
    while len(xs) > 1:
        xs = [op(xs[j], xs[j + 1]) if j + 1 < len(xs) else xs[j] for j in range(0, len(xs), 2)]
    return xs[0]


def _row_groups(x):
    return [x[r:r + SUBLANES, :] for r in range(0, x.shape[0], SUBLANES)]


def _all_sublanes(op, x):
    for shift in (4, 2, 1):
        x = op(x, pltpu.roll(x, shift, 0))
    return x


def _dsa_kernel(qit_ref, wit_ref, ki_ref, qat_ref, ka_ref, vat_ref, bias_ref, gt_ref, tri_ref, o_ref,
                key_scr, selm_scr, logit_scr, out_scr, *, topk, n_q_tiles):
    t = DSA_TILE
    kf = float(topk)
    krow = lax.broadcasted_iota(I32, (t, t), 0)
    qcol = lax.broadcasted_iota(I32, (t, t), 1)
    causal = krow <= qcol

    def tile_work(nkb):
        diag = nkb - 1

        wt = wit_ref[0]
        for kb in range(nkb):
            for r0 in range(0, t, SCORE_ROWS):
                kblk = ki_ref[0, kb * t + r0:kb * t + r0 + SCORE_ROWS, :]
                acc = None
                for hd in range(N_HEADS):
                    sc = _dot(kblk, qit_ref[0, hd * IDX_DIM:(hd + 1) * IDX_DIM, :])
                    term = jnp.maximum(sc, 0.0) * wt[hd:hd + 1, :]
                    acc = term if acc is None else acc + term
                acc = acc + 0.0
                if kb == diag:
                    acc = jnp.where(causal[r0:r0 + SCORE_ROWS, :], acc, -jnp.inf)
                bits = pltpu.bitcast(acc, I32)
                key_scr[kb, r0:r0 + SCORE_ROWS, :] = bits ^ ((bits >> 31) & jnp.int32(0x7FFFFFFF))

        def count_ge(cand):
            accs = [jnp.zeros((SUBLANES, t), F32) for _ in range(4)]
            n = 0
            for kb in range(nkb):
                for grp in _row_groups(key_scr[kb]):
                    accs[n % 4] = accs[n % 4] + jnp.where(grp >= cand, 1.0, 0.0)
                    n += 1
            return _all_sublanes(jnp.add, _tree(jnp.add, accs))

        def search_step(j, ans):
            cand_off = ans | lax.shift_left(jnp.int32(1), 31 - j)
            cnt = count_ge(cand_off ^ INT_MIN)
            return jnp.where(cnt >= kf, cand_off, ans)

        if nkb * t <= topk:
            thr = jnp.full((SUBLANES, t), INT_MIN, I32)
        else:
            thr = lax.fori_loop(0, 32, search_step, jnp.zeros((SUBLANES, t), I32)) ^ INT_MIN

        n_ge = jnp.zeros((SUBLANES, t), F32)
        for kb in range(nkb):
            blk = key_scr[kb]
            for r, grp in enumerate(_row_groups(blk)):
                rows = slice(r * SUBLANES, (r + 1) * SUBLANES)
                sel = grp >= thr
                n_ge = n_ge + jnp.where(sel, 1.0, 0.0)
                m = jnp.where(sel, 0.0, NEG_BIG)
                if kb == diag:
                    m = jnp.where(causal[rows, :], m, NEG_BIG)
                selm_scr[kb, rows, :] = m
        n_ge = _all_sublanes(jnp.add, n_ge)
        excess = jnp.max(jnp.where(n_ge > kf, 1.0, 0.0))

        @pl.when(excess > 0.0)
        def _():
            thr1 = thr[0:1, :]

            def tied(kb):
                return jnp.where(key_scr[kb] == thr1, 1.0, 0.0)

            def count_tied(kb, acc):
                return acc + jnp.sum(tied(kb), axis=0, keepdims=True)

            n_tied = lax.fori_loop(0, nkb, count_tied, jnp.zeros((1, t), F32))
            need = kf - (n_ge[0:1, :] - n_tied)

            def rewrite(kb, before):
                e = tied(kb)
                rank = _dot(tri_ref[...], e.astype(BF16)) + before
                k = key_scr[kb]
                keep = jnp.where(k > thr1, 0.0,
                                 jnp.where(k == thr1, jnp.where(rank < need, 0.0, NEG_BIG), NEG_BIG))
                selm_scr[kb] = jnp.where(selm_scr[kb] < 0.0, NEG_BIG, keep)
                return before + jnp.sum(e, axis=0, keepdims=True)

            lax.fori_loop(0, nkb, rewrite, jnp.zeros((1, t), F32))

        def head(hd, c):
            hrows = pl.ds(pl.multiple_of(hd * HEAD_DIM, HEAD_DIM), HEAD_DIM)
            qt = qat_ref[0, hrows, :]
            maxes = []
            for kb in range(nkb):
                lg = _dot(ka_ref[0, hd, kb * t:(kb + 1) * t, :], qt) + selm_scr[kb]
                if kb >= nkb - 2:
                    lg = lg + bias_ref[hd, diag - kb]
                logit_scr[kb] = lg
                maxes.append(jnp.max(lg, axis=0, keepdims=True))
            m = _tree(jnp.maximum, maxes)
            ssum = None
            acc = None
            for kb in range(nkb):
                p = jnp.exp2(logit_scr[kb] - m)
                ps = jnp.sum(p, axis=0, keepdims=True)
                pv = _dot(vat_ref[0, hrows, kb * t:(kb + 1) * t], p.astype(BF16))
                ssum = ps if ssum is None else ssum + ps
                acc = pv if acc is None else acc + pv
            out_scr[hrows, :] = acc * (1.0 / ssum)
            return c

        lax.fori_loop(0, N_HEADS, head, 0, unroll=2)

    i = pl.program_id(1)
    for k in range(n_q_tiles):
        pl.when(i == k)(functools.partial(tile_work, k + 1))

    o = out_scr[...]
    o = o * lax.rsqrt(jnp.mean(o * o, axis=0, keepdims=True) + EPS) * gt_ref[...]
    o_ref[0] = o.T.astype(BF16)


def _dsa(qit, wit, ki, qat, ka, vat, bias, g_out, bsz, seq):
    t = DSA_TILE
    nq = seq // t
    gw = GROUP_WIDTH
    topk = min(TOPK_MAX, seq // 4)
    q_fm = pl.BlockSpec((1, gw, t), lambda b, i: (b, 0, i))
    gt = jnp.broadcast_to(g_out.reshape(gw, 1), (gw, t))
    r = np.arange(t)
    lower = jnp.asarray(r[None, :] < r[:, None], dtype=BF16)
    return pl.pallas_call(
        functools.partial(_dsa_kernel, topk=topk, n_q_tiles=nq),
        grid=(bsz, nq),
        in_specs=[q_fm,
                  pl.BlockSpec((1, N_HEADS, t), lambda b, i: (b, 0, i)),
                  pl.BlockSpec((1, seq, IDX_DIM), lambda b, i: (b, 0, 0)),
                  q_fm,
                  pl.BlockSpec((1, N_HEADS, seq, HEAD_DIM), lambda b, i: (b, 0, 0, 0)),
                  pl.BlockSpec((1, gw, seq), lambda b, i: (b, 0, 0)),
                  _resident(bias.shape),
                  _resident((gw, t)),
                  _resident((t, t))],
        out_specs=pl.BlockSpec((1, t, gw), lambda b, i: (b, i, 0)),
        out_shape=jax.ShapeDtypeStruct((bsz, seq, gw), BF16),
        scratch_shapes=[pltpu.VMEM((nq, t, t), I32), pltpu.VMEM((nq, t, t), F32),
                        pltpu.VMEM((nq, t, t), F32), pltpu.VMEM((gw, t), F32)],
        compiler_params=_params(2),
        name="dsa",
    )(qit, wit, ki, qat, ka, vat, bias, gt, lower)


def _log1m_sigmoid(z):
    return -(jnp.maximum(z, 0.0) + jnp.log(1.0 + jnp.exp(-jnp.abs(z))))


def _split_bf16(x):
    hi = x.astype(BF16)
    lo = (x - hi.astype(F32)).astype(BF16)
    return jnp.concatenate([hi, lo], axis=1)


def _sb_kernel(q_ref, k_ref, v_ref, tri_ref, g_ref, o_ref, z_scr, lh_scr, cs_scr, r_scr, acc_scr):
    t = SB_TILE
    nw = SB_WINDOW
    i = pl.program_id(1)
    wb = jnp.maximum(i - (nw - 1), 0)
    wrows = pl.ds(pl.multiple_of(wb * t, t), nw * t)
    row = lax.broadcasted_iota(I32, (t, t), 0)
    col = lax.broadcasted_iota(I32, (t, t), 1)

    def causal(w):
        return (col - row) < (i - wb - w) * t

    def window(masked):
        def keep(w, x):
            return jnp.where(causal(w), x, 0.0) if w in masked else x

        for hd in range(N_HEADS):
            z = _dot_nt(q_ref[0, hd], k_ref[0, hd, wrows, :])
            z_scr[hd] = z
            for w in range(nw):
                lg = keep(w, _log1m_sigmoid(z[:, w * t:(w + 1) * t]))
                lh_scr[(hd * nw + w) * t:(hd * nw + w + 1) * t, :] = _split_bf16(lg)
        cs_scr[...] = _dot(lh_scr[...], tri_ref[...])
        for hd in range(N_HEADS):
            after = None
            a_blocks = [None] * nw
            for w in reversed(range(nw)):
                cs = cs_scr[(hd * nw + w) * t:(hd * nw + w + 1) * t, :]
                c = cs[:, :t] if after is None else cs[:, :t] + after
                a = keep(w, jnp.exp(z_scr[hd, :, w * t:(w + 1) * t] + c))
                a_blocks[w] = a.astype(BF16)
                after = cs[:, t:] if after is None else after + cs[:, t:]
            acc_scr[hd] = _dot(jnp.concatenate(a_blocks, axis=1), v_ref[0, hd, wrows, :])
            r_scr[hd] = after

    pl.when(i >= nw - 1)(functools.partial(window, (nw - 1,)))
    pl.when(i < nw - 1)(functools.partial(window, tuple(range(nw))))
    rmax = _tree(jnp.maximum, [r_scr[hd] for hd in range(N_HEADS)])

    def far_block(kb):
        rows = pl.ds(pl.multiple_of(kb * t, t), t)
        rmax = None
        for hd in range(N_HEADS):
            z = _dot_nt(q_ref[0, hd], k_ref[0, hd, rows, :])
            cs = _dot(_split_bf16(_log1m_sigmoid(z)), tri_ref[...])
            r_old = r_scr[hd]
            a = jnp.exp(z + cs[:, :t] + r_old)
            acc_scr[hd] += _dot(a.astype(BF16), v_ref[0, hd, rows, :])
            r_new = r_old + cs[:, t:]
            r_scr[hd] = r_new
            rmax = r_new if rmax is None else jnp.maximum(rmax, r_new)
        return jnp.max(rmax)

    def cond(c):
        kb, rmax = c
        return jnp.logical_and(kb >= 0, rmax > SB_DEAD)

    def body(c):
        kb, _ = c
        return kb - 1, far_block(kb)

    lax.while_loop(cond, body, (wb - 1, jnp.max(rmax)))
    o = jnp.concatenate([acc_scr[hd] for hd in range(N_HEADS)], axis=1)
    o_ref[0] = _rms(o, g_ref[...]).astype(BF16)


def _sb(q, k, v, g_out, bsz, seq):
    t = SB_TILE
    nq = seq // t
    assert seq >= SB_WINDOW * t
    j = np.arange(t)[:, None]
    s = np.arange(t)[None, :]
    tri = np.concatenate([(j >= s), np.ones((t, t), bool)], axis=1).astype(np.float32)
    tri = np.concatenate([tri, tri], axis=0)
    q_hm = pl.BlockSpec((1, N_HEADS, t, HEAD_DIM), lambda b, i: (b, 0, i, 0))
    kv_hm = pl.BlockSpec((1, N_HEADS, seq, HEAD_DIM), lambda b, i: (b, 0, 0, 0))
    n_rows = N_HEADS * SB_WINDOW * t
    return pl.pallas_call(
        _sb_kernel,
        grid=(bsz, nq),
        in_specs=[q_hm, kv_hm, kv_hm, _resident((2 * t, 2 * t)),
                  pl.BlockSpec((1, GROUP_WIDTH), lambda b, i: (0, 0))],
        out_specs=pl.BlockSpec((1, t, GROUP_WIDTH), lambda b, i: (b, i, 0)),
        out_shape=jax.ShapeDtypeStruct((bsz, seq, GROUP_WIDTH), BF16),
        scratch_shapes=[pltpu.VMEM((N_HEADS, t, SB_WINDOW * t), F32),
                        pltpu.VMEM((n_rows, 2 * t), BF16),
                        pltpu.VMEM((n_rows, 2 * t), F32),
                        pltpu.VMEM((N_HEADS, t, t), F32),
                        pltpu.VMEM((N_HEADS, t, HEAD_DIM), F32)],
        compiler_params=_params(2),
        name="sb",
    )(q, k, v, jnp.asarray(tri, dtype=BF16), g_out)


def kernel(x, c, w_ada, b_ada, g_pre, g_post, w_ffn_gate, w_ffn_up, w_ffn_down, w_in, g_kidx,
           rel_bias, g_out_a, g_out_b, w_out):
    bsz, seq, d = x.shape
    depth = w_ada.shape[0]
    gw = GROUP_WIDTH
    h = x.reshape(bsz * seq, d)
    bias = _bias_tiles(rel_bias, DSA_TILE)
    for l in range(depth):
        mods = _ada(c, w_ada[l], b_ada[l]).reshape(bsz, -1, d)
        ffn_w = [(w_ffn_gate[l, j].astype(BF16), w_ffn_up[l, j].astype(BF16),
                  w_ffn_down[l, j].astype(BF16)) for j in range(2)]
        w = w_in[l]
        o_ki = 4 * gw
        o_wi = o_ki + IDX_DIM
        o_b = o_wi + N_HEADS
        wt = jnp.concatenate([w[:, :gw], w[:, 2 * gw:3 * gw], w[:, 3 * gw:o_ki]], axis=1).T.astype(BF16)
        wka = w[:, gw:2 * gw].astype(BF16)
        wki = w[:, o_ki:o_wi].astype(BF16)
        wwit = jnp.pad(w[:, o_wi:o_b].T, ((0, 16 - N_HEADS), (0, 0))).astype(BF16)
        wb = w[:, o_b:].astype(BF16)
        woa = w_out[l, :gw].astype(BF16)
        wob = w_out[l, gw:].astype(BF16)
        vec = lambda g: g.reshape(1, -1)

        h1 = _ffn(h, mods, vec(g_pre[l, 0]), vec(g_post[l, 0]), *ffn_w[0], 0, seq)
        qat, ka, vat, qit, ki, wit, qb, kb, vb = _proj(h1, mods, vec(g_pre[l, 1]), vec(g_kidx[l]),
                                                       wt, wka, wki, wwit, wb, bsz, seq)
        oa = _dsa(qit, wit, ki, qat, ka, vat, bias, g_out_a[l], bsz, seq)
        ob = _sb(qb, kb, vb, vec(g_out_b[l]), bsz, seq)
        h = _ffn(h1, mods, vec(g_pre[l, 2]), vec(g_post[l, 2]), *ffn_w[1], 2, seq,
                 mix=(oa.reshape(bsz * seq, gw), ob.reshape(bsz * seq, gw), woa, wob,
                      vec(g_post[l, 1])))
    return h.reshape(bsz, seq, d)
```

```python
import functools
import math

import numpy as np
import jax
import jax.numpy as jnp
from jax import lax
from jax.experimental import pallas as pl
from jax.experimental.pallas import tpu as pltpu

F32 = jnp.float32
BF16 = jnp.bfloat16
I32 = jnp.int32
I16 = jnp.int16

EPS = 1e-6
HEAD_DIM = 64
N_HEADS = 8
GROUP_WIDTH = N_HEADS * HEAD_DIM
IDX_DIM = 64
TOPK_MAX = 256
N_BUCKETS = 32
MAX_DISTANCE = 128
SUBLANES = 8
FFN_CHUNK = 256
TOKEN_TILE = 512
DSA_TILE = 256
SCORE_ROWS = 64
SB_TILE = 128
SB_WINDOW = 3
LOG2E = math.log2(math.e)
NEG_BIG = -1e30
SB_DEAD = -104.0
VMEM_LIMIT = 56 * 1024 * 1024
INT_MIN = np.int32(-2 ** 31)
HALF = 2 ** 15
ONE16, ZERO16, MIN16 = np.int16(1), np.int16(0), np.int16(-HALF)


def _dot(a, b):
    return jnp.dot(a, b, preferred_element_type=F32)


def _dot_nt(a, b):
    return lax.dot_general(a, b, (((1,), (1,)), ((), ())), preferred_element_type=F32)


def _rms(x, g):
    return x * lax.rsqrt(jnp.mean(x * x, axis=-1, keepdims=True) + EPS) * g


def _resident(shape):
    n = len(shape)
    return pl.BlockSpec(shape, lambda *_: (0,) * n, pipeline_mode=pl.Buffered(1))


def _params(n_grid):
    return pltpu.CompilerParams(dimension_semantics=("arbitrary",) * n_grid,
                                vmem_limit_bytes=VMEM_LIMIT)


def _ada_kernel(c_ref, w_ref, b_ref, o_ref):
    c = c_ref[...]
    a = c * jax.nn.sigmoid(c)
    w = w_ref[...]
    a_hi = a.astype(BF16)
    a_lo = (a - a_hi.astype(F32)).astype(BF16)
    w_hi = w.astype(BF16)
    w_lo = (w - w_hi.astype(F32)).astype(BF16)
    o_ref[...] = _dot(a_hi, w_hi) + (_dot(a_hi, w_lo) + _dot(a_lo, w_hi)) + b_ref[...]


def _ada(c, w, b):
    bsz, d = c.shape
    n = w.shape[1]
    tn = 1152 if n % 1152 == 0 else n
    return pl.pallas_call(
        _ada_kernel,
        grid=(n // tn,),
        in_specs=[pl.BlockSpec((bsz, d), lambda j: (0, 0)),
                  pl.BlockSpec((d, tn), lambda j: (0, j)),
                  pl.BlockSpec((1, tn), lambda j: (0, j))],
        out_specs=pl.BlockSpec((bsz, tn), lambda j: (0, j)),
        out_shape=jax.ShapeDtypeStruct((bsz, n), F32),
        compiler_params=_params(1),
        name="ada",
    )(c, w, b.reshape(1, n))


def _ffn_kernel(*refs, sub, mix):
    if mix:
        (h_ref, oa_ref, ob_ref, woa_ref, wob_ref, gpm_ref, mods_ref, gpre_ref, gpost_ref,
         wg_ref, wu_ref, wd_ref, o_ref, n_scr, a_scr) = refs
    else:
        (h_ref, mods_ref, gpre_ref, gpost_ref, wg_ref, wu_ref, wd_ref, o_ref, n_scr, a_scr) = refs
    x = h_ref[...]
    if mix:
        o = _dot(oa_ref[...], woa_ref[...]) + _dot(ob_ref[...], wob_ref[...])
        x = x + mods_ref[0, 5:6, :] * _rms(o, gpm_ref[...])
    shift = mods_ref[0, 3 * sub:3 * sub + 1, :]
    scale = mods_ref[0, 3 * sub + 1:3 * sub + 2, :]
    gate = mods_ref[0, 3 * sub + 2:3 * sub + 3, :]
    n = _rms(x, gpre_ref[...]) * (1.0 + scale) + shift
    n_scr[...] = n.astype(BF16)
    fc = FFN_CHUNK
    for c0 in range(0, wg_ref.shape[1], fc):
        g = _dot(n_scr[...], wg_ref[:, c0:c0 + fc])
        u = _dot(n_scr[...], wu_ref[:, c0:c0 + fc])
        a_scr[:, c0:c0 + fc] = (g * jax.nn.sigmoid(g) * u).astype(BF16)
    f = _dot(a_scr[...], wd_ref[...])
    o_ref[...] = x + 0.5 * gate * _rms(f, gpost_ref[...])


def _ffn(h, mods, g_pre, g_post, wg, wu, wd, sub, seq, mix=None):
    n_tok, d = h.shape
    tm = min(TOKEN_TILE, seq)
    tiles_per_batch = seq // tm
    row = pl.BlockSpec((tm, d), lambda i: (i, 0))
    vec = pl.BlockSpec((1, d), lambda i: (0, 0))
    mod_spec = pl.BlockSpec((1, mods.shape[1], d), lambda i: (i // tiles_per_batch, 0, 0))
    args, specs = [h], [row]
    if mix is not None:
        oa, ob, woa, wob, g_post_mix = mix
        gw = oa.shape[1]
        half = pl.BlockSpec((tm, gw), lambda i: (i, 0))
        args += [oa, ob, woa, wob, g_post_mix]
        specs += [half, half, _resident(woa.shape), _resident(wob.shape), vec]
    args += [mods, g_pre, g_post, wg, wu, wd]
    specs += [mod_spec, vec, vec, _resident(wg.shape), _resident(wu.shape), _resident(wd.shape)]
    return pl.pallas_call(
        functools.partial(_ffn_kernel, sub=sub, mix=mix is not None),
        grid=(n_tok // tm,),
        in_specs=specs,
        out_specs=row,
        out_shape=jax.ShapeDtypeStruct((n_tok, d), F32),
        scratch_shapes=[pltpu.VMEM((tm, d), BF16), pltpu.VMEM((tm, wd.shape[0]), BF16)],
        compiler_params=_params(1),
        name="ffn_mix" if mix is not None else "ffn",
    )(*args)


def _proj_kernel(h_ref, mods_ref, gpre_ref, gk_ref, wt_ref, wka_ref, wki_ref, wwit_ref, wb_ref,
                 qat_ref, ka_ref, vat_ref, qit_ref, ki_ref, wit_ref, qb_ref, kb_ref, vb_ref, n_scr):
    x = h_ref[...]
    shift = mods_ref[0, 3:4, :]
    scale = mods_ref[0, 4:5, :]
    n_scr[...] = (_rms(x, gpre_ref[...]) * (1.0 + scale) + shift).astype(BF16)
    gw = GROUP_WIDTH
    qscale = HEAD_DIM ** -0.5

    def heads(w, out_ref, mul):
        p = _dot(n_scr[...], w)
        if mul != 1.0:
            p = p * mul
        for hd in range(N_HEADS):
            out_ref[0, hd] = p[:, hd * HEAD_DIM:(hd + 1) * HEAD_DIM].astype(BF16)

    def feature_major(col, out_ref, mul):
        p = _dot_nt(wt_ref[col * gw:(col + 1) * gw, :], n_scr[...])
        if mul != 1.0:
            p = p * mul
        out_ref[0] = p.astype(BF16)

    feature_major(0, qat_ref, qscale * LOG2E)
    feature_major(1, vat_ref, 1.0)
    feature_major(2, qit_ref, 1.0)
    heads(wka_ref[...], ka_ref, 1.0)
    heads(wb_ref[:, :gw], qb_ref, qscale)
    heads(wb_ref[:, gw:2 * gw], kb_ref, 1.0)
    heads(wb_ref[:, 2 * gw:], vb_ref, 1.0)
    ki = _dot(n_scr[...], wki_ref[...])
    ki_ref[0] = _rms(ki, gk_ref[...]).astype(BF16)
    wit = _dot_nt(wwit_ref[...], n_scr[...])
    wit_ref[0] = wit[:N_HEADS, :] * ((N_HEADS * IDX_DIM) ** -0.5)


def _proj(h1, mods, g_pre, g_kidx, wt, wka, wki, wwit, wb, bsz, seq):
    n_tok, d = h1.shape
    tm = min(TOKEN_TILE, seq)
    tpb = seq // tm
    gw = GROUP_WIDTH
    row = pl.BlockSpec((tm, d), lambda i: (i, 0))
    vec = pl.BlockSpec((1, d), lambda i: (0, 0))
    mod_spec = pl.BlockSpec((1, mods.shape[1], d), lambda i: (i // tpb, 0, 0))
    hm = pl.BlockSpec((1, N_HEADS, tm, HEAD_DIM), lambda i: (i // tpb, 0, i % tpb, 0))
    hm_shape = jax.ShapeDtypeStruct((bsz, N_HEADS, seq, HEAD_DIM), BF16)
    fm = pl.BlockSpec((1, gw, tm), lambda i: (i // tpb, 0, i % tpb))
    fm_shape = jax.ShapeDtypeStruct((bsz, gw, seq), BF16)
    ki_spec = pl.BlockSpec((1, tm, IDX_DIM), lambda i: (i // tpb, i % tpb, 0))
    wit_spec = pl.BlockSpec((1, N_HEADS, tm), lambda i: (i // tpb, 0, i % tpb))
    return pl.pallas_call(
        _proj_kernel,
        grid=(n_tok // tm,),
        in_specs=[row, mod_spec, vec, pl.BlockSpec((1, IDX_DIM), lambda i: (0, 0)),
                  _resident(wt.shape), _resident(wka.shape), _resident(wki.shape),
                  _resident(wwit.shape), _resident(wb.shape)],
        out_specs=[fm, hm, fm, fm, ki_spec, wit_spec, hm, hm, hm],
        out_shape=[fm_shape, hm_shape, fm_shape, fm_shape,
                   jax.ShapeDtypeStruct((bsz, seq, IDX_DIM), BF16),
                   jax.ShapeDtypeStruct((bsz, N_HEADS, seq), F32),
                   hm_shape, hm_shape, hm_shape],
        scratch_shapes=[pltpu.VMEM((tm, d), BF16)],
        compiler_params=_params(1),
        name="proj",
    )(h1, mods, g_pre, g_kidx, wt, wka, wki, wwit, wb)


def _t5_bucket_np(n):
    n = np.maximum(n, 0)
    max_exact = N_BUCKETS // 2
    nf = np.maximum(n, 1).astype(np.float32)
    large = max_exact + (np.log(nf / np.float32(max_exact)) / np.float32(math.log(MAX_DISTANCE / max_exact))
                         * np.float32(N_BUCKETS - max_exact)).astype(np.int32)
    large = np.minimum(large, N_BUCKETS - 1)
    return np.where(n < max_exact, n, large).astype(np.int32)


def _bias_kernel(relb_ref, bucket_ref, o_ref):
    hd = pl.program_id(0)
    far = relb_ref[N_BUCKETS - 1, hd]
    for r in range(2):
        bk = bucket_ref[r]
        acc = jnp.zeros(bk.shape, F32)
        for k in range(N_BUCKETS - 1):
            acc = jnp.where(bk == k, (relb_ref[k, hd] - far) * LOG2E, acc)
        o_ref[0, r] = acc


def _bias_tiles(rel_bias, t):
    assert t >= MAX_DISTANCE
    s = np.arange(t)[:, None]
    q = np.arange(t)[None, :]
    buckets = np.stack([_t5_bucket_np(q - s), _t5_bucket_np(t + q - s)]).astype(np.int32)
    return pl.pallas_call(
        _bias_kernel,
        grid=(N_HEADS,),
        in_specs=[pl.BlockSpec(memory_space=pltpu.SMEM),
                  pl.BlockSpec((2, t, t), lambda hd: (0, 0, 0))],
        out_specs=pl.BlockSpec((1, 2, t, t), lambda hd: (hd, 0, 0, 0)),
        out_shape=jax.ShapeDtypeStruct((N_HEADS, 2, t, t), F32),
        compiler_params=_params(1),
        name="bias_tiles",
    )(rel_bias, jnp.asarray(buckets))


def _tree(op, xs):
    xs = list(xs)
    while len(xs) > 1:
        xs = [op(xs[j], xs[j + 1]) if j + 1 < len(xs) else xs[j] for j in range(0, len(xs), 2)]
    return xs[0]


def _row_groups(x):
    return [x[r:r + SUBLANES, :] for r in range(0, x.shape[0], SUBLANES)]


def _all_sublanes(op, x):
    for shift in (4, 2, 1):
        x = op(x, pltpu.roll(x, shift, 0))
    return x


def _dsa_kernel(qit_ref, wit_ref, ki_ref, qat_ref, ka_ref, vat_ref, bias_ref, gt_ref, tri_ref, o_ref,
                key_scr, hi_scr, lo_scr, selm_scr, logit_scr, out_scr, *, topk, n_q_tiles):
    t = DSA_TILE
    kf = float(topk)
    krow = lax.broadcasted_iota(I32, (t, t), 0)
    qcol = lax.broadcasted_iota(I32, (t, t), 1)
    causal = krow <= qcol

    def tile_work(nkb):
        diag = nkb - 1

        wt = wit_ref[0]
        for kb in range(nkb):
            for r0 in range(0, t, SCORE_ROWS):
                kblk = ki_ref[0, kb * t + r0:kb * t + r0 + SCORE_ROWS, :]
                acc = None
                for hd in range(N_HEADS):
                    sc = _dot(kblk, qit_ref[0, hd * IDX_DIM:(hd + 1) * IDX_DIM, :])
                    term = jnp.maximum(sc, 0.0) * wt[hd:hd + 1, :]
                    acc = term if acc is None else acc + term
                acc = acc + 0.0
                if kb == diag:
                    acc = jnp.where(causal[r0:r0 + SCORE_ROWS, :], acc, -jnp.inf)
                bits = pltpu.bitcast(acc, I32)
                key = bits ^ ((bits >> 31) & jnp.int32(0x7FFFFFFF))
                key_scr[kb, r0:r0 + SCORE_ROWS, :] = key
                hi_scr[kb, r0:r0 + SCORE_ROWS, :] = (key >> 16).astype(I16)

        def packed(x):
            return jnp.concatenate([x, x], axis=0).astype(I16)

        def column_total(parts):
            tot = _tree(jnp.add, parts).astype(F32)
            return _all_sublanes(jnp.add, tot[:SUBLANES, :] + tot[SUBLANES:, :])

        def search16(src_scr, need):
            def step(j, ans):
                cand_off = ans | lax.shift_left(jnp.int32(1), 15 - j)
                cand = packed(cand_off - HALF)
                accs = [jnp.zeros((2 * SUBLANES, t), I16) for _ in range(4)]
                n = 0
                for kb in range(nkb):
                    blk = src_scr[kb]
                    for r in range(0, t, 2 * SUBLANES):
                        hit = jnp.where(blk[r:r + 2 * SUBLANES, :] >= cand, ONE16, ZERO16)
                        accs[n % 4] = accs[n % 4] + hit
                        n += 1
                return jnp.where(column_total(accs) >= need, cand_off, ans)
            return lax.fori_loop(0, 16, step, jnp.zeros((SUBLANES, t), I32))

        if nkb * t <= topk:
            thr = jnp.full((SUBLANES, t), INT_MIN, I32)
        else:
            t_hi = search16(hi_scr, kf) - HALF
            t_hi16 = packed(t_hi)
            above = [jnp.zeros((2 * SUBLANES, t), I16) for _ in range(4)]
            n = 0
            for kb in range(nkb):
                for r in range(0, t, 2 * SUBLANES):
                    rows = slice(r, r + 2 * SUBLANES)
                    h = hi_scr[kb, rows, :]
                    lo = ((key_scr[kb, rows, :] & jnp.int32(0xFFFF)) - HALF).astype(I16)
                    lo_scr[kb, rows, :] = jnp.where(h == t_hi16, lo, MIN16)
                    above[n % 4] = above[n % 4] + jnp.where(h > t_hi16, ONE16, ZERO16)
                    n += 1
            t_lo = search16(lo_scr, kf - column_total(above))
            thr = lax.shift_left(t_hi, 16) | t_lo

        n_ge = jnp.zeros((SUBLANES, t), F32)
        for kb in range(nkb):
            blk = key_scr[kb]
            for r, grp in enumerate(_row_groups(blk)):
                rows = slice(r * SUBLANES, (r + 1) * SUBLANES)
                sel = grp >= thr
                n_ge = n_ge + jnp.where(sel, 1.0, 0.0)
                m = jnp.where(sel, 0.0, NEG_BIG)
                if kb == diag:
                    m = jnp.where(causal[rows, :], m, NEG_BIG)
                selm_scr[kb, rows, :] = m
        n_ge = _all_sublanes(jnp.add, n_ge)
        excess = jnp.max(jnp.where(n_ge > kf, 1.0, 0.0))

        @pl.when(excess > 0.0)
        def _():
            thr1 = thr[0:1, :]

            def tied(kb):
                return jnp.where(key_scr[kb] == thr1, 1.0, 0.0)

            def count_tied(kb, acc):
                return acc + jnp.sum(tied(kb), axis=0, keepdims=True)

            n_tied = lax.fori_loop(0, nkb, count_tied, jnp.zeros((1, t), F32))
            need = kf - (n_ge[0:1, :] - n_tied)

            def rewrite(kb, before):
                e = tied(kb)
                rank = _dot(tri_ref[...], e.astype(BF16)) + before
                k = key_scr[kb]
                keep = jnp.where(k > thr1, 0.0,
                                 jnp.where(k == thr1, jnp.where(rank < need, 0.0, NEG_BIG), NEG_BIG))
                selm_scr[kb] = jnp.where(selm_scr[kb] < 0.0, NEG_BIG, keep)
                return before + jnp.sum(e, axis=0, keepdims=True)

            lax.fori_loop(0, nkb, rewrite, jnp.zeros((1, t), F32))

        def head(hd, c):
            hrows = pl.ds(pl.multiple_of(hd * HEAD_DIM, HEAD_DIM), HEAD_DIM)
            qt = qat_ref[0, hrows, :]
            maxes = []
            for kb in range(nkb):
                lg = _dot(ka_ref[0, hd, kb * t:(kb + 1) * t, :], qt) + selm_scr[kb]
                if kb >= nkb - 2:
                    lg = lg + bias_ref[hd, diag - kb]
                logit_scr[kb] = lg
                maxes.append(jnp.max(lg, axis=0, keepdims=True))
            m = _tree(jnp.maximum, maxes)
            ssum = None
            acc = None
            for kb in range(nkb):
                p = jnp.exp2(logit_scr[kb] - m)
                ps = jnp.sum(p, axis=0, keepdims=True)
                pv = _dot(vat_ref[0, hrows, kb * t:(kb + 1) * t], p.astype(BF16))
                ssum = ps if ssum is None else ssum + ps
                acc = pv if acc is None else acc + pv
            out_scr[hrows, :] = acc * (1.0 / ssum)
            return c

        lax.fori_loop(0, N_HEADS, head, 0, unroll=2)

    i = pl.program_id(1)
    for k in range(n_q_tiles):
        pl.when(i == k)(functools.partial(tile_work, k + 1))

    o = out_scr[...]
    o = o * lax.rsqrt(jnp.mean(o * o, axis=0, keepdims=True) + EPS) * gt_ref[...]
    o_ref[0] = o.T.astype(BF16)


def _dsa(qit, wit, ki, qat, ka, vat, bias, g_out, bsz, seq):
    t = DSA_TILE
    nq = seq // t
    gw = GROUP_WIDTH
    topk = min(TOPK_MAX, seq // 4)
    q_fm = pl.BlockSpec((1, gw, t), lambda b, i: (b, 0, i))
    gt = jnp.broadcast_to(g_out.reshape(gw, 1), (gw, t))
    r = np.arange(t)
    lower = jnp.asarray(r[None, :] < r[:, None], dtype=BF16)
    return pl.pallas_call(
        functools.partial(_dsa_kernel, topk=topk, n_q_tiles=nq),
        grid=(bsz, nq),
        in_specs=[q_fm,
                  pl.BlockSpec((1, N_HEADS, t), lambda b, i: (b, 0, i)),
                  pl.BlockSpec((1, seq, IDX_DIM), lambda b, i: (b, 0, 0)),
                  q_fm,
                  pl.BlockSpec((1, N_HEADS, seq, HEAD_DIM), lambda b, i: (b, 0, 0, 0)),
                  pl.BlockSpec((1, gw, seq), lambda b, i: (b, 0, 0)),
                  _resident(bias.shape),
                  _resident((gw, t)),
                  _resident((t, t))],
        out_specs=pl.BlockSpec((1, t, gw), lambda b, i: (b, i, 0)),
        out_shape=jax.ShapeDtypeStruct((bsz, seq, gw), BF16),
        scratch_shapes=[pltpu.VMEM((nq, t, t), I32), pltpu.VMEM((nq, t, t), I16),
                        pltpu.VMEM((nq, t, t), I16), pltpu.VMEM((nq, t, t), F32),
                        pltpu.VMEM((nq, t, t), F32), pltpu.VMEM((gw, t), F32)],
        compiler_params=_params(2),
        name="dsa",
    )(qit, wit, ki, qat, ka, vat, bias, gt, lower)


def _log1m_sigmoid(z):
    return -(jnp.maximum(z, 0.0) + jnp.log(1.0 + jnp.exp(-jnp.abs(z))))


def _split_bf16(x):
    hi = x.astype(BF16)
    lo = (x - hi.astype(F32)).astype(BF16)
    return jnp.concatenate([hi, lo], axis=1)


def _sb_kernel(q_ref, k_ref, v_ref, tri_ref, g_ref, o_ref, z_scr, lh_scr, cs_scr, r_scr, acc_scr):
    t = SB_TILE
    nw = SB_WINDOW
    i = pl.program_id(1)
    wb = jnp.maximum(i - (nw - 1), 0)
    wrows = pl.ds(pl.multiple_of(wb * t, t), nw * t)
    row = lax.broadcasted_iota(I32, (t, t), 0)
    col = lax.broadcasted_iota(I32, (t, t), 1)

    def causal(w):
        return (col - row) < (i - wb - w) * t

    for hd in range(N_HEADS):
        z = _dot_nt(q_ref[0, hd], k_ref[0, hd, wrows, :])
        z_scr[hd] = z
        for w in range(nw):
            lg = jnp.where(causal(w), _log1m_sigmoid(z[:, w * t:(w + 1) * t]), 0.0)
            lh_scr[(hd * nw + w) * t:(hd * nw + w + 1) * t, :] = _split_bf16(lg)
    cs_scr[...] = _dot(lh_scr[...], tri_ref[...])
    rmax = None
    for hd in range(N_HEADS):
        after = None
        a_blocks = [None] * nw
        for w in reversed(range(nw)):
            cs = cs_scr[(hd * nw + w) * t:(hd * nw + w + 1) * t, :]
            c = cs[:, :t] if after is None else cs[:, :t] + after
            a = jnp.where(causal(w), jnp.exp(z_scr[hd, :, w * t:(w + 1) * t] + c), 0.0)
            a_blocks[w] = a.astype(BF16)
            after = cs[:, t:] if after is None else after + cs[:, t:]
        acc_scr[hd] = _dot(jnp.concatenate(a_blocks, axis=1), v_ref[0, hd, wrows, :])
        r_scr[hd] = after
        rmax = after if rmax is None else jnp.maximum(rmax, after)

    def far_block(kb):
        rows = pl.ds(pl.multiple_of(kb * t, t), t)
        rmax = None
        for hd in range(N_HEADS):
            z = _dot_nt(q_ref[0, hd], k_ref[0, hd, rows, :])
            cs = _dot(_split_bf16(_log1m_sigmoid(z)), tri_ref[...])
            r_old = r_scr[hd]
            a = jnp.exp(z + cs[:, :t] + r_old)
            acc_scr[hd] += _dot(a.astype(BF16), v_ref[0, hd, rows, :])
            r_new = r_old + cs[:, t:]
            r_scr[hd] = r_new
            rmax = r_new if rmax is None else jnp.maximum(rmax, r_new)
        return jnp.max(rmax)

    def cond(c):
        kb, rmax = c
        return jnp.logical_and(kb >= 0, rmax > SB_DEAD)

    def body(c):
        kb, _ = c
        return kb - 1, far_block(kb)

    lax.while_loop(cond, body, (wb - 1, jnp.max(rmax)))
    o = jnp.concatenate([acc_scr[hd] for hd in range(N_HEADS)], axis=1)
    o_ref[0] = _rms(o, g_ref[...]).astype(BF16)


def _sb(q, k, v, g_out, bsz, seq):
    t = SB_TILE
    nq = seq // t
    assert seq >= SB_WINDOW * t
    j = np.arange(t)[:, None]
    s = np.arange(t)[None, :]
    tri = np.concatenate([(j >= s), np.ones((t, t), bool)], axis=1).astype(np.float32)
    tri = np.concatenate([tri, tri], axis=0)
    q_hm = pl.BlockSpec((1, N_HEADS, t, HEAD_DIM), lambda b, i: (b, 0, i, 0))
    kv_hm = pl.BlockSpec((1, N_HEADS, seq, HEAD_DIM), lambda b, i: (b, 0, 0, 0))
    n_rows = N_HEADS * SB_WINDOW * t
    return pl.pallas_call(
        _sb_kernel,
        grid=(bsz, nq),
        in_specs=[q_hm, kv_hm, kv_hm, _resident((2 * t, 2 * t)),
                  pl.BlockSpec((1, GROUP_WIDTH), lambda b, i: (0, 0))],
        out_specs=pl.BlockSpec((1, t, GROUP_WIDTH), lambda b, i: (b, i, 0)),
        out_shape=jax.ShapeDtypeStruct((bsz, seq, GROUP_WIDTH), BF16),
        scratch_shapes=[pltpu.VMEM((N_HEADS, t, SB_WINDOW * t), F32),
                        pltpu.VMEM((n_rows, 2 * t), BF16),
                        pltpu.VMEM((n_rows, 2 * t), F32),
                        pltpu.VMEM((N_HEADS, t, t), F32),
                        pltpu.VMEM((N_HEADS, t, HEAD_DIM), F32)],
        compiler_params=_params(2),
        name="sb",
    )(q, k, v, jnp.asarray(tri, dtype=BF16), g_out)


def kernel(x, c, w_ada, b_ada, g_pre, g_post, w_ffn_gate, w_ffn_up, w_ffn_down, w_in, g_kidx,
           rel_bias, g_out_a, g_out_b, w_out):
    bsz, seq, d = x.shape
    depth = w_ada.shape[0]
    gw = GROUP_WIDTH
    h = x.reshape(bsz * seq, d)
    bias = _bias_tiles(rel_bias, DSA_TILE)
    for l in range(depth):
        mods = _ada(c, w_ada[l], b_ada[l]).reshape(bsz, -1, d)
        ffn_w = [(w_ffn_gate[l, j].astype(BF16), w_ffn_up[l, j].astype(BF16),
                  w_ffn_down[l, j].astype(BF16)) for j in range(2)]
        w = w_in[l]
        o_ki = 4 * gw
        o_wi = o_ki + IDX_DIM
        o_b = o_wi + N_HEADS
        wt = jnp.concatenate([w[:, :gw], w[:, 2 * gw:3 * gw], w[:, 3 * gw:o_ki]], axis=1).T.astype(BF16)
        wka = w[:, gw:2 * gw].astype(BF16)
        wki = w[:, o_ki:o_wi].astype(BF16)
        wwit = jnp.pad(w[:, o_wi:o_b].T, ((0, 16 - N_HEADS), (0, 0))).astype(BF16)
        wb = w[:, o_b:].astype(BF16)
        woa = w_out[l, :gw].astype(BF16)
        wob = w_out[l, gw:].astype(BF16)
        vec = lambda g: g.reshape(1, -1)

        h1 = _ffn(h, mods, vec(g_pre[l, 0]), vec(g_post[l, 0]), *ffn_w[0], 0, seq)
        qat, ka, vat, qit, ki, wit, qb, kb, vb = _proj(h1, mods, vec(g_pre[l, 1]), vec(g_kidx[l]),
                                                       wt, wka, wki, wwit, wb, bsz, seq)
        oa = _dsa(qit, wit, ki, qat, ka, vat, bias, g_out_a[l], bsz, seq)
        ob = _sb(qb, kb, vb, vec(g_out_b[l]), bsz, seq)
        h = _ffn(h1, mods, vec(g_pre[l, 2]), vec(g_post[l, 2]), *ffn_w[1], 2, seq,
                 mix=(oa.reshape(bsz * seq, gw), ob.reshape(bsz * seq, gw), woa, wob,
                      vec(g_post[l, 1])))
    return h.reshape(bsz, seq, d)
```

```python
import functools
import math

import numpy as np
import jax
import jax.numpy as jnp
from jax import lax
from jax.experimental import pallas as pl
from jax.experimental.pallas import tpu as pltpu

F32 = jnp.float32
BF16 = jnp.bfloat16
I32 = jnp.int32
I16 = jnp.int16

EPS = 1e-6
HEAD_DIM = 64
N_HEADS = 8
GROUP_WIDTH = N_HEADS * HEAD_DIM
IDX_DIM = 64
TOPK_MAX = 256
N_BUCKETS = 32
MAX_DISTANCE = 128
SUBLANES = 8
FFN_CHUNK = 256
TOKEN_TILE = 512
DSA_TILE = 256
SCORE_ROWS = 64
SB_TILE = 128
SB_WINDOW = 3
LOG2E = math.log2(math.e)
NEG_BIG = -1e30
SB_DEAD = -104.0
VMEM_LIMIT = 56 * 1024 * 1024
INT_MIN = np.int32(-2 ** 31)
HALF = 2 ** 15
ONE16, ZERO16, MIN16 = np.int16(1), np.int16(0), np.int16(-HALF)


def _dot(a, b):
    return jnp.dot(a, b, preferred_element_type=F32)


def _dot_nt(a, b):
    return lax.dot_general(a, b, (((1,), (1,)), ((), ())), preferred_element_type=F32)


def _rms(x, g):
    return x * lax.rsqrt(jnp.mean(x * x, axis=-1, keepdims=True) + EPS) * g


def _resident(shape):
    n = len(shape)
    return pl.BlockSpec(shape, lambda *_: (0,) * n, pipeline_mode=pl.Buffered(1))


def _params(n_grid):
    return pltpu.CompilerParams(dimension_semantics=("arbitrary",) * n_grid,
                                vmem_limit_bytes=VMEM_LIMIT)


def _ada_kernel(c_ref, w_ref, b_ref, o_ref):
    c = c_ref[...]
    a = c * jax.nn.sigmoid(c)
    w = w_ref[...]
    a_hi = a.astype(BF16)
    a_lo = (a - a_hi.astype(F32)).astype(BF16)
    w_hi = w.astype(BF16)
    w_lo = (w - w_hi.astype(F32)).astype(BF16)
    o_ref[...] = _dot(a_hi, w_hi) + (_dot(a_hi, w_lo) + _dot(a_lo, w_hi)) + b_ref[...]


def _ada(c, w, b):
    bsz, d = c.shape
    n = w.shape[1]
    tn = 1152 if n % 1152 == 0 else n
    return pl.pallas_call(
        _ada_kernel,
        grid=(n // tn,),
        in_specs=[pl.BlockSpec((bsz, d), lambda j: (0, 0)),
                  pl.BlockSpec((d, tn), lambda j: (0, j)),
                  pl.BlockSpec((1, tn), lambda j: (0, j))],
        out_specs=pl.BlockSpec((bsz, tn), lambda j: (0, j)),
        out_shape=jax.ShapeDtypeStruct((bsz, n), F32),
        compiler_params=_params(1),
        name="ada",
    )(c, w, b.reshape(1, n))


def _ffn_kernel(*refs, sub, mix):
    if mix:
        (h_ref, oa_ref, ob_ref, woa_ref, wob_ref, gpm_ref, mods_ref, gpre_ref, gpost_ref,
         wg_ref, wu_ref, wd_ref, o_ref, n_scr, a_scr) = refs
    else:
        (h_ref, mods_ref, gpre_ref, gpost_ref, wg_ref, wu_ref, wd_ref, o_ref, n_scr, a_scr) = refs
    x = h_ref[...]
    if mix:
        o = _dot(oa_ref[...], woa_ref[...]) + _dot(ob_ref[...], wob_ref[...])
        x = x + mods_ref[0, 5:6, :] * _rms(o, gpm_ref[...])
    shift = mods_ref[0, 3 * sub:3 * sub + 1, :]
    scale = mods_ref[0, 3 * sub + 1:3 * sub + 2, :]
    gate = mods_ref[0, 3 * sub + 2:3 * sub + 3, :]
    n = _rms(x, gpre_ref[...]) * (1.0 + scale) + shift
    n_scr[...] = n.astype(BF16)
    fc = FFN_CHUNK
    for c0 in range(0, wg_ref.shape[1], fc):
        g = _dot(n_scr[...], wg_ref[:, c0:c0 + fc])
        u = _dot(n_scr[...], wu_ref[:, c0:c0 + fc])
        a_scr[:, c0:c0 + fc] = (g * jax.nn.sigmoid(g) * u).astype(BF16)
    f = _dot(a_scr[...], wd_ref[...])
    o_ref[...] = x + 0.5 * gate * _rms(f, gpost_ref[...])


def _ffn(h, mods, g_pre, g_post, wg, wu, wd, sub, seq, mix=None):
    n_tok, d = h.shape
    tm = min(TOKEN_TILE, seq)
    tiles_per_batch = seq // tm
    row = pl.BlockSpec((tm, d), lambda i: (i, 0))
    vec = pl.BlockSpec((1, d), lambda i: (0, 0))
    mod_spec = pl.BlockSpec((1, mods.shape[1], d), lambda i: (i // tiles_per_batch, 0, 0))
    args, specs = [h], [row]
    if mix is not None:
        oa, ob, woa, wob, g_post_mix = mix
        gw = oa.shape[1]
        half = pl.BlockSpec((tm, gw), lambda i: (i, 0))
        args += [oa, ob, woa, wob, g_post_mix]
        specs += [half, half, _resident(woa.shape), _resident(wob.shape), vec]
    args += [mods, g_pre, g_post, wg, wu, wd]
    specs += [mod_spec, vec, vec, _resident(wg.shape), _resident(wu.shape), _resident(wd.shape)]
    return pl.pallas_call(
        functools.partial(_ffn_kernel, sub=sub, mix=mix is not None),
        grid=(n_tok // tm,),
        in_specs=specs,
        out_specs=row,
        out_shape=jax.ShapeDtypeStruct((n_tok, d), F32),
        scratch_shapes=[pltpu.VMEM((tm, d), BF16), pltpu.VMEM((tm, wd.shape[0]), BF16)],
        compiler_params=_params(1),
        name="ffn_mix" if mix is not None else "ffn",
    )(*args)


def _proj_kernel(h_ref, mods_ref, gpre_ref, gk_ref, wt_ref, wka_ref, wki_ref, wwit_ref, wb_ref,
                 qat_ref, ka_ref, vat_ref, qit_ref, ki_ref, wit_ref, qb_ref, kb_ref, vb_ref, n_scr):
    x = h_ref[...]
    shift = mods_ref[0, 3:4, :]
    scale = mods_ref[0, 4:5, :]
    n_scr[...] = (_rms(x, gpre_ref[...]) * (1.0 + scale) + shift).astype(BF16)
    gw = GROUP_WIDTH
    qscale = HEAD_DIM ** -0.5

    def heads(w, out_ref, mul):
        p = _dot(n_scr[...], w)
        if mul != 1.0:
            p = p * mul
        for hd in range(N_HEADS):
            out_ref[0, hd] = p[:, hd * HEAD_DIM:(hd + 1) * HEAD_DIM].astype(BF16)

    def feature_major(col, out_ref, mul):
        p = _dot_nt(wt_ref[col * gw:(col + 1) * gw, :], n_scr[...])
        if mul != 1.0:
            p = p * mul
        out_ref[0] = p.astype(BF16)

    feature_major(0, qat_ref, qscale * LOG2E)
    feature_major(1, vat_ref, 1.0)
    feature_major(2, qit_ref, 1.0)
    heads(wka_ref[...], ka_ref, 1.0)
    heads(wb_ref[:, :gw], qb_ref, qscale)
    heads(wb_ref[:, gw:2 * gw], kb_ref, 1.0)
    heads(wb_ref[:, 2 * gw:], vb_ref, 1.0)
    ki = _dot(n_scr[...], wki_ref[...])
    ki_ref[0] = _rms(ki, gk_ref[...]).astype(BF16)
    wit = _dot_nt(wwit_ref[...], n_scr[...])
    wit_ref[0] = wit[:N_HEADS, :] * ((N_HEADS * IDX_DIM) ** -0.5)


def _proj(h1, mods, g_pre, g_kidx, wt, wka, wki, wwit, wb, bsz, seq):
    n_tok, d = h1.shape
    tm = min(TOKEN_TILE, seq)
    tpb = seq // tm
    gw = GROUP_WIDTH
    row = pl.BlockSpec((tm, d), lambda i: (i, 0))
    vec = pl.BlockSpec((1, d), lambda i: (0, 0))
    mod_spec = pl.BlockSpec((1, mods.shape[1], d), lambda i: (i // tpb, 0, 0))
    hm = pl.BlockSpec((1, N_HEADS, tm, HEAD_DIM), lambda i: (i // tpb, 0, i % tpb, 0))
    hm_shape = jax.ShapeDtypeStruct((bsz, N_HEADS, seq, HEAD_DIM), BF16)
    fm = pl.BlockSpec((1, gw, tm), lambda i: (i // tpb, 0, i % tpb))
    fm_shape = jax.ShapeDtypeStruct((bsz, gw, seq), BF16)
    ki_spec = pl.BlockSpec((1, tm, IDX_DIM), lambda i: (i // tpb, i % tpb, 0))
    wit_spec = pl.BlockSpec((1, N_HEADS, tm), lambda i: (i // tpb, 0, i % tpb))
    return pl.pallas_call(
        _proj_kernel,
        grid=(n_tok // tm,),
        in_specs=[row, mod_spec, vec, pl.BlockSpec((1, IDX_DIM), lambda i: (0, 0)),
                  _resident(wt.shape), _resident(wka.shape), _resident(wki.shape),
                  _resident(wwit.shape), _resident(wb.shape)],
        out_specs=[fm, hm, fm, fm, ki_spec, wit_spec, hm, hm, hm],
        out_shape=[fm_shape, hm_shape, fm_shape, fm_shape,
                   jax.ShapeDtypeStruct((bsz, seq, IDX_DIM), BF16),
                   jax.ShapeDtypeStruct((bsz, N_HEADS, seq), F32),
                   hm_shape, hm_shape, hm_shape],
        scratch_shapes=[pltpu.VMEM((tm, d), BF16)],
        compiler_params=_params(1),
        name="proj",
    )(h1, mods, g_pre, g_kidx, wt, wka, wki, wwit, wb)


def _t5_bucket_np(n):
    n = np.maximum(n, 0)
    max_exact = N_BUCKETS // 2
    nf = np.maximum(n, 1).astype(np.float32)
    large = max_exact + (np.log(nf / np.float32(max_exact)) / np.float32(math.log(MAX_DISTANCE / max_exact))
                         * np.float32(N_BUCKETS - max_exact)).astype(np.int32)
    large = np.minimum(large, N_BUCKETS - 1)
    return np.where(n < max_exact, n, large).astype(np.int32)


def _bias_kernel(relb_ref, bucket_ref, o_ref):
    hd = pl.program_id(0)
    far = relb_ref[N_BUCKETS - 1, hd]
    for r in range(2):
        bk = bucket_ref[r]
        acc = jnp.zeros(bk.shape, F32)
        for k in range(N_BUCKETS - 1):
            acc = jnp.where(bk == k, (relb_ref[k, hd] - far) * LOG2E, acc)
        o_ref[0, r] = acc


def _bias_tiles(rel_bias, t):
    assert t >= MAX_DISTANCE
    s = np.arange(t)[:, None]
    q = np.arange(t)[None, :]
    buckets = np.stack([_t5_bucket_np(q - s), _t5_bucket_np(t + q - s)]).astype(np.int32)
    return pl.pallas_call(
        _bias_kernel,
        grid=(N_HEADS,),
        in_specs=[pl.BlockSpec(memory_space=pltpu.SMEM),
                  pl.BlockSpec((2, t, t), lambda hd: (0, 0, 0))],
        out_specs=pl.BlockSpec((1, 2, t, t), lambda hd: (hd, 0, 0, 0)),
        out_shape=jax.ShapeDtypeStruct((N_HEADS, 2, t, t), F32),
        compiler_params=_params(1),
        name="bias_tiles",
    )(rel_bias, jnp.asarray(buckets))


def _tree(op, xs):
    xs = list(xs)
    while len(xs) > 1:
        xs = [op(xs[j], xs[j + 1]) if j + 1 < len(xs) else xs[j] for j in range(0, len(xs), 2)]
    return xs[0]


def _row_groups(x):
    return [x[r:r + SUBLANES, :] for r in range(0, x.shape[0], SUBLANES)]


def _all_sublanes(op, x):
    for shift in (4, 2, 1):
        x = op(x, pltpu.roll(x, shift, 0))
    return x


def _dsa_kernel(qit_ref, wit_ref, ki_ref, qat_ref, ka_ref, vat_ref, bias_ref, gt_ref, tri_ref, o_ref,
                key_scr, hi_scr, lo_scr, selm_scr, logit_scr, out_scr, *, topk, n_q_tiles):
    t = DSA_TILE
    kf = float(topk)
    krow = lax.broadcasted_iota(I32, (t, t), 0)
    qcol = lax.broadcasted_iota(I32, (t, t), 1)
    causal = krow <= qcol

    i = pl.program_id(1)

    def score_block(kb, on_diagonal):
        wt = wit_ref[0]
        for r0 in range(0, t, SCORE_ROWS):
            kblk = ki_ref[0, pl.ds(pl.multiple_of(kb * t + r0, SCORE_ROWS), SCORE_ROWS), :]
            acc = None
            for hd in range(N_HEADS):
                sc = _dot(kblk, qit_ref[0, hd * IDX_DIM:(hd + 1) * IDX_DIM, :])
                term = jnp.maximum(sc, 0.0) * wt[hd:hd + 1, :]
                acc = term if acc is None else acc + term
            acc = acc + 0.0
            if on_diagonal:
                acc = jnp.where(causal[r0:r0 + SCORE_ROWS, :], acc, -jnp.inf)
            bits = pltpu.bitcast(acc, I32)
            key = bits ^ ((bits >> 31) & jnp.int32(0x7FFFFFFF))
            key_scr[kb, r0:r0 + SCORE_ROWS, :] = key
            hi_scr[kb, r0:r0 + SCORE_ROWS, :] = (key >> 16).astype(I16)

    def score_far(kb, c):
        score_block(kb, False)
        return c

    lax.fori_loop(0, i, score_far, 0)
    score_block(i, True)

    def tile_work(nkb):
        diag = nkb - 1

        def packed(x):
            return jnp.concatenate([x, x], axis=0).astype(I16)

        def column_total(parts):
            tot = _tree(jnp.add, parts).astype(F32)
            return _all_sublanes(jnp.add, tot[:SUBLANES, :] + tot[SUBLANES:, :])

        def search16(src_scr, need):
            def step(j, ans):
                cand_off = ans | lax.shift_left(jnp.int32(1), 15 - j)
                cand = packed(cand_off - HALF)
                accs = [jnp.zeros((2 * SUBLANES, t), I16) for _ in range(4)]
                n = 0
                for kb in range(nkb):
                    blk = src_scr[kb]
                    for r in range(0, t, 2 * SUBLANES):
                        hit = jnp.where(blk[r:r + 2 * SUBLANES, :] >= cand, ONE16, ZERO16)
                        accs[n % 4] = accs[n % 4] + hit
                        n += 1
                return jnp.where(column_total(accs) >= need, cand_off, ans)
            return lax.fori_loop(0, 16, step, jnp.zeros((SUBLANES, t), I32))

        if nkb * t <= topk:
            thr = jnp.full((SUBLANES, t), INT_MIN, I32)
        else:
            t_hi = search16(hi_scr, kf) - HALF
            t_hi16 = packed(t_hi)
            above = [jnp.zeros((2 * SUBLANES, t), I16) for _ in range(4)]
            n = 0
            for kb in range(nkb):
                for r in range(0, t, 2 * SUBLANES):
                    rows = slice(r, r + 2 * SUBLANES)
                    h = hi_scr[kb, rows, :]
                    lo = ((key_scr[kb, rows, :] & jnp.int32(0xFFFF)) - HALF).astype(I16)
                    lo_scr[kb, rows, :] = jnp.where(h == t_hi16, lo, MIN16)
                    above[n % 4] = above[n % 4] + jnp.where(h > t_hi16, ONE16, ZERO16)
                    n += 1
            t_lo = search16(lo_scr, kf - column_total(above))
            thr = lax.shift_left(t_hi, 16) | t_lo

        n_ge = jnp.zeros((SUBLANES, t), F32)
        for kb in range(nkb):
            blk = key_scr[kb]
            for r, grp in enumerate(_row_groups(blk)):
                rows = slice(r * SUBLANES, (r + 1) * SUBLANES)
                sel = grp >= thr
                n_ge = n_ge + jnp.where(sel, 1.0, 0.0)
                m = jnp.where(sel, 0.0, NEG_BIG)
                if kb == diag:
                    m = jnp.where(causal[rows, :], m, NEG_BIG)
                selm_scr[kb, rows, :] = m
        n_ge = _all_sublanes(jnp.add, n_ge)
        excess = jnp.max(jnp.where(n_ge > kf, 1.0, 0.0))

        @pl.when(excess > 0.0)
        def _():
            thr1 = thr[0:1, :]

            def tied(kb):
                return jnp.where(key_scr[kb] == thr1, 1.0, 0.0)

            def count_tied(kb, acc):
                return acc + jnp.sum(tied(kb), axis=0, keepdims=True)

            n_tied = lax.fori_loop(0, nkb, count_tied, jnp.zeros((1, t), F32))
            need = kf - (n_ge[0:1, :] - n_tied)

            def rewrite(kb, before):
                e = tied(kb)
                rank = _dot(tri_ref[...], e.astype(BF16)) + before
                k = key_scr[kb]
                keep = jnp.where(k > thr1, 0.0,
                                 jnp.where(k == thr1, jnp.where(rank < need, 0.0, NEG_BIG), NEG_BIG))
                selm_scr[kb] = jnp.where(selm_scr[kb] < 0.0, NEG_BIG, keep)
                return before + jnp.sum(e, axis=0, keepdims=True)

            lax.fori_loop(0, nkb, rewrite, jnp.zeros((1, t), F32))

        def head(hd, c):
            hrows = pl.ds(pl.multiple_of(hd * HEAD_DIM, HEAD_DIM), HEAD_DIM)
            qt = qat_ref[0, hrows, :]
            maxes = []
            for kb in range(nkb):
                lg = _dot(ka_ref[0, hd, kb * t:(kb + 1) * t, :], qt) + selm_scr[kb]
                if kb >= nkb - 2:
                    lg = lg + bias_ref[hd, diag - kb]
                logit_scr[kb] = lg
                maxes.append(jnp.max(lg, axis=0, keepdims=True))
            m = _tree(jnp.maximum, maxes)
            ssum = None
            acc = None
            for kb in range(nkb):
                p = jnp.exp2(logit_scr[kb] - m)
                ps = jnp.sum(p, axis=0, keepdims=True)
                pv = _dot(vat_ref[0, hrows, kb * t:(kb + 1) * t], p.astype(BF16))
                ssum = ps if ssum is None else ssum + ps
                acc = pv if acc is None else acc + pv
            out_scr[hrows, :] = acc * (1.0 / ssum)
            return c

        lax.fori_loop(0, N_HEADS, head, 0, unroll=4)

    for k in range(n_q_tiles):
        pl.when(i == k)(functools.partial(tile_work, k + 1))

    o = out_scr[...]
    o = o * lax.rsqrt(jnp.mean(o * o, axis=0, keepdims=True) + EPS) * gt_ref[...]
    o_ref[0] = o.T.astype(BF16)


def _dsa(qit, wit, ki, qat, ka, vat, bias, g_out, bsz, seq):
    t = DSA_TILE
    nq = seq // t
    gw = GROUP_WIDTH
    topk = min(TOPK_MAX, seq // 4)
    q_fm = pl.BlockSpec((1, gw, t), lambda b, i: (b, 0, i))
    gt = jnp.broadcast_to(g_out.reshape(gw, 1), (gw, t))
    r = np.arange(t)
    lower = jnp.asarray(r[None, :] < r[:, None], dtype=BF16)
    return pl.pallas_call(
        functools.partial(_dsa_kernel, topk=topk, n_q_tiles=nq),
        grid=(bsz, nq),
        in_specs=[q_fm,
                  pl.BlockSpec((1, N_HEADS, t), lambda b, i: (b, 0, i)),
                  pl.BlockSpec((1, seq, IDX_DIM), lambda b, i: (b, 0, 0)),
                  q_fm,
                  pl.BlockSpec((1, N_HEADS, seq, HEAD_DIM), lambda b, i: (b, 0, 0, 0)),
                  pl.BlockSpec((1, gw, seq), lambda b, i: (b, 0, 0)),
                  _resident(bias.shape),
                  _resident((gw, t)),
                  _resident((t, t))],
        out_specs=pl.BlockSpec((1, t, gw), lambda b, i: (b, i, 0)),
        out_shape=jax.ShapeDtypeStruct((bsz, seq, gw), BF16),
        scratch_shapes=[pltpu.VMEM((nq, t, t), I32), pltpu.VMEM((nq, t, t), I16),
                        pltpu.VMEM((nq, t, t), I16), pltpu.VMEM((nq, t, t), F32),
                        pltpu.VMEM((nq, t, t), F32), pltpu.VMEM((gw, t), F32)],
        compiler_params=_params(2),
        name="dsa",
    )(qit, wit, ki, qat, ka, vat, bias, gt, lower)


def _log1m_sigmoid(z):
    return -(jnp.maximum(z, 0.0) + jnp.log(1.0 + jnp.exp(-jnp.abs(z))))


def _split_bf16(x):
    hi = x.astype(BF16)
    lo = (x - hi.astype(F32)).astype(BF16)
    return jnp.concatenate([hi, lo], axis=1)


def _sb_kernel(q_ref, k_ref, v_ref, tri_ref, g_ref, o_ref, z_scr, lh_scr, cs_scr, r_scr, acc_scr):
    t = SB_TILE
    nw = SB_WINDOW
    i = pl.program_id(1)
    wb = jnp.maximum(i - (nw - 1), 0)
    wrows = pl.ds(pl.multiple_of(wb * t, t), nw * t)
    row = lax.broadcasted_iota(I32, (t, t), 0)
    col = lax.broadcasted_iota(I32, (t, t), 1)

    def causal(w):
        return (col - row) < (i - wb - w) * t

    for hd in range(N_HEADS):
        z = _dot_nt(q_ref[0, hd], k_ref[0, hd, wrows, :])
        z_scr[hd] = z
        for w in range(nw):
            lg = jnp.where(causal(w), _log1m_sigmoid(z[:, w * t:(w + 1) * t]), 0.0)
            lh_scr[(hd * nw + w) * t:(hd * nw + w + 1) * t, :] = _split_bf16(lg)
    cs_scr[...] = _dot(lh_scr[...], tri_ref[...])
    rmax = None
    for hd in range(N_HEADS):
        after = None
        a_blocks = [None] * nw
        for w in reversed(range(nw)):
            cs = cs_scr[(hd * nw + w) * t:(hd * nw + w + 1) * t, :]
            c = cs[:, :t] if after is None else cs[:, :t] + after
            a = jnp.where(causal(w), jnp.exp(z_scr[hd, :, w * t:(w + 1) * t] + c), 0.0)
            a_blocks[w] = a.astype(BF16)
            after = cs[:, t:] if after is None else after + cs[:, t:]
        acc_scr[hd] = _dot(jnp.concatenate(a_blocks, axis=1), v_ref[0, hd, wrows, :])
        r_scr[hd] = after
        rmax = after if rmax is None else jnp.maximum(rmax, after)

    def far_block(kb):
        rows = pl.ds(pl.multiple_of(kb * t, t), t)
        rmax = None
        for hd in range(N_HEADS):
            z = _dot_nt(q_ref[0, hd], k_ref[0, hd, rows, :])
            cs = _dot(_split_bf16(_log1m_sigmoid(z)), tri_ref[...])
            r_old = r_scr[hd]
            a = jnp.exp(z + cs[:, :t] + r_old)
            acc_scr[hd] += _dot(a.astype(BF16), v_ref[0, hd, rows, :])
            r_new = r_old + cs[:, t:]
            r_scr[hd] = r_new
            rmax = r_new if rmax is None else jnp.maximum(rmax, r_new)
        return jnp.max(rmax)

    def cond(c):
        kb, rmax = c
        return jnp.logical_and(kb >= 0, rmax > SB_DEAD)

    def body(c):
        kb, _ = c
        return kb - 1, far_block(kb)

    lax.while_loop(cond, body, (wb - 1, jnp.max(rmax)))
    o = jnp.concatenate([acc_scr[hd] for hd in range(N_HEADS)], axis=1)
    o_ref[0] = _rms(o, g_ref[...]).astype(BF16)


def _sb(q, k, v, g_out, bsz, seq):
    t = SB_TILE
    nq = seq // t
    assert seq >= SB_WINDOW * t
    j = np.arange(t)[:, None]
    s = np.arange(t)[None, :]
    tri = np.concatenate([(j >= s), np.ones((t, t), bool)], axis=1).astype(np.float32)
    tri = np.concatenate([tri, tri], axis=0)
    q_hm = pl.BlockSpec((1, N_HEADS, t, HEAD_DIM), lambda b, i: (b, 0, i, 0))
    kv_hm = pl.BlockSpec((1, N_HEADS, seq, HEAD_DIM), lambda b, i: (b, 0, 0, 0))
    n_rows = N_HEADS * SB_WINDOW * t
    return pl.pallas_call(
        _sb_kernel,
        grid=(bsz, nq),
        in_specs=[q_hm, kv_hm, kv_hm, _resident((2 * t, 2 * t)),
                  pl.BlockSpec((1, GROUP_WIDTH), lambda b, i: (0, 0))],
        out_specs=pl.BlockSpec((1, t, GROUP_WIDTH), lambda b, i: (b, i, 0)),
        out_shape=jax.ShapeDtypeStruct((bsz, seq, GROUP_WIDTH), BF16),
        scratch_shapes=[pltpu.VMEM((N_HEADS, t, SB_WINDOW * t), F32),
                        pltpu.VMEM((n_rows, 2 * t), BF16),
                        pltpu.VMEM((n_rows, 2 * t), F32),
                        pltpu.VMEM((N_HEADS, t, t), F32),
                        pltpu.VMEM((N_HEADS, t, HEAD_DIM), F32)],
        compiler_params=_params(2),
        name="sb",
    )(q, k, v, jnp.asarray(tri, dtype=BF16), g_out)


def kernel(x, c, w_ada, b_ada, g_pre, g_post, w_ffn_gate, w_ffn_up, w_ffn_down, w_in, g_kidx,
           rel_bias, g_out_a, g_out_b, w_out):
    bsz, seq, d = x.shape
    depth = w_ada.shape[0]
    gw = GROUP_WIDTH
    h = x.reshape(bsz * seq, d)
    bias = _bias_tiles(rel_bias, DSA_TILE)
    for l in range(depth):
        mods = _ada(c, w_ada[l], b_ada[l]).reshape(bsz, -1, d)
        ffn_w = [(w_ffn_gate[l, j].astype(BF16), w_ffn_up[l, j].astype(BF16),
                  w_ffn_down[l, j].astype(BF16)) for j in range(2)]
        w = w_in[l]
        o_ki = 4 * gw
        o_wi = o_ki + IDX_DIM
        o_b = o_wi + N_HEADS
        wt = jnp.concatenate([w[:, :gw], w[:, 2 * gw:3 * gw], w[:, 3 * gw:o_ki]], axis=1).T.astype(BF16)
        wka = w[:, gw:2 * gw].astype(BF16)
        wki = w[:, o_ki:o_wi].astype(BF16)
        wwit = jnp.pad(w[:, o_wi:o_b].T, ((0, 16 - N_HEADS), (0, 0))).astype(BF16)
        wb = w[:, o_b:].astype(BF16)
        woa = w_out[l, :gw].astype(BF16)
        wob = w_out[l, gw:].astype(BF16)
        vec = lambda g: g.reshape(1, -1)

        h1 = _ffn(h, mods, vec(g_pre[l, 0]), vec(g_post[l, 0]), *ffn_w[0], 0, seq)
        qat, ka, vat, qit, ki, wit, qb, kb, vb = _proj(h1, mods, vec(g_pre[l, 1]), vec(g_kidx[l]),
                                                       wt, wka, wki, wwit, wb, bsz, seq)
        oa = _dsa(qit, wit, ki, qat, ka, vat, bias, g_out_a[l], bsz, seq)
        ob = _sb(qb, kb, vb, vec(g_out_b[l]), bsz, seq)
        h = _ffn(h1, mods, vec(g_pre[l, 2]), vec(g_post[l, 2]), *ffn_w[1], 2, seq,
                 mix=(oa.reshape(bsz * seq, gw), ob.reshape(bsz * seq, gw), woa, wob,
                      vec(g_post[l, 1])))
    return h.reshape(bsz, seq, d)
```

```python
import functools
import math

import numpy as np
import jax
import jax.numpy as jnp
from jax import lax
from jax.experimental import pallas as pl
from jax.experimental.pallas import tpu as pltpu

F32 = jnp.float32
BF16 = jnp.bfloat16
I32 = jnp.int32
I16 = jnp.int16

EPS = 1e-6
HEAD_DIM = 64
N_HEADS = 8
GROUP_WIDTH = N_HEADS * HEAD_DIM
IDX_DIM = 64
TOPK_MAX = 256
N_BUCKETS = 32
MAX_DISTANCE = 128
SUBLANES = 8
FFN_CHUNK = 256
TOKEN_TILE = 512
DSA_TILE = 256
SCORE_ROWS = 64
SB_TILE = 128
SB_WINDOW = 3
LOG2E = math.log2(math.e)
NEG_BIG = -1e30
SB_DEAD = 104.0 * LOG2E
VMEM_LIMIT = 56 * 1024 * 1024
INT_MIN = np.int32(-2 ** 31)
HALF = 2 ** 15
ONE16, ZERO16, MIN16 = np.int16(1), np.int16(0), np.int16(-HALF)


def _dot(a, b):
    return jnp.dot(a, b, preferred_element_type=F32)


def _dot_nt(a, b):
    return lax.dot_general(a, b, (((1,), (1,)), ((), ())), preferred_element_type=F32)


def _rms(x, g):
    return x * lax.rsqrt(jnp.mean(x * x, axis=-1, keepdims=True) + EPS) * g


def _resident(shape):
    n = len(shape)
    return pl.BlockSpec(shape, lambda *_: (0,) * n, pipeline_mode=pl.Buffered(1))


def _params(n_grid):
    return pltpu.CompilerParams(dimension_semantics=("arbitrary",) * n_grid,
                                vmem_limit_bytes=VMEM_LIMIT)


def _ada_kernel(c_ref, w_ref, b_ref, o_ref):
    c = c_ref[...]
    a = c * jax.nn.sigmoid(c)
    w = w_ref[...]
    a_hi = a.astype(BF16)
    a_lo = (a - a_hi.astype(F32)).astype(BF16)
    w_hi = w.astype(BF16)
    w_lo = (w - w_hi.astype(F32)).astype(BF16)
    o_ref[...] = _dot(a_hi, w_hi) + (_dot(a_hi, w_lo) + _dot(a_lo, w_hi)) + b_ref[...]


def _ada(c, w, b):
    bsz, d = c.shape
    n = w.shape[1]
    tn = 1152 if n % 1152 == 0 else n
    return pl.pallas_call(
        _ada_kernel,
        grid=(n // tn,),
        in_specs=[pl.BlockSpec((bsz, d), lambda j: (0, 0)),
                  pl.BlockSpec((d, tn), lambda j: (0, j)),
                  pl.BlockSpec((1, tn), lambda j: (0, j))],
        out_specs=pl.BlockSpec((bsz, tn), lambda j: (0, j)),
        out_shape=jax.ShapeDtypeStruct((bsz, n), F32),
        compiler_params=_params(1),
        name="ada",
    )(c, w, b.reshape(1, n))


def _ffn_kernel(*refs, sub, mix):
    if mix:
        (h_ref, oa_ref, ob_ref, woa_ref, wob_ref, gpm_ref, mods_ref, gpre_ref, gpost_ref,
         wg_ref, wu_ref, wd_ref, o_ref, n_scr, a_scr) = refs
    else:
        (h_ref, mods_ref, gpre_ref, gpost_ref, wg_ref, wu_ref, wd_ref, o_ref, n_scr, a_scr) = refs
    x = h_ref[...]
    if mix:
        o = _dot(oa_ref[...], woa_ref[...]) + _dot(ob_ref[...], wob_ref[...])
        x = x + mods_ref[0, 5:6, :] * _rms(o, gpm_ref[...])
    shift = mods_ref[0, 3 * sub:3 * sub + 1, :]
    scale = mods_ref[0, 3 * sub + 1:3 * sub + 2, :]
    gate = mods_ref[0, 3 * sub + 2:3 * sub + 3, :]
    n = _rms(x, gpre_ref[...]) * (1.0 + scale) + shift
    n_scr[...] = n.astype(BF16)
    fc = FFN_CHUNK
    for c0 in range(0, wg_ref.shape[1], fc):
        g = _dot(n_scr[...], wg_ref[:, c0:c0 + fc])
        u = _dot(n_scr[...], wu_ref[:, c0:c0 + fc])
        a_scr[:, c0:c0 + fc] = (g * jax.nn.sigmoid(g) * u).astype(BF16)
    f = _dot(a_scr[...], wd_ref[...])
    o_ref[...] = x + 0.5 * gate * _rms(f, gpost_ref[...])


def _ffn(h, mods, g_pre, g_post, wg, wu, wd, sub, seq, mix=None):
    n_tok, d = h.shape
    tm = min(TOKEN_TILE, seq)
    tiles_per_batch = seq // tm
    row = pl.BlockSpec((tm, d), lambda i: (i, 0))
    vec = pl.BlockSpec((1, d), lambda i: (0, 0))
    mod_spec = pl.BlockSpec((1, mods.shape[1], d), lambda i: (i // tiles_per_batch, 0, 0))
    args, specs = [h], [row]
    if mix is not None:
        oa, ob, woa, wob, g_post_mix = mix
        gw = oa.shape[1]
        half = pl.BlockSpec((tm, gw), lambda i: (i, 0))
        args += [oa, ob, woa, wob, g_post_mix]
        specs += [half, half, _resident(woa.shape), _resident(wob.shape), vec]
    args += [mods, g_pre, g_post, wg, wu, wd]
    specs += [mod_spec, vec, vec, _resident(wg.shape), _resident(wu.shape), _resident(wd.shape)]
    return pl.pallas_call(
        functools.partial(_ffn_kernel, sub=sub, mix=mix is not None),
        grid=(n_tok // tm,),
        in_specs=specs,
        out_specs=row,
        out_shape=jax.ShapeDtypeStruct((n_tok, d), F32),
        scratch_shapes=[pltpu.VMEM((tm, d), BF16), pltpu.VMEM((tm, wd.shape[0]), BF16)],
        compiler_params=_params(1),
        name="ffn_mix" if mix is not None else "ffn",
    )(*args)


def _proj_kernel(h_ref, mods_ref, gpre_ref, gk_ref, wt_ref, wka_ref, wki_ref, wwit_ref, wb_ref,
                 qat_ref, ka_ref, vat_ref, qit_ref, ki_ref, wit_ref, qb_ref, kb_ref, vb_ref, n_scr):
    x = h_ref[...]
    shift = mods_ref[0, 3:4, :]
    scale = mods_ref[0, 4:5, :]
    n_scr[...] = (_rms(x, gpre_ref[...]) * (1.0 + scale) + shift).astype(BF16)
    gw = GROUP_WIDTH
    qscale = HEAD_DIM ** -0.5

    def heads(w, out_ref, mul):
        p = _dot(n_scr[...], w)
        if mul != 1.0:
            p = p * mul
        for hd in range(N_HEADS):
            out_ref[0, hd] = p[:, hd * HEAD_DIM:(hd + 1) * HEAD_DIM].astype(BF16)

    def feature_major(col, out_ref, mul):
        p = _dot_nt(wt_ref[col * gw:(col + 1) * gw, :], n_scr[...])
        if mul != 1.0:
            p = p * mul
        out_ref[0] = p.astype(BF16)

    feature_major(0, qat_ref, qscale * LOG2E)
    feature_major(1, vat_ref, 1.0)
    feature_major(2, qit_ref, 1.0)
    heads(wka_ref[...], ka_ref, 1.0)
    heads(wb_ref[:, :gw], qb_ref, qscale * LOG2E)
    heads(wb_ref[:, gw:2 * gw], kb_ref, 1.0)
    heads(wb_ref[:, 2 * gw:], vb_ref, 1.0)
    ki = _dot(n_scr[...], wki_ref[...])
    ki_ref[0] = _rms(ki, gk_ref[...]).astype(BF16)
    wit = _dot_nt(wwit_ref[...], n_scr[...])
    wit_ref[0] = wit[:N_HEADS, :] * ((N_HEADS * IDX_DIM) ** -0.5)


def _proj(h1, mods, g_pre, g_kidx, wt, wka, wki, wwit, wb, bsz, seq):
    n_tok, d = h1.shape
    tm = min(TOKEN_TILE, seq)
    tpb = seq // tm
    gw = GROUP_WIDTH
    row = pl.BlockSpec((tm, d), lambda i: (i, 0))
    vec = pl.BlockSpec((1, d), lambda i: (0, 0))
    mod_spec = pl.BlockSpec((1, mods.shape[1], d), lambda i: (i // tpb, 0, 0))
    hm = pl.BlockSpec((1, N_HEADS, tm, HEAD_DIM), lambda i: (i // tpb, 0, i % tpb, 0))
    hm_shape = jax.ShapeDtypeStruct((bsz, N_HEADS, seq, HEAD_DIM), BF16)
    fm = pl.BlockSpec((1, gw, tm), lambda i: (i // tpb, 0, i % tpb))
    fm_shape = jax.ShapeDtypeStruct((bsz, gw, seq), BF16)
    ki_spec = pl.BlockSpec((1, tm, IDX_DIM), lambda i: (i // tpb, i % tpb, 0))
    wit_spec = pl.BlockSpec((1, N_HEADS, tm), lambda i: (i // tpb, 0, i % tpb))
    return pl.pallas_call(
        _proj_kernel,
        grid=(n_tok // tm,),
        in_specs=[row, mod_spec, vec, pl.BlockSpec((1, IDX_DIM), lambda i: (0, 0)),
                  _resident(wt.shape), _resident(wka.shape), _resident(wki.shape),
                  _resident(wwit.shape), _resident(wb.shape)],
        out_specs=[fm, hm, fm, fm, ki_spec, wit_spec, hm, hm, hm],
        out_shape=[fm_shape, hm_shape, fm_shape, fm_shape,
                   jax.ShapeDtypeStruct((bsz, seq, IDX_DIM), BF16),
                   jax.ShapeDtypeStruct((bsz, N_HEADS, seq), F32),
                   hm_shape, hm_shape, hm_shape],
        scratch_shapes=[pltpu.VMEM((tm, d), BF16)],
        compiler_params=_params(1),
        name="proj",
    )(h1, mods, g_pre, g_kidx, wt, wka, wki, wwit, wb)


def _t5_bucket_np(n):
    n = np.maximum(n, 0)
    max_exact = N_BUCKETS // 2
    nf = np.maximum(n, 1).astype(np.float32)
    large = max_exact + (np.log(nf / np.float32(max_exact)) / np.float32(math.log(MAX_DISTANCE / max_exact))
                         * np.float32(N_BUCKETS - max_exact)).astype(np.int32)
    large = np.minimum(large, N_BUCKETS - 1)
    return np.where(n < max_exact, n, large).astype(np.int32)


def _bias_kernel(relb_ref, bucket_ref, o_ref):
    hd = pl.program_id(0)
    far = relb_ref[N_BUCKETS - 1, hd]
    for r in range(2):
        bk = bucket_ref[r]
        acc = jnp.zeros(bk.shape, F32)
        for k in range(N_BUCKETS - 1):
            acc = jnp.where(bk == k, (relb_ref[k, hd] - far) * LOG2E, acc)
        o_ref[0, r] = acc


def _bias_tiles(rel_bias, t):
    assert t >= MAX_DISTANCE
    s = np.arange(t)[:, None]
    q = np.arange(t)[None, :]
    buckets = np.stack([_t5_bucket_np(q - s), _t5_bucket_np(t + q - s)]).astype(np.int32)
    return pl.pallas_call(
        _bias_kernel,
        grid=(N_HEADS,),
        in_specs=[pl.BlockSpec(memory_space=pltpu.SMEM),
                  pl.BlockSpec((2, t, t), lambda hd: (0, 0, 0))],
        out_specs=pl.BlockSpec((1, 2, t, t), lambda hd: (hd, 0, 0, 0)),
        out_shape=jax.ShapeDtypeStruct((N_HEADS, 2, t, t), F32),
        compiler_params=_params(1),
        name="bias_tiles",
    )(rel_bias, jnp.asarray(buckets))


def _tree(op, xs):
    xs = list(xs)
    while len(xs) > 1:
        xs = [op(xs[j], xs[j + 1]) if j + 1 < len(xs) else xs[j] for j in range(0, len(xs), 2)]
    return xs[0]


def _row_groups(x):
    return [x[r:r + SUBLANES, :] for r in range(0, x.shape[0], SUBLANES)]


def _all_sublanes(op, x):
    for shift in (4, 2, 1):
        x = op(x, pltpu.roll(x, shift, 0))
    return x


def _dsa_kernel(qit_ref, wit_ref, ki_ref, qat_ref, ka_ref, vat_ref, bias_ref, gt_ref, tri_ref, o_ref,
                key_scr, hi_scr, lo_scr, selm_scr, logit_a, logit_b, out_scr, *, topk, n_q_tiles):
    t = DSA_TILE
    kf = float(topk)
    krow = lax.broadcasted_iota(I32, (t, t), 0)
    qcol = lax.broadcasted_iota(I32, (t, t), 1)
    causal = krow <= qcol

    def tile_work(nkb):
        diag = nkb - 1

        wt = wit_ref[0]
        for kb in range(nkb):
            for r0 in range(0, t, SCORE_ROWS):
                kblk = ki_ref[0, kb * t + r0:kb * t + r0 + SCORE_ROWS, :]
                acc = None
                for hd in range(N_HEADS):
                    sc = _dot(kblk, qit_ref[0, hd * IDX_DIM:(hd + 1) * IDX_DIM, :])
                    term = jnp.maximum(sc, 0.0) * wt[hd:hd + 1, :]
                    acc = term if acc is None else acc + term
                acc = acc + 0.0
                if kb == diag:
                    acc = jnp.where(causal[r0:r0 + SCORE_ROWS, :], acc, -jnp.inf)
                bits = pltpu.bitcast(acc, I32)
                key = bits ^ ((bits >> 31) & jnp.int32(0x7FFFFFFF))
                key_scr[kb, r0:r0 + SCORE_ROWS, :] = key
                hi_scr[kb, r0:r0 + SCORE_ROWS, :] = (key >> 16).astype(I16)

        def packed(x):
            return jnp.concatenate([x, x], axis=0).astype(I16)

        def column_total(parts):
            tot = _tree(jnp.add, parts).astype(F32)
            return _all_sublanes(jnp.add, tot[:SUBLANES, :] + tot[SUBLANES:, :])

        def search16(src_scr, need):
            def step(j, ans):
                cand_off = ans | lax.shift_left(jnp.int32(1), 15 - j)
                cand = packed(cand_off - HALF)
                accs = [jnp.zeros((2 * SUBLANES, t), I16) for _ in range(4)]
                n = 0
                for kb in range(nkb):
                    blk = src_scr[kb]
                    for r in range(0, t, 2 * SUBLANES):
                        hit = jnp.where(blk[r:r + 2 * SUBLANES, :] >= cand, ONE16, ZERO16)
                        accs[n % 4] = accs[n % 4] + hit
                        n += 1
                return jnp.where(column_total(accs) >= need, cand_off, ans)
            return lax.fori_loop(0, 16, step, jnp.zeros((SUBLANES, t), I32))

        if nkb * t <= topk:
            thr = jnp.full((SUBLANES, t), INT_MIN, I32)
        else:
            t_hi = search16(hi_scr, kf) - HALF
            t_hi16 = packed(t_hi)
            above = [jnp.zeros((2 * SUBLANES, t), I16) for _ in range(4)]
            n = 0
            for kb in range(nkb):
                for r in range(0, t, 2 * SUBLANES):
                    rows = slice(r, r + 2 * SUBLANES)
                    h = hi_scr[kb, rows, :]
                    lo = ((key_scr[kb, rows, :] & jnp.int32(0xFFFF)) - HALF).astype(I16)
                    lo_scr[kb, rows, :] = jnp.where(h == t_hi16, lo, MIN16)
                    above[n % 4] = above[n % 4] + jnp.where(h > t_hi16, ONE16, ZERO16)
                    n += 1
            t_lo = search16(lo_scr, kf - column_total(above))
            thr = lax.shift_left(t_hi, 16) | t_lo

        n_ge = jnp.zeros((SUBLANES, t), F32)
        for kb in range(nkb):
            blk = key_scr[kb]
            for r, grp in enumerate(_row_groups(blk)):
                rows = slice(r * SUBLANES, (r + 1) * SUBLANES)
                sel = grp >= thr
                n_ge = n_ge + jnp.where(sel, 1.0, 0.0)
                m = jnp.where(sel, 0.0, NEG_BIG)
                if kb == diag:
                    m = jnp.where(causal[rows, :], m, NEG_BIG)
                selm_scr[kb, rows, :] = m
        n_ge = _all_sublanes(jnp.add, n_ge)
        excess = jnp.max(jnp.where(n_ge > kf, 1.0, 0.0))

        @pl.when(excess > 0.0)
        def _():
            thr1 = thr[0:1, :]

            def tied(kb):
                return jnp.where(key_scr[kb] == thr1, 1.0, 0.0)

            def count_tied(kb, acc):
                return acc + jnp.sum(tied(kb), axis=0, keepdims=True)

            n_tied = lax.fori_loop(0, nkb, count_tied, jnp.zeros((1, t), F32))
            need = kf - (n_ge[0:1, :] - n_tied)

            def rewrite(kb, before):
                e = tied(kb)
                rank = _dot(tri_ref[...], e.astype(BF16)) + before
                k = key_scr[kb]
                keep = jnp.where(k > thr1, 0.0,
                                 jnp.where(k == thr1, jnp.where(rank < need, 0.0, NEG_BIG), NEG_BIG))
                selm_scr[kb] = jnp.where(selm_scr[kb] < 0.0, NEG_BIG, keep)
                return before + jnp.sum(e, axis=0, keepdims=True)

            lax.fori_loop(0, nkb, rewrite, jnp.zeros((1, t), F32))

        def head_rows(hd):
            if isinstance(hd, int):
                return slice(hd * HEAD_DIM, (hd + 1) * HEAD_DIM)
            return pl.ds(pl.multiple_of(hd * HEAD_DIM, HEAD_DIM), HEAD_DIM)

        def logits(hd, buf):
            qt = qat_ref[0, head_rows(hd), :]
            maxes = []
            for kb in range(nkb):
                lg = _dot(ka_ref[0, hd, kb * t:(kb + 1) * t, :], qt) + selm_scr[kb]
                if kb >= nkb - 2:
                    lg = lg + bias_ref[hd, diag - kb]
                buf[kb] = lg
                maxes.append(jnp.max(lg, axis=0, keepdims=True))
            return _tree(jnp.maximum, maxes)

        def weighted_values(hd, buf, m):
            hrows = head_rows(hd)
            ssum = None
            acc = None
            for kb in range(nkb):
                p = jnp.exp2(buf[kb] - m)
                ps = jnp.sum(p, axis=0, keepdims=True)
                pv = _dot(vat_ref[0, hrows, kb * t:(kb + 1) * t], p.astype(BF16))
                ssum = ps if ssum is None else ssum + ps
                acc = pv if acc is None else acc + pv
            out_scr[hrows, :] = acc * (1.0 / ssum)

        def head_pair(j, m_even):
            m_odd = logits(2 * j + 1, logit_b)
            weighted_values(2 * j, logit_a, m_even)
            m_next = logits((2 * j + 2) % N_HEADS, logit_a)
            weighted_values(2 * j + 1, logit_b, m_odd)
            return m_next

        lax.fori_loop(0, N_HEADS // 2, head_pair, logits(0, logit_a))

    i = pl.program_id(1)
    for k in range(n_q_tiles):
        pl.when(i == k)(functools.partial(tile_work, k + 1))

    o = out_scr[...]
    o = o * lax.rsqrt(jnp.mean(o * o, axis=0, keepdims=True) + EPS) * gt_ref[...]
    o_ref[0] = o.T.astype(BF16)


def _dsa(qit, wit, ki, qat, ka, vat, bias, g_out, bsz, seq):
    t = DSA_TILE
    nq = seq // t
    gw = GROUP_WIDTH
    topk = min(TOPK_MAX, seq // 4)
    q_fm = pl.BlockSpec((1, gw, t), lambda b, i: (b, 0, i))
    gt = jnp.broadcast_to(g_out.reshape(gw, 1), (gw, t))
    r = np.arange(t)
    lower = jnp.asarray(r[None, :] < r[:, None], dtype=BF16)
    return pl.pallas_call(
        functools.partial(_dsa_kernel, topk=topk, n_q_tiles=nq),
        grid=(bsz, nq),
        in_specs=[q_fm,
                  pl.BlockSpec((1, N_HEADS, t), lambda b, i: (b, 0, i)),
                  pl.BlockSpec((1, seq, IDX_DIM), lambda b, i: (b, 0, 0)),
                  q_fm,
                  pl.BlockSpec((1, N_HEADS, seq, HEAD_DIM), lambda b, i: (b, 0, 0, 0)),
                  pl.BlockSpec((1, gw, seq), lambda b, i: (b, 0, 0)),
                  _resident(bias.shape),
                  _resident((gw, t)),
                  _resident((t, t))],
        out_specs=pl.BlockSpec((1, t, gw), lambda b, i: (b, i, 0)),
        out_shape=jax.ShapeDtypeStruct((bsz, seq, gw), BF16),
        scratch_shapes=[pltpu.VMEM((nq, t, t), I32), pltpu.VMEM((nq, t, t), I16),
                        pltpu.VMEM((nq, t, t), I16), pltpu.VMEM((nq, t, t), F32),
                        pltpu.VMEM((nq, t, t), F32), pltpu.VMEM((nq, t, t), F32),
                        pltpu.VMEM((gw, t), F32)],
        compiler_params=_params(2),
        name="dsa",
    )(qit, wit, ki, qat, ka, vat, bias, gt, lower)


def _neg_log2_1m_sigmoid(z2):
    return jnp.maximum(z2, 0.0) + jnp.log2(1.0 + jnp.exp2(-jnp.abs(z2)))


def _split_bf16(x):
    hi = x.astype(BF16)
    lo = (x - hi.astype(F32)).astype(BF16)
    return jnp.concatenate([hi, lo], axis=1)


def _sb_kernel(q_ref, k_ref, v_ref, tri_ref, g_ref, o_ref, z_scr, lh_scr, cs_scr, r_scr, acc_scr):
    t = SB_TILE
    nw = SB_WINDOW
    i = pl.program_id(1)
    wb = jnp.maximum(i - (nw - 1), 0)
    wrows = pl.ds(pl.multiple_of(wb * t, t), nw * t)
    row = lax.broadcasted_iota(I32, (t, t), 0)
    col = lax.broadcasted_iota(I32, (t, t), 1)

    def causal(w):
        return (col - row) < (i - wb - w) * t

    for hd in range(N_HEADS):
        z = _dot_nt(q_ref[0, hd], k_ref[0, hd, wrows, :])
        z_scr[hd] = z
        for w in range(nw):
            m = jnp.where(causal(w), _neg_log2_1m_sigmoid(z[:, w * t:(w + 1) * t]), 0.0)
            lh_scr[(hd * nw + w) * t:(hd * nw + w + 1) * t, :] = _split_bf16(m)
    cs_scr[...] = _dot(lh_scr[...], tri_ref[...])
    rmin = None
    for hd in range(N_HEADS):
        after = None
        a_blocks = [None] * nw
        for w in reversed(range(nw)):
            cs = cs_scr[(hd * nw + w) * t:(hd * nw + w + 1) * t, :]
            c = cs[:, :t] if after is None else cs[:, :t] + after
            a = jnp.where(causal(w), jnp.exp2(z_scr[hd, :, w * t:(w + 1) * t] - c), 0.0)
            a_blocks[w] = a.astype(BF16)
            after = cs[:, t:] if after is None else after + cs[:, t:]
        acc_scr[hd] = _dot(jnp.concatenate(a_blocks, axis=1), v_ref[0, hd, wrows, :])
        r_scr[hd] = after
        rmin = after if rmin is None else jnp.minimum(rmin, after)

    def far_block(kb):
        rows = pl.ds(pl.multiple_of(kb * t, t), t)
        rmin = None
        for hd in range(N_HEADS):
            z = _dot_nt(q_ref[0, hd], k_ref[0, hd, rows, :])
            cs = _dot(_split_bf16(_neg_log2_1m_sigmoid(z)), tri_ref[...])
            r_old = r_scr[hd]
            a = jnp.exp2(z - cs[:, :t] - r_old)
            acc_scr[hd] += _dot(a.astype(BF16), v_ref[0, hd, rows, :])
            r_new = r_old + cs[:, t:]
            r_scr[hd] = r_new
            rmin = r_new if rmin is None else jnp.minimum(rmin, r_new)
        return jnp.min(rmin)

    def cond(c):
        kb, rmin = c
        return jnp.logical_and(kb >= 0, rmin < SB_DEAD)

    def body(c):
        kb, _ = c
        return kb - 1, far_block(kb)

    lax.while_loop(cond, body, (wb - 1, jnp.min(rmin)))
    o = jnp.concatenate([acc_scr[hd] for hd in range(N_HEADS)], axis=1)
    o_ref[0] = _rms(o, g_ref[...]).astype(BF16)


def _sb(q, k, v, g_out, bsz, seq):
    t = SB_TILE
    nq = seq // t
    assert seq >= SB_WINDOW * t
    j = np.arange(t)[:, None]
    s = np.arange(t)[None, :]
    tri = np.concatenate([(j >= s), np.ones((t, t), bool)], axis=1).astype(np.float32)
    tri = np.concatenate([tri, tri], axis=0)
    q_hm = pl.BlockSpec((1, N_HEADS, t, HEAD_DIM), lambda b, i: (b, 0, i, 0))
    kv_hm = pl.BlockSpec((1, N_HEADS, seq, HEAD_DIM), lambda b, i: (b, 0, 0, 0))
    n_rows = N_HEADS * SB_WINDOW * t
    return pl.pallas_call(
        _sb_kernel,
        grid=(bsz, nq),
        in_specs=[q_hm, kv_hm, kv_hm, _resident((2 * t, 2 * t)),
                  pl.BlockSpec((1, GROUP_WIDTH), lambda b, i: (0, 0))],
        out_specs=pl.BlockSpec((1, t, GROUP_WIDTH), lambda b, i: (b, i, 0)),
        out_shape=jax.ShapeDtypeStruct((bsz, seq, GROUP_WIDTH), BF16),
        scratch_shapes=[pltpu.VMEM((N_HEADS, t, SB_WINDOW * t), F32),
                        pltpu.VMEM((n_rows, 2 * t), BF16),
                        pltpu.VMEM((n_rows, 2 * t), F32),
                        pltpu.VMEM((N_HEADS, t, t), F32),
                        pltpu.VMEM((N_HEADS, t, HEAD_DIM), F32)],
        compiler_params=_params(2),
        name="sb",
    )(q, k, v, jnp.asarray(tri, dtype=BF16), g_out)


def kernel(x, c, w_ada, b_ada, g_pre, g_post, w_ffn_gate, w_ffn_up, w_ffn_down, w_in, g_kidx,
           rel_bias, g_out_a, g_out_b, w_out):
    bsz, seq, d = x.shape
    depth = w_ada.shape[0]
    gw = GROUP_WIDTH
    h = x.reshape(bsz * seq, d)
    bias = _bias_tiles(rel_bias, DSA_TILE)
    for l in range(depth):
        mods = _ada(c, w_ada[l], b_ada[l]).reshape(bsz, -1, d)
        ffn_w = [(w_ffn_gate[l, j].astype(BF16), w_ffn_up[l, j].astype(BF16),
                  w_ffn_down[l, j].astype(BF16)) for j in range(2)]
        w = w_in[l]
        o_ki = 4 * gw
        o_wi = o_ki + IDX_DIM
        o_b = o_wi + N_HEADS
        wt = jnp.concatenate([w[:, :gw], w[:, 2 * gw:3 * gw], w[:, 3 * gw:o_ki]], axis=1).T.astype(BF16)
        wka = w[:, gw:2 * gw].astype(BF16)
        wki = w[:, o_ki:o_wi].astype(BF16)
        wwit = jnp.pad(w[:, o_wi:o_b].T, ((0, 16 - N_HEADS), (0, 0))).astype(BF16)
        wb = w[:, o_b:].astype(BF16)
        woa = w_out[l, :gw].astype(BF16)
        wob = w_out[l, gw:].astype(BF16)
        vec = lambda g: g.reshape(1, -1)

        h1 = _ffn(h, mods, vec(g_pre[l, 0]), vec(g_post[l, 0]), *ffn_w[0], 0, seq)
        qat, ka, vat, qit, ki, wit, qb, kb, vb = _proj(h1, mods, vec(g_pre[l, 1]), vec(g_kidx[l]),
                                                       wt, wka, wki, wwit, wb, bsz, seq)
        oa = _dsa(qit, wit, ki, qat, ka, vat, bias, g_out_a[l], bsz, seq)
        ob = _sb(qb, kb, vb, vec(g_out_b[l]), bsz, seq)
        h = _ffn(h1, mods, vec(g_pre[l, 2]), vec(g_post[l, 2]), *ffn_w[1], 2, seq,
                 mix=(oa.reshape(bsz * seq, gw), ob.reshape(bsz * seq, gw), woa, wob,
                      vec(g_post[l, 1])))
    return h.reshape(bsz, seq, d)
```

```python
import functools
import math

import numpy as np
import jax
import jax.numpy as jnp
from jax import lax
from jax.experimental import pallas as pl
from jax.experimental.pallas import tpu as pltpu

F32 = jnp.float32
BF16 = jnp.bfloat16
I32 = jnp.int32
I16 = jnp.int16

EPS = 1e-6
HEAD_DIM = 64
N_HEADS = 8
GROUP_WIDTH = N_HEADS * HEAD_DIM
IDX_DIM = 64
TOPK_MAX = 256
N_BUCKETS = 32
MAX_DISTANCE = 128
SUBLANES = 8
FFN_CHUNK = 256
TOKEN_TILE = 512
DSA_TILE = 256
SCORE_ROWS = 64
SB_TILE = 128
SB_WINDOW = 3
SB_PAIR = 2
LOG2E = math.log2(math.e)
NEG_BIG = -1e30
SB_DEAD = 104.0 * LOG2E
VMEM_LIMIT = 56 * 1024 * 1024
INT_MIN = np.int32(-2 ** 31)
HALF = 2 ** 15
ONE16, ZERO16, MIN16 = np.int16(1), np.int16(0), np.int16(-HALF)


def _dot(a, b):
    return jnp.dot(a, b, preferred_element_type=F32)


def _dot_nt(a, b):
    return lax.dot_general(a, b, (((1,), (1,)), ((), ())), preferred_element_type=F32)


def _rms(x, g):
    return x * lax.rsqrt(jnp.mean(x * x, axis=-1, keepdims=True) + EPS) * g


def _resident(shape):
    n = len(shape)
    return pl.BlockSpec(shape, lambda *_: (0,) * n, pipeline_mode=pl.Buffered(1))


def _params(n_grid):
    return pltpu.CompilerParams(dimension_semantics=("arbitrary",) * n_grid,
                                vmem_limit_bytes=VMEM_LIMIT)


def _ada_kernel(c_ref, w_ref, b_ref, o_ref):
    c = c_ref[...]
    a = c * jax.nn.sigmoid(c)
    w = w_ref[...]
    a_hi = a.astype(BF16)
    a_lo = (a - a_hi.astype(F32)).astype(BF16)
    w_hi = w.astype(BF16)
    w_lo = (w - w_hi.astype(F32)).astype(BF16)
    o_ref[...] = _dot(a_hi, w_hi) + (_dot(a_hi, w_lo) + _dot(a_lo, w_hi)) + b_ref[...]


def _ada(c, w, b):
    bsz, d = c.shape
    n = w.shape[1]
    tn = 1152 if n % 1152 == 0 else n
    return pl.pallas_call(
        _ada_kernel,
        grid=(n // tn,),
        in_specs=[pl.BlockSpec((bsz, d), lambda j: (0, 0)),
                  pl.BlockSpec((d, tn), lambda j: (0, j)),
                  pl.BlockSpec((1, tn), lambda j: (0, j))],
        out_specs=pl.BlockSpec((bsz, tn), lambda j: (0, j)),
        out_shape=jax.ShapeDtypeStruct((bsz, n), F32),
        compiler_params=_params(1),
        name="ada",
    )(c, w, b.reshape(1, n))


def _ffn_kernel(*refs, sub, mix):
    if mix:
        (h_ref, oa_ref, ob_ref, woa_ref, wob_ref, gpm_ref, mods_ref, gpre_ref, gpost_ref,
         wg_ref, wu_ref, wd_ref, o_ref, n_scr, a_scr) = refs
    else:
        (h_ref, mods_ref, gpre_ref, gpost_ref, wg_ref, wu_ref, wd_ref, o_ref, n_scr, a_scr) = refs
    x = h_ref[...]
    if mix:
        o = _dot(oa_ref[...], woa_ref[...]) + _dot(ob_ref[...], wob_ref[...])
        x = x + mods_ref[0, 5:6, :] * _rms(o, gpm_ref[...])
    shift = mods_ref[0, 3 * sub:3 * sub + 1, :]
    scale = mods_ref[0, 3 * sub + 1:3 * sub + 2, :]
    gate = mods_ref[0, 3 * sub + 2:3 * sub + 3, :]
    n = _rms(x, gpre_ref[...]) * (1.0 + scale) + shift
    n_scr[...] = n.astype(BF16)
    fc = FFN_CHUNK
    for c0 in range(0, wg_ref.shape[1], fc):
        g = _dot(n_scr[...], wg_ref[:, c0:c0 + fc])
        u = _dot(n_scr[...], wu_ref[:, c0:c0 + fc])
        a_scr[:, c0:c0 + fc] = (g * jax.nn.sigmoid(g) * u).astype(BF16)
    f = _dot(a_scr[...], wd_ref[...])
    o_ref[...] = x + 0.5 * gate * _rms(f, gpost_ref[...])


def _ffn(h, mods, g_pre, g_post, wg, wu, wd, sub, seq, mix=None):
    n_tok, d = h.shape
    tm = min(TOKEN_TILE, seq)
    tiles_per_batch = seq // tm
    row = pl.BlockSpec((tm, d), lambda i: (i, 0))
    vec = pl.BlockSpec((1, d), lambda i: (0, 0))
    mod_spec = pl.BlockSpec((1, mods.shape[1], d), lambda i: (i // tiles_per_batch, 0, 0))
    args, specs = [h], [row]
    if mix is not None:
        oa, ob, woa, wob, g_post_mix = mix
        gw = oa.shape[1]
        half = pl.BlockSpec((tm, gw), lambda i: (i, 0))
        args += [oa, ob, woa, wob, g_post_mix]
        specs += [half, half, _resident(woa.shape), _resident(wob.shape), vec]
    args += [mods, g_pre, g_post, wg, wu, wd]
    specs += [mod_spec, vec, vec, _resident(wg.shape), _resident(wu.shape), _resident(wd.shape)]
    return pl.pallas_call(
        functools.partial(_ffn_kernel, sub=sub, mix=mix is not None),
        grid=(n_tok // tm,),
        in_specs=specs,
        out_specs=row,
        out_shape=jax.ShapeDtypeStruct((n_tok, d), F32),
        scratch_shapes=[pltpu.VMEM((tm, d), BF16), pltpu.VMEM((tm, wd.shape[0]), BF16)],
        compiler_params=_params(1),
        name="ffn_mix" if mix is not None else "ffn",
    )(*args)


def _proj_kernel(h_ref, mods_ref, gpre_ref, gk_ref, wt_ref, wka_ref, wki_ref, wwit_ref, wb_ref,
                 qat_ref, ka_ref, vat_ref, qit_ref, ki_ref, wit_ref, qb_ref, kb_ref, vb_ref, n_scr):
    x = h_ref[...]
    shift = mods_ref[0, 3:4, :]
    scale = mods_ref[0, 4:5, :]
    n_scr[...] = (_rms(x, gpre_ref[...]) * (1.0 + scale) + shift).astype(BF16)
    gw = GROUP_WIDTH
    qscale = HEAD_DIM ** -0.5

    def heads(w, out_ref, mul):
        p = _dot(n_scr[...], w)
        if mul != 1.0:
            p = p * mul
        for hd in range(N_HEADS):
            out_ref[0, hd] = p[:, hd * HEAD_DIM:(hd + 1) * HEAD_DIM].astype(BF16)

    def feature_major(col, out_ref, mul):
        p = _dot_nt(wt_ref[col * gw:(col + 1) * gw, :], n_scr[...])
        if mul != 1.0:
            p = p * mul
        out_ref[0] = p.astype(BF16)

    feature_major(0, qat_ref, qscale * LOG2E)
    feature_major(1, vat_ref, 1.0)
    feature_major(2, qit_ref, 1.0)
    heads(wka_ref[...], ka_ref, 1.0)
    heads(wb_ref[:, :gw], qb_ref, qscale * LOG2E)
    heads(wb_ref[:, gw:2 * gw], kb_ref, 1.0)
    heads(wb_ref[:, 2 * gw:], vb_ref, 1.0)
    ki = _dot(n_scr[...], wki_ref[...])
    ki_ref[0] = _rms(ki, gk_ref[...]).astype(BF16)
    wit = _dot_nt(wwit_ref[...], n_scr[...])
    wit_ref[0] = wit[:N_HEADS, :] * ((N_HEADS * IDX_DIM) ** -0.5)


def _proj(h1, mods, g_pre, g_kidx, wt, wka, wki, wwit, wb, bsz, seq):
    n_tok, d = h1.shape
    tm = min(TOKEN_TILE, seq)
    tpb = seq // tm
    gw = GROUP_WIDTH
    row = pl.BlockSpec((tm, d), lambda i: (i, 0))
    vec = pl.BlockSpec((1, d), lambda i: (0, 0))
    mod_spec = pl.BlockSpec((1, mods.shape[1], d), lambda i: (i // tpb, 0, 0))
    hm = pl.BlockSpec((1, N_HEADS, tm, HEAD_DIM), lambda i: (i // tpb, 0, i % tpb, 0))
    hm_shape = jax.ShapeDtypeStruct((bsz, N_HEADS, seq, HEAD_DIM), BF16)
    fm = pl.BlockSpec((1, gw, tm), lambda i: (i // tpb, 0, i % tpb))
    fm_shape = jax.ShapeDtypeStruct((bsz, gw, seq), BF16)
    ki_spec = pl.BlockSpec((1, tm, IDX_DIM), lambda i: (i // tpb, i % tpb, 0))
    wit_spec = pl.BlockSpec((1, N_HEADS, tm), lambda i: (i // tpb, 0, i % tpb))
    return pl.pallas_call(
        _proj_kernel,
        grid=(n_tok // tm,),
        in_specs=[row, mod_spec, vec, pl.BlockSpec((1, IDX_DIM), lambda i: (0, 0)),
                  _resident(wt.shape), _resident(wka.shape), _resident(wki.shape),
                  _resident(wwit.shape), _resident(wb.shape)],
        out_specs=[fm, hm, fm, fm, ki_spec, wit_spec, hm, hm, hm],
        out_shape=[fm_shape, hm_shape, fm_shape, fm_shape,
                   jax.ShapeDtypeStruct((bsz, seq, IDX_DIM), BF16),
                   jax.ShapeDtypeStruct((bsz, N_HEADS, seq), F32),
                   hm_shape, hm_shape, hm_shape],
        scratch_shapes=[pltpu.VMEM((tm, d), BF16)],
        compiler_params=_params(1),
        name="proj",
    )(h1, mods, g_pre, g_kidx, wt, wka, wki, wwit, wb)


def _t5_bucket_np(n):
    n = np.maximum(n, 0)
    max_exact = N_BUCKETS // 2
    nf = np.maximum(n, 1).astype(np.float32)
    large = max_exact + (np.log(nf / np.float32(max_exact)) / np.float32(math.log(MAX_DISTANCE / max_exact))
                         * np.float32(N_BUCKETS - max_exact)).astype(np.int32)
    large = np.minimum(large, N_BUCKETS - 1)
    return np.where(n < max_exact, n, large).astype(np.int32)


def _bias_kernel(relb_ref, bucket_ref, o_ref):
    hd = pl.program_id(0)
    far = relb_ref[N_BUCKETS - 1, hd]
    for r in range(2):
        bk = bucket_ref[r]
        acc = jnp.zeros(bk.shape, F32)
        for k in range(N_BUCKETS - 1):
            acc = jnp.where(bk == k, (relb_ref[k, hd] - far) * LOG2E, acc)
        o_ref[0, r] = acc


def _bias_tiles(rel_bias, t):
    assert t >= MAX_DISTANCE
    s = np.arange(t)[:, None]
    q = np.arange(t)[None, :]
    buckets = np.stack([_t5_bucket_np(q - s), _t5_bucket_np(t + q - s)]).astype(np.int32)
    return pl.pallas_call(
        _bias_kernel,
        grid=(N_HEADS,),
        in_specs=[pl.BlockSpec(memory_space=pltpu.SMEM),
                  pl.BlockSpec((2, t, t), lambda hd: (0, 0, 0))],
        out_specs=pl.BlockSpec((1, 2, t, t), lambda hd: (hd, 0, 0, 0)),
        out_shape=jax.ShapeDtypeStruct((N_HEADS, 2, t, t), F32),
        compiler_params=_params(1),
        name="bias_tiles",
    )(rel_bias, jnp.asarray(buckets))


def _tree(op, xs):
    xs = list(xs)
    while len(xs) > 1:
        xs = [op(xs[j], xs[j + 1]) if j + 1 < len(xs) else xs[j] for j in range(0, len(xs), 2)]
    return xs[0]


def _row_groups(x):
    return [x[r:r + SUBLANES, :] for r in range(0, x.shape[0], SUBLANES)]


def _all_sublanes(op, x):
    for shift in (4, 2, 1):
        x = op(x, pltpu.roll(x, shift, 0))
    return x


def _dsa_kernel(qit_ref, wit_ref, ki_ref, qat_ref, ka_ref, vat_ref, bias_ref, gt_ref, tri_ref, o_ref,
                key_scr, hi_scr, lo_scr, selm_scr, logit_a, logit_b, out_scr, *, topk, n_q_tiles):
    t = DSA_TILE
    kf = float(topk)
    krow = lax.broadcasted_iota(I32, (t, t), 0)
    qcol = lax.broadcasted_iota(I32, (t, t), 1)
    causal = krow <= qcol

    def tile_work(nkb):
        diag = nkb - 1

        wt = wit_ref[0]
        for kb in range(nkb):
            for r0 in range(0, t, SCORE_ROWS):
                kblk = ki_ref[0, kb * t + r0:kb * t + r0 + SCORE_ROWS, :]
                acc = None
                for hd in range(N_HEADS):
                    sc = _dot(kblk, qit_ref[0, hd * IDX_DIM:(hd + 1) * IDX_DIM, :])
                    term = jnp.maximum(sc, 0.0) * wt[hd:hd + 1, :]
                    acc = term if acc is None else acc + term
                acc = acc + 0.0
                if kb == diag:
                    acc = jnp.where(causal[r0:r0 + SCORE_ROWS, :], acc, -jnp.inf)
                bits = pltpu.bitcast(acc, I32)
                key = bits ^ ((bits >> 31) & jnp.int32(0x7FFFFFFF))
                key_scr[kb, r0:r0 + SCORE_ROWS, :] = key
                hi_scr[kb, r0:r0 + SCORE_ROWS, :] = (key >> 16).astype(I16)

        def packed(x):
            return jnp.concatenate([x, x], axis=0).astype(I16)

        def column_total(parts):
            tot = _tree(jnp.add, parts).astype(F32)
            return _all_sublanes(jnp.add, tot[:SUBLANES, :] + tot[SUBLANES:, :])

        def search16(src_scr, need):
            def step(j, ans):
                cand_off = ans | lax.shift_left(jnp.int32(1), 15 - j)
                cand = packed(cand_off - HALF)
                accs = [jnp.zeros((2 * SUBLANES, t), I16) for _ in range(4)]
                n = 0
                for kb in range(nkb):
                    blk = src_scr[kb]
                    for r in range(0, t, 2 * SUBLANES):
                        hit = jnp.where(blk[r:r + 2 * SUBLANES, :] >= cand, ONE16, ZERO16)
                        accs[n % 4] = accs[n % 4] + hit
                        n += 1
                return jnp.where(column_total(accs) >= need, cand_off, ans)
            return lax.fori_loop(0, 16, step, jnp.zeros((SUBLANES, t), I32))

        if nkb * t <= topk:
            thr = jnp.full((SUBLANES, t), INT_MIN, I32)
        else:
            t_hi = search16(hi_scr, kf) - HALF
            t_hi16 = packed(t_hi)
            above = [jnp.zeros((2 * SUBLANES, t), I16) for _ in range(4)]
            n = 0
            for kb in range(nkb):
                for r in range(0, t, 2 * SUBLANES):
                    rows = slice(r, r + 2 * SUBLANES)
                    h = hi_scr[kb, rows, :]
                    lo = ((key_scr[kb, rows, :] & jnp.int32(0xFFFF)) - HALF).astype(I16)
                    lo_scr[kb, rows, :] = jnp.where(h == t_hi16, lo, MIN16)
                    above[n % 4] = above[n % 4] + jnp.where(h > t_hi16, ONE16, ZERO16)
                    n += 1
            t_lo = search16(lo_scr, kf - column_total(above))
            thr = lax.shift_left(t_hi, 16) | t_lo

        n_ge = jnp.zeros((SUBLANES, t), F32)
        for kb in range(nkb):
            blk = key_scr[kb]
            for r, grp in enumerate(_row_groups(blk)):
                rows = slice(r * SUBLANES, (r + 1) * SUBLANES)
                sel = grp >= thr
                n_ge = n_ge + jnp.where(sel, 1.0, 0.0)
                m = jnp.where(sel, 0.0, NEG_BIG)
                if kb == diag:
                    m = jnp.where(causal[rows, :], m, NEG_BIG)
                selm_scr[kb, rows, :] = m
        n_ge = _all_sublanes(jnp.add, n_ge)
        excess = jnp.max(jnp.where(n_ge > kf, 1.0, 0.0))

        @pl.when(excess > 0.0)
        def _():
            thr1 = thr[0:1, :]

            def tied(kb):
                return jnp.where(key_scr[kb] == thr1, 1.0, 0.0)

            def count_tied(kb, acc):
                return acc + jnp.sum(tied(kb), axis=0, keepdims=True)

            n_tied = lax.fori_loop(0, nkb, count_tied, jnp.zeros((1, t), F32))
            need = kf - (n_ge[0:1, :] - n_tied)

            def rewrite(kb, before):
                e = tied(kb)
                rank = _dot(tri_ref[...], e.astype(BF16)) + before
                k = key_scr[kb]
                keep = jnp.where(k > thr1, 0.0,
                                 jnp.where(k == thr1, jnp.where(rank < need, 0.0, NEG_BIG), NEG_BIG))
                selm_scr[kb] = jnp.where(selm_scr[kb] < 0.0, NEG_BIG, keep)
                return before + jnp.sum(e, axis=0, keepdims=True)

            lax.fori_loop(0, nkb, rewrite, jnp.zeros((1, t), F32))

        def head_rows(hd):
            if isinstance(hd, int):
                return slice(hd * HEAD_DIM, (hd + 1) * HEAD_DIM)
            return pl.ds(pl.multiple_of(hd * HEAD_DIM, HEAD_DIM), HEAD_DIM)

        def logits(hd, buf):
            qt = qat_ref[0, head_rows(hd), :]
            maxes = []
            for kb in range(nkb):
                lg = _dot(ka_ref[0, hd, kb * t:(kb + 1) * t, :], qt) + selm_scr[kb]
                if kb >= nkb - 2:
                    lg = lg + bias_ref[hd, diag - kb]
                buf[kb] = lg
                maxes.append(jnp.max(lg, axis=0, keepdims=True))
            return _tree(jnp.maximum, maxes)

        def weighted_values(hd, buf, m):
            hrows = head_rows(hd)
            ssum = None
            acc = None
            for kb in range(nkb):
                p = jnp.exp2(buf[kb] - m)
                ps = jnp.sum(p, axis=0, keepdims=True)
                pv = _dot(vat_ref[0, hrows, kb * t:(kb + 1) * t], p.astype(BF16))
                ssum = ps if ssum is None else ssum + ps
                acc = pv if acc is None else acc + pv
            out_scr[hrows, :] = acc * (1.0 / ssum)

        def head_pair(j, m_even):
            m_odd = logits(2 * j + 1, logit_b)
            weighted_values(2 * j, logit_a, m_even)
            m_next = logits((2 * j + 2) % N_HEADS, logit_a)
            weighted_values(2 * j + 1, logit_b, m_odd)
            return m_next

        lax.fori_loop(0, N_HEADS // 2, head_pair, logits(0, logit_a))

    i = pl.program_id(1)
    for k in range(n_q_tiles):
        pl.when(i == k)(functools.partial(tile_work, k + 1))

    o = out_scr[...]
    o = o * lax.rsqrt(jnp.mean(o * o, axis=0, keepdims=True) + EPS) * gt_ref[...]
    o_ref[0] = o.T.astype(BF16)


def _dsa(qit, wit, ki, qat, ka, vat, bias, g_out, bsz, seq):
    t = DSA_TILE
    nq = seq // t
    gw = GROUP_WIDTH
    topk = min(TOPK_MAX, seq // 4)
    q_fm = pl.BlockSpec((1, gw, t), lambda b, i: (b, 0, i))
    gt = jnp.broadcast_to(g_out.reshape(gw, 1), (gw, t))
    r = np.arange(t)
    lower = jnp.asarray(r[None, :] < r[:, None], dtype=BF16)
    return pl.pallas_call(
        functools.partial(_dsa_kernel, topk=topk, n_q_tiles=nq),
        grid=(bsz, nq),
        in_specs=[q_fm,
                  pl.BlockSpec((1, N_HEADS, t), lambda b, i: (b, 0, i)),
                  pl.BlockSpec((1, seq, IDX_DIM), lambda b, i: (b, 0, 0)),
                  q_fm,
                  pl.BlockSpec((1, N_HEADS, seq, HEAD_DIM), lambda b, i: (b, 0, 0, 0)),
                  pl.BlockSpec((1, gw, seq), lambda b, i: (b, 0, 0)),
                  _resident(bias.shape),
                  _resident((gw, t)),
                  _resident((t, t))],
        out_specs=pl.BlockSpec((1, t, gw), lambda b, i: (b, i, 0)),
        out_shape=jax.ShapeDtypeStruct((bsz, seq, gw), BF16),
        scratch_shapes=[pltpu.VMEM((nq, t, t), I32), pltpu.VMEM((nq, t, t), I16),
                        pltpu.VMEM((nq, t, t), I16), pltpu.VMEM((nq, t, t), F32),
                        pltpu.VMEM((nq, t, t), F32), pltpu.VMEM((nq, t, t), F32),
                        pltpu.VMEM((gw, t), F32)],
        compiler_params=_params(2),
        name="dsa",
    )(qit, wit, ki, qat, ka, vat, bias, gt, lower)


def _neg_log2_1m_sigmoid(z2):
    return jnp.maximum(z2, 0.0) + jnp.log2(1.0 + jnp.exp2(-jnp.abs(z2)))


def _split_bf16(x):
    hi = x.astype(BF16)
    lo = (x - hi.astype(F32)).astype(BF16)
    return jnp.concatenate([hi, lo], axis=1)


def _sb_kernel(q_ref, k_ref, v_ref, tri_ref, g_ref, o_ref, z_scr, lh_scr, cs_scr, r_scr, acc_scr):
    t = SB_TILE
    nw = SB_WINDOW
    row = lax.broadcasted_iota(I32, (t, t), 0)
    col = lax.broadcasted_iota(I32, (t, t), 1)
    tiles = []
    for sub in range(SB_PAIR):
        i = pl.program_id(1) * SB_PAIR + sub
        wb = jnp.maximum(i - (nw - 1), 0)
        tiles.append((sub, i, wb, pl.ds(pl.multiple_of(wb * t, t), nw * t)))

    def causal(i, wb, w):
        return (col - row) < (i - wb - w) * t

    def q_rows(sub):
        return slice(sub * t, (sub + 1) * t)

    for sub, i, wb, wrows in tiles:
        for hd in range(N_HEADS):
            p = sub * N_HEADS + hd
            z = _dot_nt(q_ref[0, hd, q_rows(sub), :], k_ref[0, hd, wrows, :])
            z_scr[p] = z
            for w in range(nw):
                m = jnp.where(causal(i, wb, w), _neg_log2_1m_sigmoid(z[:, w * t:(w + 1) * t]), 0.0)
                lh_scr[(p * nw + w) * t:(p * nw + w + 1) * t, :] = _split_bf16(m)
    cs_scr[...] = _dot(lh_scr[...], tri_ref[...])
    for sub, i, wb, wrows in tiles:
        for hd in range(N_HEADS):
            p = sub * N_HEADS + hd
            after = None
            a_blocks = [None] * nw
            for w in reversed(range(nw)):
                cs = cs_scr[(p * nw + w) * t:(p * nw + w + 1) * t, :]
                c = cs[:, :t] if after is None else cs[:, :t] + after
                a = jnp.where(causal(i, wb, w), jnp.exp2(z_scr[p, :, w * t:(w + 1) * t] - c), 0.0)
                a_blocks[w] = a.astype(BF16)
                after = cs[:, t:] if after is None else after + cs[:, t:]
            acc_scr[p] = _dot(jnp.concatenate(a_blocks, axis=1), v_ref[0, hd, wrows, :])
            r_scr[p] = after

    for sub, i, wb, wrows in tiles:
        def far_block(kb, sub=sub):
            rows = pl.ds(pl.multiple_of(kb * t, t), t)
            rmin = None
            for hd in range(N_HEADS):
                p = sub * N_HEADS + hd
                z = _dot_nt(q_ref[0, hd, q_rows(sub), :], k_ref[0, hd, rows, :])
                cs = _dot(_split_bf16(_neg_log2_1m_sigmoid(z)), tri_ref[...])
                r_old = r_scr[p]
                a = jnp.exp2(z - cs[:, :t] - r_old)
                acc_scr[p] += _dot(a.astype(BF16), v_ref[0, hd, rows, :])
                r_new = r_old + cs[:, t:]
                r_scr[p] = r_new
                rmin = r_new if rmin is None else jnp.minimum(rmin, r_new)
            return jnp.min(rmin)

        def cond(c):
            kb, rmin = c
            return jnp.logical_and(kb >= 0, rmin < SB_DEAD)

        def body(c, far_block=far_block):
            kb, _ = c
            return kb - 1, far_block(kb)

        rmin0 = _tree(jnp.minimum, [r_scr[sub * N_HEADS + hd] for hd in range(N_HEADS)])
        lax.while_loop(cond, body, (wb - 1, jnp.min(rmin0)))
        o = jnp.concatenate([acc_scr[sub * N_HEADS + hd] for hd in range(N_HEADS)], axis=1)
        o_ref[0, q_rows(sub), :] = _rms(o, g_ref[...]).astype(BF16)


def _sb(q, k, v, g_out, bsz, seq):
    t = SB_TILE
    tq = SB_PAIR * t
    assert seq >= SB_WINDOW * t and seq % tq == 0
    j = np.arange(t)[:, None]
    s = np.arange(t)[None, :]
    tri = np.concatenate([(j >= s), np.ones((t, t), bool)], axis=1).astype(np.float32)
    tri = np.concatenate([tri, tri], axis=0)
    q_hm = pl.BlockSpec((1, N_HEADS, tq, HEAD_DIM), lambda b, i: (b, 0, i, 0))
    kv_hm = pl.BlockSpec((1, N_HEADS, seq, HEAD_DIM), lambda b, i: (b, 0, 0, 0))
    n_p = SB_PAIR * N_HEADS
    n_rows = n_p * SB_WINDOW * t
    return pl.pallas_call(
        _sb_kernel,
        grid=(bsz, seq // tq),
        in_specs=[q_hm, kv_hm, kv_hm, _resident((2 * t, 2 * t)),
                  pl.BlockSpec((1, GROUP_WIDTH), lambda b, i: (0, 0))],
        out_specs=pl.BlockSpec((1, tq, GROUP_WIDTH), lambda b, i: (b, i, 0)),
        out_shape=jax.ShapeDtypeStruct((bsz, seq, GROUP_WIDTH), BF16),
        scratch_shapes=[pltpu.VMEM((n_p, t, SB_WINDOW * t), F32),
                        pltpu.VMEM((n_rows, 2 * t), BF16),
                        pltpu.VMEM((n_rows, 2 * t), F32),
                        pltpu.VMEM((n_p, t, t), F32),
                        pltpu.VMEM((n_p, t, HEAD_DIM), F32)],
        compiler_params=_params(2),
        name="sb",
    )(q, k, v, jnp.asarray(tri, dtype=BF16), g_out)


def kernel(x, c, w_ada, b_ada, g_pre, g_post, w_ffn_gate, w_ffn_up, w_ffn_down, w_in, g_kidx,
           rel_bias, g_out_a, g_out_b, w_out):
    bsz, seq, d = x.shape
    depth = w_ada.shape[0]
    gw = GROUP_WIDTH
    h = x.reshape(bsz * seq, d)
    bias = _bias_tiles(rel_bias, DSA_TILE)
    for l in range(depth):
        mods = _ada(c, w_ada[l], b_ada[l]).reshape(bsz, -1, d)
        ffn_w = [(w_ffn_gate[l, j].astype(BF16), w_ffn_up[l, j].astype(BF16),
                  w_ffn_down[l, j].astype(BF16)) for j in range(2)]
        w = w_in[l]
        o_ki = 4 * gw
        o_wi = o_ki + IDX_DIM
        o_b = o_wi + N_HEADS
        wt = jnp.concatenate([w[:, :gw], w[:, 2 * gw:3 * gw], w[:, 3 * gw:o_ki]], axis=1).T.astype(BF16)
        wka = w[:, gw:2 * gw].astype(BF16)
        wki = w[:, o_ki:o_wi].astype(BF16)
        wwit = jnp.pad(w[:, o_wi:o_b].T, ((0, 16 - N_HEADS), (0, 0))).astype(BF16)
        wb = w[:, o_b:].astype(BF16)
        woa = w_out[l, :gw].astype(BF16)
        wob = w_out[l, gw:].astype(BF16)
        vec = lambda g: g.reshape(1, -1)

        h1 = _ffn(h, mods, vec(g_pre[l, 0]), vec(g_post[l, 0]), *ffn_w[0], 0, seq)
        qat, ka, vat, qit, ki, wit, qb, kb, vb = _proj(h1, mods, vec(g_pre[l, 1]), vec(g_kidx[l]),
                                                       wt, wka, wki, wwit, wb, bsz, seq)
        oa = _dsa(qit, wit, ki, qat, ka, vat, bias, g_out_a[l], bsz, seq)
        ob = _sb(qb, kb, vb, vec(g_out_b[l]), bsz, seq)
        h = _ffn(h1, mods, vec(g_pre[l, 2]), vec(g_post[l, 2]), *ffn_w[1], 2, seq,
                 mix=(oa.reshape(bsz * seq, gw), ob.reshape(bsz * seq, gw), woa, wob,
                      vec(g_post[l, 1])))
    return h.reshape(bsz, seq, d)
```

```python
import functools
import math

import numpy as np
import jax
import jax.numpy as jnp
from jax import lax
from jax.experimental import pallas as pl
from jax.experimental.pallas import tpu as pltpu

F32 = jnp.float32
BF16 = jnp.bfloat16
I32 = jnp.int32
I16 = jnp.int16

EPS = 1e-6
HEAD_DIM = 64
N_HEADS = 8
GROUP_WIDTH = N_HEADS * HEAD_DIM
IDX_DIM = 64
TOPK_MAX = 256
N_BUCKETS = 32
MAX_DISTANCE = 128
SUBLANES = 8
FFN_CHUNK = 256
TOKEN_TILE = 512
DSA_TILE = 256
SCORE_ROWS = 64
SB_TILE = 128
SB_WINDOW = 3
SB_PAIR = 4
LOG2E = math.log2(math.e)
NEG_BIG = -1e30
SB_DEAD = 104.0 * LOG2E
VMEM_LIMIT = 56 * 1024 * 1024
INT_MIN = np.int32(-2 ** 31)
HALF = 2 ** 15
ONE16, ZERO16, MIN16 = np.int16(1), np.int16(0), np.int16(-HALF)


def _dot(a, b):
    return jnp.dot(a, b, preferred_element_type=F32)


def _dot_nt(a, b):
    return lax.dot_general(a, b, (((1,), (1,)), ((), ())), preferred_element_type=F32)


def _rms(x, g):
    return x * lax.rsqrt(jnp.mean(x * x, axis=-1, keepdims=True) + EPS) * g


def _resident(shape):
    n = len(shape)
    return pl.BlockSpec(shape, lambda *_: (0,) * n, pipeline_mode=pl.Buffered(1))


def _params(n_grid):
    return pltpu.CompilerParams(dimension_semantics=("arbitrary",) * n_grid,
                                vmem_limit_bytes=VMEM_LIMIT)


def _ada_kernel(c_ref, w_ref, b_ref, o_ref):
    c = c_ref[...]
    a = c * jax.nn.sigmoid(c)
    w = w_ref[...]
    a_hi = a.astype(BF16)
    a_lo = (a - a_hi.astype(F32)).astype(BF16)
    w_hi = w.astype(BF16)
    w_lo = (w - w_hi.astype(F32)).astype(BF16)
    o_ref[...] = _dot(a_hi, w_hi) + (_dot(a_hi, w_lo) + _dot(a_lo, w_hi)) + b_ref[...]


def _ada(c, w, b):
    bsz, d = c.shape
    n = w.shape[1]
    tn = 1152 if n % 1152 == 0 else n
    return pl.pallas_call(
        _ada_kernel,
        grid=(n // tn,),
        in_specs=[pl.BlockSpec((bsz, d), lambda j: (0, 0)),
                  pl.BlockSpec((d, tn), lambda j: (0, j)),
                  pl.BlockSpec((1, tn), lambda j: (0, j))],
        out_specs=pl.BlockSpec((bsz, tn), lambda j: (0, j)),
        out_shape=jax.ShapeDtypeStruct((bsz, n), F32),
        compiler_params=_params(1),
        name="ada",
    )(c, w, b.reshape(1, n))


def _ffn_kernel(*refs, sub, mix, proj):
    refs = list(refs)
    take = lambda n: [refs.pop(0) for _ in range(n)]
    (h_ref,) = take(1)
    if mix:
        oa_ref, ob_ref, woa_ref, wob_ref, gpm_ref = take(5)
    mods_ref, gpre_ref, gpost_ref, wg_ref, wu_ref, wd_ref = take(6)
    if proj:
        gpre_next_ref, *proj_in = take(7)
    (o_ref,) = take(1)
    if proj:
        proj_out = take(9)
    n_scr, a_scr = refs
    x = h_ref[...]
    if mix:
        o = _dot(oa_ref[...], woa_ref[...]) + _dot(ob_ref[...], wob_ref[...])
        x = x + mods_ref[0, 5:6, :] * _rms(o, gpm_ref[...])
    shift = mods_ref[0, 3 * sub:3 * sub + 1, :]
    scale = mods_ref[0, 3 * sub + 1:3 * sub + 2, :]
    gate = mods_ref[0, 3 * sub + 2:3 * sub + 3, :]
    n = _rms(x, gpre_ref[...]) * (1.0 + scale) + shift
    n_scr[...] = n.astype(BF16)
    fc = FFN_CHUNK
    for c0 in range(0, wg_ref.shape[1], fc):
        g = _dot(n_scr[...], wg_ref[:, c0:c0 + fc])
        u = _dot(n_scr[...], wu_ref[:, c0:c0 + fc])
        a_scr[:, c0:c0 + fc] = (g * jax.nn.sigmoid(g) * u).astype(BF16)
    f = _dot(a_scr[...], wd_ref[...])
    y = x + 0.5 * gate * _rms(f, gpost_ref[...])
    o_ref[...] = y
    if proj:
        shift = mods_ref[0, 3 * (sub + 1):3 * (sub + 1) + 1, :]
        scale = mods_ref[0, 3 * (sub + 1) + 1:3 * (sub + 1) + 2, :]
        n_scr[...] = (_rms(y, gpre_next_ref[...]) * (1.0 + scale) + shift).astype(BF16)
        _project(n_scr, *proj_in, *proj_out)


def _ffn(h, mods, g_pre, g_post, wg, wu, wd, sub, seq, mix=None, proj=None):
    n_tok, d = h.shape
    tm = min(TOKEN_TILE, seq)
    tpb = seq // tm
    row = pl.BlockSpec((tm, d), lambda i: (i, 0))
    vec = pl.BlockSpec((1, d), lambda i: (0, 0))
    mod_spec = pl.BlockSpec((1, mods.shape[1], d), lambda i: (i // tpb, 0, 0))
    args, specs = [h], [row]
    out_specs, out_shape = [row], [jax.ShapeDtypeStruct((n_tok, d), F32)]
    if mix is not None:
        oa, ob, woa, wob, g_post_mix = mix
        gw = oa.shape[1]
        half = pl.BlockSpec((tm, gw), lambda i: (i, 0))
        args += [oa, ob, woa, wob, g_post_mix]
        specs += [half, half, _resident(woa.shape), _resident(wob.shape), vec]
    args += [mods, g_pre, g_post, wg, wu, wd]
    specs += [mod_spec, vec, vec, _resident(wg.shape), _resident(wu.shape), _resident(wd.shape)]
    if proj is not None:
        g_pre_next, g_kidx, bsz, *weights = proj
        gw = GROUP_WIDTH
        args += [g_pre_next, g_kidx, *weights]
        specs += [vec, pl.BlockSpec((1, IDX_DIM), lambda i: (0, 0))] + [_resident(w.shape) for w in weights]
        hm = pl.BlockSpec((1, N_HEADS, tm, HEAD_DIM), lambda i: (i // tpb, 0, i % tpb, 0))
        hm_shape = jax.ShapeDtypeStruct((bsz, N_HEADS, seq, HEAD_DIM), BF16)
        fm = pl.BlockSpec((1, gw, tm), lambda i: (i // tpb, 0, i % tpb))
        fm_shape = jax.ShapeDtypeStruct((bsz, gw, seq), BF16)
        out_specs += [fm, hm, fm, fm,
                      pl.BlockSpec((1, tm, IDX_DIM), lambda i: (i // tpb, i % tpb, 0)),
                      pl.BlockSpec((1, N_HEADS, tm), lambda i: (i // tpb, 0, i % tpb)),
                      hm, hm, hm]
        out_shape += [fm_shape, hm_shape, fm_shape, fm_shape,
                      jax.ShapeDtypeStruct((bsz, seq, IDX_DIM), BF16),
                      jax.ShapeDtypeStruct((bsz, N_HEADS, seq), F32),
                      hm_shape, hm_shape, hm_shape]
    out = pl.pallas_call(
        functools.partial(_ffn_kernel, sub=sub, mix=mix is not None, proj=proj is not None),
        grid=(n_tok // tm,),
        in_specs=specs,
        out_specs=out_specs,
        out_shape=out_shape,
        scratch_shapes=[pltpu.VMEM((tm, d), BF16), pltpu.VMEM((tm, wd.shape[0]), BF16)],
        compiler_params=_params(1),
        name="ffn_mix" if mix is not None else "ffn_proj" if proj is not None else "ffn",
    )(*args)
    return out if proj is not None else out[0]


def _project(n_scr, gk_ref, wt_ref, wka_ref, wki_ref, wwit_ref, wb_ref,
             qat_ref, ka_ref, vat_ref, qit_ref, ki_ref, wit_ref, qb_ref, kb_ref, vb_ref):
    gw = GROUP_WIDTH
    qscale = HEAD_DIM ** -0.5

    def heads(w, out_ref, mul):
        p = _dot(n_scr[...], w)
        if mul != 1.0:
            p = p * mul
        for hd in range(N_HEADS):
            out_ref[0, hd] = p[:, hd * HEAD_DIM:(hd + 1) * HEAD_DIM].astype(BF16)

    def feature_major(col, out_ref, mul):
        p = _dot_nt(wt_ref[col * gw:(col + 1) * gw, :], n_scr[...])
        if mul != 1.0:
            p = p * mul
        out_ref[0] = p.astype(BF16)

    feature_major(0, qat_ref, qscale * LOG2E)
    feature_major(1, vat_ref, 1.0)
    feature_major(2, qit_ref, 1.0)
    heads(wka_ref[...], ka_ref, 1.0)
    heads(wb_ref[:, :gw], qb_ref, qscale * LOG2E)
    heads(wb_ref[:, gw:2 * gw], kb_ref, 1.0)
    heads(wb_ref[:, 2 * gw:], vb_ref, 1.0)
    ki = _dot(n_scr[...], wki_ref[...])
    ki_ref[0] = _rms(ki, gk_ref[...]).astype(BF16)
    wit = _dot_nt(wwit_ref[...], n_scr[...])
    wit_ref[0] = wit[:N_HEADS, :] * ((N_HEADS * IDX_DIM) ** -0.5)


def _t5_bucket_np(n):
    n = np.maximum(n, 0)
    max_exact = N_BUCKETS // 2
    nf = np.maximum(n, 1).astype(np.float32)
    large = max_exact + (np.log(nf / np.float32(max_exact)) / np.float32(math.log(MAX_DISTANCE / max_exact))
                         * np.float32(N_BUCKETS - max_exact)).astype(np.int32)
    large = np.minimum(large, N_BUCKETS - 1)
    return np.where(n < max_exact, n, large).astype(np.int32)


def _bias_kernel(relb_ref, bucket_ref, o_ref):
    hd = pl.program_id(0)
    far = relb_ref[N_BUCKETS - 1, hd]
    for r in range(2):
        bk = bucket_ref[r]
        acc = jnp.zeros(bk.shape, F32)
        for k in range(N_BUCKETS - 1):
            acc = jnp.where(bk == k, (relb_ref[k, hd] - far) * LOG2E, acc)
        o_ref[0, r] = acc


def _bias_tiles(rel_bias, t):
    assert t >= MAX_DISTANCE
    s = np.arange(t)[:, None]
    q = np.arange(t)[None, :]
    buckets = np.stack([_t5_bucket_np(q - s), _t5_bucket_np(t + q - s)]).astype(np.int32)
    return pl.pallas_call(
        _bias_kernel,
        grid=(N_HEADS,),
        in_specs=[pl.BlockSpec(memory_space=pltpu.SMEM),
                  pl.BlockSpec((2, t, t), lambda hd: (0, 0, 0))],
        out_specs=pl.BlockSpec((1, 2, t, t), lambda hd: (hd, 0, 0, 0)),
        out_shape=jax.ShapeDtypeStruct((N_HEADS, 2, t, t), F32),
        compiler_params=_params(1),
        name="bias_tiles",
    )(rel_bias, jnp.asarray(buckets))


def _tree(op, xs):
    xs = list(xs)
    while len(xs) > 1:
        xs = [op(xs[j], xs[j + 1]) if j + 1 < len(xs) else xs[j] for j in range(0, len(xs), 2)]
    return xs[0]


def _row_groups(x):
    return [x[r:r + SUBLANES, :] for r in range(0, x.shape[0], SUBLANES)]


def _all_sublanes(op, x):
    for shift in (4, 2, 1):
        x = op(x, pltpu.roll(x, shift, 0))
    return x


def _dsa_kernel(qit_ref, wit_ref, ki_ref, qat_ref, ka_ref, vat_ref, bias_ref, gt_ref, tri_ref, o_ref,
                key_scr, hi_scr, lo_scr, selm_scr, logit_a, logit_b, out_scr, *, topk, n_q_tiles):
    t = DSA_TILE
    kf = float(topk)
    krow = lax.broadcasted_iota(I32, (t, t), 0)
    qcol = lax.broadcasted_iota(I32, (t, t), 1)
    causal = krow <= qcol

    def tile_work(nkb):
        diag = nkb - 1

        wt = wit_ref[0]
        for kb in range(nkb):
            for r0 in range(0, t, SCORE_ROWS):
                kblk = ki_ref[0, kb * t + r0:kb * t + r0 + SCORE_ROWS, :]
                acc = None
                for hd in range(N_HEADS):
                    sc = _dot(kblk, qit_ref[0, hd * IDX_DIM:(hd + 1) * IDX_DIM, :])
                    term = jnp.maximum(sc, 0.0) * wt[hd:hd + 1, :]
                    acc = term if acc is None else acc + term
                acc = acc + 0.0
                if kb == diag:
                    acc = jnp.where(causal[r0:r0 + SCORE_ROWS, :], acc, -jnp.inf)
                bits = pltpu.bitcast(acc, I32)
                key = bits ^ ((bits >> 31) & jnp.int32(0x7FFFFFFF))
                key_scr[kb, r0:r0 + SCORE_ROWS, :] = key
                hi_scr[kb, r0:r0 + SCORE_ROWS, :] = (key >> 16).astype(I16)

        def packed(x):
            return jnp.concatenate([x, x], axis=0).astype(I16)

        def column_total(parts):
            tot = _tree(jnp.add, parts).astype(F32)
            return _all_sublanes(jnp.add, tot[:SUBLANES, :] + tot[SUBLANES:, :])

        def search16(src_scr, need):
            def step(j, ans):
                cand_off = ans | lax.shift_left(jnp.int32(1), 15 - j)
                cand = packed(cand_off - HALF)
                accs = [jnp.zeros((2 * SUBLANES, t), I16) for _ in range(4)]
                n = 0
                for kb in range(nkb):
                    blk = src_scr[kb]
                    for r in range(0, t, 2 * SUBLANES):
                        hit = jnp.where(blk[r:r + 2 * SUBLANES, :] >= cand, ONE16, ZERO16)
                        accs[n % 4] = accs[n % 4] + hit
                        n += 1
                return jnp.where(column_total(accs) >= need, cand_off, ans)
            return lax.fori_loop(0, 16, step, jnp.zeros((SUBLANES, t), I32))

        if nkb * t <= topk:
            thr = jnp.full((SUBLANES, t), INT_MIN, I32)
        else:
            t_hi = search16(hi_scr, kf) - HALF
            t_hi16 = packed(t_hi)
            above = [jnp.zeros((2 * SUBLANES, t), I16) for _ in range(4)]
            n = 0
            for kb in range(nkb):
                for r in range(0, t, 2 * SUBLANES):
                    rows = slice(r, r + 2 * SUBLANES)
                    h = hi_scr[kb, rows, :]
                    lo = ((key_scr[kb, rows, :] & jnp.int32(0xFFFF)) - HALF).astype(I16)
                    lo_scr[kb, rows, :] = jnp.where(h == t_hi16, lo, MIN16)
                    above[n % 4] = above[n % 4] + jnp.where(h > t_hi16, ONE16, ZERO16)
                    n += 1
            t_lo = search16(lo_scr, kf - column_total(above))
            thr = lax.shift_left(t_hi, 16) | t_lo

        n_ge = jnp.zeros((SUBLANES, t), F32)
        for kb in range(nkb):
            blk = key_scr[kb]
            for r, grp in enumerate(_row_groups(blk)):
                rows = slice(r * SUBLANES, (r + 1) * SUBLANES)
                sel = grp >= thr
                n_ge = n_ge + jnp.where(sel, 1.0, 0.0)
                m = jnp.where(sel, 0.0, NEG_BIG)
                if kb == diag:
                    m = jnp.where(causal[rows, :], m, NEG_BIG)
                selm_scr[kb, rows, :] = m
        n_ge = _all_sublanes(jnp.add, n_ge)
        excess = jnp.max(jnp.where(n_ge > kf, 1.0, 0.0))

        @pl.when(excess > 0.0)
        def _():
            thr1 = thr[0:1, :]

            def tied(kb):
                return jnp.where(key_scr[kb] == thr1, 1.0, 0.0)

            def count_tied(kb, acc):
                return acc + jnp.sum(tied(kb), axis=0, keepdims=True)

            n_tied = lax.fori_loop(0, nkb, count_tied, jnp.zeros((1, t), F32))
            need = kf - (n_ge[0:1, :] - n_tied)

            def rewrite(kb, before):
                e = tied(kb)
                rank = _dot(tri_ref[...], e.astype(BF16)) + before
                k = key_scr[kb]
                keep = jnp.where(k > thr1, 0.0,
                                 jnp.where(k == thr1, jnp.where(rank < need, 0.0, NEG_BIG), NEG_BIG))
                selm_scr[kb] = jnp.where(selm_scr[kb] < 0.0, NEG_BIG, keep)
                return before + jnp.sum(e, axis=0, keepdims=True)

            lax.fori_loop(0, nkb, rewrite, jnp.zeros((1, t), F32))

        def head_rows(hd):
            if isinstance(hd, int):
                return slice(hd * HEAD_DIM, (hd + 1) * HEAD_DIM)
            return pl.ds(pl.multiple_of(hd * HEAD_DIM, HEAD_DIM), HEAD_DIM)

        def logits(hd, buf):
            qt = qat_ref[0, head_rows(hd), :]
            maxes = []
            for kb in range(nkb):
                lg = _dot(ka_ref[0, hd, kb * t:(kb + 1) * t, :], qt) + selm_scr[kb]
                if kb >= nkb - 2:
                    lg = lg + bias_ref[hd, diag - kb]
                buf[kb] = lg
                maxes.append(jnp.max(lg, axis=0, keepdims=True))
            return _tree(jnp.maximum, maxes)

        def weighted_values(hd, buf, m):
            hrows = head_rows(hd)
            ssum = None
            acc = None
            for kb in range(nkb):
                p = jnp.exp2(buf[kb] - m)
                ps = jnp.sum(p, axis=0, keepdims=True)
                pv = _dot(vat_ref[0, hrows, kb * t:(kb + 1) * t], p.astype(BF16))
                ssum = ps if ssum is None else ssum + ps
                acc = pv if acc is None else acc + pv
            out_scr[hrows, :] = acc * (1.0 / ssum)

        def head_pair(j, m_even):
            m_odd = logits(2 * j + 1, logit_b)
            weighted_values(2 * j, logit_a, m_even)
            m_next = logits((2 * j + 2) % N_HEADS, logit_a)
            weighted_values(2 * j + 1, logit_b, m_odd)
            return m_next

        lax.fori_loop(0, N_HEADS // 2, head_pair, logits(0, logit_a))

    i = pl.program_id(1)
    for k in range(n_q_tiles):
        pl.when(i == k)(functools.partial(tile_work, k + 1))

    o = out_scr[...]
    o = o * lax.rsqrt(jnp.mean(o * o, axis=0, keepdims=True) + EPS) * gt_ref[...]
    o_ref[0] = o.T.astype(BF16)


def _dsa(qit, wit, ki, qat, ka, vat, bias, g_out, bsz, seq):
    t = DSA_TILE
    nq = seq // t
    gw = GROUP_WIDTH
    topk = min(TOPK_MAX, seq // 4)
    q_fm = pl.BlockSpec((1, gw, t), lambda b, i: (b, 0, i))
    gt = jnp.broadcast_to(g_out.reshape(gw, 1), (gw, t))
    r = np.arange(t)
    lower = jnp.asarray(r[None, :] < r[:, None], dtype=BF16)
    return pl.pallas_call(
        functools.partial(_dsa_kernel, topk=topk, n_q_tiles=nq),
        grid=(bsz, nq),
        in_specs=[q_fm,
                  pl.BlockSpec((1, N_HEADS, t), lambda b, i: (b, 0, i)),
                  pl.BlockSpec((1, seq, IDX_DIM), lambda b, i: (b, 0, 0)),
                  q_fm,
                  pl.BlockSpec((1, N_HEADS, seq, HEAD_DIM), lambda b, i: (b, 0, 0, 0)),
                  pl.BlockSpec((1, gw, seq), lambda b, i: (b, 0, 0)),
                  _resident(bias.shape),
                  _resident((gw, t)),
                  _resident((t, t))],
        out_specs=pl.BlockSpec((1, t, gw), lambda b, i: (b, i, 0)),
        out_shape=jax.ShapeDtypeStruct((bsz, seq, gw), BF16),
        scratch_shapes=[pltpu.VMEM((nq, t, t), I32), pltpu.VMEM((nq, t, t), I16),
                        pltpu.VMEM((nq, t, t), I16), pltpu.VMEM((nq, t, t), F32),
                        pltpu.VMEM((nq, t, t), F32), pltpu.VMEM((nq, t, t), F32),
                        pltpu.VMEM((gw, t), F32)],
        compiler_params=_params(2),
        name="dsa",
    )(qit, wit, ki, qat, ka, vat, bias, gt, lower)


def _neg_log2_1m_sigmoid(z2):
    return jnp.maximum(z2, 0.0) + jnp.log2(1.0 + jnp.exp2(-jnp.abs(z2)))


def _split_bf16(x):
    hi = x.astype(BF16)
    lo = (x - hi.astype(F32)).astype(BF16)
    return jnp.concatenate([hi, lo], axis=1)


def _sb_kernel(q_ref, k_ref, v_ref, tri_ref, g_ref, o_ref, z_scr, lh_scr, cs_scr, r_scr, acc_scr):
    t = SB_TILE
    nw = SB_WINDOW
    row = lax.broadcasted_iota(I32, (t, t), 0)
    col = lax.broadcasted_iota(I32, (t, t), 1)
    tiles = []
    for sub in range(SB_PAIR):
        i = pl.program_id(1) * SB_PAIR + sub
        wb = jnp.maximum(i - (nw - 1), 0)
        tiles.append((sub, i, wb, pl.ds(pl.multiple_of(wb * t, t), nw * t)))

    def causal(i, wb, w):
        return (col - row) < (i - wb - w) * t

    def q_rows(sub):
        return slice(sub * t, (sub + 1) * t)

    for sub, i, wb, wrows in tiles:
        for hd in range(N_HEADS):
            p = sub * N_HEADS + hd
            z = _dot_nt(q_ref[0, hd, q_rows(sub), :], k_ref[0, hd, wrows, :])
            z_scr[p] = z
            for w in range(nw):
                m = jnp.where(causal(i, wb, w), _neg_log2_1m_sigmoid(z[:, w * t:(w + 1) * t]), 0.0)
                lh_scr[(p * nw + w) * t:(p * nw + w + 1) * t, :] = _split_bf16(m)
    cs_scr[...] = _dot(lh_scr[...], tri_ref[...])
    for sub, i, wb, wrows in tiles:
        for hd in range(N_HEADS):
            p = sub * N_HEADS + hd
            after = None
            a_blocks = [None] * nw
            for w in reversed(range(nw)):
                cs = cs_scr[(p * nw + w) * t:(p * nw + w + 1) * t, :]
                c = cs[:, :t] if after is None else cs[:, :t] + after
                a = jnp.where(causal(i, wb, w), jnp.exp2(z_scr[p, :, w * t:(w + 1) * t] - c), 0.0)
                a_blocks[w] = a.astype(BF16)
                after = cs[:, t:] if after is None else after + cs[:, t:]
            acc_scr[p] = _dot(jnp.concatenate(a_blocks, axis=1), v_ref[0, hd, wrows, :])
            r_scr[p] = after

    for sub, i, wb, wrows in tiles:
        def far_block(kb, sub=sub):
            rows = pl.ds(pl.multiple_of(kb * t, t), t)
            rmin = None
            for hd in range(N_HEADS):
                p = sub * N_HEADS + hd
                z = _dot_nt(q_ref[0, hd, q_rows(sub), :], k_ref[0, hd, rows, :])
                cs = _dot(_split_bf16(_neg_log2_1m_sigmoid(z)), tri_ref[...])
                r_old = r_scr[p]
                a = jnp.exp2(z - cs[:, :t] - r_old)
                acc_scr[p] += _dot(a.astype(BF16), v_ref[0, hd, rows, :])
                r_new = r_old + cs[:, t:]
                r_scr[p] = r_new
                rmin = r_new if rmin is None else jnp.minimum(rmin, r_new)
            return jnp.min(rmin)

        def cond(c):
            kb, rmin = c
            return jnp.logical_and(kb >= 0, rmin < SB_DEAD)

        def body(c, far_block=far_block):
            kb, _ = c
            return kb - 1, far_block(kb)

        rmin0 = _tree(jnp.minimum, [r_scr[sub * N_HEADS + hd] for hd in range(N_HEADS)])
        lax.while_loop(cond, body, (wb - 1, jnp.min(rmin0)))
        o = jnp.concatenate([acc_scr[sub * N_HEADS + hd] for hd in range(N_HEADS)], axis=1)
        o_ref[0, q_rows(sub), :] = _rms(o, g_ref[...]).astype(BF16)


def _sb(q, k, v, g_out, bsz, seq):
    t = SB_TILE
    tq = SB_PAIR * t
    assert seq >= SB_WINDOW * t and seq % tq == 0
    j = np.arange(t)[:, None]
    s = np.arange(t)[None, :]
    tri = np.concatenate([(j >= s), np.ones((t, t), bool)], axis=1).astype(np.float32)
    tri = np.concatenate([tri, tri], axis=0)
    q_hm = pl.BlockSpec((1, N_HEADS, tq, HEAD_DIM), lambda b, i: (b, 0, i, 0))
    kv_hm = pl.BlockSpec((1, N_HEADS, seq, HEAD_DIM), lambda b, i: (b, 0, 0, 0))
    n_p = SB_PAIR * N_HEADS
    n_rows = n_p * SB_WINDOW * t
    return pl.pallas_call(
        _sb_kernel,
        grid=(bsz, seq // tq),
        in_specs=[q_hm, kv_hm, kv_hm, _resident((2 * t, 2 * t)),
                  pl.BlockSpec((1, GROUP_WIDTH), lambda b, i: (0, 0))],
        out_specs=pl.BlockSpec((1, tq, GROUP_WIDTH), lambda b, i: (b, i, 0)),
        out_shape=jax.ShapeDtypeStruct((bsz, seq, GROUP_WIDTH), BF16),
        scratch_shapes=[pltpu.VMEM((n_p, t, SB_WINDOW * t), F32),
                        pltpu.VMEM((n_rows, 2 * t), BF16),
                        pltpu.VMEM((n_rows, 2 * t), F32),
                        pltpu.VMEM((n_p, t, t), F32),
                        pltpu.VMEM((n_p, t, HEAD_DIM), F32)],
        compiler_params=_params(2),
        name="sb",
    )(q, k, v, jnp.asarray(tri, dtype=BF16), g_out)


def kernel(x, c, w_ada, b_ada, g_pre, g_post, w_ffn_gate, w_ffn_up, w_ffn_down, w_in, g_kidx,
           rel_bias, g_out_a, g_out_b, w_out):
    bsz, seq, d = x.shape
    depth = w_ada.shape[0]
    gw = GROUP_WIDTH
    h = x.reshape(bsz * seq, d)
    bias = _bias_tiles(rel_bias, DSA_TILE)
    for l in range(depth):
        mods = _ada(c, w_ada[l], b_ada[l]).reshape(bsz, -1, d)
        ffn_w = [(w_ffn_gate[l, j].astype(BF16), w_ffn_up[l, j].astype(BF16),
                  w_ffn_down[l, j].astype(BF16)) for j in range(2)]
        w = w_in[l]
        o_ki = 4 * gw
        o_wi = o_ki + IDX_DIM
        o_b = o_wi + N_HEADS
        wt = jnp.concatenate([w[:, :gw], w[:, 2 * gw:3 * gw], w[:, 3 * gw:o_ki]], axis=1).T.astype(BF16)
        wka = w[:, gw:2 * gw].astype(BF16)
        wki = w[:, o_ki:o_wi].astype(BF16)
        wwit = jnp.pad(w[:, o_wi:o_b].T, ((0, 16 - N_HEADS), (0, 0))).astype(BF16)
        wb = w[:, o_b:].astype(BF16)
        woa = w_out[l, :gw].astype(BF16)
        wob = w_out[l, gw:].astype(BF16)
        vec = lambda g: g.reshape(1, -1)

        h1, qat, ka, vat, qit, ki, wit, qb, kb, vb = _ffn(
            h, mods, vec(g_pre[l, 0]), vec(g_post[l, 0]), *ffn_w[0], 0, seq,
            proj=(vec(g_pre[l, 1]), vec(g_kidx[l]), bsz, wt, wka, wki, wwit, wb))
        oa = _dsa(qit, wit, ki, qat, ka, vat, bias, g_out_a[l], bsz, seq)
        ob = _sb(qb, kb, vb, vec(g_out_b[l]), bsz, seq)
        h = _ffn(h1, mods, vec(g_pre[l, 2]), vec(g_post[l, 2]), *ffn_w[1], 2, seq,
                 mix=(oa.reshape(bsz * seq, gw), ob.reshape(bsz * seq, gw), woa, wob,
                      vec(g_post[l, 1])))
    return h.reshape(bsz, seq, d)
```

```python
import functools
import math

import numpy as np
import jax
import jax.numpy as jnp
from jax import lax
from jax.experimental import pallas as pl
from jax.experimental.pallas import tpu as pltpu

F32 = jnp.float32
BF16 = jnp.bfloat16
I32 = jnp.int32
I16 = jnp.int16

EPS = 1e-6
HEAD_DIM = 64
N_HEADS = 8
GROUP_WIDTH = N_HEADS * HEAD_DIM
IDX_DIM = 64
TOPK_MAX = 256
N_BUCKETS = 32
MAX_DISTANCE = 128
SUBLANES = 8
V_ROWS = 80
FFN_CHUNK = 256
TOKEN_TILE = 512
DSA_TILE = 256
SCORE_ROWS = 64
SB_TILE = 128
SB_WINDOW = 3
SB_PAIR = 4
LOG2E = math.log2(math.e)
NEG_BIG = -1e30
SB_DEAD = 104.0 * LOG2E
VMEM_LIMIT = 56 * 1024 * 1024
INT_MIN = np.int32(-2 ** 31)
HALF = 2 ** 15
ONE16, ZERO16, MIN16 = np.int16(1), np.int16(0), np.int16(-HALF)


def _dot(a, b):
    return jnp.dot(a, b, preferred_element_type=F32)


def _dot_nt(a, b):
    return lax.dot_general(a, b, (((1,), (1,)), ((), ())), preferred_element_type=F32)


def _rms(x, g):
    return x * lax.rsqrt(jnp.mean(x * x, axis=-1, keepdims=True) + EPS) * g


def _resident(shape):
    n = len(shape)
    return pl.BlockSpec(shape, lambda *_: (0,) * n, pipeline_mode=pl.Buffered(1))


def _params(n_grid):
    return pltpu.CompilerParams(dimension_semantics=("arbitrary",) * n_grid,
                                vmem_limit_bytes=VMEM_LIMIT)


def _ada_kernel(c_ref, w_ref, b_ref, o_ref):
    c = c_ref[...]
    a = c * jax.nn.sigmoid(c)
    w = w_ref[...]
    a_hi = a.astype(BF16)
    a_lo = (a - a_hi.astype(F32)).astype(BF16)
    w_hi = w.astype(BF16)
    w_lo = (w - w_hi.astype(F32)).astype(BF16)
    o_ref[...] = _dot(a_hi, w_hi) + (_dot(a_hi, w_lo) + _dot(a_lo, w_hi)) + b_ref[...]


def _ada(c, w, b):
    bsz, d = c.shape
    n = w.shape[1]
    tn = 1152 if n % 1152 == 0 else n
    return pl.pallas_call(
        _ada_kernel,
        grid=(n // tn,),
        in_specs=[pl.BlockSpec((bsz, d), lambda j: (0, 0)),
                  pl.BlockSpec((d, tn), lambda j: (0, j)),
                  pl.BlockSpec((1, tn), lambda j: (0, j))],
        out_specs=pl.BlockSpec((bsz, tn), lambda j: (0, j)),
        out_shape=jax.ShapeDtypeStruct((bsz, n), F32),
        compiler_params=_params(1),
        name="ada",
    )(c, w, b.reshape(1, n))


def _ffn_kernel(*refs, sub, mix, proj):
    refs = list(refs)
    take = lambda n: [refs.pop(0) for _ in range(n)]
    (h_ref,) = take(1)
    if mix:
        oa_ref, ob_ref, woa_ref, wob_ref, gpm_ref = take(5)
    mods_ref, gpre_ref, gpost_ref, wg_ref, wu_ref, wd_ref = take(6)
    if proj:
        gpre_next_ref, *proj_in = take(7)
    (o_ref,) = take(1)
    if proj:
        proj_out = take(9)
    n_scr, a_scr = refs
    x = h_ref[...]
    if mix:
        o = _dot(oa_ref[...], woa_ref[...]) + _dot(ob_ref[...], wob_ref[...])
        x = x + mods_ref[0, 5:6, :] * _rms(o, gpm_ref[...])
    shift = mods_ref[0, 3 * sub:3 * sub + 1, :]
    scale = mods_ref[0, 3 * sub + 1:3 * sub + 2, :]
    gate = mods_ref[0, 3 * sub + 2:3 * sub + 3, :]
    n = _rms(x, gpre_ref[...]) * (1.0 + scale) + shift
    n_scr[...] = n.astype(BF16)
    fc = FFN_CHUNK
    for c0 in range(0, wg_ref.shape[1], fc):
        g = _dot(n_scr[...], wg_ref[:, c0:c0 + fc])
        u = _dot(n_scr[...], wu_ref[:, c0:c0 + fc])
        a_scr[:, c0:c0 + fc] = (g * jax.nn.sigmoid(g) * u).astype(BF16)
    f = _dot(a_scr[...], wd_ref[...])
    y = x + 0.5 * gate * _rms(f, gpost_ref[...])
    o_ref[...] = y
    if proj:
        shift = mods_ref[0, 3 * (sub + 1):3 * (sub + 1) + 1, :]
        scale = mods_ref[0, 3 * (sub + 1) + 1:3 * (sub + 1) + 2, :]
        n_scr[...] = (_rms(y, gpre_next_ref[...]) * (1.0 + scale) + shift).astype(BF16)
        _project(n_scr, *proj_in, *proj_out)


def _ffn(h, mods, g_pre, g_post, wg, wu, wd, sub, seq, mix=None, proj=None):
    n_tok, d = h.shape
    tm = min(TOKEN_TILE, seq)
    tpb = seq // tm
    row = pl.BlockSpec((tm, d), lambda i: (i, 0))
    vec = pl.BlockSpec((1, d), lambda i: (0, 0))
    mod_spec = pl.BlockSpec((1, mods.shape[1], d), lambda i: (i // tpb, 0, 0))
    args, specs = [h], [row]
    out_specs, out_shape = [row], [jax.ShapeDtypeStruct((n_tok, d), F32)]
    if mix is not None:
        oa, ob, woa, wob, g_post_mix = mix
        gw = oa.shape[1]
        half = pl.BlockSpec((tm, gw), lambda i: (i, 0))
        args += [oa, ob, woa, wob, g_post_mix]
        specs += [half, half, _resident(woa.shape), _resident(wob.shape), vec]
    args += [mods, g_pre, g_post, wg, wu, wd]
    specs += [mod_spec, vec, vec, _resident(wg.shape), _resident(wu.shape), _resident(wd.shape)]
    if proj is not None:
        g_pre_next, g_kidx, bsz, *weights = proj
        gw = GROUP_WIDTH
        args += [g_pre_next, g_kidx, *weights]
        specs += [vec, pl.BlockSpec((1, IDX_DIM), lambda i: (0, 0))] + [_resident(w.shape) for w in weights]
        hm = pl.BlockSpec((1, N_HEADS, tm, HEAD_DIM), lambda i: (i // tpb, 0, i % tpb, 0))
        hm_shape = jax.ShapeDtypeStruct((bsz, N_HEADS, seq, HEAD_DIM), BF16)
        fm = pl.BlockSpec((1, gw, tm), lambda i: (i // tpb, 0, i % tpb))
        fm_shape = jax.ShapeDtypeStruct((bsz, gw, seq), BF16)
        vfm = pl.BlockSpec((1, N_HEADS * V_ROWS, tm), lambda i: (i // tpb, 0, i % tpb))
        out_specs += [fm, hm, vfm, fm,
                      pl.BlockSpec((1, tm, IDX_DIM), lambda i: (i // tpb, i % tpb, 0)),
                      pl.BlockSpec((1, N_HEADS, tm), lambda i: (i // tpb, 0, i % tpb)),
                      hm, hm, hm]
        out_shape += [fm_shape, hm_shape, jax.ShapeDtypeStruct((bsz, N_HEADS * V_ROWS, seq), BF16),
                      fm_shape,
                      jax.ShapeDtypeStruct((bsz, seq, IDX_DIM), BF16),
                      jax.ShapeDtypeStruct((bsz, N_HEADS, seq), F32),
                      hm_shape, hm_shape, hm_shape]
    out = pl.pallas_call(
        functools.partial(_ffn_kernel, sub=sub, mix=mix is not None, proj=proj is not None),
        grid=(n_tok // tm,),
        in_specs=specs,
        out_specs=out_specs,
        out_shape=out_shape,
        scratch_shapes=[pltpu.VMEM((tm, d), BF16), pltpu.VMEM((tm, wd.shape[0]), BF16)],
        compiler_params=_params(1),
        name="ffn_mix" if mix is not None else "ffn_proj" if proj is not None else "ffn",
    )(*args)
    return out if proj is not None else out[0]


def _project(n_scr, gk_ref, wt_ref, wka_ref, wki_ref, wwit_ref, wb_ref,
             qat_ref, ka_ref, vat_ref, qit_ref, ki_ref, wit_ref, qb_ref, kb_ref, vb_ref):
    gw = GROUP_WIDTH
    qscale = HEAD_DIM ** -0.5

    def heads(w, out_ref, mul):
        p = _dot(n_scr[...], w)
        if mul != 1.0:
            p = p * mul
        for hd in range(N_HEADS):
            out_ref[0, hd] = p[:, hd * HEAD_DIM:(hd + 1) * HEAD_DIM].astype(BF16)

    def feature_major(col, out_ref, mul):
        p = _dot_nt(wt_ref[col * gw:(col + 1) * gw, :], n_scr[...])
        if mul != 1.0:
            p = p * mul
        out_ref[0] = p.astype(BF16)

    feature_major(0, qat_ref, qscale * LOG2E)
    v = _dot_nt(wt_ref[gw:2 * gw, :], n_scr[...])
    pad = V_ROWS - HEAD_DIM
    ones_row = (lax.broadcasted_iota(I32, (pad, v.shape[1]), 0) == 0).astype(BF16)
    for hd in range(N_HEADS):
        vat_ref[0, hd * V_ROWS:hd * V_ROWS + HEAD_DIM, :] = v[hd * HEAD_DIM:(hd + 1) * HEAD_DIM, :].astype(BF16)
        vat_ref[0, hd * V_ROWS + HEAD_DIM:(hd + 1) * V_ROWS, :] = ones_row
    feature_major(2, qit_ref, 1.0)
    heads(wka_ref[...], ka_ref, 1.0)
    heads(wb_ref[:, :gw], qb_ref, qscale * LOG2E)
    heads(wb_ref[:, gw:2 * gw], kb_ref, 1.0)
    heads(wb_ref[:, 2 * gw:], vb_ref, 1.0)
    ki = _dot(n_scr[...], wki_ref[...])
    ki_ref[0] = _rms(ki, gk_ref[...]).astype(BF16)
    wit = _dot_nt(wwit_ref[...], n_scr[...])
    wit_ref[0] = wit[:N_HEADS, :] * ((N_HEADS * IDX_DIM) ** -0.5)


def _t5_bucket_np(n):
    n = np.maximum(n, 0)
    max_exact = N_BUCKETS // 2
    nf = np.maximum(n, 1).astype(np.float32)
    large = max_exact + (np.log(nf / np.float32(max_exact)) / np.float32(math.log(MAX_DISTANCE / max_exact))
                         * np.float32(N_BUCKETS - max_exact)).astype(np.int32)
    large = np.minimum(large, N_BUCKETS - 1)
    return np.where(n < max_exact, n, large).astype(np.int32)


def _bias_kernel(relb_ref, bucket_ref, o_ref):
    hd = pl.program_id(0)
    far = relb_ref[N_BUCKETS - 1, hd]
    for r in range(2):
        bk = bucket_ref[r]
        acc = jnp.zeros(bk.shape, F32)
        for k in range(N_BUCKETS - 1):
            acc = jnp.where(bk == k, (relb_ref[k, hd] - far) * LOG2E, acc)
        o_ref[0, r] = acc


def _bias_tiles(rel_bias, t):
    assert t >= MAX_DISTANCE
    s = np.arange(t)[:, None]
    q = np.arange(t)[None, :]
    buckets = np.stack([_t5_bucket_np(q - s), _t5_bucket_np(t + q - s)]).astype(np.int32)
    return pl.pallas_call(
        _bias_kernel,
        grid=(N_HEADS,),
        in_specs=[pl.BlockSpec(memory_space=pltpu.SMEM),
                  pl.BlockSpec((2, t, t), lambda hd: (0, 0, 0))],
        out_specs=pl.BlockSpec((1, 2, t, t), lambda hd: (hd, 0, 0, 0)),
        out_shape=jax.ShapeDtypeStruct((N_HEADS, 2, t, t), F32),
        compiler_params=_params(1),
        name="bias_tiles",
    )(rel_bias, jnp.asarray(buckets))


def _tree(op, xs):
    xs = list(xs)
    while len(xs) > 1:
        xs = [op(xs[j], xs[j + 1]) if j + 1 < len(xs) else xs[j] for j in range(0, len(xs), 2)]
    return xs[0]


def _row_groups(x):
    return [x[r:r + SUBLANES, :] for r in range(0, x.shape[0], SUBLANES)]


def _all_sublanes(op, x):
    for shift in (4, 2, 1):
        x = op(x, pltpu.roll(x, shift, 0))
    return x


def _dsa_kernel(qit_ref, wit_ref, ki_ref, qat_ref, ka_ref, vat_ref, bias_ref, gt_ref, tri_ref, o_ref,
                key_scr, hi_scr, lo_scr, selm_scr, logit_a, logit_b, out_scr, *, topk, n_q_tiles):
    t = DSA_TILE
    kf = float(topk)
    krow = lax.broadcasted_iota(I32, (t, t), 0)
    qcol = lax.broadcasted_iota(I32, (t, t), 1)
    causal = krow <= qcol

    def tile_work(nkb):
        diag = nkb - 1

        wt = wit_ref[0]
        for kb in range(nkb):
            for r0 in range(0, t, SCORE_ROWS):
                kblk = ki_ref[0, kb * t + r0:kb * t + r0 + SCORE_ROWS, :]
                acc = None
                for hd in range(N_HEADS):
                    sc = _dot(kblk, qit_ref[0, hd * IDX_DIM:(hd + 1) * IDX_DIM, :])
                    term = jnp.maximum(sc, 0.0) * wt[hd:hd + 1, :]
                    acc = term if acc is None else acc + term
                acc = acc + 0.0
                if kb == diag:
                    acc = jnp.where(causal[r0:r0 + SCORE_ROWS, :], acc, -jnp.inf)
                bits = pltpu.bitcast(acc, I32)
                key = bits ^ ((bits >> 31) & jnp.int32(0x7FFFFFFF))
                key_scr[kb, r0:r0 + SCORE_ROWS, :] = key
                hi_scr[kb, r0:r0 + SCORE_ROWS, :] = (key >> 16).astype(I16)

        def packed(x):
            return jnp.concatenate([x, x], axis=0).astype(I16)

        def column_total(parts):
            tot = _tree(jnp.add, parts).astype(F32)
            return _all_sublanes(jnp.add, tot[:SUBLANES, :] + tot[SUBLANES:, :])

        def search16(src_scr, need):
            def step(j, ans):
                cand_off = ans | lax.shift_left(jnp.int32(1), 15 - j)
                cand = packed(cand_off - HALF)
                accs = [jnp.zeros((2 * SUBLANES, t), I16) for _ in range(4)]
                n = 0
                for kb in range(nkb):
                    blk = src_scr[kb]
                    for r in range(0, t, 2 * SUBLANES):
                        hit = jnp.where(blk[r:r + 2 * SUBLANES, :] >= cand, ONE16, ZERO16)
                        accs[n % 4] = accs[n % 4] + hit
                        n += 1
                return jnp.where(column_total(accs) >= need, cand_off, ans)
            return lax.fori_loop(0, 16, step, jnp.zeros((SUBLANES, t), I32))

        if nkb * t <= topk:
            thr = jnp.full((SUBLANES, t), INT_MIN, I32)
        else:
            t_hi = search16(hi_scr, kf) - HALF
            t_hi16 = packed(t_hi)
            above = [jnp.zeros((2 * SUBLANES, t), I16) for _ in range(4)]
            n = 0
            for kb in range(nkb):
                for r in range(0, t, 2 * SUBLANES):
                    rows = slice(r, r + 2 * SUBLANES)
                    h = hi_scr[kb, rows, :]
                    lo = ((key_scr[kb, rows, :] & jnp.int32(0xFFFF)) - HALF).astype(I16)
                    lo_scr[kb, rows, :] = jnp.where(h == t_hi16, lo, MIN16)
                    above[n % 4] = above[n % 4] + jnp.where(h > t_hi16, ONE16, ZERO16)
                    n += 1
            t_lo = search16(lo_scr, kf - column_total(above))
            thr = lax.shift_left(t_hi, 16) | t_lo

        n_ge = jnp.zeros((SUBLANES, t), F32)
        for kb in range(nkb):
            blk = key_scr[kb]
            for r, grp in enumerate(_row_groups(blk)):
                rows = slice(r * SUBLANES, (r + 1) * SUBLANES)
                sel = grp >= thr
                n_ge = n_ge + jnp.where(sel, 1.0, 0.0)
                m = jnp.where(sel, 0.0, NEG_BIG)
                if kb == diag:
                    m = jnp.where(causal[rows, :], m, NEG_BIG)
                selm_scr[kb, rows, :] = m
        n_ge = _all_sublanes(jnp.add, n_ge)
        excess = jnp.max(jnp.where(n_ge > kf, 1.0, 0.0))

        @pl.when(excess > 0.0)
        def _():
            thr1 = thr[0:1, :]

            def tied(kb):
                return jnp.where(key_scr[kb] == thr1, 1.0, 0.0)

            def count_tied(kb, acc):
                return acc + jnp.sum(tied(kb), axis=0, keepdims=True)

            n_tied = lax.fori_loop(0, nkb, count_tied, jnp.zeros((1, t), F32))
            need = kf - (n_ge[0:1, :] - n_tied)

            def rewrite(kb, before):
                e = tied(kb)
                rank = _dot(tri_ref[...], e.astype(BF16)) + before
                k = key_scr[kb]
                keep = jnp.where(k > thr1, 0.0,
                                 jnp.where(k == thr1, jnp.where(rank < need, 0.0, NEG_BIG), NEG_BIG))
                selm_scr[kb] = jnp.where(selm_scr[kb] < 0.0, NEG_BIG, keep)
                return before + jnp.sum(e, axis=0, keepdims=True)

            lax.fori_loop(0, nkb, rewrite, jnp.zeros((1, t), F32))

        def head_rows(hd):
            if isinstance(hd, int):
                return slice(hd * HEAD_DIM, (hd + 1) * HEAD_DIM)
            return pl.ds(pl.multiple_of(hd * HEAD_DIM, HEAD_DIM), HEAD_DIM)

        def logits(hd, buf):
            qt = qat_ref[0, head_rows(hd), :]
            maxes = []
            for kb in range(nkb):
                lg = _dot(ka_ref[0, hd, kb * t:(kb + 1) * t, :], qt) + selm_scr[kb]
                if kb >= nkb - 2:
                    lg = lg + bias_ref[hd, diag - kb]
                buf[kb] = lg
                maxes.append(jnp.max(lg, axis=0, keepdims=True))
            return _tree(jnp.maximum, maxes)

        def weighted_values(hd, buf, m):
            if isinstance(hd, int):
                vrows = slice(hd * V_ROWS, (hd + 1) * V_ROWS)
            else:
                vrows = pl.ds(pl.multiple_of(hd * V_ROWS, 16), V_ROWS)
            acc = None
            for kb in range(nkb):
                p = jnp.exp2(buf[kb] - m).astype(BF16)
                pv = _dot(vat_ref[0, vrows, kb * t:(kb + 1) * t], p)
                acc = pv if acc is None else acc + pv
            out_scr[head_rows(hd), :] = acc[:HEAD_DIM, :] * (1.0 / acc[HEAD_DIM:HEAD_DIM + 1, :])

        def head_pair(j, m_even):
            m_odd = logits(2 * j + 1, logit_b)
            weighted_values(2 * j, logit_a, m_even)
            m_next = logits((2 * j + 2) % N_HEADS, logit_a)
            weighted_values(2 * j + 1, logit_b, m_odd)
            return m_next

        lax.fori_loop(0, N_HEADS // 2, head_pair, logits(0, logit_a))

    i = pl.program_id(1)
    for k in range(n_q_tiles):
        pl.when(i == k)(functools.partial(tile_work, k + 1))

    o = out_scr[...]
    o = o * lax.rsqrt(jnp.mean(o * o, axis=0, keepdims=True) + EPS) * gt_ref[...]
    o_ref[0] = o.T.astype(BF16)


def _dsa(qit, wit, ki, qat, ka, vat, bias, g_out, bsz, seq):
    t = DSA_TILE
    nq = seq // t
    gw = GROUP_WIDTH
    topk = min(TOPK_MAX, seq // 4)
    q_fm = pl.BlockSpec((1, gw, t), lambda b, i: (b, 0, i))
    gt = jnp.broadcast_to(g_out.reshape(gw, 1), (gw, t))
    r = np.arange(t)
    lower = jnp.asarray(r[None, :] < r[:, None], dtype=BF16)
    return pl.pallas_call(
        functools.partial(_dsa_kernel, topk=topk, n_q_tiles=nq),
        grid=(bsz, nq),
        in_specs=[q_fm,
                  pl.BlockSpec((1, N_HEADS, t), lambda b, i: (b, 0, i)),
                  pl.BlockSpec((1, seq, IDX_DIM), lambda b, i: (b, 0, 0)),
                  q_fm,
                  pl.BlockSpec((1, N_HEADS, seq, HEAD_DIM), lambda b, i: (b, 0, 0, 0)),
                  pl.BlockSpec((1, N_HEADS * V_ROWS, seq), lambda b, i: (b, 0, 0)),
                  _resident(bias.shape),
                  _resident((gw, t)),
                  _resident((t, t))],
        out_specs=pl.BlockSpec((1, t, gw), lambda b, i: (b, i, 0)),
        out_shape=jax.ShapeDtypeStruct((bsz, seq, gw), BF16),
        scratch_shapes=[pltpu.VMEM((nq, t, t), I32), pltpu.VMEM((nq, t, t), I16),
                        pltpu.VMEM((nq, t, t), I16), pltpu.VMEM((nq, t, t), F32),
                        pltpu.VMEM((nq, t, t), F32), pltpu.VMEM((nq, t, t), F32),
                        pltpu.VMEM((gw, t), F32)],
        compiler_params=_params(2),
        name="dsa",
    )(qit, wit, ki, qat, ka, vat, bias, gt, lower)


def _neg_log2_1m_sigmoid(z2):
    return jnp.maximum(z2, 0.0) + jnp.log2(1.0 + jnp.exp2(-jnp.abs(z2)))


def _split_bf16(x):
    hi = x.astype(BF16)
    lo = (x - hi.astype(F32)).astype(BF16)
    return jnp.concatenate([hi, lo], axis=1)


def _sb_kernel(q_ref, k_ref, v_ref, tri_ref, g_ref, o_ref, z_scr, lh_scr, cs_scr, r_scr, acc_scr):
    t = SB_TILE
    nw = SB_WINDOW
    row = lax.broadcasted_iota(I32, (t, t), 0)
    col = lax.broadcasted_iota(I32, (t, t), 1)
    tiles = []
    for sub in range(SB_PAIR):
        i = pl.program_id(1) * SB_PAIR + sub
        wb = jnp.maximum(i - (nw - 1), 0)
        tiles.append((sub, i, wb, pl.ds(pl.multiple_of(wb * t, t), nw * t)))

    def causal(i, wb, w):
        return (col - row) < (i - wb - w) * t

    def q_rows(sub):
        return slice(sub * t, (sub + 1) * t)

    for sub, i, wb, wrows in tiles:
        for hd in range(N_HEADS):
            p = sub * N_HEADS + hd
            z = _dot_nt(q_ref[0, hd, q_rows(sub), :], k_ref[0, hd, wrows, :])
            z_scr[p] = z
            for w in range(nw):
                m = jnp.where(causal(i, wb, w), _neg_log2_1m_sigmoid(z[:, w * t:(w + 1) * t]), 0.0)
                lh_scr[(p * nw + w) * t:(p * nw + w + 1) * t, :] = _split_bf16(m)
    cs_scr[...] = _dot(lh_scr[...], tri_ref[...])
    for sub, i, wb, wrows in tiles:
        for hd in range(N_HEADS):
            p = sub * N_HEADS + hd
            after = None
            a_blocks = [None] * nw
            for w in reversed(range(nw)):
                cs = cs_scr[(p * nw + w) * t:(p * nw + w + 1) * t, :]
                c = cs[:, :t] if after is None else cs[:, :t] + after
                a = jnp.where(causal(i, wb, w), jnp.exp2(z_scr[p, :, w * t:(w + 1) * t] - c), 0.0)
                a_blocks[w] = a.astype(BF16)
                after = cs[:, t:] if after is None else after + cs[:, t:]
            acc_scr[p] = _dot(jnp.concatenate(a_blocks, axis=1), v_ref[0, hd, wrows, :])
            r_scr[p] = after

    for sub, i, wb, wrows in tiles:
        def far_block(kb, sub=sub):
            rows = pl.ds(pl.multiple_of(kb * t, t), t)
            rmin = None
            for hd in range(N_HEADS):
                p = sub * N_HEADS + hd
                z = _dot_nt(q_ref[0, hd, q_rows(sub), :], k_ref[0, hd, rows, :])
                cs = _dot(_split_bf16(_neg_log2_1m_sigmoid(z)), tri_ref[...])
                r_old = r_scr[p]
                a = jnp.exp2(z - cs[:, :t] - r_old)
                acc_scr[p] += _dot(a.astype(BF16), v_ref[0, hd, rows, :])
                r_new = r_old + cs[:, t:]
                r_scr[p] = r_new
                rmin = r_new if rmin is None else jnp.minimum(rmin, r_new)
            return jnp.min(rmin)

        def cond(c):
            kb, rmin = c
            return jnp.logical_and(kb >= 0, rmin < SB_DEAD)

        def body(c, far_block=far_block):
            kb, _ = c
            return kb - 1, far_block(kb)

        rmin0 = _tree(jnp.minimum, [r_scr[sub * N_HEADS + hd] for hd in range(N_HEADS)])
        lax.while_loop(cond, body, (wb - 1, jnp.min(rmin0)))
        o = jnp.concatenate([acc_scr[sub * N_HEADS + hd] for hd in range(N_HEADS)], axis=1)
        o_ref[0, q_rows(sub), :] = _rms(o, g_ref[...]).astype(BF16)


def _sb(q, k, v, g_out, bsz, seq):
    t = SB_TILE
    tq = SB_PAIR * t
    assert seq >= SB_WINDOW * t and seq % tq == 0
    j = np.arange(t)[:, None]
    s = np.arange(t)[None, :]
    tri = np.concatenate([(j >= s), np.ones((t, t), bool)], axis=1).astype(np.float32)
    tri = np.concatenate([tri, tri], axis=0)
    q_hm = pl.BlockSpec((1, N_HEADS, tq, HEAD_DIM), lambda b, i: (b, 0, i, 0))
    kv_hm = pl.BlockSpec((1, N_HEADS, seq, HEAD_DIM), lambda b, i: (b, 0, 0, 0))
    n_p = SB_PAIR * N_HEADS
    n_rows = n_p * SB_WINDOW * t
    return pl.pallas_call(
        _sb_kernel,
        grid=(bsz, seq // tq),
        in_specs=[q_hm, kv_hm, kv_hm, _resident((2 * t, 2 * t)),
                  pl.BlockSpec((1, GROUP_WIDTH), lambda b, i: (0, 0))],
        out_specs=pl.BlockSpec((1, tq, GROUP_WIDTH), lambda b, i: (b, i, 0)),
        out_shape=jax.ShapeDtypeStruct((bsz, seq, GROUP_WIDTH), BF16),
        scratch_shapes=[pltpu.VMEM((n_p, t, SB_WINDOW * t), F32),
                        pltpu.VMEM((n_rows, 2 * t), BF16),
                        pltpu.VMEM((n_rows, 2 * t), F32),
                        pltpu.VMEM((n_p, t, t), F32),
                        pltpu.VMEM((n_p, t, HEAD_DIM), F32)],
        compiler_params=_params(2),
        name="sb",
    )(q, k, v, jnp.asarray(tri, dtype=BF16), g_out)


def kernel(x, c, w_ada, b_ada, g_pre, g_post, w_ffn_gate, w_ffn_up, w_ffn_down, w_in, g_kidx,
           rel_bias, g_out_a, g_out_b, w_out):
    bsz, seq, d = x.shape
    depth = w_ada.shape[0]
    gw = GROUP_WIDTH
    h = x.reshape(bsz * seq, d)
    bias = _bias_tiles(rel_bias, DSA_TILE)
    for l in range(depth):
        mods = _ada(c, w_ada[l], b_ada[l]).reshape(bsz, -1, d)
        ffn_w = [(w_ffn_gate[l, j].astype(BF16), w_ffn_up[l, j].astype(BF16),
                  w_ffn_down[l, j].astype(BF16)) for j in range(2)]
        w = w_in[l]
        o_ki = 4 * gw
        o_wi = o_ki + IDX_DIM
        o_b = o_wi + N_HEADS
        wt = jnp.concatenate([w[:, :gw], w[:, 2 * gw:3 * gw], w[:, 3 * gw:o_ki]], axis=1).T.astype(BF16)
        wka = w[:, gw:2 * gw].astype(BF16)
        wki = w[:, o_ki:o_wi].astype(BF16)
        wwit = jnp.pad(w[:, o_wi:o_b].T, ((0, 16 - N_HEADS), (0, 0))).astype(BF16)
        wb = w[:, o_b:].astype(BF16)
        woa = w_out[l, :gw].astype(BF16)
        wob = w_out[l, gw:].astype(BF16)
        vec = lambda g: g.reshape(1, -1)

        h1, qat, ka, vat, qit, ki, wit, qb, kb, vb = _ffn(
            h, mods, vec(g_pre[l, 0]), vec(g_post[l, 0]), *ffn_w[0], 0, seq,
            proj=(vec(g_pre[l, 1]), vec(g_kidx[l]), bsz, wt, wka, wki, wwit, wb))
        oa = _dsa(qit, wit, ki, qat, ka, vat, bias, g_out_a[l], bsz, seq)
        ob = _sb(qb, kb, vb, vec(g_out_b[l]), bsz, seq)
        h = _ffn(h1, mods, vec(g_pre[l, 2]), vec(g_post[l, 2]), *ffn_w[1], 2, seq,
                 mix=(oa.reshape(bsz * seq, gw), ob.reshape(bsz * seq, gw), woa, wob,
                      vec(g_post[l, 1])))
    return h.reshape(bsz, seq, d)
```

```python
import functools
import math

import numpy as np
import jax
import jax.numpy as jnp
from jax import lax
from jax.experimental import pallas as pl
from jax.experimental.pallas import tpu as pltpu

F32 = jnp.float32
BF16 = jnp.bfloat16
I32 = jnp.int32
I16 = jnp.int16

EPS = 1e-6
HEAD_DIM = 64
N_HEADS = 8
GROUP_WIDTH = N_HEADS * HEAD_DIM
IDX_DIM = 64
TOPK_MAX = 256
N_BUCKETS = 32
MAX_DISTANCE = 128
SUBLANES = 8
V_ROWS = 80
FFN_CHUNK = 256
TOKEN_TILE = 512
MIX_TOKEN_TILE = 1024
DSA_TILE = 256
SCORE_ROWS = 64
SB_TILE = 128
SB_WINDOW = 3
SB_PAIR = 4
LOG2E = math.log2(math.e)
NEG_BIG = -1e30
SB_DEAD = 104.0 * LOG2E
VMEM_LIMIT = 56 * 1024 * 1024
INT_MIN = np.int32(-2 ** 31)
HALF = 2 ** 15
ONE16, ZERO16, MIN16 = np.int16(1), np.int16(0), np.int16(-HALF)


def _dot(a, b):
    return jnp.dot(a, b, preferred_element_type=F32)


def _dot_nt(a, b):
    return lax.dot_general(a, b, (((1,), (1,)), ((), ())), preferred_element_type=F32)


def _rms(x, g):
    return x * lax.rsqrt(jnp.mean(x * x, axis=-1, keepdims=True) + EPS) * g


def _resident(shape):
    n = len(shape)
    return pl.BlockSpec(shape, lambda *_: (0,) * n, pipeline_mode=pl.Buffered(1))


def _params(n_grid):
    return pltpu.CompilerParams(dimension_semantics=("arbitrary",) * n_grid,
                                vmem_limit_bytes=VMEM_LIMIT)


def _ada_kernel(c_ref, w_ref, b_ref, o_ref):
    c = c_ref[...]
    a = c * jax.nn.sigmoid(c)
    w = w_ref[...]
    a_hi = a.astype(BF16)
    a_lo = (a - a_hi.astype(F32)).astype(BF16)
    w_hi = w.astype(BF16)
    w_lo = (w - w_hi.astype(F32)).astype(BF16)
    o_ref[...] = _dot(a_hi, w_hi) + (_dot(a_hi, w_lo) + _dot(a_lo, w_hi)) + b_ref[...]


def _ada(c, w, b):
    bsz, d = c.shape
    n = w.shape[1]
    tn = 1152 if n % 1152 == 0 else n
    return pl.pallas_call(
        _ada_kernel,
        grid=(n // tn,),
        in_specs=[pl.BlockSpec((bsz, d), lambda j: (0, 0)),
                  pl.BlockSpec((d, tn), lambda j: (0, j)),
                  pl.BlockSpec((1, tn), lambda j: (0, j))],
        out_specs=pl.BlockSpec((bsz, tn), lambda j: (0, j)),
        out_shape=jax.ShapeDtypeStruct((bsz, n), F32),
        compiler_params=_params(1),
        name="ada",
    )(c, w, b.reshape(1, n))


def _ffn_kernel(*refs, sub, mix, proj):
    refs = list(refs)
    take = lambda n: [refs.pop(0) for _ in range(n)]
    (h_ref,) = take(1)
    if mix:
        oa_ref, ob_ref, woa_ref, wob_ref, gpm_ref = take(5)
    mods_ref, gpre_ref, gpost_ref, wg_ref, wu_ref, wd_ref = take(6)
    if proj:
        gpre_next_ref, *proj_in = take(7)
    (o_ref,) = take(1)
    if proj:
        proj_out = take(9)
    n_scr, a_scr = refs
    x = h_ref[...]
    if mix:
        o = _dot(oa_ref[...], woa_ref[...]) + _dot(ob_ref[...], wob_ref[...])
        x = x + mods_ref[0, 5:6, :] * _rms(o, gpm_ref[...])
    shift = mods_ref[0, 3 * sub:3 * sub + 1, :]
    scale = mods_ref[0, 3 * sub + 1:3 * sub + 2, :]
    gate = mods_ref[0, 3 * sub + 2:3 * sub + 3, :]
    n = _rms(x, gpre_ref[...]) * (1.0 + scale) + shift
    n_scr[...] = n.astype(BF16)
    fc = FFN_CHUNK
    for c0 in range(0, wg_ref.shape[1], fc):
        g = _dot(n_scr[...], wg_ref[:, c0:c0 + fc])
        u = _dot(n_scr[...], wu_ref[:, c0:c0 + fc])
        a_scr[:, c0:c0 + fc] = (g * jax.nn.sigmoid(g) * u).astype(BF16)
    f = _dot(a_scr[...], wd_ref[...])
    y = x + 0.5 * gate * _rms(f, gpost_ref[...])
    o_ref[...] = y
    if proj:
        shift = mods_ref[0, 3 * (sub + 1):3 * (sub + 1) + 1, :]
        scale = mods_ref[0, 3 * (sub + 1) + 1:3 * (sub + 1) + 2, :]
        n_scr[...] = (_rms(y, gpre_next_ref[...]) * (1.0 + scale) + shift).astype(BF16)
        _project(n_scr, *proj_in, *proj_out)


def _ffn(h, mods, g_pre, g_post, wg, wu, wd, sub, seq, mix=None, proj=None):
    n_tok, d = h.shape
    tm = min(TOKEN_TILE if proj is not None else MIX_TOKEN_TILE, seq)
    tpb = seq // tm
    row = pl.BlockSpec((tm, d), lambda i: (i, 0))
    vec = pl.BlockSpec((1, d), lambda i: (0, 0))
    mod_spec = pl.BlockSpec((1, mods.shape[1], d), lambda i: (i // tpb, 0, 0))
    args, specs = [h], [row]
    out_specs, out_shape = [row], [jax.ShapeDtypeStruct((n_tok, d), F32)]
    if mix is not None:
        oa, ob, woa, wob, g_post_mix = mix
        gw = oa.shape[1]
        half = pl.BlockSpec((tm, gw), lambda i: (i, 0))
        args += [oa, ob, woa, wob, g_post_mix]
        specs += [half, half, _resident(woa.shape), _resident(wob.shape), vec]
    args += [mods, g_pre, g_post, wg, wu, wd]
    specs += [mod_spec, vec, vec, _resident(wg.shape), _resident(wu.shape), _resident(wd.shape)]
    if proj is not None:
        g_pre_next, g_kidx, bsz, *weights = proj
        gw = GROUP_WIDTH
        args += [g_pre_next, g_kidx, *weights]
        specs += [vec, pl.BlockSpec((1, IDX_DIM), lambda i: (0, 0))] + [_resident(w.shape) for w in weights]
        hm = pl.BlockSpec((1, N_HEADS, tm, HEAD_DIM), lambda i: (i // tpb, 0, i % tpb, 0))
        hm_shape = jax.ShapeDtypeStruct((bsz, N_HEADS, seq, HEAD_DIM), BF16)
        fm = pl.BlockSpec((1, gw, tm), lambda i: (i // tpb, 0, i % tpb))
        fm_shape = jax.ShapeDtypeStruct((bsz, gw, seq), BF16)
        vfm = pl.BlockSpec((1, N_HEADS * V_ROWS, tm), lambda i: (i // tpb, 0, i % tpb))
        out_specs += [fm, hm, vfm, fm,
                      pl.BlockSpec((1, tm, IDX_DIM), lambda i: (i // tpb, i % tpb, 0)),
                      pl.BlockSpec((1, N_HEADS, tm), lambda i: (i // tpb, 0, i % tpb)),
                      hm, hm, hm]
        out_shape += [fm_shape, hm_shape, jax.ShapeDtypeStruct((bsz, N_HEADS * V_ROWS, seq), BF16),
                      fm_shape,
                      jax.ShapeDtypeStruct((bsz, seq, IDX_DIM), BF16),
                      jax.ShapeDtypeStruct((bsz, N_HEADS, seq), F32),
                      hm_shape, hm_shape, hm_shape]
    out = pl.pallas_call(
        functools.partial(_ffn_kernel, sub=sub, mix=mix is not None, proj=proj is not None),
        grid=(n_tok // tm,),
        in_specs=specs,
        out_specs=out_specs,
        out_shape=out_shape,
        scratch_shapes=[pltpu.VMEM((tm, d), BF16), pltpu.VMEM((tm, wd.shape[0]), BF16)],
        compiler_params=_params(1),
        name="ffn_mix" if mix is not None else "ffn_proj" if proj is not None else "ffn",
    )(*args)
    return out if proj is not None else out[0]


def _project(n_scr, gk_ref, wt_ref, wka_ref, wki_ref, wwit_ref, wb_ref,
             qat_ref, ka_ref, vat_ref, qit_ref, ki_ref, wit_ref, qb_ref, kb_ref, vb_ref):
    gw = GROUP_WIDTH
    qscale = HEAD_DIM ** -0.5

    def heads(w, out_ref, mul):
        p = _dot(n_scr[...], w)
        if mul != 1.0:
            p = p * mul
        for hd in range(N_HEADS):
            out_ref[0, hd] = p[:, hd * HEAD_DIM:(hd + 1) * HEAD_DIM].astype(BF16)

    def feature_major(col, out_ref, mul):
        p = _dot_nt(wt_ref[col * gw:(col + 1) * gw, :], n_scr[...])
        if mul != 1.0:
            p = p * mul
        out_ref[0] = p.astype(BF16)

    feature_major(0, qat_ref, qscale * LOG2E)
    v = _dot_nt(wt_ref[gw:2 * gw, :], n_scr[...])
    pad = V_ROWS - HEAD_DIM
    ones_row = (lax.broadcasted_iota(I32, (pad, v.shape[1]), 0) == 0).astype(BF16)
    for hd in range(N_HEADS):
        vat_ref[0, hd * V_ROWS:hd * V_ROWS + HEAD_DIM, :] = v[hd * HEAD_DIM:(hd + 1) * HEAD_DIM, :].astype(BF16)
        vat_ref[0, hd * V_ROWS + HEAD_DIM:(hd + 1) * V_ROWS, :] = ones_row
    feature_major(2, qit_ref, 1.0)
    heads(wka_ref[...], ka_ref, 1.0)
    heads(wb_ref[:, :gw], qb_ref, qscale * LOG2E)
    heads(wb_ref[:, gw:2 * gw], kb_ref, 1.0)
    heads(wb_ref[:, 2 * gw:], vb_ref, 1.0)
    ki = _dot(n_scr[...], wki_ref[...])
    ki_ref[0] = _rms(ki, gk_ref[...]).astype(BF16)
    wit = _dot_nt(wwit_ref[...], n_scr[...])
    wit_ref[0] = wit[:N_HEADS, :] * ((N_HEADS * IDX_DIM) ** -0.5)


def _t5_bucket_np(n):
    n = np.maximum(n, 0)
    max_exact = N_BUCKETS // 2
    nf = np.maximum(n, 1).astype(np.float32)
    large = max_exact + (np.log(nf / np.float32(max_exact)) / np.float32(math.log(MAX_DISTANCE / max_exact))
                         * np.float32(N_BUCKETS - max_exact)).astype(np.int32)
    large = np.minimum(large, N_BUCKETS - 1)
    return np.where(n < max_exact, n, large).astype(np.int32)


def _bias_kernel(relb_ref, bucket_ref, o_ref):
    hd = pl.program_id(0)
    far = relb_ref[N_BUCKETS - 1, hd]
    for r in range(2):
        bk = bucket_ref[r]
        acc = jnp.zeros(bk.shape, F32)
        for k in range(N_BUCKETS - 1):
            acc = jnp.where(bk == k, (relb_ref[k, hd] - far) * LOG2E, acc)
        o_ref[0, r] = acc


def _bias_tiles(rel_bias, t):
    assert t >= MAX_DISTANCE
    s = np.arange(t)[:, None]
    q = np.arange(t)[None, :]
    buckets = np.stack([_t5_bucket_np(q - s), _t5_bucket_np(t + q - s)]).astype(np.int32)
    return pl.pallas_call(
        _bias_kernel,
        grid=(N_HEADS,),
        in_specs=[pl.BlockSpec(memory_space=pltpu.SMEM),
                  pl.BlockSpec((2, t, t), lambda hd: (0, 0, 0))],
        out_specs=pl.BlockSpec((1, 2, t, t), lambda hd: (hd, 0, 0, 0)),
        out_shape=jax.ShapeDtypeStruct((N_HEADS, 2, t, t), F32),
        compiler_params=_params(1),
        name="bias_tiles",
    )(rel_bias, jnp.asarray(buckets))


def _tree(op, xs):
    xs = list(xs)
    while len(xs) > 1:
        xs = [op(xs[j], xs[j + 1]) if j + 1 < len(xs) else xs[j] for j in range(0, len(xs), 2)]
    return xs[0]


def _row_groups(x):
    return [x[r:r + SUBLANES, :] for r in range(0, x.shape[0], SUBLANES)]


def _all_sublanes(op, x):
    for shift in (4, 2, 1):
        x = op(x, pltpu.roll(x, shift, 0))
    return x


def _dsa_kernel(qit_ref, wit_ref, ki_ref, qat_ref, ka_ref, vat_ref, bias_ref, gt_ref, tri_ref, o_ref,
                key_scr, hi_scr, lo_scr, selm_scr, logit_a, logit_b, out_scr, *, topk, n_q_tiles):
    t = DSA_TILE
    kf = float(topk)
    krow = lax.broadcasted_iota(I32, (t, t), 0)
    qcol = lax.broadcasted_iota(I32, (t, t), 1)
    causal = krow <= qcol

    def tile_work(nkb):
        diag = nkb - 1

        wt = wit_ref[0]
        for kb in range(nkb):
            for r0 in range(0, t, SCORE_ROWS):
                kblk = ki_ref[0, kb * t + r0:kb * t + r0 + SCORE_ROWS, :]
                acc = None
                for hd in range(N_HEADS):
                    sc = _dot(kblk, qit_ref[0, hd * IDX_DIM:(hd + 1) * IDX_DIM, :])
                    term = jnp.maximum(sc, 0.0) * wt[hd:hd + 1, :]
                    acc = term if acc is None else acc + term
                acc = acc + 0.0
                if kb == diag:
                    acc = jnp.where(causal[r0:r0 + SCORE_ROWS, :], acc, -jnp.inf)
                bits = pltpu.bitcast(acc, I32)
                key = bits ^ ((bits >> 31) & jnp.int32(0x7FFFFFFF))
                key_scr[kb, r0:r0 + SCORE_ROWS, :] = key
                hi_scr[kb, r0:r0 + SCORE_ROWS, :] = (key >> 16).astype(I16)

        def packed(x):
            return jnp.concatenate([x, x], axis=0).astype(I16)

        def column_total(parts):
            tot = _tree(jnp.add, parts).astype(F32)
            return _all_sublanes(jnp.add, tot[:SUBLANES, :] + tot[SUBLANES:, :])

        def search16(src_scr, need):
            def step(j, ans):
                cand_off = ans | lax.shift_left(jnp.int32(1), 15 - j)
                cand = packed(cand_off - HALF)
                accs = [jnp.zeros((2 * SUBLANES, t), I16) for _ in range(4)]
                n = 0
                for kb in range(nkb):
                    blk = src_scr[kb]
                    for r in range(0, t, 2 * SUBLANES):
                        hit = jnp.where(blk[r:r + 2 * SUBLANES, :] >= cand, ONE16, ZERO16)
                        accs[n % 4] = accs[n % 4] + hit
                        n += 1
                return jnp.where(column_total(accs) >= need, cand_off, ans)
            return lax.fori_loop(0, 16, step, jnp.zeros((SUBLANES, t), I32))

        if nkb * t <= topk:
            thr = jnp.full((SUBLANES, t), INT_MIN, I32)
        else:
            t_hi = search16(hi_scr, kf) - HALF
            t_hi16 = packed(t_hi)
            above = [jnp.zeros((2 * SUBLANES, t), I16) for _ in range(4)]
            n = 0
            for kb in range(nkb):
                for r in range(0, t, 2 * SUBLANES):
                    rows = slice(r, r + 2 * SUBLANES)
                    h = hi_scr[kb, rows, :]
                    lo = ((key_scr[kb, rows, :] & jnp.int32(0xFFFF)) - HALF).astype(I16)
                    lo_scr[kb, rows, :] = jnp.where(h == t_hi16, lo, MIN16)
                    above[n % 4] = above[n % 4] + jnp.where(h > t_hi16, ONE16, ZERO16)
                    n += 1
            t_lo = search16(lo_scr, kf - column_total(above))
            thr = lax.shift_left(t_hi, 16) | t_lo

        n_ge = jnp.zeros((SUBLANES, t), F32)
        for kb in range(nkb):
            blk = key_scr[kb]
            for r, grp in enumerate(_row_groups(blk)):
                rows = slice(r * SUBLANES, (r + 1) * SUBLANES)
                sel = grp >= thr
                n_ge = n_ge + jnp.where(sel, 1.0, 0.0)
                m = jnp.where(sel, 0.0, NEG_BIG)
                if kb == diag:
                    m = jnp.where(causal[rows, :], m, NEG_BIG)
                selm_scr[kb, rows, :] = m
        n_ge = _all_sublanes(jnp.add, n_ge)
        excess = jnp.max(jnp.where(n_ge > kf, 1.0, 0.0))

        @pl.when(excess > 0.0)
        def _():
            thr1 = thr[0:1, :]

            def tied(kb):
                return jnp.where(key_scr[kb] == thr1, 1.0, 0.0)

            def count_tied(kb, acc):
                return acc + jnp.sum(tied(kb), axis=0, keepdims=True)

            n_tied = lax.fori_loop(0, nkb, count_tied, jnp.zeros((1, t), F32))
            need = kf - (n_ge[0:1, :] - n_tied)

            def rewrite(kb, before):
                e = tied(kb)
                rank = _dot(tri_ref[...], e.astype(BF16)) + before
                k = key_scr[kb]
                keep = jnp.where(k > thr1, 0.0,
                                 jnp.where(k == thr1, jnp.where(rank < need, 0.0, NEG_BIG), NEG_BIG))
                selm_scr[kb] = jnp.where(selm_scr[kb] < 0.0, NEG_BIG, keep)
                return before + jnp.sum(e, axis=0, keepdims=True)

            lax.fori_loop(0, nkb, rewrite, jnp.zeros((1, t), F32))

        def head_rows(hd):
            if isinstance(hd, int):
                return slice(hd * HEAD_DIM, (hd + 1) * HEAD_DIM)
            return pl.ds(pl.multiple_of(hd * HEAD_DIM, HEAD_DIM), HEAD_DIM)

        def logits(hd, buf):
            qt = qat_ref[0, head_rows(hd), :]
            maxes = []
            for kb in range(nkb):
                lg = _dot(ka_ref[0, hd, kb * t:(kb + 1) * t, :], qt) + selm_scr[kb]
                if kb >= nkb - 2:
                    lg = lg + bias_ref[hd, diag - kb]
                buf[kb] = lg
                maxes.append(jnp.max(lg, axis=0, keepdims=True))
            return _tree(jnp.maximum, maxes)

        def weighted_values(hd, buf, m):
            if isinstance(hd, int):
                vrows = slice(hd * V_ROWS, (hd + 1) * V_ROWS)
            else:
                vrows = pl.ds(pl.multiple_of(hd * V_ROWS, 16), V_ROWS)
            acc = None
            for kb in range(nkb):
                p = jnp.exp2(buf[kb] - m).astype(BF16)
                pv = _dot(vat_ref[0, vrows, kb * t:(kb + 1) * t], p)
                acc = pv if acc is None else acc + pv
            out_scr[head_rows(hd), :] = acc[:HEAD_DIM, :] * (1.0 / acc[HEAD_DIM:HEAD_DIM + 1, :])

        def head_pair(j, m_even):
            m_odd = logits(2 * j + 1, logit_b)
            weighted_values(2 * j, logit_a, m_even)
            m_next = logits((2 * j + 2) % N_HEADS, logit_a)
            weighted_values(2 * j + 1, logit_b, m_odd)
            return m_next

        lax.fori_loop(0, N_HEADS // 2, head_pair, logits(0, logit_a))

    i = pl.program_id(1)
    for k in range(n_q_tiles):
        pl.when(i == k)(functools.partial(tile_work, k + 1))

    o = out_scr[...]
    o = o * lax.rsqrt(jnp.mean(o * o, axis=0, keepdims=True) + EPS) * gt_ref[...]
    o_ref[0] = o.T.astype(BF16)


def _dsa(qit, wit, ki, qat, ka, vat, bias, g_out, bsz, seq):
    t = DSA_TILE
    nq = seq // t
    gw = GROUP_WIDTH
    topk = min(TOPK_MAX, seq // 4)
    q_fm = pl.BlockSpec((1, gw, t), lambda b, i: (b, 0, i))
    gt = jnp.broadcast_to(g_out.reshape(gw, 1), (gw, t))
    r = np.arange(t)
    lower = jnp.asarray(r[None, :] < r[:, None], dtype=BF16)
    return pl.pallas_call(
        functools.partial(_dsa_kernel, topk=topk, n_q_tiles=nq),
        grid=(bsz, nq),
        in_specs=[q_fm,
                  pl.BlockSpec((1, N_HEADS, t), lambda b, i: (b, 0, i)),
                  pl.BlockSpec((1, seq, IDX_DIM), lambda b, i: (b, 0, 0)),
                  q_fm,
                  pl.BlockSpec((1, N_HEADS, seq, HEAD_DIM), lambda b, i: (b, 0, 0, 0)),
                  pl.BlockSpec((1, N_HEADS * V_ROWS, seq), lambda b, i: (b, 0, 0)),
                  _resident(bias.shape),
                  _resident((gw, t)),
                  _resident((t, t))],
        out_specs=pl.BlockSpec((1, t, gw), lambda b, i: (b, i, 0)),
        out_shape=jax.ShapeDtypeStruct((bsz, seq, gw), BF16),
        scratch_shapes=[pltpu.VMEM((nq, t, t), I32), pltpu.VMEM((nq, t, t), I16),
                        pltpu.VMEM((nq, t, t), I16), pltpu.VMEM((nq, t, t), F32),
                        pltpu.VMEM((nq, t, t), F32), pltpu.VMEM((nq, t, t), F32),
                        pltpu.VMEM((gw, t), F32)],
        compiler_params=_params(2),
        name="dsa",
    )(qit, wit, ki, qat, ka, vat, bias, gt, lower)


def _neg_log2_1m_sigmoid(z2):
    return jnp.maximum(z2, 0.0) + jnp.log2(1.0 + jnp.exp2(-jnp.abs(z2)))


def _split_bf16(x):
    hi = x.astype(BF16)
    lo = (x - hi.astype(F32)).astype(BF16)
    return jnp.concatenate([hi, lo], axis=1)


def _sb_kernel(q_ref, k_ref, v_ref, tri_ref, g_ref, o_ref, z_scr, lh_scr, cs_scr, r_scr, acc_scr):
    t = SB_TILE
    nw = SB_WINDOW
    row = lax.broadcasted_iota(I32, (t, t), 0)
    col = lax.broadcasted_iota(I32, (t, t), 1)
    tiles = []
    for sub in range(SB_PAIR):
        i = pl.program_id(1) * SB_PAIR + sub
        wb = jnp.maximum(i - (nw - 1), 0)
        tiles.append((sub, i, wb, pl.ds(pl.multiple_of(wb * t, t), nw * t)))

    def causal(i, wb, w):
        return (col - row) < (i - wb - w) * t

    def q_rows(sub):
        return slice(sub * t, (sub + 1) * t)

    for sub, i, wb, wrows in tiles:
        for hd in range(N_HEADS):
            p = sub * N_HEADS + hd
            z = _dot_nt(q_ref[0, hd, q_rows(sub), :], k_ref[0, hd, wrows, :])
            z_scr[p] = z
            for w in range(nw):
                m = jnp.where(causal(i, wb, w), _neg_log2_1m_sigmoid(z[:, w * t:(w + 1) * t]), 0.0)
                lh_scr[(p * nw + w) * t:(p * nw + w + 1) * t, :] = _split_bf16(m)
    cs_scr[...] = _dot(lh_scr[...], tri_ref[...])
    for sub, i, wb, wrows in tiles:
        for hd in range(N_HEADS):
            p = sub * N_HEADS + hd
            after = None
            a_blocks = [None] * nw
            for w in reversed(range(nw)):
                cs = cs_scr[(p * nw + w) * t:(p * nw + w + 1) * t, :]
                c = cs[:, :t] if after is None else cs[:, :t] + after
                a = jnp.where(causal(i, wb, w), jnp.exp2(z_scr[p, :, w * t:(w + 1) * t] - c), 0.0)
                a_blocks[w] = a.astype(BF16)
                after = cs[:, t:] if after is None else after + cs[:, t:]
            acc_scr[p] = _dot(jnp.concatenate(a_blocks, axis=1), v_ref[0, hd, wrows, :])
            r_scr[p] = after

    for sub, i, wb, wrows in tiles:
        def far_block(kb, sub=sub):
            rows = pl.ds(pl.multiple_of(kb * t, t), t)
            rmin = None
            for hd in range(N_HEADS):
                p = sub * N_HEADS + hd
                z = _dot_nt(q_ref[0, hd, q_rows(sub), :], k_ref[0, hd, rows, :])
                cs = _dot(_split_bf16(_neg_log2_1m_sigmoid(z)), tri_ref[...])
                r_old = r_scr[p]
                a = jnp.exp2(z - cs[:, :t] - r_old)
                acc_scr[p] += _dot(a.astype(BF16), v_ref[0, hd, rows, :])
                r_new = r_old + cs[:, t:]
                r_scr[p] = r_new
                rmin = r_new if rmin is None else jnp.minimum(rmin, r_new)
            return jnp.min(rmin)

        def cond(c):
            kb, rmin = c
            return jnp.logical_and(kb >= 0, rmin < SB_DEAD)

        def body(c, far_block=far_block):
            kb, _ = c
            return kb - 1, far_block(kb)

        rmin0 = _tree(jnp.minimum, [r_scr[sub * N_HEADS + hd] for hd in range(N_HEADS)])
        lax.while_loop(cond, body, (wb - 1, jnp.min(rmin0)))
        o = jnp.concatenate([acc_scr[sub * N_HEADS + hd] for hd in range(N_HEADS)], axis=1)
        o_ref[0, q_rows(sub), :] = _rms(o, g_ref[...]).astype(BF16)


def _sb(q, k, v, g_out, bsz, seq):
    t = SB_TILE
    tq = SB_PAIR * t
    assert seq >= SB_WINDOW * t and seq % tq == 0
    j = np.arange(t)[:, None]
    s = np.arange(t)[None, :]
    tri = np.concatenate([(j >= s), np.ones((t, t), bool)], axis=1).astype(np.float32)
    tri = np.concatenate([tri, tri], axis=0)
    q_hm = pl.BlockSpec((1, N_HEADS, tq, HEAD_DIM), lambda b, i: (b, 0, i, 0))
    kv_hm = pl.BlockSpec((1, N_HEADS, seq, HEAD_DIM), lambda b, i: (b, 0, 0, 0))
    n_p = SB_PAIR * N_HEADS
    n_rows = n_p * SB_WINDOW * t
    return pl.pallas_call(
        _sb_kernel,
        grid=(bsz, seq // tq),
        in_specs=[q_hm, kv_hm, kv_hm, _resident((2 * t, 2 * t)),
                  pl.BlockSpec((1, GROUP_WIDTH), lambda b, i: (0, 0))],
        out_specs=pl.BlockSpec((1, tq, GROUP_WIDTH), lambda b, i: (b, i, 0)),
        out_shape=jax.ShapeDtypeStruct((bsz, seq, GROUP_WIDTH), BF16),
        scratch_shapes=[pltpu.VMEM((n_p, t, SB_WINDOW * t), F32),
                        pltpu.VMEM((n_rows, 2 * t), BF16),
                        pltpu.VMEM((n_rows, 2 * t), F32),
                        pltpu.VMEM((n_p, t, t), F32),
                        pltpu.VMEM((n_p, t, HEAD_DIM), F32)],
        compiler_params=_params(2),
        name="sb",
    )(q, k, v, jnp.asarray(tri, dtype=BF16), g_out)


def kernel(x, c, w_ada, b_ada, g_pre, g_post, w_ffn_gate, w_ffn_up, w_ffn_down, w_in, g_kidx,
           rel_bias, g_out_a, g_out_b, w_out):
    bsz, seq, d = x.shape
    depth = w_ada.shape[0]
    gw = GROUP_WIDTH
    h = x.reshape(bsz * seq, d)
    bias = _bias_tiles(rel_bias, DSA_TILE)
    for l in range(depth):
        mods = _ada(c, w_ada[l], b_ada[l]).reshape(bsz, -1, d)
        ffn_w = [(w_ffn_gate[l, j].astype(BF16), w_ffn_up[l, j].astype(BF16),
                  w_ffn_down[l, j].astype(BF16)) for j in range(2)]
        w = w_in[l]
        o_ki = 4 * gw
        o_wi = o_ki + IDX_DIM
        o_b = o_wi + N_HEADS
        wt = jnp.concatenate([w[:, :gw], w[:, 2 * gw:3 * gw], w[:, 3 * gw:o_ki]], axis=1).T.astype(BF16)
        wka = w[:, gw:2 * gw].astype(BF16)
        wki = w[:, o_ki:o_wi].astype(BF16)
        wwit = jnp.pad(w[:, o_wi:o_b].T, ((0, 16 - N_HEADS), (0, 0))).astype(BF16)
        wb = w[:, o_b:].astype(BF16)
        woa = w_out[l, :gw].astype(BF16)
        wob = w_out[l, gw:].astype(BF16)
        vec = lambda g: g.reshape(1, -1)

        h1, qat, ka, vat, qit, ki, wit, qb, kb, vb = _ffn(
            h, mods, vec(g_pre[l, 0]), vec(g_post[l, 0]), *ffn_w[0], 0, seq,
            proj=(vec(g_pre[l, 1]), vec(g_kidx[l]), bsz, wt, wka, wki, wwit, wb))
        oa = _dsa(qit, wit, ki, qat, ka, vat, bias, g_out_a[l], bsz, seq)
        ob = _sb(qb, kb, vb, vec(g_out_b[l]), bsz, seq)
        h = _ffn(h1, mods, vec(g_pre[l, 2]), vec(g_post[l, 2]), *ffn_w[1], 2, seq,
                 mix=(oa.reshape(bsz * seq, gw), ob.reshape(bsz * seq, gw), woa, wob,
                      vec(g_post[l, 1])))
    return h.reshape(bsz, seq, d)
```

```python
import functools
import math

import numpy as np
import jax
import jax.numpy as jnp
from jax import lax
from jax.experimental import pallas as pl
from jax.experimental.pallas import tpu as pltpu

F32 = jnp.float32
BF16 = jnp.bfloat16
I32 = jnp.int32
I16 = jnp.int16

EPS = 1e-6
HEAD_DIM = 64
N_HEADS = 8
GROUP_WIDTH = N_HEADS * HEAD_DIM
IDX_DIM = 64
TOPK_MAX = 256
N_BUCKETS = 32
MAX_DISTANCE = 128
SUBLANES = 8
V_ROWS = 80
FFN_CHUNK = 256
TOKEN_TILE = 512
MIX_TOKEN_TILE = 1024
DSA_TILE = 256
SCORE_ROWS = 64
HEADS_UNROLL_BLOCKS = 5
SB_TILE = 128
SB_WINDOW = 3
SB_PAIR = 4
LOG2E = math.log2(math.e)
NEG_BIG = -1e30
SB_DEAD = 104.0 * LOG2E
VMEM_LIMIT = 56 * 1024 * 1024
INT_MIN = np.int32(-2 ** 31)
HALF = 2 ** 15
ONE16, ZERO16, MIN16 = np.int16(1), np.int16(0), np.int16(-HALF)


def _dot(a, b):
    return jnp.dot(a, b, preferred_element_type=F32)


def _dot_nt(a, b):
    return lax.dot_general(a, b, (((1,), (1,)), ((), ())), preferred_element_type=F32)


def _rms(x, g):
    return x * lax.rsqrt(jnp.mean(x * x, axis=-1, keepdims=True) + EPS) * g


def _resident(shape):
    n = len(shape)
    return pl.BlockSpec(shape, lambda *_: (0,) * n, pipeline_mode=pl.Buffered(1))


def _params(n_grid):
    return pltpu.CompilerParams(dimension_semantics=("arbitrary",) * n_grid,
                                vmem_limit_bytes=VMEM_LIMIT)


def _ada_kernel(c_ref, w_ref, b_ref, o_ref):
    c = c_ref[...]
    a = c * jax.nn.sigmoid(c)
    w = w_ref[...]
    a_hi = a.astype(BF16)
    a_lo = (a - a_hi.astype(F32)).astype(BF16)
    w_hi = w.astype(BF16)
    w_lo = (w - w_hi.astype(F32)).astype(BF16)
    o_ref[...] = _dot(a_hi, w_hi) + (_dot(a_hi, w_lo) + _dot(a_lo, w_hi)) + b_ref[...]


def _ada(c, w, b):
    bsz, d = c.shape
    n = w.shape[1]
    tn = 1152 if n % 1152 == 0 else n
    return pl.pallas_call(
        _ada_kernel,
        grid=(n // tn,),
        in_specs=[pl.BlockSpec((bsz, d), lambda j: (0, 0)),
                  pl.BlockSpec((d, tn), lambda j: (0, j)),
                  pl.BlockSpec((1, tn), lambda j: (0, j))],
        out_specs=pl.BlockSpec((bsz, tn), lambda j: (0, j)),
        out_shape=jax.ShapeDtypeStruct((bsz, n), F32),
        compiler_params=_params(1),
        name="ada",
    )(c, w, b.reshape(1, n))


def _ffn_kernel(*refs, sub, mix, proj):
    refs = list(refs)
    take = lambda n: [refs.pop(0) for _ in range(n)]
    (h_ref,) = take(1)
    if mix:
        oa_ref, ob_ref, woa_ref, wob_ref, gpm_ref = take(5)
    mods_ref, gpre_ref, gpost_ref, wg_ref, wu_ref, wd_ref = take(6)
    if proj:
        gpre_next_ref, *proj_in = take(7)
    (o_ref,) = take(1)
    if proj:
        proj_out = take(9)
    n_scr, a_scr = refs
    x = h_ref[...]
    if mix:
        o = _dot(oa_ref[...], woa_ref[...]) + _dot(ob_ref[...], wob_ref[...])
        x = x + mods_ref[0, 5:6, :] * _rms(o, gpm_ref[...])
    shift = mods_ref[0, 3 * sub:3 * sub + 1, :]
    scale = mods_ref[0, 3 * sub + 1:3 * sub + 2, :]
    gate = mods_ref[0, 3 * sub + 2:3 * sub + 3, :]
    n = _rms(x, gpre_ref[...]) * (1.0 + scale) + shift
    n_scr[...] = n.astype(BF16)
    fc = FFN_CHUNK
    for c0 in range(0, wg_ref.shape[1], fc):
        g = _dot(n_scr[...], wg_ref[:, c0:c0 + fc])
        u = _dot(n_scr[...], wu_ref[:, c0:c0 + fc])
        a_scr[:, c0:c0 + fc] = (g * jax.nn.sigmoid(g) * u).astype(BF16)
    f = _dot(a_scr[...], wd_ref[...])
    y = x + 0.5 * gate * _rms(f, gpost_ref[...])
    o_ref[...] = y
    if proj:
        shift = mods_ref[0, 3 * (sub + 1):3 * (sub + 1) + 1, :]
        scale = mods_ref[0, 3 * (sub + 1) + 1:3 * (sub + 1) + 2, :]
        n_scr[...] = (_rms(y, gpre_next_ref[...]) * (1.0 + scale) + shift).astype(BF16)
        _project(n_scr, *proj_in, *proj_out)


def _ffn(h, mods, g_pre, g_post, wg, wu, wd, sub, seq, mix=None, proj=None):
    n_tok, d = h.shape
    tm = min(TOKEN_TILE if proj is not None else MIX_TOKEN_TILE, seq)
    tpb = seq // tm
    row = pl.BlockSpec((tm, d), lambda i: (i, 0))
    vec = pl.BlockSpec((1, d), lambda i: (0, 0))
    mod_spec = pl.BlockSpec((1, mods.shape[1], d), lambda i: (i // tpb, 0, 0))
    args, specs = [h], [row]
    out_specs, out_shape = [row], [jax.ShapeDtypeStruct((n_tok, d), F32)]
    if mix is not None:
        oa, ob, woa, wob, g_post_mix = mix
        gw = oa.shape[1]
        half = pl.BlockSpec((tm, gw), lambda i: (i, 0))
        args += [oa, ob, woa, wob, g_post_mix]
        specs += [half, half, _resident(woa.shape), _resident(wob.shape), vec]
    args += [mods, g_pre, g_post, wg, wu, wd]
    specs += [mod_spec, vec, vec, _resident(wg.shape), _resident(wu.shape), _resident(wd.shape)]
    if proj is not None:
        g_pre_next, g_kidx, bsz, *weights = proj
        gw = GROUP_WIDTH
        args += [g_pre_next, g_kidx, *weights]
        specs += [vec, pl.BlockSpec((1, IDX_DIM), lambda i: (0, 0))] + [_resident(w.shape) for w in weights]
        hm = pl.BlockSpec((1, N_HEADS, tm, HEAD_DIM), lambda i: (i // tpb, 0, i % tpb, 0))
        hm_shape = jax.ShapeDtypeStruct((bsz, N_HEADS, seq, HEAD_DIM), BF16)
        fm = pl.BlockSpec((1, gw, tm), lambda i: (i // tpb, 0, i % tpb))
        fm_shape = jax.ShapeDtypeStruct((bsz, gw, seq), BF16)
        vfm = pl.BlockSpec((1, N_HEADS * V_ROWS, tm), lambda i: (i // tpb, 0, i % tpb))
        out_specs += [fm, hm, vfm, fm,
                      pl.BlockSpec((1, tm, IDX_DIM), lambda i: (i // tpb, i % tpb, 0)),
                      pl.BlockSpec((1, N_HEADS, tm), lambda i: (i // tpb, 0, i % tpb)),
                      hm, hm, hm]
        out_shape += [fm_shape, hm_shape, jax.ShapeDtypeStruct((bsz, N_HEADS * V_ROWS, seq), BF16),
                      fm_shape,
                      jax.ShapeDtypeStruct((bsz, seq, IDX_DIM), BF16),
                      jax.ShapeDtypeStruct((bsz, N_HEADS, seq), F32),
                      hm_shape, hm_shape, hm_shape]
    out = pl.pallas_call(
        functools.partial(_ffn_kernel, sub=sub, mix=mix is not None, proj=proj is not None),
        grid=(n_tok // tm,),
        in_specs=specs,
        out_specs=out_specs,
        out_shape=out_shape,
        scratch_shapes=[pltpu.VMEM((tm, d), BF16), pltpu.VMEM((tm, wd.shape[0]), BF16)],
        compiler_params=_params(1),
        name="ffn_mix" if mix is not None else "ffn_proj" if proj is not None else "ffn",
    )(*args)
    return out if proj is not None else out[0]


def _project(n_scr, gk_ref, wt_ref, wka_ref, wki_ref, wwit_ref, wb_ref,
             qat_ref, ka_ref, vat_ref, qit_ref, ki_ref, wit_ref, qb_ref, kb_ref, vb_ref):
    gw = GROUP_WIDTH
    qscale = HEAD_DIM ** -0.5

    def heads(w, out_ref, mul):
        p = _dot(n_scr[...], w)
        if mul != 1.0:
            p = p * mul
        for hd in range(N_HEADS):
            out_ref[0, hd] = p[:, hd * HEAD_DIM:(hd + 1) * HEAD_DIM].astype(BF16)

    def feature_major(col, out_ref, mul):
        p = _dot_nt(wt_ref[col * gw:(col + 1) * gw, :], n_scr[...])
        if mul != 1.0:
            p = p * mul
        out_ref[0] = p.astype(BF16)

    feature_major(0, qat_ref, qscale * LOG2E)
    v = _dot_nt(wt_ref[gw:2 * gw, :], n_scr[...])
    pad = V_ROWS - HEAD_DIM
    ones_row = (lax.broadcasted_iota(I32, (pad, v.shape[1]), 0) == 0).astype(BF16)
    for hd in range(N_HEADS):
        vat_ref[0, hd * V_ROWS:hd * V_ROWS + HEAD_DIM, :] = v[hd * HEAD_DIM:(hd + 1) * HEAD_DIM, :].astype(BF16)
        vat_ref[0, hd * V_ROWS + HEAD_DIM:(hd + 1) * V_ROWS, :] = ones_row
    feature_major(2, qit_ref, 1.0)
    heads(wka_ref[...], ka_ref, 1.0)
    heads(wb_ref[:, :gw], qb_ref, qscale * LOG2E)
    heads(wb_ref[:, gw:2 * gw], kb_ref, 1.0)
    heads(wb_ref[:, 2 * gw:], vb_ref, 1.0)
    ki = _dot(n_scr[...], wki_ref[...])
    ki_ref[0] = _rms(ki, gk_ref[...]).astype(BF16)
    wit = _dot_nt(wwit_ref[...], n_scr[...])
    wit_ref[0] = wit[:N_HEADS, :] * ((N_HEADS * IDX_DIM) ** -0.5)


def _t5_bucket_np(n):
    n = np.maximum(n, 0)
    max_exact = N_BUCKETS // 2
    nf = np.maximum(n, 1).astype(np.float32)
    large = max_exact + (np.log(nf / np.float32(max_exact)) / np.float32(math.log(MAX_DISTANCE / max_exact))
                         * np.float32(N_BUCKETS - max_exact)).astype(np.int32)
    large = np.minimum(large, N_BUCKETS - 1)
    return np.where(n < max_exact, n, large).astype(np.int32)


def _bias_kernel(relb_ref, bucket_ref, o_ref):
    hd = pl.program_id(0)
    far = relb_ref[N_BUCKETS - 1, hd]
    for r in range(2):
        bk = bucket_ref[r]
        acc = jnp.zeros(bk.shape, F32)
        for k in range(N_BUCKETS - 1):
            acc = jnp.where(bk == k, (relb_ref[k, hd] - far) * LOG2E, acc)
        o_ref[0, r] = acc


def _bias_tiles(rel_bias, t):
    assert t >= MAX_DISTANCE
    s = np.arange(t)[:, None]
    q = np.arange(t)[None, :]
    buckets = np.stack([_t5_bucket_np(q - s), _t5_bucket_np(t + q - s)]).astype(np.int32)
    return pl.pallas_call(
        _bias_kernel,
        grid=(N_HEADS,),
        in_specs=[pl.BlockSpec(memory_space=pltpu.SMEM),
                  pl.BlockSpec((2, t, t), lambda hd: (0, 0, 0))],
        out_specs=pl.BlockSpec((1, 2, t, t), lambda hd: (hd, 0, 0, 0)),
        out_shape=jax.ShapeDtypeStruct((N_HEADS, 2, t, t), F32),
        compiler_params=_params(1),
        name="bias_tiles",
    )(rel_bias, jnp.asarray(buckets))


def _tree(op, xs):
    xs = list(xs)
    while len(xs) > 1:
        xs = [op(xs[j], xs[j + 1]) if j + 1 < len(xs) else xs[j] for j in range(0, len(xs), 2)]
    return xs[0]


def _row_groups(x):
    return [x[r:r + SUBLANES, :] for r in range(0, x.shape[0], SUBLANES)]


def _all_sublanes(op, x):
    for shift in (4, 2, 1):
        x = op(x, pltpu.roll(x, shift, 0))
    return x


def _dsa_kernel(qit_ref, wit_ref, ki_ref, qat_ref, ka_ref, vat_ref, bias_ref, gt_ref, tri_ref, o_ref,
                key_scr, hi_scr, lo_scr, selm_scr, logit_a, logit_b, out_scr, *, topk, n_q_tiles):
    t = DSA_TILE
    kf = float(topk)
    krow = lax.broadcasted_iota(I32, (t, t), 0)
    qcol = lax.broadcasted_iota(I32, (t, t), 1)
    causal = krow <= qcol

    def tile_work(nkb):
        diag = nkb - 1

        wt = wit_ref[0]
        for kb in range(nkb):
            for r0 in range(0, t, SCORE_ROWS):
                kblk = ki_ref[0, kb * t + r0:kb * t + r0 + SCORE_ROWS, :]
                acc = None
                for hd in range(N_HEADS):
                    sc = _dot(kblk, qit_ref[0, hd * IDX_DIM:(hd + 1) * IDX_DIM, :])
                    term = jnp.maximum(sc, 0.0) * wt[hd:hd + 1, :]
                    acc = term if acc is None else acc + term
                acc = acc + 0.0
                if kb == diag:
                    acc = jnp.where(causal[r0:r0 + SCORE_ROWS, :], acc, -jnp.inf)
                bits = pltpu.bitcast(acc, I32)
                key = bits ^ ((bits >> 31) & jnp.int32(0x7FFFFFFF))
                key_scr[kb, r0:r0 + SCORE_ROWS, :] = key
                hi_scr[kb, r0:r0 + SCORE_ROWS, :] = (key >> 16).astype(I16)

        def packed(x):
            return jnp.concatenate([x, x], axis=0).astype(I16)

        def column_total(parts):
            tot = _tree(jnp.add, parts).astype(F32)
            return _all_sublanes(jnp.add, tot[:SUBLANES, :] + tot[SUBLANES:, :])

        def search16(src_scr, need):
            def step(j, ans):
                cand_off = ans | lax.shift_left(jnp.int32(1), 15 - j)
                cand = packed(cand_off - HALF)
                accs = [jnp.zeros((2 * SUBLANES, t), I16) for _ in range(4)]
                n = 0
                for kb in range(nkb):
                    blk = src_scr[kb]
                    for r in range(0, t, 2 * SUBLANES):
                        hit = jnp.where(blk[r:r + 2 * SUBLANES, :] >= cand, ONE16, ZERO16)
                        accs[n % 4] = accs[n % 4] + hit
                        n += 1
                return jnp.where(column_total(accs) >= need, cand_off, ans)
            return lax.fori_loop(0, 16, step, jnp.zeros((SUBLANES, t), I32))

        if nkb * t <= topk:
            thr = jnp.full((SUBLANES, t), INT_MIN, I32)
        else:
            t_hi = search16(hi_scr, kf) - HALF
            t_hi16 = packed(t_hi)
            above = [jnp.zeros((2 * SUBLANES, t), I16) for _ in range(4)]
            n = 0
            for kb in range(nkb):
                for r in range(0, t, 2 * SUBLANES):
                    rows = slice(r, r + 2 * SUBLANES)
                    h = hi_scr[kb, rows, :]
                    lo = ((key_scr[kb, rows, :] & jnp.int32(0xFFFF)) - HALF).astype(I16)
                    lo_scr[kb, rows, :] = jnp.where(h == t_hi16, lo, MIN16)
                    above[n % 4] = above[n % 4] + jnp.where(h > t_hi16, ONE16, ZERO16)
                    n += 1
            t_lo = search16(lo_scr, kf - column_total(above))
            thr = lax.shift_left(t_hi, 16) | t_lo

        n_ge = jnp.zeros((SUBLANES, t), F32)
        for kb in range(nkb):
            blk = key_scr[kb]
            for r, grp in enumerate(_row_groups(blk)):
                rows = slice(r * SUBLANES, (r + 1) * SUBLANES)
                sel = grp >= thr
                n_ge = n_ge + jnp.where(sel, 1.0, 0.0)
                m = jnp.where(sel, 0.0, NEG_BIG)
                if kb == diag:
                    m = jnp.where(causal[rows, :], m, NEG_BIG)
                selm_scr[kb, rows, :] = m
        n_ge = _all_sublanes(jnp.add, n_ge)
        excess = jnp.max(jnp.where(n_ge > kf, 1.0, 0.0))

        @pl.when(excess > 0.0)
        def _():
            thr1 = thr[0:1, :]

            def tied(kb):
                return jnp.where(key_scr[kb] == thr1, 1.0, 0.0)

            def count_tied(kb, acc):
                return acc + jnp.sum(tied(kb), axis=0, keepdims=True)

            n_tied = lax.fori_loop(0, nkb, count_tied, jnp.zeros((1, t), F32))
            need = kf - (n_ge[0:1, :] - n_tied)

            def rewrite(kb, before):
                e = tied(kb)
                rank = _dot(tri_ref[...], e.astype(BF16)) + before
                k = key_scr[kb]
                keep = jnp.where(k > thr1, 0.0,
                                 jnp.where(k == thr1, jnp.where(rank < need, 0.0, NEG_BIG), NEG_BIG))
                selm_scr[kb] = jnp.where(selm_scr[kb] < 0.0, NEG_BIG, keep)
                return before + jnp.sum(e, axis=0, keepdims=True)

            lax.fori_loop(0, nkb, rewrite, jnp.zeros((1, t), F32))

        def head_rows(hd):
            if isinstance(hd, int):
                return slice(hd * HEAD_DIM, (hd + 1) * HEAD_DIM)
            return pl.ds(pl.multiple_of(hd * HEAD_DIM, HEAD_DIM), HEAD_DIM)

        def logits(hd, buf):
            qt = qat_ref[0, head_rows(hd), :]
            maxes = []
            for kb in range(nkb):
                lg = _dot(ka_ref[0, hd, kb * t:(kb + 1) * t, :], qt) + selm_scr[kb]
                if kb >= nkb - 2:
                    lg = lg + bias_ref[hd, diag - kb]
                buf[kb] = lg
                maxes.append(jnp.max(lg, axis=0, keepdims=True))
            return _tree(jnp.maximum, maxes)

        def weighted_values(hd, buf, m):
            if isinstance(hd, int):
                vrows = slice(hd * V_ROWS, (hd + 1) * V_ROWS)
            else:
                vrows = pl.ds(pl.multiple_of(hd * V_ROWS, 16), V_ROWS)
            acc = None
            for kb in range(nkb):
                p = jnp.exp2(buf[kb] - m).astype(BF16)
                pv = _dot(vat_ref[0, vrows, kb * t:(kb + 1) * t], p)
                acc = pv if acc is None else acc + pv
            out_scr[head_rows(hd), :] = acc[:HEAD_DIM, :] * (1.0 / acc[HEAD_DIM:HEAD_DIM + 1, :])

        def head_pair(j, m_even):
            m_odd = logits(2 * j + 1, logit_b)
            weighted_values(2 * j, logit_a, m_even)
            m_next = logits((2 * j + 2) % N_HEADS, logit_a)
            weighted_values(2 * j + 1, logit_b, m_odd)
            return m_next

        lax.fori_loop(0, N_HEADS // 2, head_pair, logits(0, logit_a), unroll=nkb <= HEADS_UNROLL_BLOCKS)

    i = pl.program_id(1)
    for k in range(n_q_tiles):
        pl.when(i == k)(functools.partial(tile_work, k + 1))

    o = out_scr[...]
    o = o * lax.rsqrt(jnp.mean(o * o, axis=0, keepdims=True) + EPS) * gt_ref[...]
    o_ref[0] = o.T.astype(BF16)


def _dsa(qit, wit, ki, qat, ka, vat, bias, g_out, bsz, seq):
    t = DSA_TILE
    nq = seq // t
    gw = GROUP_WIDTH
    topk = min(TOPK_MAX, seq // 4)
    q_fm = pl.BlockSpec((1, gw, t), lambda b, i: (b, 0, i))
    gt = jnp.broadcast_to(g_out.reshape(gw, 1), (gw, t))
    r = np.arange(t)
    lower = jnp.asarray(r[None, :] < r[:, None], dtype=BF16)
    return pl.pallas_call(
        functools.partial(_dsa_kernel, topk=topk, n_q_tiles=nq),
        grid=(bsz, nq),
        in_specs=[q_fm,
                  pl.BlockSpec((1, N_HEADS, t), lambda b, i: (b, 0, i)),
                  pl.BlockSpec((1, seq, IDX_DIM), lambda b, i: (b, 0, 0)),
                  q_fm,
                  pl.BlockSpec((1, N_HEADS, seq, HEAD_DIM), lambda b, i: (b, 0, 0, 0)),
                  pl.BlockSpec((1, N_HEADS * V_ROWS, seq), lambda b, i: (b, 0, 0)),
                  _resident(bias.shape),
                  _resident((gw, t)),
                  _resident((t, t))],
        out_specs=pl.BlockSpec((1, t, gw), lambda b, i: (b, i, 0)),
        out_shape=jax.ShapeDtypeStruct((bsz, seq, gw), BF16),
        scratch_shapes=[pltpu.VMEM((nq, t, t), I32), pltpu.VMEM((nq, t, t), I16),
                        pltpu.VMEM((nq, t, t), I16), pltpu.VMEM((nq, t, t), F32),
                        pltpu.VMEM((nq, t, t), F32), pltpu.VMEM((nq, t, t), F32),
                        pltpu.VMEM((gw, t), F32)],
        compiler_params=_params(2),
        name="dsa",
    )(qit, wit, ki, qat, ka, vat, bias, gt, lower)


def _neg_log2_1m_sigmoid(z2):
    return jnp.maximum(z2, 0.0) + jnp.log2(1.0 + jnp.exp2(-jnp.abs(z2)))


def _split_bf16(x):
    hi = x.astype(BF16)
    lo = (x - hi.astype(F32)).astype(BF16)
    return jnp.concatenate([hi, lo], axis=1)


def _sb_kernel(q_ref, k_ref, v_ref, tri_ref, g_ref, o_ref, z_scr, lh_scr, cs_scr, r_scr, acc_scr):
    t = SB_TILE
    nw = SB_WINDOW
    row = lax.broadcasted_iota(I32, (t, t), 0)
    col = lax.broadcasted_iota(I32, (t, t), 1)
    tiles = []
    for sub in range(SB_PAIR):
        i = pl.program_id(1) * SB_PAIR + sub
        wb = jnp.maximum(i - (nw - 1), 0)
        tiles.append((sub, i, wb, pl.ds(pl.multiple_of(wb * t, t), nw * t)))

    def causal(i, wb, w):
        return (col - row) < (i - wb - w) * t

    def q_rows(sub):
        return slice(sub * t, (sub + 1) * t)

    for sub, i, wb, wrows in tiles:
        for hd in range(N_HEADS):
            p = sub * N_HEADS + hd
            z = _dot_nt(q_ref[0, hd, q_rows(sub), :], k_ref[0, hd, wrows, :])
            z_scr[p] = z
            for w in range(nw):
                m = jnp.where(causal(i, wb, w), _neg_log2_1m_sigmoid(z[:, w * t:(w + 1) * t]), 0.0)
                lh_scr[(p * nw + w) * t:(p * nw + w + 1) * t, :] = _split_bf16(m)
    cs_scr[...] = _dot(lh_scr[...], tri_ref[...])
    for sub, i, wb, wrows in tiles:
        for hd in range(N_HEADS):
            p = sub * N_HEADS + hd
            after = None
            a_blocks = [None] * nw
            for w in reversed(range(nw)):
                cs = cs_scr[(p * nw + w) * t:(p * nw + w + 1) * t, :]
                c = cs[:, :t] if after is None else cs[:, :t] + after
                a = jnp.where(causal(i, wb, w), jnp.exp2(z_scr[p, :, w * t:(w + 1) * t] - c), 0.0)
                a_blocks[w] = a.astype(BF16)
                after = cs[:, t:] if after is None else after + cs[:, t:]
            acc_scr[p] = _dot(jnp.concatenate(a_blocks, axis=1), v_ref[0, hd, wrows, :])
            r_scr[p] = after

    for sub, i, wb, wrows in tiles:
        def far_block(kb, sub=sub):
            rows = pl.ds(pl.multiple_of(kb * t, t), t)
            rmin = None
            for hd in range(N_HEADS):
                p = sub * N_HEADS + hd
                z = _dot_nt(q_ref[0, hd, q_rows(sub), :], k_ref[0, hd, rows, :])
                cs = _dot(_split_bf16(_neg_log2_1m_sigmoid(z)), tri_ref[...])
                r_old = r_scr[p]
                a = jnp.exp2(z - cs[:, :t] - r_old)
                acc_scr[p] += _dot(a.astype(BF16), v_ref[0, hd, rows, :])
                r_new = r_old + cs[:, t:]
                r_scr[p] = r_new
                rmin = r_new if rmin is None else jnp.minimum(rmin, r_new)
            return jnp.min(rmin)

        def cond(c):
            kb, rmin = c
            return jnp.logical_and(kb >= 0, rmin < SB_DEAD)

        def body(c, far_block=far_block):
            kb, _ = c
            return kb - 1, far_block(kb)

        rmin0 = _tree(jnp.minimum, [r_scr[sub * N_HEADS + hd] for hd in range(N_HEADS)])
        lax.while_loop(cond, body, (wb - 1, jnp.min(rmin0)))
        o = jnp.concatenate([acc_scr[sub * N_HEADS + hd] for hd in range(N_HEADS)], axis=1)
        o_ref[0, q_rows(sub), :] = _rms(o, g_ref[...]).astype(BF16)


def _sb(q, k, v, g_out, bsz, seq):
    t = SB_TILE
    tq = SB_PAIR * t
    assert seq >= SB_WINDOW * t and seq % tq == 0
    j = np.arange(t)[:, None]
    s = np.arange(t)[None, :]
    tri = np.concatenate([(j >= s), np.ones((t, t), bool)], axis=1).astype(np.float32)
    tri = np.concatenate([tri, tri], axis=0)
    q_hm = pl.BlockSpec((1, N_HEADS, tq, HEAD_DIM), lambda b, i: (b, 0, i, 0))
    kv_hm = pl.BlockSpec((1, N_HEADS, seq, HEAD_DIM), lambda b, i: (b, 0, 0, 0))
    n_p = SB_PAIR * N_HEADS
    n_rows = n_p * SB_WINDOW * t
    return pl.pallas_call(
        _sb_kernel,
        grid=(bsz, seq // tq),
        in_specs=[q_hm, kv_hm, kv_hm, _resident((2 * t, 2 * t)),
                  pl.BlockSpec((1, GROUP_WIDTH), lambda b, i: (0, 0))],
        out_specs=pl.BlockSpec((1, tq, GROUP_WIDTH), lambda b, i: (b, i, 0)),
        out_shape=jax.ShapeDtypeStruct((bsz, seq, GROUP_WIDTH), BF16),
        scratch_shapes=[pltpu.VMEM((n_p, t, SB_WINDOW * t), F32),
                        pltpu.VMEM((n_rows, 2 * t), BF16),
                        pltpu.VMEM((n_rows, 2 * t), F32),
                        pltpu.VMEM((n_p, t, t), F32),
                        pltpu.VMEM((n_p, t, HEAD_DIM), F32)],
        compiler_params=_params(2),
        name="sb",
    )(q, k, v, jnp.asarray(tri, dtype=BF16), g_out)


def kernel(x, c, w_ada, b_ada, g_pre, g_post, w_ffn_gate, w_ffn_up, w_ffn_down, w_in, g_kidx,
           rel_bias, g_out_a, g_out_b, w_out):
    bsz, seq, d = x.shape
    depth = w_ada.shape[0]
    gw = GROUP_WIDTH
    h = x.reshape(bsz * seq, d)
    bias = _bias_tiles(rel_bias, DSA_TILE)
    for l in range(depth):
        mods = _ada(c, w_ada[l], b_ada[l]).reshape(bsz, -1, d)
        ffn_w = [(w_ffn_gate[l, j].astype(BF16), w_ffn_up[l, j].astype(BF16),
                  w_ffn_down[l, j].astype(BF16)) for j in range(2)]
        w = w_in[l]
        o_ki = 4 * gw
        o_wi = o_ki + IDX_DIM
        o_b = o_wi + N_HEADS
        wt = jnp.concatenate([w[:, :gw], w[:, 2 * gw:3 * gw], w[:, 3 * gw:o_ki]], axis=1).T.astype(BF16)
        wka = w[:, gw:2 * gw].astype(BF16)
        wki = w[:, o_ki:o_wi].astype(BF16)
        wwit = jnp.pad(w[:, o_wi:o_b].T, ((0, 16 - N_HEADS), (0, 0))).astype(BF16)
        wb = w[:, o_b:].astype(BF16)
        woa = w_out[l, :gw].astype(BF16)
        wob = w_out[l, gw:].astype(BF16)
        vec = lambda g: g.reshape(1, -1)

        h1, qat, ka, vat, qit, ki, wit, qb, kb, vb = _ffn(
            h, mods, vec(g_pre[l, 0]), vec(g_post[l, 0]), *ffn_w[0], 0, seq,
            proj=(vec(g_pre[l, 1]), vec(g_kidx[l]), bsz, wt, wka, wki, wwit, wb))
        oa = _dsa(qit, wit, ki, qat, ka, vat, bias, g_out_a[l], bsz, seq)
        ob = _sb(qb, kb, vb, vec(g_out_b[l]), bsz, seq)
        h = _ffn(h1, mods, vec(g_pre[l, 2]), vec(g_post[l, 2]), *ffn_w[1], 2, seq,
                 mix=(oa.reshape(bsz * seq, gw), ob.reshape(bsz * seq, gw), woa, wob,
                      vec(g_post[l, 1])))
    return h.reshape(bsz, seq, d)
```

```python
import functools
import math

import numpy as np
import jax
import jax.numpy as jnp
from jax import lax
from jax.experimental import pallas as pl
from jax.experimental.pallas import tpu as pltpu

F32 = jnp.float32
BF16 = jnp.bfloat16
I32 = jnp.int32
I16 = jnp.int16

EPS = 1e-6
HEAD_DIM = 64
N_HEADS = 8
GROUP_WIDTH = N_HEADS * HEAD_DIM
IDX_DIM = 64
TOPK_MAX = 256
N_BUCKETS = 32
MAX_DISTANCE = 128
SUBLANES = 8
V_ROWS = 80
FFN_CHUNK = 256
TOKEN_TILE = 512
MIX_TOKEN_TILE = 1024
DSA_TILE = 256
SCORE_ROWS = 64
HEADS_UNROLL_BLOCKS = 8
SB_TILE = 128
SB_WINDOW = 3
SB_PAIR = 4
LOG2E = math.log2(math.e)
NEG_BIG = -1e30
SB_DEAD = 104.0 * LOG2E
VMEM_LIMIT = 56 * 1024 * 1024
INT_MIN = np.int32(-2 ** 31)
HALF = 2 ** 15
ONE16, ZERO16, MIN16 = np.int16(1), np.int16(0), np.int16(-HALF)


def _dot(a, b):
    return jnp.dot(a, b, preferred_element_type=F32)


def _dot_nt(a, b):
    return lax.dot_general(a, b, (((1,), (1,)), ((), ())), preferred_element_type=F32)


def _rms(x, g):
    return x * lax.rsqrt(jnp.mean(x * x, axis=-1, keepdims=True) + EPS) * g


def _resident(shape):
    n = len(shape)
    return pl.BlockSpec(shape, lambda *_: (0,) * n, pipeline_mode=pl.Buffered(1))


def _params(n_grid):
    return pltpu.CompilerParams(dimension_semantics=("arbitrary",) * n_grid,
                                vmem_limit_bytes=VMEM_LIMIT)


def _ada_kernel(c_ref, w_ref, b_ref, o_ref):
    c = c_ref[...]
    a = c * jax.nn.sigmoid(c)
    w = w_ref[...]
    a_hi = a.astype(BF16)
    a_lo = (a - a_hi.astype(F32)).astype(BF16)
    w_hi = w.astype(BF16)
    w_lo = (w - w_hi.astype(F32)).astype(BF16)
    o_ref[...] = _dot(a_hi, w_hi) + (_dot(a_hi, w_lo) + _dot(a_lo, w_hi)) + b_ref[...]


def _ada(c, w, b):
    bsz, d = c.shape
    n = w.shape[1]
    tn = 1152 if n % 1152 == 0 else n
    return pl.pallas_call(
        _ada_kernel,
        grid=(n // tn,),
        in_specs=[pl.BlockSpec((bsz, d), lambda j: (0, 0)),
                  pl.BlockSpec((d, tn), lambda j: (0, j)),
                  pl.BlockSpec((1, tn), lambda j: (0, j))],
        out_specs=pl.BlockSpec((bsz, tn), lambda j: (0, j)),
        out_shape=jax.ShapeDtypeStruct((bsz, n), F32),
        compiler_params=_params(1),
        name="ada",
    )(c, w, b.reshape(1, n))


def _ffn_kernel(*refs, sub, mix, proj):
    refs = list(refs)
    take = lambda n: [refs.pop(0) for _ in range(n)]
    (h_ref,) = take(1)
    if mix:
        oa_ref, ob_ref, woa_ref, wob_ref, gpm_ref = take(5)
    mods_ref, gpre_ref, gpost_ref, wg_ref, wu_ref, wd_ref = take(6)
    if proj:
        gpre_next_ref, *proj_in = take(7)
    (o_ref,) = take(1)
    if proj:
        proj_out = take(9)
    n_scr, a_scr = refs
    x = h_ref[...]
    if mix:
        o = _dot(oa_ref[...], woa_ref[...]) + _dot(ob_ref[...], wob_ref[...])
        x = x + mods_ref[0, 5:6, :] * _rms(o, gpm_ref[...])
    shift = mods_ref[0, 3 * sub:3 * sub + 1, :]
    scale = mods_ref[0, 3 * sub + 1:3 * sub + 2, :]
    gate = mods_ref[0, 3 * sub + 2:3 * sub + 3, :]
    n = _rms(x, gpre_ref[...]) * (1.0 + scale) + shift
    n_scr[...] = n.astype(BF16)
    fc = FFN_CHUNK
    for c0 in range(0, wg_ref.shape[1], fc):
        g = _dot(n_scr[...], wg_ref[:, c0:c0 + fc])
        u = _dot(n_scr[...], wu_ref[:, c0:c0 + fc])
        a_scr[:, c0:c0 + fc] = (g * jax.nn.sigmoid(g) * u).astype(BF16)
    f = _dot(a_scr[...], wd_ref[...])
    y = x + 0.5 * gate * _rms(f, gpost_ref[...])
    o_ref[...] = y
    if proj:
        shift = mods_ref[0, 3 * (sub + 1):3 * (sub + 1) + 1, :]
        scale = mods_ref[0, 3 * (sub + 1) + 1:3 * (sub + 1) + 2, :]
        n_scr[...] = (_rms(y, gpre_next_ref[...]) * (1.0 + scale) + shift).astype(BF16)
        _project(n_scr, *proj_in, *proj_out)


def _ffn(h, mods, g_pre, g_post, wg, wu, wd, sub, seq, mix=None, proj=None):
    n_tok, d = h.shape
    tm = min(TOKEN_TILE if proj is not None else MIX_TOKEN_TILE, seq)
    tpb = seq // tm
    row = pl.BlockSpec((tm, d), lambda i: (i, 0))
    vec = pl.BlockSpec((1, d), lambda i: (0, 0))
    mod_spec = pl.BlockSpec((1, mods.shape[1], d), lambda i: (i // tpb, 0, 0))
    args, specs = [h], [row]
    out_specs, out_shape = [row], [jax.ShapeDtypeStruct((n_tok, d), F32)]
    if mix is not None:
        oa, ob, woa, wob, g_post_mix = mix
        gw = oa.shape[1]
        half = pl.BlockSpec((tm, gw), lambda i: (i, 0))
        args += [oa, ob, woa, wob, g_post_mix]
        specs += [half, half, _resident(woa.shape), _resident(wob.shape), vec]
    args += [mods, g_pre, g_post, wg, wu, wd]
    specs += [mod_spec, vec, vec, _resident(wg.shape), _resident(wu.shape), _resident(wd.shape)]
    if proj is not None:
        g_pre_next, g_kidx, bsz, *weights = proj
        gw = GROUP_WIDTH
        args += [g_pre_next, g_kidx, *weights]
        specs += [vec, pl.BlockSpec((1, IDX_DIM), lambda i: (0, 0))] + [_resident(w.shape) for w in weights]
        hm = pl.BlockSpec((1, N_HEADS, tm, HEAD_DIM), lambda i: (i // tpb, 0, i % tpb, 0))
        hm_shape = jax.ShapeDtypeStruct((bsz, N_HEADS, seq, HEAD_DIM), BF16)
        fm = pl.BlockSpec((1, gw, tm), lambda i: (i // tpb, 0, i % tpb))
        fm_shape = jax.ShapeDtypeStruct((bsz, gw, seq), BF16)
        vfm = pl.BlockSpec((1, N_HEADS * V_ROWS, tm), lambda i: (i // tpb, 0, i % tpb))
        out_specs += [fm, hm, vfm, fm,
                      pl.BlockSpec((1, tm, IDX_DIM), lambda i: (i // tpb, i % tpb, 0)),
                      pl.BlockSpec((1, N_HEADS, tm), lambda i: (i // tpb, 0, i % tpb)),
                      hm, hm, hm]
        out_shape += [fm_shape, hm_shape, jax.ShapeDtypeStruct((bsz, N_HEADS * V_ROWS, seq), BF16),
                      fm_shape,
                      jax.ShapeDtypeStruct((bsz, seq, IDX_DIM), BF16),
                      jax.ShapeDtypeStruct((bsz, N_HEADS, seq), F32),
                      hm_shape, hm_shape, hm_shape]
    out = pl.pallas_call(
        functools.partial(_ffn_kernel, sub=sub, mix=mix is not None, proj=proj is not None),
        grid=(n_tok // tm,),
        in_specs=specs,
        out_specs=out_specs,
        out_shape=out_shape,
        scratch_shapes=[pltpu.VMEM((tm, d), BF16), pltpu.VMEM((tm, wd.shape[0]), BF16)],
        compiler_params=_params(1),
        name="ffn_mix" if mix is not None else "ffn_proj" if proj is not None else "ffn",
    )(*args)
    return out if proj is not None else out[0]


def _project(n_scr, gk_ref, wt_ref, wka_ref, wki_ref, wwit_ref, wb_ref,
             qat_ref, ka_ref, vat_ref, qit_ref, ki_ref, wit_ref, qb_ref, kb_ref, vb_ref):
    gw = GROUP_WIDTH
    qscale = HEAD_DIM ** -0.5

    def heads(w, out_ref, mul):
        p = _dot(n_scr[...], w)
        if mul != 1.0:
            p = p * mul
        for hd in range(N_HEADS):
            out_ref[0, hd] = p[:, hd * HEAD_DIM:(hd + 1) * HEAD_DIM].astype(BF16)

    def feature_major(col, out_ref, mul):
        p = _dot_nt(wt_ref[col * gw:(col + 1) * gw, :], n_scr[...])
        if mul != 1.0:
            p = p * mul
        out_ref[0] = p.astype(BF16)

    feature_major(0, qat_ref, qscale * LOG2E)
    v = _dot_nt(wt_ref[gw:2 * gw, :], n_scr[...])
    pad = V_ROWS - HEAD_DIM
    ones_row = (lax.broadcasted_iota(I32, (pad, v.shape[1]), 0) == 0).astype(BF16)
    for hd in range(N_HEADS):
        vat_ref[0, hd * V_ROWS:hd * V_ROWS + HEAD_DIM, :] = v[hd * HEAD_DIM:(hd + 1) * HEAD_DIM, :].astype(BF16)
        vat_ref[0, hd * V_ROWS + HEAD_DIM:(hd + 1) * V_ROWS, :] = ones_row
    feature_major(2, qit_ref, 1.0)
    heads(wka_ref[...], ka_ref, 1.0)
    heads(wb_ref[:, :gw], qb_ref, qscale * LOG2E)
    heads(wb_ref[:, gw:2 * gw], kb_ref, 1.0)
    heads(wb_ref[:, 2 * gw:], vb_ref, 1.0)
    ki = _dot(n_scr[...], wki_ref[...])
    ki_ref[0] = _rms(ki, gk_ref[...]).astype(BF16)
    wit = _dot_nt(wwit_ref[...], n_scr[...])
    wit_ref[0] = wit[:N_HEADS, :] * ((N_HEADS * IDX_DIM) ** -0.5)


def _t5_bucket_np(n):
    n = np.maximum(n, 0)
    max_exact = N_BUCKETS // 2
    nf = np.maximum(n, 1).astype(np.float32)
    large = max_exact + (np.log(nf / np.float32(max_exact)) / np.float32(math.log(MAX_DISTANCE / max_exact))
                         * np.float32(N_BUCKETS - max_exact)).astype(np.int32)
    large = np.minimum(large, N_BUCKETS - 1)
    return np.where(n < max_exact, n, large).astype(np.int32)


def _bias_kernel(relb_ref, bucket_ref, o_ref):
    hd = pl.program_id(0)
    far = relb_ref[N_BUCKETS - 1, hd]
    for r in range(2):
        bk = bucket_ref[r]
        acc = jnp.zeros(bk.shape, F32)
        for k in range(N_BUCKETS - 1):
            acc = jnp.where(bk == k, (relb_ref[k, hd] - far) * LOG2E, acc)
        o_ref[0, r] = acc


def _bias_tiles(rel_bias, t):
    assert t >= MAX_DISTANCE
    s = np.arange(t)[:, None]
    q = np.arange(t)[None, :]
    buckets = np.stack([_t5_bucket_np(q - s), _t5_bucket_np(t + q - s)]).astype(np.int32)
    return pl.pallas_call(
        _bias_kernel,
        grid=(N_HEADS,),
        in_specs=[pl.BlockSpec(memory_space=pltpu.SMEM),
                  pl.BlockSpec((2, t, t), lambda hd: (0, 0, 0))],
        out_specs=pl.BlockSpec((1, 2, t, t), lambda hd: (hd, 0, 0, 0)),
        out_shape=jax.ShapeDtypeStruct((N_HEADS, 2, t, t), F32),
        compiler_params=_params(1),
        name="bias_tiles",
    )(rel_bias, jnp.asarray(buckets))


def _tree(op, xs):
    xs = list(xs)
    while len(xs) > 1:
        xs = [op(xs[j], xs[j + 1]) if j + 1 < len(xs) else xs[j] for j in range(0, len(xs), 2)]
    return xs[0]


def _row_groups(x):
    return [x[r:r + SUBLANES, :] for r in range(0, x.shape[0], SUBLANES)]


def _all_sublanes(op, x):
    for shift in (4, 2, 1):
        x = op(x, pltpu.roll(x, shift, 0))
    return x


def _dsa_kernel(qit_ref, wit_ref, ki_ref, qat_ref, ka_ref, vat_ref, bias_ref, gt_ref, tri_ref, o_ref,
                key_scr, hi_scr, lo_scr, selm_scr, logit_a, logit_b, out_scr, *, topk, n_q_tiles):
    t = DSA_TILE
    kf = float(topk)
    krow = lax.broadcasted_iota(I32, (t, t), 0)
    qcol = lax.broadcasted_iota(I32, (t, t), 1)
    causal = krow <= qcol

    def tile_work(nkb):
        diag = nkb - 1

        wt = wit_ref[0]
        for kb in range(nkb):
            for r0 in range(0, t, SCORE_ROWS):
                kblk = ki_ref[0, kb * t + r0:kb * t + r0 + SCORE_ROWS, :]
                acc = None
                for hd in range(N_HEADS):
                    sc = _dot(kblk, qit_ref[0, hd * IDX_DIM:(hd + 1) * IDX_DIM, :])
                    term = jnp.maximum(sc, 0.0) * wt[hd:hd + 1, :]
                    acc = term if acc is None else acc + term
                acc = acc + 0.0
                if kb == diag:
                    acc = jnp.where(causal[r0:r0 + SCORE_ROWS, :], acc, -jnp.inf)
                bits = pltpu.bitcast(acc, I32)
                key = bits ^ ((bits >> 31) & jnp.int32(0x7FFFFFFF))
                key_scr[kb, r0:r0 + SCORE_ROWS, :] = key
                hi_scr[kb, r0:r0 + SCORE_ROWS, :] = (key >> 16).astype(I16)

        def packed(x):
            return jnp.concatenate([x, x], axis=0).astype(I16)

        def column_total(parts):
            tot = _tree(jnp.add, parts).astype(F32)
            return _all_sublanes(jnp.add, tot[:SUBLANES, :] + tot[SUBLANES:, :])

        def search16(src_scr, need):
            def step(j, ans):
                cand_off = ans | lax.shift_left(jnp.int32(1), 15 - j)
                cand = packed(cand_off - HALF)
                accs = [jnp.zeros((2 * SUBLANES, t), I16) for _ in range(4)]
                n = 0
                for kb in range(nkb):
                    blk = src_scr[kb]
                    for r in range(0, t, 2 * SUBLANES):
                        hit = jnp.where(blk[r:r + 2 * SUBLANES, :] >= cand, ONE16, ZERO16)
                        accs[n % 4] = accs[n % 4] + hit
                        n += 1
                return jnp.where(column_total(accs) >= need, cand_off, ans)
            return lax.fori_loop(0, 16, step, jnp.zeros((SUBLANES, t), I32))

        if nkb * t <= topk:
            thr = jnp.full((SUBLANES, t), INT_MIN, I32)
        else:
            t_hi = search16(hi_scr, kf) - HALF
            t_hi16 = packed(t_hi)
            above = [jnp.zeros((2 * SUBLANES, t), I16) for _ in range(4)]
            n = 0
            for kb in range(nkb):
                for r in range(0, t, 2 * SUBLANES):
                    rows = slice(r, r + 2 * SUBLANES)
                    h = hi_scr[kb, rows, :]
                    lo = ((key_scr[kb, rows, :] & jnp.int32(0xFFFF)) - HALF).astype(I16)
                    lo_scr[kb, rows, :] = jnp.where(h == t_hi16, lo, MIN16)
                    above[n % 4] = above[n % 4] + jnp.where(h > t_hi16, ONE16, ZERO16)
                    n += 1
            t_lo = search16(lo_scr, kf - column_total(above))
            thr = lax.shift_left(t_hi, 16) | t_lo

        n_ge = jnp.zeros((SUBLANES, t), F32)
        for kb in range(nkb):
            blk = key_scr[kb]
            for r, grp in enumerate(_row_groups(blk)):
                rows = slice(r * SUBLANES, (r + 1) * SUBLANES)
                sel = grp >= thr
                n_ge = n_ge + jnp.where(sel, 1.0, 0.0)
                m = jnp.where(sel, 0.0, NEG_BIG)
                if kb == diag:
                    m = jnp.where(causal[rows, :], m, NEG_BIG)
                selm_scr[kb, rows, :] = m
        n_ge = _all_sublanes(jnp.add, n_ge)
        excess = jnp.max(jnp.where(n_ge > kf, 1.0, 0.0))

        @pl.when(excess > 0.0)
        def _():
            thr1 = thr[0:1, :]

            def tied(kb):
                return jnp.where(key_scr[kb] == thr1, 1.0, 0.0)

            def count_tied(kb, acc):
                return acc + jnp.sum(tied(kb), axis=0, keepdims=True)

            n_tied = lax.fori_loop(0, nkb, count_tied, jnp.zeros((1, t), F32))
            need = kf - (n_ge[0:1, :] - n_tied)

            def rewrite(kb, before):
                e = tied(kb)
                rank = _dot(tri_ref[...], e.astype(BF16)) + before
                k = key_scr[kb]
                keep = jnp.where(k > thr1, 0.0,
                                 jnp.where(k == thr1, jnp.where(rank < need, 0.0, NEG_BIG), NEG_BIG))
                selm_scr[kb] = jnp.where(selm_scr[kb] < 0.0, NEG_BIG, keep)
                return before + jnp.sum(e, axis=0, keepdims=True)

            lax.fori_loop(0, nkb, rewrite, jnp.zeros((1, t), F32))

        def head_rows(hd):
            if isinstance(hd, int):
                return slice(hd * HEAD_DIM, (hd + 1) * HEAD_DIM)
            return pl.ds(pl.multiple_of(hd * HEAD_DIM, HEAD_DIM), HEAD_DIM)

        def logits(hd, buf):
            qt = qat_ref[0, head_rows(hd), :]
            maxes = []
            for kb in range(nkb):
                lg = _dot(ka_ref[0, hd, kb * t:(kb + 1) * t, :], qt) + selm_scr[kb]
                if kb >= nkb - 2:
                    lg = lg + bias_ref[hd, diag - kb]
                buf[kb] = lg
                maxes.append(jnp.max(lg, axis=0, keepdims=True))
            return _tree(jnp.maximum, maxes)

        def weighted_values(hd, buf, m):
            if isinstance(hd, int):
                vrows = slice(hd * V_ROWS, (hd + 1) * V_ROWS)
            else:
                vrows = pl.ds(pl.multiple_of(hd * V_ROWS, 16), V_ROWS)
            acc = None
            for kb in range(nkb):
                p = jnp.exp2(buf[kb] - m).astype(BF16)
                pv = _dot(vat_ref[0, vrows, kb * t:(kb + 1) * t], p)
                acc = pv if acc is None else acc + pv
            out_scr[head_rows(hd), :] = acc[:HEAD_DIM, :] * (1.0 / acc[HEAD_DIM:HEAD_DIM + 1, :])

        def head_pair(j, m_even):
            m_odd = logits(2 * j + 1, logit_b)
            weighted_values(2 * j, logit_a, m_even)
            m_next = logits((2 * j + 2) % N_HEADS, logit_a)
            weighted_values(2 * j + 1, logit_b, m_odd)
            return m_next

        lax.fori_loop(0, N_HEADS // 2, head_pair, logits(0, logit_a), unroll=nkb <= HEADS_UNROLL_BLOCKS)

    i = pl.program_id(1)
    for k in range(n_q_tiles):
        pl.when(i == k)(functools.partial(tile_work, k + 1))

    o = out_scr[...]
    o = o * lax.rsqrt(jnp.mean(o * o, axis=0, keepdims=True) + EPS) * gt_ref[...]
    o_ref[0] = o.T.astype(BF16)


def _dsa(qit, wit, ki, qat, ka, vat, bias, g_out, bsz, seq):
    t = DSA_TILE
    nq = seq // t
    gw = GROUP_WIDTH
    topk = min(TOPK_MAX, seq // 4)
    q_fm = pl.BlockSpec((1, gw, t), lambda b, i: (b, 0, i))
    gt = jnp.broadcast_to(g_out.reshape(gw, 1), (gw, t))
    r = np.arange(t)
    lower = jnp.asarray(r[None, :] < r[:, None], dtype=BF16)
    return pl.pallas_call(
        functools.partial(_dsa_kernel, topk=topk, n_q_tiles=nq),
        grid=(bsz, nq),
        in_specs=[q_fm,
                  pl.BlockSpec((1, N_HEADS, t), lambda b, i: (b, 0, i)),
                  pl.BlockSpec((1, seq, IDX_DIM), lambda b, i: (b, 0, 0)),
                  q_fm,
                  pl.BlockSpec((1, N_HEADS, seq, HEAD_DIM), lambda b, i: (b, 0, 0, 0)),
                  pl.BlockSpec((1, N_HEADS * V_ROWS, seq), lambda b, i: (b, 0, 0)),
                  _resident(bias.shape),
                  _resident((gw, t)),
                  _resident((t, t))],
        out_specs=pl.BlockSpec((1, t, gw), lambda b, i: (b, i, 0)),
        out_shape=jax.ShapeDtypeStruct((bsz, seq, gw), BF16),
        scratch_shapes=[pltpu.VMEM((nq, t, t), I32), pltpu.VMEM((nq, t, t), I16),
                        pltpu.VMEM((nq, t, t), I16), pltpu.VMEM((nq, t, t), F32),
                        pltpu.VMEM((nq, t, t), F32), pltpu.VMEM((nq, t, t), F32),
                        pltpu.VMEM((gw, t), F32)],
        compiler_params=_params(2),
        name="dsa",
    )(qit, wit, ki, qat, ka, vat, bias, gt, lower)


def _neg_log2_1m_sigmoid(z2):
    return jnp.maximum(z2, 0.0) + jnp.log2(1.0 + jnp.exp2(-jnp.abs(z2)))


def _split_bf16(x):
    hi = x.astype(BF16)
    lo = (x - hi.astype(F32)).astype(BF16)
    return jnp.concatenate([hi, lo], axis=1)


def _sb_kernel(q_ref, k_ref, v_ref, tri_ref, g_ref, o_ref, z_scr, lh_scr, cs_scr, r_scr, acc_scr):
    t = SB_TILE
    nw = SB_WINDOW
    row = lax.broadcasted_iota(I32, (t, t), 0)
    col = lax.broadcasted_iota(I32, (t, t), 1)
    tiles = []
    for sub in range(SB_PAIR):
        i = pl.program_id(1) * SB_PAIR + sub
        wb = jnp.maximum(i - (nw - 1), 0)
        tiles.append((sub, i, wb, pl.ds(pl.multiple_of(wb * t, t), nw * t)))

    def causal(i, wb, w):
        return (col - row) < (i - wb - w) * t

    def q_rows(sub):
        return slice(sub * t, (sub + 1) * t)

    for sub, i, wb, wrows in tiles:
        for hd in range(N_HEADS):
            p = sub * N_HEADS + hd
            z = _dot_nt(q_ref[0, hd, q_rows(sub), :], k_ref[0, hd, wrows, :])
            z_scr[p] = z
            for w in range(nw):
                m = jnp.where(causal(i, wb, w), _neg_log2_1m_sigmoid(z[:, w * t:(w + 1) * t]), 0.0)
                lh_scr[(p * nw + w) * t:(p * nw + w + 1) * t, :] = _split_bf16(m)
    cs_scr[...] = _dot(lh_scr[...], tri_ref[...])
    for sub, i, wb, wrows in tiles:
        for hd in range(N_HEADS):
            p = sub * N_HEADS + hd
            after = None
            a_blocks = [None] * nw
            for w in reversed(range(nw)):
                cs = cs_scr[(p * nw + w) * t:(p * nw + w + 1) * t, :]
                c = cs[:, :t] if after is None else cs[:, :t] + after
                a = jnp.where(causal(i, wb, w), jnp.exp2(z_scr[p, :, w * t:(w + 1) * t] - c), 0.0)
                a_blocks[w] = a.astype(BF16)
                after = cs[:, t:] if after is None else after + cs[:, t:]
            acc_scr[p] = _dot(jnp.concatenate(a_blocks, axis=1), v_ref[0, hd, wrows, :])
            r_scr[p] = after

    for sub, i, wb, wrows in tiles:
        def far_block(kb, sub=sub):
            rows = pl.ds(pl.multiple_of(kb * t, t), t)
            rmin = None
            for hd in range(N_HEADS):
                p = sub * N_HEADS + hd
                z = _dot_nt(q_ref[0, hd, q_rows(sub), :], k_ref[0, hd, rows, :])
                cs = _dot(_split_bf16(_neg_log2_1m_sigmoid(z)), tri_ref[...])
                r_old = r_scr[p]
                a = jnp.exp2(z - cs[:, :t] - r_old)
                acc_scr[p] += _dot(a.astype(BF16), v_ref[0, hd, rows, :])
                r_new = r_old + cs[:, t:]
                r_scr[p] = r_new
                rmin = r_new if rmin is None else jnp.minimum(rmin, r_new)
            return jnp.min(rmin)

        def cond(c):
            kb, rmin = c
            return jnp.logical_and(kb >= 0, rmin < SB_DEAD)

        def body(c, far_block=far_block):
            kb, _ = c
            return kb - 1, far_block(kb)

        rmin0 = _tree(jnp.minimum, [r_scr[sub * N_HEADS + hd] for hd in range(N_HEADS)])
        lax.while_loop(cond, body, (wb - 1, jnp.min(rmin0)))
        o = jnp.concatenate([acc_scr[sub * N_HEADS + hd] for hd in range(N_HEADS)], axis=1)
        o_ref[0, q_rows(sub), :] = _rms(o, g_ref[...]).astype(BF16)


def _sb(q, k, v, g_out, bsz, seq):
    t = SB_TILE
    tq = SB_PAIR * t
    assert seq >= SB_WINDOW * t and seq % tq == 0
    j = np.arange(t)[:, None]
    s = np.arange(t)[None, :]
    tri = np.concatenate([(j >= s), np.ones((t, t), bool)], axis=1).astype(np.float32)
    tri = np.concatenate([tri, tri], axis=0)
    q_hm = pl.BlockSpec((1, N_HEADS, tq, HEAD_DIM), lambda b, i: (b, 0, i, 0))
    kv_hm = pl.BlockSpec((1, N_HEADS, seq, HEAD_DIM), lambda b, i: (b, 0, 0, 0))
    n_p = SB_PAIR * N_HEADS
    n_rows = n_p * SB_WINDOW * t
    return pl.pallas_call(
        _sb_kernel,
        grid=(bsz, seq // tq),
        in_specs=[q_hm, kv_hm, kv_hm, _resident((2 * t, 2 * t)),
                  pl.BlockSpec((1, GROUP_WIDTH), lambda b, i: (0, 0))],
        out_specs=pl.BlockSpec((1, tq, GROUP_WIDTH), lambda b, i: (b, i, 0)),
        out_shape=jax.ShapeDtypeStruct((bsz, seq, GROUP_WIDTH), BF16),
        scratch_shapes=[pltpu.VMEM((n_p, t, SB_WINDOW * t), F32),
                        pltpu.VMEM((n_rows, 2 * t), BF16),
                        pltpu.VMEM((n_rows, 2 * t), F32),
                        pltpu.VMEM((n_p, t, t), F32),
                        pltpu.VMEM((n_p, t, HEAD_DIM), F32)],
        compiler_params=_params(2),
        name="sb",
    )(q, k, v, jnp.asarray(tri, dtype=BF16), g_out)


def kernel(x, c, w_ada, b_ada, g_pre, g_post, w_ffn_gate, w_ffn_up, w_ffn_down, w_in, g_kidx,
           rel_bias, g_out_a, g_out_b, w_out):
    bsz, seq, d = x.shape
    depth = w_ada.shape[0]
    gw = GROUP_WIDTH
    h = x.reshape(bsz * seq, d)
    bias = _bias_tiles(rel_bias, DSA_TILE)
    for l in range(depth):
        mods = _ada(c, w_ada[l], b_ada[l]).reshape(bsz, -1, d)
        ffn_w = [(w_ffn_gate[l, j].astype(BF16), w_ffn_up[l, j].astype(BF16),
                  w_ffn_down[l, j].astype(BF16)) for j in range(2)]
        w = w_in[l]
        o_ki = 4 * gw
        o_wi = o_ki + IDX_DIM
        o_b = o_wi + N_HEADS
        wt = jnp.concatenate([w[:, :gw], w[:, 2 * gw:3 * gw], w[:, 3 * gw:o_ki]], axis=1).T.astype(BF16)
        wka = w[:, gw:2 * gw].astype(BF16)
        wki = w[:, o_ki:o_wi].astype(BF16)
        wwit = jnp.pad(w[:, o_wi:o_b].T, ((0, 16 - N_HEADS), (0, 0))).astype(BF16)
        wb = w[:, o_b:].astype(BF16)
        woa = w_out[l, :gw].astype(BF16)
        wob = w_out[l, gw:].astype(BF16)
        vec = lambda g: g.reshape(1, -1)

        h1, qat, ka, vat, qit, ki, wit, qb, kb, vb = _ffn(
            h, mods, vec(g_pre[l, 0]), vec(g_post[l, 0]), *ffn_w[0], 0, seq,
            proj=(vec(g_pre[l, 1]), vec(g_kidx[l]), bsz, wt, wka, wki, wwit, wb))
        oa = _dsa(qit, wit, ki, qat, ka, vat, bias, g_out_a[l], bsz, seq)
        ob = _sb(qb, kb, vb, vec(g_out_b[l]), bsz, seq)
        h = _ffn(h1, mods, vec(g_pre[l, 2]), vec(g_post[l, 2]), *ffn_w[1], 2, seq,
                 mix=(oa.reshape(bsz * seq, gw), ob.reshape(bsz * seq, gw), woa, wob,
                      vec(g_post[l, 1])))
    return h.reshape(bsz, seq, d)
```

```python
import functools
import math

import numpy as np
import jax
import jax.numpy as jnp
from jax import lax
from jax.experimental import pallas as pl
from jax.experimental.pallas import tpu as pltpu

F32 = jnp.float32
BF16 = jnp.bfloat16
I32 = jnp.int32
I16 = jnp.int16

EPS = 1e-6
HEAD_DIM = 64
N_HEADS = 8
GROUP_WIDTH = N_HEADS * HEAD_DIM
IDX_DIM = 64
TOPK_MAX = 256
N_BUCKETS = 32
MAX_DISTANCE = 128
SUBLANES = 8
V_ROWS = 80
FFN_CHUNK = 256
TOKEN_TILE = 512
MIX_TOKEN_TILE = 1024
DSA_TILE = 256
SCORE_ROWS = 64
HEADS_UNROLL_BLOCKS = 5
SB_TILE = 128
SB_WINDOW = 3
SB_PAIR = 4
LOG2E = math.log2(math.e)
NEG_BIG = -1e30
SB_DEAD = 104.0 * LOG2E
VMEM_LIMIT = 56 * 1024 * 1024
INT_MIN = np.int32(-2 ** 31)
HALF = 2 ** 15
ONE16, ZERO16, MIN16 = np.int16(1), np.int16(0), np.int16(-HALF)


def _dot(a, b):
    return jnp.dot(a, b, preferred_element_type=F32)


def _dot_nt(a, b):
    return lax.dot_general(a, b, (((1,), (1,)), ((), ())), preferred_element_type=F32)


def _rms(x, g):
    return x * lax.rsqrt(jnp.mean(x * x, axis=-1, keepdims=True) + EPS) * g


def _resident(shape):
    n = len(shape)
    return pl.BlockSpec(shape, lambda *_: (0,) * n, pipeline_mode=pl.Buffered(1))


def _params(n_grid):
    return pltpu.CompilerParams(dimension_semantics=("arbitrary",) * n_grid,
                                vmem_limit_bytes=VMEM_LIMIT)


def _ada_kernel(c_ref, w_ref, b_ref, o_ref):
    c = c_ref[...]
    a = c * jax.nn.sigmoid(c)
    w = w_ref[...]
    a_hi = a.astype(BF16)
    a_lo = (a - a_hi.astype(F32)).astype(BF16)
    w_hi = w.astype(BF16)
    w_lo = (w - w_hi.astype(F32)).astype(BF16)
    o_ref[...] = _dot(a_hi, w_hi) + (_dot(a_hi, w_lo) + _dot(a_lo, w_hi)) + b_ref[...]


def _ada(c, w, b):
    bsz, d = c.shape
    n = w.shape[1]
    tn = 1152 if n % 1152 == 0 else n
    return pl.pallas_call(
        _ada_kernel,
        grid=(n // tn,),
        in_specs=[pl.BlockSpec((bsz, d), lambda j: (0, 0)),
                  pl.BlockSpec((d, tn), lambda j: (0, j)),
                  pl.BlockSpec((1, tn), lambda j: (0, j))],
        out_specs=pl.BlockSpec((bsz, tn), lambda j: (0, j)),
        out_shape=jax.ShapeDtypeStruct((bsz, n), F32),
        compiler_params=_params(1),
        name="ada",
    )(c, w, b.reshape(1, n))


def _ffn_kernel(*refs, sub, mix, proj):
    refs = list(refs)
    take = lambda n: [refs.pop(0) for _ in range(n)]
    (h_ref,) = take(1)
    if mix:
        oa_ref, ob_ref, woa_ref, wob_ref, gpm_ref = take(5)
    mods_ref, gpre_ref, gpost_ref, wg_ref, wu_ref, wd_ref = take(6)
    if proj:
        gpre_next_ref, *proj_in = take(7)
    (o_ref,) = take(1)
    if proj:
        proj_out = take(9)
    n_scr, a_scr = refs
    x = h_ref[...]
    if mix:
        o = _dot(oa_ref[...], woa_ref[...]) + _dot(ob_ref[...], wob_ref[...])
        x = x + mods_ref[0, 5:6, :] * _rms(o, gpm_ref[...])
    shift = mods_ref[0, 3 * sub:3 * sub + 1, :]
    scale = mods_ref[0, 3 * sub + 1:3 * sub + 2, :]
    gate = mods_ref[0, 3 * sub + 2:3 * sub + 3, :]
    n = _rms(x, gpre_ref[...]) * (1.0 + scale) + shift
    n_scr[...] = n.astype(BF16)
    fc = FFN_CHUNK
    for c0 in range(0, wg_ref.shape[1], fc):
        g = _dot(n_scr[...], wg_ref[:, c0:c0 + fc])
        u = _dot(n_scr[...], wu_ref[:, c0:c0 + fc])
        a_scr[:, c0:c0 + fc] = (g * jax.nn.sigmoid(g) * u).astype(BF16)
    f = _dot(a_scr[...], wd_ref[...])
    y = x + 0.5 * gate * _rms(f, gpost_ref[...])
    o_ref[...] = y
    if proj:
        shift = mods_ref[0, 3 * (sub + 1):3 * (sub + 1) + 1, :]
        scale = mods_ref[0, 3 * (sub + 1) + 1:3 * (sub + 1) + 2, :]
        n_scr[...] = (_rms(y, gpre_next_ref[...]) * (1.0 + scale) + shift).astype(BF16)
        _project(n_scr, *proj_in, *proj_out)


def _ffn(h, mods, g_pre, g_post, wg, wu, wd, sub, seq, mix=None, proj=None):
    n_tok, d = h.shape
    tm = min(TOKEN_TILE if proj is not None else MIX_TOKEN_TILE, seq)
    tpb = seq // tm
    row = pl.BlockSpec((tm, d), lambda i: (i, 0))
    vec = pl.BlockSpec((1, d), lambda i: (0, 0))
    mod_spec = pl.BlockSpec((1, mods.shape[1], d), lambda i: (i // tpb, 0, 0))
    args, specs = [h], [row]
    out_specs, out_shape = [row], [jax.ShapeDtypeStruct((n_tok, d), F32)]
    if mix is not None:
        oa, ob, woa, wob, g_post_mix = mix
        gw = oa.shape[1]
        half = pl.BlockSpec((tm, gw), lambda i: (i, 0))
        args += [oa, ob, woa, wob, g_post_mix]
        specs += [half, half, _resident(woa.shape), _resident(wob.shape), vec]
    args += [mods, g_pre, g_post, wg, wu, wd]
    specs += [mod_spec, vec, vec, _resident(wg.shape), _resident(wu.shape), _resident(wd.shape)]
    if proj is not None:
        g_pre_next, g_kidx, bsz, *weights = proj
        gw = GROUP_WIDTH
        args += [g_pre_next, g_kidx, *weights]
        specs += [vec, pl.BlockSpec((1, IDX_DIM), lambda i: (0, 0))] + [_resident(w.shape) for w in weights]
        hm = pl.BlockSpec((1, N_HEADS, tm, HEAD_DIM), lambda i: (i // tpb, 0, i % tpb, 0))
        hm_shape = jax.ShapeDtypeStruct((bsz, N_HEADS, seq, HEAD_DIM), BF16)
        fm = pl.BlockSpec((1, gw, tm), lambda i: (i // tpb, 0, i % tpb))
        fm_shape = jax.ShapeDtypeStruct((bsz, gw, seq), BF16)
        vfm = pl.BlockSpec((1, N_HEADS * V_ROWS, tm), lambda i: (i // tpb, 0, i % tpb))
        out_specs += [fm, hm, vfm, fm,
                      pl.BlockSpec((1, tm, IDX_DIM), lambda i: (i // tpb, i % tpb, 0)),
                      pl.BlockSpec((1, N_HEADS, tm), lambda i: (i // tpb, 0, i % tpb)),
                      hm, hm, hm]
        out_shape += [fm_shape, hm_shape, jax.ShapeDtypeStruct((bsz, N_HEADS * V_ROWS, seq), BF16),
                      fm_shape,
                      jax.ShapeDtypeStruct((bsz, seq, IDX_DIM), BF16),
                      jax.ShapeDtypeStruct((bsz, N_HEADS, seq), F32),
                      hm_shape, hm_shape, hm_shape]
    out = pl.pallas_call(
        functools.partial(_ffn_kernel, sub=sub, mix=mix is not None, proj=proj is not None),
        grid=(n_tok // tm,),
        in_specs=specs,
        out_specs=out_specs,
        out_shape=out_shape,
        scratch_shapes=[pltpu.VMEM((tm, d), BF16), pltpu.VMEM((tm, wd.shape[0]), BF16)],
        compiler_params=_params(1),
        name="ffn_mix" if mix is not None else "ffn_proj" if proj is not None else "ffn",
    )(*args)
    return out if proj is not None else out[0]


def _project(n_scr, gk_ref, wt_ref, wka_ref, wki_ref, wwit_ref, wb_ref,
             qat_ref, ka_ref, vat_ref, qit_ref, ki_ref, wit_ref, qb_ref, kb_ref, vb_ref):
    gw = GROUP_WIDTH
    qscale = HEAD_DIM ** -0.5

    def heads(w, out_ref, mul):
        p = _dot(n_scr[...], w)
        if mul != 1.0:
            p = p * mul
        for hd in range(N_HEADS):
            out_ref[0, hd] = p[:, hd * HEAD_DIM:(hd + 1) * HEAD_DIM].astype(BF16)

    def feature_major(col, out_ref, mul):
        p = _dot_nt(wt_ref[col * gw:(col + 1) * gw, :], n_scr[...])
        if mul != 1.0:
            p = p * mul
        out_ref[0] = p.astype(BF16)

    feature_major(0, qat_ref, qscale * LOG2E)
    v = _dot_nt(wt_ref[gw:2 * gw, :], n_scr[...])
    pad = V_ROWS - HEAD_DIM
    ones_row = (lax.broadcasted_iota(I32, (pad, v.shape[1]), 0) == 0).astype(BF16)
    for hd in range(N_HEADS):
        vat_ref[0, hd * V_ROWS:hd * V_ROWS + HEAD_DIM, :] = v[hd * HEAD_DIM:(hd + 1) * HEAD_DIM, :].astype(BF16)
        vat_ref[0, hd * V_ROWS + HEAD_DIM:(hd + 1) * V_ROWS, :] = ones_row
    feature_major(2, qit_ref, 1.0)
    heads(wka_ref[...], ka_ref, 1.0)
    heads(wb_ref[:, :gw], qb_ref, qscale * LOG2E)
    heads(wb_ref[:, gw:2 * gw], kb_ref, 1.0)
    heads(wb_ref[:, 2 * gw:], vb_ref, 1.0)
    ki = _dot(n_scr[...], wki_ref[...])
    ki_ref[0] = _rms(ki, gk_ref[...]).astype(BF16)
    wit = _dot_nt(wwit_ref[...], n_scr[...])
    wit_ref[0] = wit[:N_HEADS, :] * ((N_HEADS * IDX_DIM) ** -0.5)


def _t5_bucket_np(n):
    n = np.maximum(n, 0)
    max_exact = N_BUCKETS // 2
    nf = np.maximum(n, 1).astype(np.float32)
    large = max_exact + (np.log(nf / np.float32(max_exact)) / np.float32(math.log(MAX_DISTANCE / max_exact))
                         * np.float32(N_BUCKETS - max_exact)).astype(np.int32)
    large = np.minimum(large, N_BUCKETS - 1)
    return np.where(n < max_exact, n, large).astype(np.int32)


def _bias_kernel(relb_ref, bucket_ref, o_ref):
    hd = pl.program_id(0)
    far = relb_ref[N_BUCKETS - 1, hd]
    for r in range(2):
        bk = bucket_ref[r]
        acc = jnp.zeros(bk.shape, F32)
        for k in range(N_BUCKETS - 1):
            acc = jnp.where(bk == k, (relb_ref[k, hd] - far) * LOG2E, acc)
        o_ref[0, r] = acc


def _bias_tiles(rel_bias, t):
    assert t >= MAX_DISTANCE
    s = np.arange(t)[:, None]
    q = np.arange(t)[None, :]
    buckets = np.stack([_t5_bucket_np(q - s), _t5_bucket_np(t + q - s)]).astype(np.int32)
    return pl.pallas_call(
        _bias_kernel,
        grid=(N_HEADS,),
        in_specs=[pl.BlockSpec(memory_space=pltpu.SMEM),
                  pl.BlockSpec((2, t, t), lambda hd: (0, 0, 0))],
        out_specs=pl.BlockSpec((1, 2, t, t), lambda hd: (hd, 0, 0, 0)),
        out_shape=jax.ShapeDtypeStruct((N_HEADS, 2, t, t), F32),
        compiler_params=_params(1),
        name="bias_tiles",
    )(rel_bias, jnp.asarray(buckets))


def _tree(op, xs):
    xs = list(xs)
    while len(xs) > 1:
        xs = [op(xs[j], xs[j + 1]) if j + 1 < len(xs) else xs[j] for j in range(0, len(xs), 2)]
    return xs[0]


def _row_groups(x):
    return [x[r:r + SUBLANES, :] for r in range(0, x.shape[0], SUBLANES)]


def _all_sublanes(op, x):
    for shift in (4, 2, 1):
        x = op(x, pltpu.roll(x, shift, 0))
    return x


def _dsa_kernel(qit_ref, wit_ref, ki_ref, qat_ref, ka_ref, vat_ref, bias_ref, gt_ref, tri_ref, o_ref,
                key_scr, hi_scr, lo_scr, selm_scr, logit_a, logit_b, out_scr, *, topk, n_q_tiles):
    t = DSA_TILE
    kf = float(topk)
    krow = lax.broadcasted_iota(I32, (t, t), 0)
    qcol = lax.broadcasted_iota(I32, (t, t), 1)
    causal = krow <= qcol

    def tile_work(nkb):
        diag = nkb - 1

        wt = wit_ref[0]
        for kb in range(nkb):
            for r0 in range(0, t, SCORE_ROWS):
                kblk = ki_ref[0, kb * t + r0:kb * t + r0 + SCORE_ROWS, :]
                acc = None
                for hd in range(N_HEADS):
                    sc = _dot(kblk, qit_ref[0, hd * IDX_DIM:(hd + 1) * IDX_DIM, :])
                    term = jnp.maximum(sc, 0.0) * wt[hd:hd + 1, :]
                    acc = term if acc is None else acc + term
                acc = acc + 0.0
                if kb == diag:
                    acc = jnp.where(causal[r0:r0 + SCORE_ROWS, :], acc, -jnp.inf)
                bits = pltpu.bitcast(acc, I32)
                key = bits ^ ((bits >> 31) & jnp.int32(0x7FFFFFFF))
                key_scr[kb, r0:r0 + SCORE_ROWS, :] = key
                hi_scr[kb, r0:r0 + SCORE_ROWS, :] = (key >> 16).astype(I16)

        def packed(x):
            return jnp.concatenate([x, x], axis=0).astype(I16)

        def column_total(parts):
            tot = _tree(jnp.add, parts).astype(F32)
            return _all_sublanes(jnp.add, tot[:SUBLANES, :] + tot[SUBLANES:, :])

        def search16(src_scr, need):
            def step(j, ans):
                cand_off = ans | lax.shift_left(jnp.int32(1), 15 - j)
                cand = packed(cand_off - HALF)
                accs = [jnp.zeros((2 * SUBLANES, t), I16) for _ in range(4)]
                n = 0
                for kb in range(nkb):
                    blk = src_scr[kb]
                    for r in range(0, t, 2 * SUBLANES):
                        hit = jnp.where(blk[r:r + 2 * SUBLANES, :] >= cand, ONE16, ZERO16)
                        accs[n % 4] = accs[n % 4] + hit
                        n += 1
                return jnp.where(column_total(accs) >= need, cand_off, ans)
            return lax.fori_loop(0, 16, step, jnp.zeros((SUBLANES, t), I32))

        if nkb * t <= topk:
            thr = jnp.full((SUBLANES, t), INT_MIN, I32)
        else:
            t_hi = search16(hi_scr, kf) - HALF
            t_hi16 = packed(t_hi)
            above = [jnp.zeros((2 * SUBLANES, t), I16) for _ in range(4)]
            n = 0
            for kb in range(nkb):
                for r in range(0, t, 2 * SUBLANES):
                    rows = slice(r, r + 2 * SUBLANES)
                    h = hi_scr[kb, rows, :]
                    lo = ((key_scr[kb, rows, :] & jnp.int32(0xFFFF)) - HALF).astype(I16)
                    lo_scr[kb, rows, :] = jnp.where(h == t_hi16, lo, MIN16)
                    above[n % 4] = above[n % 4] + jnp.where(h > t_hi16, ONE16, ZERO16)
                    n += 1
            t_lo = search16(lo_scr, kf - column_total(above))
            thr = lax.shift_left(t_hi, 16) | t_lo

        n_ge = jnp.zeros((SUBLANES, t), F32)
        for kb in range(nkb):
            blk = key_scr[kb]
            for r, grp in enumerate(_row_groups(blk)):
                rows = slice(r * SUBLANES, (r + 1) * SUBLANES)
                sel = grp >= thr
                n_ge = n_ge + jnp.where(sel, 1.0, 0.0)
                m = jnp.where(sel, 0.0, NEG_BIG)
                if kb == diag:
                    m = jnp.where(causal[rows, :], m, NEG_BIG)
                selm_scr[kb, rows, :] = m
        n_ge = _all_sublanes(jnp.add, n_ge)
        excess = jnp.max(jnp.where(n_ge > kf, 1.0, 0.0))

        @pl.when(excess > 0.0)
        def _():
            thr1 = thr[0:1, :]

            def tied(kb):
                return jnp.where(key_scr[kb] == thr1, 1.0, 0.0)

            def count_tied(kb, acc):
                return acc + jnp.sum(tied(kb), axis=0, keepdims=True)

            n_tied = lax.fori_loop(0, nkb, count_tied, jnp.zeros((1, t), F32))
            need = kf - (n_ge[0:1, :] - n_tied)

            def rewrite(kb, before):
                e = tied(kb)
                rank = _dot(tri_ref[...], e.astype(BF16)) + before
                k = key_scr[kb]
                keep = jnp.where(k > thr1, 0.0,
                                 jnp.where(k == thr1, jnp.where(rank < need, 0.0, NEG_BIG), NEG_BIG))
                selm_scr[kb] = jnp.where(selm_scr[kb] < 0.0, NEG_BIG, keep)
                return before + jnp.sum(e, axis=0, keepdims=True)

            lax.fori_loop(0, nkb, rewrite, jnp.zeros((1, t), F32))

        def head_rows(hd):
            if isinstance(hd, int):
                return slice(hd * HEAD_DIM, (hd + 1) * HEAD_DIM)
            return pl.ds(pl.multiple_of(hd * HEAD_DIM, HEAD_DIM), HEAD_DIM)

        def logits(hd, buf):
            qt = qat_ref[0, head_rows(hd), :]
            maxes = []
            for kb in range(nkb):
                lg = _dot(ka_ref[0, hd, kb * t:(kb + 1) * t, :], qt) + selm_scr[kb]
                if kb >= nkb - 2:
                    lg = lg + bias_ref[hd, diag - kb]
                buf[kb] = lg
                maxes.append(jnp.max(lg, axis=0, keepdims=True))
            return _tree(jnp.maximum, maxes)

        def weighted_values(hd, buf, m):
            if isinstance(hd, int):
                vrows = slice(hd * V_ROWS, (hd + 1) * V_ROWS)
            else:
                vrows = pl.ds(pl.multiple_of(hd * V_ROWS, 16), V_ROWS)
            acc = None
            for kb in range(nkb):
                p = jnp.exp2(buf[kb] - m).astype(BF16)
                pv = _dot(vat_ref[0, vrows, kb * t:(kb + 1) * t], p)
                acc = pv if acc is None else acc + pv
            out_scr[head_rows(hd), :] = acc[:HEAD_DIM, :] * (1.0 / acc[HEAD_DIM:HEAD_DIM + 1, :])

        def head_pair(j, m_even):
            m_odd = logits(2 * j + 1, logit_b)
            weighted_values(2 * j, logit_a, m_even)
            m_next = logits((2 * j + 2) % N_HEADS, logit_a)
            weighted_values(2 * j + 1, logit_b, m_odd)
            return m_next

        lax.fori_loop(0, N_HEADS // 2, head_pair, logits(0, logit_a), unroll=nkb <= HEADS_UNROLL_BLOCKS)

    i = pl.program_id(1)
    for k in range(n_q_tiles):
        pl.when(i == k)(functools.partial(tile_work, k + 1))

    o = out_scr[...]
    o = o * lax.rsqrt(jnp.mean(o * o, axis=0, keepdims=True) + EPS) * gt_ref[...]
    o_ref[0] = o.T.astype(BF16)


def _dsa(qit, wit, ki, qat, ka, vat, bias, g_out, bsz, seq):
    t = DSA_TILE
    nq = seq // t
    gw = GROUP_WIDTH
    topk = min(TOPK_MAX, seq // 4)
    q_fm = pl.BlockSpec((1, gw, t), lambda b, i: (b, 0, i))
    gt = jnp.broadcast_to(g_out.reshape(gw, 1), (gw, t))
    r = np.arange(t)
    lower = jnp.asarray(r[None, :] < r[:, None], dtype=BF16)
    return pl.pallas_call(
        functools.partial(_dsa_kernel, topk=topk, n_q_tiles=nq),
        grid=(bsz, nq),
        in_specs=[q_fm,
                  pl.BlockSpec((1, N_HEADS, t), lambda b, i: (b, 0, i)),
                  pl.BlockSpec((1, seq, IDX_DIM), lambda b, i: (b, 0, 0)),
                  q_fm,
                  pl.BlockSpec((1, N_HEADS, seq, HEAD_DIM), lambda b, i: (b, 0, 0, 0)),
                  pl.BlockSpec((1, N_HEADS * V_ROWS, seq), lambda b, i: (b, 0, 0)),
                  _resident(bias.shape),
                  _resident((gw, t)),
                  _resident((t, t))],
        out_specs=pl.BlockSpec((1, t, gw), lambda b, i: (b, i, 0)),
        out_shape=jax.ShapeDtypeStruct((bsz, seq, gw), BF16),
        scratch_shapes=[pltpu.VMEM((nq, t, t), I32), pltpu.VMEM((nq, t, t), I16),
                        pltpu.VMEM((nq, t, t), I16), pltpu.VMEM((nq, t, t), F32),
                        pltpu.VMEM((nq, t, t), F32), pltpu.VMEM((nq, t, t), F32),
                        pltpu.VMEM((gw, t), F32)],
        compiler_params=_params(2),
        name="dsa",
    )(qit, wit, ki, qat, ka, vat, bias, gt, lower)


def _neg_log2_1m_sigmoid(z2):
    return jnp.maximum(z2, 0.0) + jnp.log2(1.0 + jnp.exp2(-jnp.abs(z2)))


def _split_bf16(x):
    hi = x.astype(BF16)
    lo = (x - hi.astype(F32)).astype(BF16)
    return jnp.concatenate([hi, lo], axis=1)


def _sb_kernel(q_ref, k_ref, v_ref, tri_ref, g_ref, o_ref, z_scr, lh_scr, cs_scr, r_scr, acc_scr):
    t = SB_TILE
    nw = SB_WINDOW
    row = lax.broadcasted_iota(I32, (t, t), 0)
    col = lax.broadcasted_iota(I32, (t, t), 1)
    tiles = []
    for sub in range(SB_PAIR):
        i = pl.program_id(1) * SB_PAIR + sub
        wb = jnp.maximum(i - (nw - 1), 0)
        tiles.append((sub, i, wb, pl.ds(pl.multiple_of(wb * t, t), nw * t)))

    def causal(i, wb, w):
        return (col - row) < (i - wb - w) * t

    def q_rows(sub):
        return slice(sub * t, (sub + 1) * t)

    def window_pass(masked):
        def keep(i, wb, w, x):
            return jnp.where(causal(i, wb, w), x, 0.0) if w in masked else x

        for sub, i, wb, wrows in tiles:
            for hd in range(N_HEADS):
                p = sub * N_HEADS + hd
                z = _dot_nt(q_ref[0, hd, q_rows(sub), :], k_ref[0, hd, wrows, :])
                z_scr[p] = z
                for w in range(nw):
                    m = keep(i, wb, w, _neg_log2_1m_sigmoid(z[:, w * t:(w + 1) * t]))
                    lh_scr[(p * nw + w) * t:(p * nw + w + 1) * t, :] = _split_bf16(m)
        cs_scr[...] = _dot(lh_scr[...], tri_ref[...])
        for sub, i, wb, wrows in tiles:
            for hd in range(N_HEADS):
                p = sub * N_HEADS + hd
                after = None
                a_blocks = [None] * nw
                for w in reversed(range(nw)):
                    cs = cs_scr[(p * nw + w) * t:(p * nw + w + 1) * t, :]
                    c = cs[:, :t] if after is None else cs[:, :t] + after
                    a = keep(i, wb, w, jnp.exp2(z_scr[p, :, w * t:(w + 1) * t] - c))
                    a_blocks[w] = a.astype(BF16)
                    after = cs[:, t:] if after is None else after + cs[:, t:]
                acc_scr[p] = _dot(jnp.concatenate(a_blocks, axis=1), v_ref[0, hd, wrows, :])
                r_scr[p] = after

    first_group = pl.program_id(1) * SB_PAIR < nw - 1
    pl.when(first_group)(functools.partial(window_pass, tuple(range(nw))))
    pl.when(jnp.logical_not(first_group))(functools.partial(window_pass, (nw - 1,)))

    for sub, i, wb, wrows in tiles:
        def far_block(kb, sub=sub):
            rows = pl.ds(pl.multiple_of(kb * t, t), t)
            rmin = None
            for hd in range(N_HEADS):
                p = sub * N_HEADS + hd
                z = _dot_nt(q_ref[0, hd, q_rows(sub), :], k_ref[0, hd, rows, :])
                cs = _dot(_split_bf16(_neg_log2_1m_sigmoid(z)), tri_ref[...])
                r_old = r_scr[p]
                a = jnp.exp2(z - cs[:, :t] - r_old)
                acc_scr[p] += _dot(a.astype(BF16), v_ref[0, hd, rows, :])
                r_new = r_old + cs[:, t:]
                r_scr[p] = r_new
                rmin = r_new if rmin is None else jnp.minimum(rmin, r_new)
            return jnp.min(rmin)

        def cond(c):
            kb, rmin = c
            return jnp.logical_and(kb >= 0, rmin < SB_DEAD)

        def body(c, far_block=far_block):
            kb, _ = c
            return kb - 1, far_block(kb)

        rmin0 = _tree(jnp.minimum, [r_scr[sub * N_HEADS + hd] for hd in range(N_HEADS)])
        lax.while_loop(cond, body, (wb - 1, jnp.min(rmin0)))
        o = jnp.concatenate([acc_scr[sub * N_HEADS + hd] for hd in range(N_HEADS)], axis=1)
        o_ref[0, q_rows(sub), :] = _rms(o, g_ref[...]).astype(BF16)


def _sb(q, k, v, g_out, bsz, seq):
    t = SB_TILE
    tq = SB_PAIR * t
    assert seq >= SB_WINDOW * t and seq % tq == 0
    j = np.arange(t)[:, None]
    s = np.arange(t)[None, :]
    tri = np.concatenate([(j >= s), np.ones((t, t), bool)], axis=1).astype(np.float32)
    tri = np.concatenate([tri, tri], axis=0)
    q_hm = pl.BlockSpec((1, N_HEADS, tq, HEAD_DIM), lambda b, i: (b, 0, i, 0))
    kv_hm = pl.BlockSpec((1, N_HEADS, seq, HEAD_DIM), lambda b, i: (b, 0, 0, 0))
    n_p = SB_PAIR * N_HEADS
    n_rows = n_p * SB_WINDOW * t
    return pl.pallas_call(
        _sb_kernel,
        grid=(bsz, seq // tq),
        in_specs=[q_hm, kv_hm, kv_hm, _resident((2 * t, 2 * t)),
                  pl.BlockSpec((1, GROUP_WIDTH), lambda b, i: (0, 0))],
        out_specs=pl.BlockSpec((1, tq, GROUP_WIDTH), lambda b, i: (b, i, 0)),
        out_shape=jax.ShapeDtypeStruct((bsz, seq, GROUP_WIDTH), BF16),
        scratch_shapes=[pltpu.VMEM((n_p, t, SB_WINDOW * t), F32),
                        pltpu.VMEM((n_rows, 2 * t), BF16),
                        pltpu.VMEM((n_rows, 2 * t), F32),
                        pltpu.VMEM((n_p, t, t), F32),
                        pltpu.VMEM((n_p, t, HEAD_DIM), F32)],
        compiler_params=_params(2),
        name="sb",
    )(q, k, v, jnp.asarray(tri, dtype=BF16), g_out)


def kernel(x, c, w_ada, b_ada, g_pre, g_post, w_ffn_gate, w_ffn_up, w_ffn_down, w_in, g_kidx,
           rel_bias, g_out_a, g_out_b, w_out):
    bsz, seq, d = x.shape
    depth = w_ada.shape[0]
    gw = GROUP_WIDTH
    h = x.reshape(bsz * seq, d)
    bias = _bias_tiles(rel_bias, DSA_TILE)
    for l in range(depth):
        mods = _ada(c, w_ada[l], b_ada[l]).reshape(bsz, -1, d)
        ffn_w = [(w_ffn_gate[l, j].astype(BF16), w_ffn_up[l, j].astype(BF16),
                  w_ffn_down[l, j].astype(BF16)) for j in range(2)]
        w = w_in[l]
        o_ki = 4 * gw
        o_wi = o_ki + IDX_DIM
        o_b = o_wi + N_HEADS
        wt = jnp.concatenate([w[:, :gw], w[:, 2 * gw:3 * gw], w[:, 3 * gw:o_ki]], axis=1).T.astype(BF16)
        wka = w[:, gw:2 * gw].astype(BF16)
        wki = w[:, o_ki:o_wi].astype(BF16)
        wwit = jnp.pad(w[:, o_wi:o_b].T, ((0, 16 - N_HEADS), (0, 0))).astype(BF16)
        wb = w[:, o_b:].astype(BF16)
        woa = w_out[l, :gw].astype(BF16)
        wob = w_out[l, gw:].astype(BF16)
        vec = lambda g: g.reshape(1, -1)

        h1, qat, ka, vat, qit, ki, wit, qb, kb, vb = _ffn(
            h, mods, vec(g_pre[l, 0]), vec(g_post[l, 0]), *ffn_w[0], 0, seq,
            proj=(vec(g_pre[l, 1]), vec(g_kidx[l]), bsz, wt, wka, wki, wwit, wb))
        oa = _dsa(qit, wit, ki, qat, ka, vat, bias, g_out_a[l], bsz, seq)
        ob = _sb(qb, kb, vb, vec(g_out_b[l]), bsz, seq)
        h = _ffn(h1, mods, vec(g_pre[l, 2]), vec(g_post[l, 2]), *ffn_w[1], 2, seq,
                 mix=(oa.reshape(bsz * seq, gw), ob.reshape(bsz * seq, gw), woa, wob,
                      vec(g_post[l, 1])))
    return h.reshape(bsz, seq, d)
```

```python
import functools
import math

import numpy as np
import jax
import jax.numpy as jnp
from jax import lax
from jax.experimental import pallas as pl
from jax.experimental.pallas import tpu as pltpu

F32 = jnp.float32
BF16 = jnp.bfloat16
I32 = jnp.int32
I16 = jnp.int16

EPS = 1e-6
HEAD_DIM = 64
N_HEADS = 8
GROUP_WIDTH = N_HEADS * HEAD_DIM
IDX_DIM = 64
TOPK_MAX = 256
N_BUCKETS = 32
MAX_DISTANCE = 128
SUBLANES = 8
V_ROWS = 80
FFN_CHUNK = 256
TOKEN_TILE = 512
MIX_TOKEN_TILE = 1024
DSA_TILE = 256
SCORE_ROWS = 64
HEADS_UNROLL_BLOCKS = 7
SB_TILE = 128
SB_WINDOW = 3
SB_PAIR = 4
LOG2E = math.log2(math.e)
NEG_BIG = -1e30
SB_DEAD = 104.0 * LOG2E
VMEM_LIMIT = 56 * 1024 * 1024
INT_MIN = np.int32(-2 ** 31)
HALF = 2 ** 15
ONE16, ZERO16, MIN16 = np.int16(1), np.int16(0), np.int16(-HALF)


def _dot(a, b):
    return jnp.dot(a, b, preferred_element_type=F32)


def _dot_nt(a, b):
    return lax.dot_general(a, b, (((1,), (1,)), ((), ())), preferred_element_type=F32)


def _rms(x, g):
    return x * lax.rsqrt(jnp.mean(x * x, axis=-1, keepdims=True) + EPS) * g


def _resident(shape):
    n = len(shape)
    return pl.BlockSpec(shape, lambda *_: (0,) * n, pipeline_mode=pl.Buffered(1))


def _params(n_grid):
    return pltpu.CompilerParams(dimension_semantics=("arbitrary",) * n_grid,
                                vmem_limit_bytes=VMEM_LIMIT)


def _ada_kernel(c_ref, w_ref, b_ref, o_ref):
    c = c_ref[...]
    a = c * jax.nn.sigmoid(c)
    w = w_ref[...]
    a_hi = a.astype(BF16)
    a_lo = (a - a_hi.astype(F32)).astype(BF16)
    w_hi = w.astype(BF16)
    w_lo = (w - w_hi.astype(F32)).astype(BF16)
    o_ref[...] = _dot(a_hi, w_hi) + (_dot(a_hi, w_lo) + _dot(a_lo, w_hi)) + b_ref[...]


def _ada(c, w, b):
    bsz, d = c.shape
    n = w.shape[1]
    tn = 1152 if n % 1152 == 0 else n
    return pl.pallas_call(
        _ada_kernel,
        grid=(n // tn,),
        in_specs=[pl.BlockSpec((bsz, d), lambda j: (0, 0)),
                  pl.BlockSpec((d, tn), lambda j: (0, j)),
                  pl.BlockSpec((1, tn), lambda j: (0, j))],
        out_specs=pl.BlockSpec((bsz, tn), lambda j: (0, j)),
        out_shape=jax.ShapeDtypeStruct((bsz, n), F32),
        compiler_params=_params(1),
        name="ada",
    )(c, w, b.reshape(1, n))


def _ffn_kernel(*refs, sub, mix, proj):
    refs = list(refs)
    take = lambda n: [refs.pop(0) for _ in range(n)]
    (h_ref,) = take(1)
    if mix:
        oa_lo_ref, oa_hi_ref, ob_ref, woa_ref, wob_ref, gpm_ref = take(6)
    mods_ref, gpre_ref, gpost_ref, wg_ref, wu_ref, wd_ref = take(6)
    if proj:
        gpre_next_ref, *proj_in = take(7)
    (o_ref,) = take(1)
    if proj:
        proj_out = take(9)
    n_scr, a_scr = refs
    x = h_ref[...]
    if mix:
        first_half = pl.program_id(0) % 2 == 0
        oa = jnp.where(first_half, oa_lo_ref[...], oa_hi_ref[...])
        o = _dot(oa, woa_ref[...]) + _dot(ob_ref[...], wob_ref[...])
        x = x + mods_ref[0, 5:6, :] * _rms(o, gpm_ref[...])
    shift = mods_ref[0, 3 * sub:3 * sub + 1, :]
    scale = mods_ref[0, 3 * sub + 1:3 * sub + 2, :]
    gate = mods_ref[0, 3 * sub + 2:3 * sub + 3, :]
    n = _rms(x, gpre_ref[...]) * (1.0 + scale) + shift
    n_scr[...] = n.astype(BF16)
    fc = FFN_CHUNK
    for c0 in range(0, wg_ref.shape[1], fc):
        g = _dot(n_scr[...], wg_ref[:, c0:c0 + fc])
        u = _dot(n_scr[...], wu_ref[:, c0:c0 + fc])
        a_scr[:, c0:c0 + fc] = (g * jax.nn.sigmoid(g) * u).astype(BF16)
    f = _dot(a_scr[...], wd_ref[...])
    y = x + 0.5 * gate * _rms(f, gpost_ref[...])
    o_ref[...] = y
    if proj:
        shift = mods_ref[0, 3 * (sub + 1):3 * (sub + 1) + 1, :]
        scale = mods_ref[0, 3 * (sub + 1) + 1:3 * (sub + 1) + 2, :]
        n_scr[...] = (_rms(y, gpre_next_ref[...]) * (1.0 + scale) + shift).astype(BF16)
        _project(n_scr, *proj_in, *proj_out)


def _ffn(h, mods, g_pre, g_post, wg, wu, wd, sub, seq, mix=None, proj=None):
    n_tok, d = h.shape
    tm = min(TOKEN_TILE if proj is not None else MIX_TOKEN_TILE, seq)
    tpb = seq // tm
    row = pl.BlockSpec((tm, d), lambda i: (i, 0))
    vec = pl.BlockSpec((1, d), lambda i: (0, 0))
    mod_spec = pl.BlockSpec((1, mods.shape[1], d), lambda i: (i // tpb, 0, 0))
    args, specs = [h], [row]
    out_specs, out_shape = [row], [jax.ShapeDtypeStruct((n_tok, d), F32)]
    if mix is not None:
        oa_lo, oa_hi, ob, woa, wob, g_post_mix = mix
        gw = ob.shape[1]
        assert tpb == 2 and oa_lo.shape[0] * 2 == n_tok
        half = pl.BlockSpec((tm, gw), lambda i: (i, 0))
        per_batch = pl.BlockSpec((tm, gw), lambda i: (i // 2, 0))
        args += [oa_lo, oa_hi, ob, woa, wob, g_post_mix]
        specs += [per_batch, per_batch, half, _resident(woa.shape), _resident(wob.shape), vec]
    args += [mods, g_pre, g_post, wg, wu, wd]
    specs += [mod_spec, vec, vec, _resident(wg.shape), _resident(wu.shape), _resident(wd.shape)]
    if proj is not None:
        g_pre_next, g_kidx, bsz, *weights = proj
        gw = GROUP_WIDTH
        args += [g_pre_next, g_kidx, *weights]
        specs += [vec, pl.BlockSpec((1, IDX_DIM), lambda i: (0, 0))] + [_resident(w.shape) for w in weights]
        hm = pl.BlockSpec((1, N_HEADS, tm, HEAD_DIM), lambda i: (i // tpb, 0, i % tpb, 0))
        hm_shape = jax.ShapeDtypeStruct((bsz, N_HEADS, seq, HEAD_DIM), BF16)
        fm = pl.BlockSpec((1, gw, tm), lambda i: (i // tpb, 0, i % tpb))
        fm_shape = jax.ShapeDtypeStruct((bsz, gw, seq), BF16)
        vfm = pl.BlockSpec((1, N_HEADS * V_ROWS, tm), lambda i: (i // tpb, 0, i % tpb))
        out_specs += [fm, hm, vfm, fm,
                      pl.BlockSpec((1, tm, IDX_DIM), lambda i: (i // tpb, i % tpb, 0)),
                      pl.BlockSpec((1, N_HEADS, tm), lambda i: (i // tpb, 0, i % tpb)),
                      hm, hm, hm]
        out_shape += [fm_shape, hm_shape, jax.ShapeDtypeStruct((bsz, N_HEADS * V_ROWS, seq), BF16),
                      fm_shape,
                      jax.ShapeDtypeStruct((bsz, seq, IDX_DIM), BF16),
                      jax.ShapeDtypeStruct((bsz, N_HEADS, seq), F32),
                      hm_shape, hm_shape, hm_shape]
    out = pl.pallas_call(
        functools.partial(_ffn_kernel, sub=sub, mix=mix is not None, proj=proj is not None),
        grid=(n_tok // tm,),
        in_specs=specs,
        out_specs=out_specs,
        out_shape=out_shape,
        scratch_shapes=[pltpu.VMEM((tm, d), BF16), pltpu.VMEM((tm, wd.shape[0]), BF16)],
        compiler_params=_params(1),
        name="ffn_mix" if mix is not None else "ffn_proj" if proj is not None else "ffn",
    )(*args)
    return out if proj is not None else out[0]


def _project(n_scr, gk_ref, wt_ref, wka_ref, wki_ref, wwit_ref, wb_ref,
             qat_ref, ka_ref, vat_ref, qit_ref, ki_ref, wit_ref, qb_ref, kb_ref, vb_ref):
    gw = GROUP_WIDTH
    qscale = HEAD_DIM ** -0.5

    def heads(w, out_ref, mul):
        p = _dot(n_scr[...], w)
        if mul != 1.0:
            p = p * mul
        for hd in range(N_HEADS):
            out_ref[0, hd] = p[:, hd * HEAD_DIM:(hd + 1) * HEAD_DIM].astype(BF16)

    def feature_major(col, out_ref, mul):
        p = _dot_nt(wt_ref[col * gw:(col + 1) * gw, :], n_scr[...])
        if mul != 1.0:
            p = p * mul
        out_ref[0] = p.astype(BF16)

    feature_major(0, qat_ref, qscale * LOG2E)
    v = _dot_nt(wt_ref[gw:2 * gw, :], n_scr[...])
    pad = V_ROWS - HEAD_DIM
    ones_row = (lax.broadcasted_iota(I32, (pad, v.shape[1]), 0) == 0).astype(BF16)
    for hd in range(N_HEADS):
        vat_ref[0, hd * V_ROWS:hd * V_ROWS + HEAD_DIM, :] = v[hd * HEAD_DIM:(hd + 1) * HEAD_DIM, :].astype(BF16)
        vat_ref[0, hd * V_ROWS + HEAD_DIM:(hd + 1) * V_ROWS, :] = ones_row
    feature_major(2, qit_ref, 1.0)
    heads(wka_ref[...], ka_ref, 1.0)
    heads(wb_ref[:, :gw], qb_ref, qscale * LOG2E)
    heads(wb_ref[:, gw:2 * gw], kb_ref, 1.0)
    heads(wb_ref[:, 2 * gw:], vb_ref, 1.0)
    ki = _dot(n_scr[...], wki_ref[...])
    ki_ref[0] = _rms(ki, gk_ref[...]).astype(BF16)
    wit = _dot_nt(wwit_ref[...], n_scr[...])
    wit_ref[0] = wit[:N_HEADS, :] * ((N_HEADS * IDX_DIM) ** -0.5)


def _t5_bucket_np(n):
    n = np.maximum(n, 0)
    max_exact = N_BUCKETS // 2
    nf = np.maximum(n, 1).astype(np.float32)
    large = max_exact + (np.log(nf / np.float32(max_exact)) / np.float32(math.log(MAX_DISTANCE / max_exact))
                         * np.float32(N_BUCKETS - max_exact)).astype(np.int32)
    large = np.minimum(large, N_BUCKETS - 1)
    return np.where(n < max_exact, n, large).astype(np.int32)


def _bias_kernel(relb_ref, bucket_ref, o_ref):
    hd = pl.program_id(0)
    far = relb_ref[N_BUCKETS - 1, hd]
    for r in range(2):
        bk = bucket_ref[r]
        acc = jnp.zeros(bk.shape, F32)
        for k in range(N_BUCKETS - 1):
            acc = jnp.where(bk == k, (relb_ref[k, hd] - far) * LOG2E, acc)
        o_ref[0, r] = acc


def _bias_tiles(rel_bias, t):
    assert t >= MAX_DISTANCE
    s = np.arange(t)[:, None]
    q = np.arange(t)[None, :]
    buckets = np.stack([_t5_bucket_np(q - s), _t5_bucket_np(t + q - s)]).astype(np.int32)
    return pl.pallas_call(
        _bias_kernel,
        grid=(N_HEADS,),
        in_specs=[pl.BlockSpec(memory_space=pltpu.SMEM),
                  pl.BlockSpec((2, t, t), lambda hd: (0, 0, 0))],
        out_specs=pl.BlockSpec((1, 2, t, t), lambda hd: (hd, 0, 0, 0)),
        out_shape=jax.ShapeDtypeStruct((N_HEADS, 2, t, t), F32),
        compiler_params=_params(1),
        name="bias_tiles",
    )(rel_bias, jnp.asarray(buckets))


def _tree(op, xs):
    xs = list(xs)
    while len(xs) > 1:
        xs = [op(xs[j], xs[j + 1]) if j + 1 < len(xs) else xs[j] for j in range(0, len(xs), 2)]
    return xs[0]


def _row_groups(x):
    return [x[r:r + SUBLANES, :] for r in range(0, x.shape[0], SUBLANES)]


def _all_sublanes(op, x):
    for shift in (4, 2, 1):
        x = op(x, pltpu.roll(x, shift, 0))
    return x


def _dsa_kernel(qit_ref, wit_ref, ki_ref, qat_ref, ka_ref, vat_ref, bias_ref, gt_ref, tri_ref, o_ref,
                key_scr, hi_scr, lo_scr, selm_scr, logit_a, logit_b, out_scr, *, topk, first_tile, n_q_tiles):
    t = DSA_TILE
    kf = float(topk)
    krow = lax.broadcasted_iota(I32, (t, t), 0)
    qcol = lax.broadcasted_iota(I32, (t, t), 1)
    causal = krow <= qcol

    def tile_work(nkb):
        diag = nkb - 1

        wt = wit_ref[0]
        for kb in range(nkb):
            for r0 in range(0, t, SCORE_ROWS):
                kblk = ki_ref[0, kb * t + r0:kb * t + r0 + SCORE_ROWS, :]
                acc = None
                for hd in range(N_HEADS):
                    sc = _dot(kblk, qit_ref[0, hd * IDX_DIM:(hd + 1) * IDX_DIM, :])
                    term = jnp.maximum(sc, 0.0) * wt[hd:hd + 1, :]
                    acc = term if acc is None else acc + term
                acc = acc + 0.0
                if kb == diag:
                    acc = jnp.where(causal[r0:r0 + SCORE_ROWS, :], acc, -jnp.inf)
                bits = pltpu.bitcast(acc, I32)
                key = bits ^ ((bits >> 31) & jnp.int32(0x7FFFFFFF))
                key_scr[kb, r0:r0 + SCORE_ROWS, :] = key
                hi_scr[kb, r0:r0 + SCORE_ROWS, :] = (key >> 16).astype(I16)

        def packed(x):
            return jnp.concatenate([x, x], axis=0).astype(I16)

        def column_total(parts):
            tot = _tree(jnp.add, parts).astype(F32)
            return _all_sublanes(jnp.add, tot[:SUBLANES, :] + tot[SUBLANES:, :])

        def search16(src_scr, need):
            def step(j, ans):
                cand_off = ans | lax.shift_left(jnp.int32(1), 15 - j)
                cand = packed(cand_off - HALF)
                accs = [jnp.zeros((2 * SUBLANES, t), I16) for _ in range(4)]
                n = 0
                for kb in range(nkb):
                    blk = src_scr[kb]
                    for r in range(0, t, 2 * SUBLANES):
                        hit = jnp.where(blk[r:r + 2 * SUBLANES, :] >= cand, ONE16, ZERO16)
                        accs[n % 4] = accs[n % 4] + hit
                        n += 1
                return jnp.where(column_total(accs) >= need, cand_off, ans)
            return lax.fori_loop(0, 16, step, jnp.zeros((SUBLANES, t), I32))

        if nkb * t <= topk:
            thr = jnp.full((SUBLANES, t), INT_MIN, I32)
        else:
            t_hi = search16(hi_scr, kf) - HALF
            t_hi16 = packed(t_hi)
            above = [jnp.zeros((2 * SUBLANES, t), I16) for _ in range(4)]
            n = 0
            for kb in range(nkb):
                for r in range(0, t, 2 * SUBLANES):
                    rows = slice(r, r + 2 * SUBLANES)
                    h = hi_scr[kb, rows, :]
                    lo = ((key_scr[kb, rows, :] & jnp.int32(0xFFFF)) - HALF).astype(I16)
                    lo_scr[kb, rows, :] = jnp.where(h == t_hi16, lo, MIN16)
                    above[n % 4] = above[n % 4] + jnp.where(h > t_hi16, ONE16, ZERO16)
                    n += 1
            t_lo = search16(lo_scr, kf - column_total(above))
            thr = lax.shift_left(t_hi, 16) | t_lo

        n_ge = jnp.zeros((SUBLANES, t), F32)
        for kb in range(nkb):
            blk = key_scr[kb]
            for r, grp in enumerate(_row_groups(blk)):
                rows = slice(r * SUBLANES, (r + 1) * SUBLANES)
                sel = grp >= thr
                n_ge = n_ge + jnp.where(sel, 1.0, 0.0)
                m = jnp.where(sel, 0.0, NEG_BIG)
                if kb == diag:
                    m = jnp.where(causal[rows, :], m, NEG_BIG)
                selm_scr[kb, rows, :] = m
        n_ge = _all_sublanes(jnp.add, n_ge)
        excess = jnp.max(jnp.where(n_ge > kf, 1.0, 0.0))

        @pl.when(excess > 0.0)
        def _():
            thr1 = thr[0:1, :]

            def tied(kb):
                return jnp.where(key_scr[kb] == thr1, 1.0, 0.0)

            def count_tied(kb, acc):
                return acc + jnp.sum(tied(kb), axis=0, keepdims=True)

            n_tied = lax.fori_loop(0, nkb, count_tied, jnp.zeros((1, t), F32))
            need = kf - (n_ge[0:1, :] - n_tied)

            def rewrite(kb, before):
                e = tied(kb)
                rank = _dot(tri_ref[...], e.astype(BF16)) + before
                k = key_scr[kb]
                keep = jnp.where(k > thr1, 0.0,
                                 jnp.where(k == thr1, jnp.where(rank < need, 0.0, NEG_BIG), NEG_BIG))
                selm_scr[kb] = jnp.where(selm_scr[kb] < 0.0, NEG_BIG, keep)
                return before + jnp.sum(e, axis=0, keepdims=True)

            lax.fori_loop(0, nkb, rewrite, jnp.zeros((1, t), F32))

        def head_rows(hd):
            if isinstance(hd, int):
                return slice(hd * HEAD_DIM, (hd + 1) * HEAD_DIM)
            return pl.ds(pl.multiple_of(hd * HEAD_DIM, HEAD_DIM), HEAD_DIM)

        def logits(hd, buf):
            qt = qat_ref[0, head_rows(hd), :]
            maxes = []
            for kb in range(nkb):
                lg = _dot(ka_ref[0, hd, kb * t:(kb + 1) * t, :], qt) + selm_scr[kb]
                if kb >= nkb - 2:
                    lg = lg + bias_ref[hd, diag - kb]
                buf[kb] = lg
                maxes.append(jnp.max(lg, axis=0, keepdims=True))
            return _tree(jnp.maximum, maxes)

        def weighted_values(hd, buf, m):
            if isinstance(hd, int):
                vrows = slice(hd * V_ROWS, (hd + 1) * V_ROWS)
            else:
                vrows = pl.ds(pl.multiple_of(hd * V_ROWS, 16), V_ROWS)
            acc = None
            for kb in range(nkb):
                p = jnp.exp2(buf[kb] - m).astype(BF16)
                pv = _dot(vat_ref[0, vrows, kb * t:(kb + 1) * t], p)
                acc = pv if acc is None else acc + pv
            out_scr[head_rows(hd), :] = acc[:HEAD_DIM, :] * (1.0 / acc[HEAD_DIM:HEAD_DIM + 1, :])

        def head_pair(j, m_even):
            m_odd = logits(2 * j + 1, logit_b)
            weighted_values(2 * j, logit_a, m_even)
            m_next = logits((2 * j + 2) % N_HEADS, logit_a)
            weighted_values(2 * j + 1, logit_b, m_odd)
            return m_next

        lax.fori_loop(0, N_HEADS // 2, head_pair, logits(0, logit_a), unroll=nkb <= HEADS_UNROLL_BLOCKS)

    i = pl.program_id(1)
    for k in range(n_q_tiles):
        pl.when(i == k)(functools.partial(tile_work, first_tile + k + 1))

    o = out_scr[...]
    o = o * lax.rsqrt(jnp.mean(o * o, axis=0, keepdims=True) + EPS) * gt_ref[...]
    o_ref[0] = o.T.astype(BF16)


def _dsa(qit, wit, ki, qat, ka, vat, bias, g_out, bsz, seq, first_tile, n_q_tiles):
    t = DSA_TILE
    nq = first_tile + n_q_tiles
    keys = nq * t
    gw = GROUP_WIDTH
    topk = min(TOPK_MAX, seq // 4)
    q_fm = pl.BlockSpec((1, gw, t), lambda b, i: (b, 0, i + first_tile))
    gt = jnp.broadcast_to(g_out.reshape(gw, 1), (gw, t))
    r = np.arange(t)
    lower = jnp.asarray(r[None, :] < r[:, None], dtype=BF16)
    return pl.pallas_call(
        functools.partial(_dsa_kernel, topk=topk, first_tile=first_tile, n_q_tiles=n_q_tiles),
        grid=(bsz, n_q_tiles),
        in_specs=[q_fm,
                  pl.BlockSpec((1, N_HEADS, t), lambda b, i: (b, 0, i + first_tile)),
                  pl.BlockSpec((1, keys, IDX_DIM), lambda b, i: (b, 0, 0)),
                  q_fm,
                  pl.BlockSpec((1, N_HEADS, keys, HEAD_DIM), lambda b, i: (b, 0, 0, 0)),
                  pl.BlockSpec((1, N_HEADS * V_ROWS, keys), lambda b, i: (b, 0, 0)),
                  _resident(bias.shape),
                  _resident((gw, t)),
                  _resident((t, t))],
        out_specs=pl.BlockSpec((1, t, gw), lambda b, i: (b, i, 0)),
        out_shape=jax.ShapeDtypeStruct((bsz, n_q_tiles * t, gw), BF16),
        scratch_shapes=[pltpu.VMEM((nq, t, t), I32), pltpu.VMEM((nq, t, t), I16),
                        pltpu.VMEM((nq, t, t), I16), pltpu.VMEM((nq, t, t), F32),
                        pltpu.VMEM((nq, t, t), F32), pltpu.VMEM((nq, t, t), F32),
                        pltpu.VMEM((gw, t), F32)],
        compiler_params=_params(2),
        name="dsa_%d" % first_tile,
    )(qit, wit, ki, qat, ka, vat, bias, gt, lower)


def _neg_log2_1m_sigmoid(z2):
    return jnp.maximum(z2, 0.0) + jnp.log2(1.0 + jnp.exp2(-jnp.abs(z2)))


def _split_bf16(x):
    hi = x.astype(BF16)
    lo = (x - hi.astype(F32)).astype(BF16)
    return jnp.concatenate([hi, lo], axis=1)


def _sb_kernel(q_ref, k_ref, v_ref, tri_ref, g_ref, o_ref, z_scr, lh_scr, cs_scr, r_scr, acc_scr):
    t = SB_TILE
    nw = SB_WINDOW
    row = lax.broadcasted_iota(I32, (t, t), 0)
    col = lax.broadcasted_iota(I32, (t, t), 1)
    tiles = []
    for sub in range(SB_PAIR):
        i = pl.program_id(1) * SB_PAIR + sub
        wb = jnp.maximum(i - (nw - 1), 0)
        tiles.append((sub, i, wb, pl.ds(pl.multiple_of(wb * t, t), nw * t)))

    def causal(i, wb, w):
        return (col - row) < (i - wb - w) * t

    def q_rows(sub):
        return slice(sub * t, (sub + 1) * t)

    for sub, i, wb, wrows in tiles:
        for hd in range(N_HEADS):
            p = sub * N_HEADS + hd
            z = _dot_nt(q_ref[0, hd, q_rows(sub), :], k_ref[0, hd, wrows, :])
            z_scr[p] = z
            for w in range(nw):
                m = jnp.where(causal(i, wb, w), _neg_log2_1m_sigmoid(z[:, w * t:(w + 1) * t]), 0.0)
                lh_scr[(p * nw + w) * t:(p * nw + w + 1) * t, :] = _split_bf16(m)
    cs_scr[...] = _dot(lh_scr[...], tri_ref[...])
    for sub, i, wb, wrows in tiles:
        for hd in range(N_HEADS):
            p = sub * N_HEADS + hd
            after = None
            a_blocks = [None] * nw
            for w in reversed(range(nw)):
                cs = cs_scr[(p * nw + w) * t:(p * nw + w + 1) * t, :]
                c = cs[:, :t] if after is None else cs[:, :t] + after
                a = jnp.where(causal(i, wb, w), jnp.exp2(z_scr[p, :, w * t:(w + 1) * t] - c), 0.0)
                a_blocks[w] = a.astype(BF16)
                after = cs[:, t:] if after is None else after + cs[:, t:]
            acc_scr[p] = _dot(jnp.concatenate(a_blocks, axis=1), v_ref[0, hd, wrows, :])
            r_scr[p] = after

    for sub, i, wb, wrows in tiles:
        def far_block(kb, sub=sub):
            rows = pl.ds(pl.multiple_of(kb * t, t), t)
            rmin = None
            for hd in range(N_HEADS):
                p = sub * N_HEADS + hd
                z = _dot_nt(q_ref[0, hd, q_rows(sub), :], k_ref[0, hd, rows, :])
                cs = _dot(_split_bf16(_neg_log2_1m_sigmoid(z)), tri_ref[...])
                r_old = r_scr[p]
                a = jnp.exp2(z - cs[:, :t] - r_old)
                acc_scr[p] += _dot(a.astype(BF16), v_ref[0, hd, rows, :])
                r_new = r_old + cs[:, t:]
                r_scr[p] = r_new
                rmin = r_new if rmin is None else jnp.minimum(rmin, r_new)
            return jnp.min(rmin)

        def cond(c):
            kb, rmin = c
            return jnp.logical_and(kb >= 0, rmin < SB_DEAD)

        def body(c, far_block=far_block):
            kb, _ = c
            return kb - 1, far_block(kb)

        rmin0 = _tree(jnp.minimum, [r_scr[sub * N_HEADS + hd] for hd in range(N_HEADS)])
        lax.while_loop(cond, body, (wb - 1, jnp.min(rmin0)))
        o = jnp.concatenate([acc_scr[sub * N_HEADS + hd] for hd in range(N_HEADS)], axis=1)
        o_ref[0, q_rows(sub), :] = _rms(o, g_ref[...]).astype(BF16)


def _sb(q, k, v, g_out, bsz, seq):
    t = SB_TILE
    tq = SB_PAIR * t
    assert seq >= SB_WINDOW * t and seq % tq == 0
    j = np.arange(t)[:, None]
    s = np.arange(t)[None, :]
    tri = np.concatenate([(j >= s), np.ones((t, t), bool)], axis=1).astype(np.float32)
    tri = np.concatenate([tri, tri], axis=0)
    q_hm = pl.BlockSpec((1, N_HEADS, tq, HEAD_DIM), lambda b, i: (b, 0, i, 0))
    kv_hm = pl.BlockSpec((1, N_HEADS, seq, HEAD_DIM), lambda b, i: (b, 0, 0, 0))
    n_p = SB_PAIR * N_HEADS
    n_rows = n_p * SB_WINDOW * t
    return pl.pallas_call(
        _sb_kernel,
        grid=(bsz, seq // tq),
        in_specs=[q_hm, kv_hm, kv_hm, _resident((2 * t, 2 * t)),
                  pl.BlockSpec((1, GROUP_WIDTH), lambda b, i: (0, 0))],
        out_specs=pl.BlockSpec((1, tq, GROUP_WIDTH), lambda b, i: (b, i, 0)),
        out_shape=jax.ShapeDtypeStruct((bsz, seq, GROUP_WIDTH), BF16),
        scratch_shapes=[pltpu.VMEM((n_p, t, SB_WINDOW * t), F32),
                        pltpu.VMEM((n_rows, 2 * t), BF16),
                        pltpu.VMEM((n_rows, 2 * t), F32),
                        pltpu.VMEM((n_p, t, t), F32),
                        pltpu.VMEM((n_p, t, HEAD_DIM), F32)],
        compiler_params=_params(2),
        name="sb",
    )(q, k, v, jnp.asarray(tri, dtype=BF16), g_out)


def kernel(x, c, w_ada, b_ada, g_pre, g_post, w_ffn_gate, w_ffn_up, w_ffn_down, w_in, g_kidx,
           rel_bias, g_out_a, g_out_b, w_out):
    bsz, seq, d = x.shape
    depth = w_ada.shape[0]
    gw = GROUP_WIDTH
    h = x.reshape(bsz * seq, d)
    bias = _bias_tiles(rel_bias, DSA_TILE)
    for l in range(depth):
        mods = _ada(c, w_ada[l], b_ada[l]).reshape(bsz, -1, d)
        ffn_w = [(w_ffn_gate[l, j].astype(BF16), w_ffn_up[l, j].astype(BF16),
                  w_ffn_down[l, j].astype(BF16)) for j in range(2)]
        w = w_in[l]
        o_ki = 4 * gw
        o_wi = o_ki + IDX_DIM
        o_b = o_wi + N_HEADS
        wt = jnp.concatenate([w[:, :gw], w[:, 2 * gw:3 * gw], w[:, 3 * gw:o_ki]], axis=1).T.astype(BF16)
        wka = w[:, gw:2 * gw].astype(BF16)
        wki = w[:, o_ki:o_wi].astype(BF16)
        wwit = jnp.pad(w[:, o_wi:o_b].T, ((0, 16 - N_HEADS), (0, 0))).astype(BF16)
        wb = w[:, o_b:].astype(BF16)
        woa = w_out[l, :gw].astype(BF16)
        wob = w_out[l, gw:].astype(BF16)
        vec = lambda g: g.reshape(1, -1)

        h1, qat, ka, vat, qit, ki, wit, qb, kb, vb = _ffn(
            h, mods, vec(g_pre[l, 0]), vec(g_post[l, 0]), *ffn_w[0], 0, seq,
            proj=(vec(g_pre[l, 1]), vec(g_kidx[l]), bsz, wt, wka, wki, wwit, wb))
        half_tiles = seq // DSA_TILE // 2
        oa_lo = _dsa(qit, wit, ki, qat, ka, vat, bias, g_out_a[l], bsz, seq, 0, half_tiles)
        oa_hi = _dsa(qit, wit, ki, qat, ka, vat, bias, g_out_a[l], bsz, seq, half_tiles, half_tiles)
        ob = _sb(qb, kb, vb, vec(g_out_b[l]), bsz, seq)
        h = _ffn(h1, mods, vec(g_pre[l, 2]), vec(g_post[l, 2]), *ffn_w[1], 2, seq,
                 mix=(oa_lo.reshape(bsz * seq // 2, gw), oa_hi.reshape(bsz * seq // 2, gw),
                      ob.reshape(bsz * seq, gw), woa, wob, vec(g_post[l, 1])))
    return h.reshape(bsz, seq, d)
```

```python
import functools
import math

import numpy as np
import jax
import jax.numpy as jnp
from jax import lax
from jax.experimental import pallas as pl
from jax.experimental.pallas import tpu as pltpu

F32 = jnp.float32
BF16 = jnp.bfloat16
I32 = jnp.int32
I16 = jnp.int16

EPS = 1e-6
HEAD_DIM = 64
N_HEADS = 8
GROUP_WIDTH = N_HEADS * HEAD_DIM
IDX_DIM = 64
TOPK_MAX = 256
N_BUCKETS = 32
MAX_DISTANCE = 128
SUBLANES = 8
V_ROWS = 80
FFN_CHUNK = 256
TOKEN_TILE = 512
MIX_TOKEN_TILE = 1024
DSA_TILE = 256
SCORE_ROWS = 64
HEADS_UNROLL_BLOCKS = 7
SB_TILE = 128
SB_WINDOW = 3
SB_PAIR = 4
LOG2E = math.log2(math.e)
NEG_BIG = -1e30
SB_DEAD = 104.0 * LOG2E
VMEM_LIMIT = 56 * 1024 * 1024
INT_MIN = np.int32(-2 ** 31)
HALF = 2 ** 15
ONE16, ZERO16, MIN16 = np.int16(1), np.int16(0), np.int16(-HALF)


def _dot(a, b):
    return jnp.dot(a, b, preferred_element_type=F32)


def _dot_nt(a, b):
    return lax.dot_general(a, b, (((1,), (1,)), ((), ())), preferred_element_type=F32)


def _rms(x, g):
    return x * lax.rsqrt(jnp.mean(x * x, axis=-1, keepdims=True) + EPS) * g


def _resident(shape):
    n = len(shape)
    return pl.BlockSpec(shape, lambda *_: (0,) * n, pipeline_mode=pl.Buffered(1))


def _params(n_grid):
    return pltpu.CompilerParams(dimension_semantics=("arbitrary",) * n_grid,
                                vmem_limit_bytes=VMEM_LIMIT)


def _ada_kernel(c_ref, w_ref, b_ref, o_ref):
    c = c_ref[...]
    a = c * jax.nn.sigmoid(c)
    w = w_ref[...]
    a_hi = a.astype(BF16)
    a_lo = (a - a_hi.astype(F32)).astype(BF16)
    w_hi = w.astype(BF16)
    w_lo = (w - w_hi.astype(F32)).astype(BF16)
    o_ref[...] = _dot(a_hi, w_hi) + (_dot(a_hi, w_lo) + _dot(a_lo, w_hi)) + b_ref[...]


def _ada(c, w, b):
    bsz, d = c.shape
    n = w.shape[1]
    tn = 1152 if n % 1152 == 0 else n
    return pl.pallas_call(
        _ada_kernel,
        grid=(n // tn,),
        in_specs=[pl.BlockSpec((bsz, d), lambda j: (0, 0)),
                  pl.BlockSpec((d, tn), lambda j: (0, j)),
                  pl.BlockSpec((1, tn), lambda j: (0, j))],
        out_specs=pl.BlockSpec((bsz, tn), lambda j: (0, j)),
        out_shape=jax.ShapeDtypeStruct((bsz, n), F32),
        compiler_params=_params(1),
        name="ada",
    )(c, w, b.reshape(1, n))


def _ffn_kernel(*refs, sub, mix, proj):
    refs = list(refs)
    take = lambda n: [refs.pop(0) for _ in range(n)]
    (h_ref,) = take(1)
    if mix:
        oa_lo_ref, oa_hi_ref, ob_ref, woa_ref, wob_ref, gpm_ref = take(6)
    mods_ref, gpre_ref, gpost_ref, wg_ref, wu_ref, wd_ref = take(6)
    if proj:
        gpre_next_ref, *proj_in = take(7)
    (o_ref,) = take(1)
    if proj:
        proj_out = take(9)
    n_scr, a_scr = refs
    x = h_ref[...]
    if mix:
        first_half = pl.program_id(0) % 2 == 0
        oa = jnp.where(first_half, oa_lo_ref[...], oa_hi_ref[...])
        o = _dot(oa, woa_ref[...]) + _dot(ob_ref[...], wob_ref[...])
        x = x + mods_ref[0, 5:6, :] * _rms(o, gpm_ref[...])
    shift = mods_ref[0, 3 * sub:3 * sub + 1, :]
    scale = mods_ref[0, 3 * sub + 1:3 * sub + 2, :]
    gate = mods_ref[0, 3 * sub + 2:3 * sub + 3, :]
    n = _rms(x, gpre_ref[...]) * (1.0 + scale) + shift
    n_scr[...] = n.astype(BF16)
    fc = FFN_CHUNK
    for c0 in range(0, wg_ref.shape[1], fc):
        g = _dot(n_scr[...], wg_ref[:, c0:c0 + fc])
        u = _dot(n_scr[...], wu_ref[:, c0:c0 + fc])
        a_scr[:, c0:c0 + fc] = (g * jax.nn.sigmoid(g) * u).astype(BF16)
    f = _dot(a_scr[...], wd_ref[...])
    y = x + 0.5 * gate * _rms(f, gpost_ref[...])
    o_ref[...] = y
    if proj:
        shift = mods_ref[0, 3 * (sub + 1):3 * (sub + 1) + 1, :]
        scale = mods_ref[0, 3 * (sub + 1) + 1:3 * (sub + 1) + 2, :]
        n_scr[...] = (_rms(y, gpre_next_ref[...]) * (1.0 + scale) + shift).astype(BF16)
        _project(n_scr, *proj_in, *proj_out)


def _ffn(h, mods, g_pre, g_post, wg, wu, wd, sub, seq, mix=None, proj=None):
    n_tok, d = h.shape
    tm = min(TOKEN_TILE if proj is not None else MIX_TOKEN_TILE, seq)
    tpb = seq // tm
    row = pl.BlockSpec((tm, d), lambda i: (i, 0))
    vec = pl.BlockSpec((1, d), lambda i: (0, 0))
    mod_spec = pl.BlockSpec((1, mods.shape[1], d), lambda i: (i // tpb, 0, 0))
    args, specs = [h], [row]
    out_specs, out_shape = [row], [jax.ShapeDtypeStruct((n_tok, d), F32)]
    if mix is not None:
        oa_lo, oa_hi, ob, woa, wob, g_post_mix = mix
        gw = ob.shape[1]
        assert tpb == 2 and oa_lo.shape[0] * 2 == n_tok
        half = pl.BlockSpec((tm, gw), lambda i: (i, 0))
        per_batch = pl.BlockSpec((tm, gw), lambda i: (i // 2, 0))
        args += [oa_lo, oa_hi, ob, woa, wob, g_post_mix]
        specs += [per_batch, per_batch, half, _resident(woa.shape), _resident(wob.shape), vec]
    args += [mods, g_pre, g_post, wg, wu, wd]
    specs += [mod_spec, vec, vec, _resident(wg.shape), _resident(wu.shape), _resident(wd.shape)]
    if proj is not None:
        g_pre_next, g_kidx, bsz, *weights = proj
        gw = GROUP_WIDTH
        args += [g_pre_next, g_kidx, *weights]
        specs += [vec, pl.BlockSpec((1, IDX_DIM), lambda i: (0, 0))] + [_resident(w.shape) for w in weights]
        hm = pl.BlockSpec((1, N_HEADS, tm, HEAD_DIM), lambda i: (i // tpb, 0, i % tpb, 0))
        hm_shape = jax.ShapeDtypeStruct((bsz, N_HEADS, seq, HEAD_DIM), BF16)
        fm = pl.BlockSpec((1, gw, tm), lambda i: (i // tpb, 0, i % tpb))
        fm_shape = jax.ShapeDtypeStruct((bsz, gw, seq), BF16)
        vfm = pl.BlockSpec((1, N_HEADS * V_ROWS, tm), lambda i: (i // tpb, 0, i % tpb))
        out_specs += [fm, hm, vfm, fm,
                      pl.BlockSpec((1, tm, IDX_DIM), lambda i: (i // tpb, i % tpb, 0)),
                      pl.BlockSpec((1, N_HEADS, tm), lambda i: (i // tpb, 0, i % tpb)),
                      hm, hm, hm]
        out_shape += [fm_shape, hm_shape, jax.ShapeDtypeStruct((bsz, N_HEADS * V_ROWS, seq), BF16),
                      fm_shape,
                      jax.ShapeDtypeStruct((bsz, seq, IDX_DIM), BF16),
                      jax.ShapeDtypeStruct((bsz, N_HEADS, seq), F32),
                      hm_shape, hm_shape, hm_shape]
    out = pl.pallas_call(
        functools.partial(_ffn_kernel, sub=sub, mix=mix is not None, proj=proj is not None),
        grid=(n_tok // tm,),
        in_specs=specs,
        out_specs=out_specs,
        out_shape=out_shape,
        scratch_shapes=[pltpu.VMEM((tm, d), BF16), pltpu.VMEM((tm, wd.shape[0]), BF16)],
        compiler_params=_params(1),
        name="ffn_mix" if mix is not None else "ffn_proj" if proj is not None else "ffn",
    )(*args)
    return out if proj is not None else out[0]


def _project(n_scr, gk_ref, wt_ref, wka_ref, wki_ref, wwit_ref, wb_ref,
             qat_ref, ka_ref, vat_ref, qit_ref, ki_ref, wit_ref, qb_ref, kb_ref, vb_ref):
    gw = GROUP_WIDTH
    qscale = HEAD_DIM ** -0.5

    def heads(w, out_ref, mul):
        p = _dot(n_scr[...], w)
        if mul != 1.0:
            p = p * mul
        for hd in range(N_HEADS):
            out_ref[0, hd] = p[:, hd * HEAD_DIM:(hd + 1) * HEAD_DIM].astype(BF16)

    def feature_major(col, out_ref, mul):
        p = _dot_nt(wt_ref[col * gw:(col + 1) * gw, :], n_scr[...])
        if mul != 1.0:
            p = p * mul
        out_ref[0] = p.astype(BF16)

    feature_major(0, qat_ref, qscale * LOG2E)
    v = _dot_nt(wt_ref[gw:2 * gw, :], n_scr[...])
    pad = V_ROWS - HEAD_DIM
    ones_row = (lax.broadcasted_iota(I32, (pad, v.shape[1]), 0) == 0).astype(BF16)
    for hd in range(N_HEADS):
        vat_ref[0, hd * V_ROWS:hd * V_ROWS + HEAD_DIM, :] = v[hd * HEAD_DIM:(hd + 1) * HEAD_DIM, :].astype(BF16)
        vat_ref[0, hd * V_ROWS + HEAD_DIM:(hd + 1) * V_ROWS, :] = ones_row
    feature_major(2, qit_ref, 1.0)
    heads(wka_ref[...], ka_ref, 1.0)
    heads(wb_ref[:, :gw], qb_ref, qscale * LOG2E)
    heads(wb_ref[:, gw:2 * gw], kb_ref, 1.0)
    heads(wb_ref[:, 2 * gw:], vb_ref, 1.0)
    ki = _dot(n_scr[...], wki_ref[...])
    ki_ref[0] = _rms(ki, gk_ref[...]).astype(BF16)
    wit = _dot_nt(wwit_ref[...], n_scr[...])
    wit_ref[0] = wit[:N_HEADS, :] * ((N_HEADS * IDX_DIM) ** -0.5)


def _t5_bucket_np(n):
    n = np.maximum(n, 0)
    max_exact = N_BUCKETS // 2
    nf = np.maximum(n, 1).astype(np.float32)
    large = max_exact + (np.log(nf / np.float32(max_exact)) / np.float32(math.log(MAX_DISTANCE / max_exact))
                         * np.float32(N_BUCKETS - max_exact)).astype(np.int32)
    large = np.minimum(large, N_BUCKETS - 1)
    return np.where(n < max_exact, n, large).astype(np.int32)


def _bias_kernel(relb_ref, bucket_ref, o_ref):
    hd = pl.program_id(0)
    far = relb_ref[N_BUCKETS - 1, hd]
    for r in range(2):
        bk = bucket_ref[r]
        acc = jnp.zeros(bk.shape, F32)
        for k in range(N_BUCKETS - 1):
            acc = jnp.where(bk == k, (relb_ref[k, hd] - far) * LOG2E, acc)
        o_ref[0, r] = acc


def _bias_tiles(rel_bias, t):
    assert t >= MAX_DISTANCE
    s = np.arange(t)[:, None]
    q = np.arange(t)[None, :]
    buckets = np.stack([_t5_bucket_np(q - s), _t5_bucket_np(t + q - s)]).astype(np.int32)
    return pl.pallas_call(
        _bias_kernel,
        grid=(N_HEADS,),
        in_specs=[pl.BlockSpec(memory_space=pltpu.SMEM),
                  pl.BlockSpec((2, t, t), lambda hd: (0, 0, 0))],
        out_specs=pl.BlockSpec((1, 2, t, t), lambda hd: (hd, 0, 0, 0)),
        out_shape=jax.ShapeDtypeStruct((N_HEADS, 2, t, t), F32),
        compiler_params=_params(1),
        name="bias_tiles",
    )(rel_bias, jnp.asarray(buckets))


def _tree(op, xs):
    xs = list(xs)
    while len(xs) > 1:
        xs = [op(xs[j], xs[j + 1]) if j + 1 < len(xs) else xs[j] for j in range(0, len(xs), 2)]
    return xs[0]


def _row_groups(x):
    return [x[r:r + SUBLANES, :] for r in range(0, x.shape[0], SUBLANES)]


def _all_sublanes(op, x):
    for shift in (4, 2, 1):
        x = op(x, pltpu.roll(x, shift, 0))
    return x


def _dsa_kernel(qit_ref, wit_ref, ki_ref, qat_ref, ka_ref, vat_ref, bias_ref, gt_ref, tri_ref, o_ref,
                key_scr, hi_scr, lo_scr, selm_scr, logit_a, logit_b, out_scr, *, topk, first_tile, n_q_tiles):
    t = DSA_TILE
    kf = float(topk)
    krow = lax.broadcasted_iota(I32, (t, t), 0)
    qcol = lax.broadcasted_iota(I32, (t, t), 1)
    causal = krow <= qcol

    def tile_work(nkb):
        diag = nkb - 1

        wt = wit_ref[0]
        for kb in range(nkb):
            for r0 in range(0, t, SCORE_ROWS):
                kblk = ki_ref[0, kb * t + r0:kb * t + r0 + SCORE_ROWS, :]
                acc = None
                for hd in range(N_HEADS):
                    sc = _dot(kblk, qit_ref[0, hd * IDX_DIM:(hd + 1) * IDX_DIM, :])
                    term = jnp.maximum(sc, 0.0) * wt[hd:hd + 1, :]
                    acc = term if acc is None else acc + term
                acc = acc + 0.0
                if kb == diag:
                    acc = jnp.where(causal[r0:r0 + SCORE_ROWS, :], acc, -jnp.inf)
                bits = pltpu.bitcast(acc, I32)
                key = bits ^ ((bits >> 31) & jnp.int32(0x7FFFFFFF))
                key_scr[kb, r0:r0 + SCORE_ROWS, :] = key
                hi_scr[kb, r0:r0 + SCORE_ROWS, :] = (key >> 16).astype(I16)

        def packed(x):
            return jnp.concatenate([x, x], axis=0).astype(I16)

        def column_total(parts):
            tot = _tree(jnp.add, parts).astype(F32)
            return _all_sublanes(jnp.add, tot[:SUBLANES, :] + tot[SUBLANES:, :])

        def search16(src_scr, need):
            def step(j, ans):
                cand_off = ans | lax.shift_left(jnp.int32(1), 15 - j)
                cand = packed(cand_off - HALF)
                accs = [jnp.zeros((2 * SUBLANES, t), I16) for _ in range(4)]
                n = 0
                for kb in range(nkb):
                    blk = src_scr[kb]
                    for r in range(0, t, 2 * SUBLANES):
                        hit = jnp.where(blk[r:r + 2 * SUBLANES, :] >= cand, ONE16, ZERO16)
                        accs[n % 4] = accs[n % 4] + hit
                        n += 1
                return jnp.where(column_total(accs) >= need, cand_off, ans)
            return lax.fori_loop(0, 16, step, jnp.zeros((SUBLANES, t), I32))

        if nkb * t <= topk:
            thr = jnp.full((SUBLANES, t), INT_MIN, I32)
        else:
            t_hi = search16(hi_scr, kf) - HALF
            t_hi16 = packed(t_hi)
            above = [jnp.zeros((2 * SUBLANES, t), I16) for _ in range(4)]
            n = 0
            for kb in range(nkb):
                for r in range(0, t, 2 * SUBLANES):
                    rows = slice(r, r + 2 * SUBLANES)
                    h = hi_scr[kb, rows, :]
                    lo = ((key_scr[kb, rows, :] & jnp.int32(0xFFFF)) - HALF).astype(I16)
                    lo_scr[kb, rows, :] = jnp.where(h == t_hi16, lo, MIN16)
                    above[n % 4] = above[n % 4] + jnp.where(h > t_hi16, ONE16, ZERO16)
                    n += 1
            t_lo = search16(lo_scr, kf - column_total(above))
            thr = lax.shift_left(t_hi, 16) | t_lo

        n_ge = jnp.zeros((SUBLANES, t), F32)
        for kb in range(nkb):
            blk = key_scr[kb]
            for r, grp in enumerate(_row_groups(blk)):
                rows = slice(r * SUBLANES, (r + 1) * SUBLANES)
                sel = grp >= thr
                n_ge = n_ge + jnp.where(sel, 1.0, 0.0)
                m = jnp.where(sel, 0.0, NEG_BIG)
                if kb == diag:
                    m = jnp.where(causal[rows, :], m, NEG_BIG)
                selm_scr[kb, rows, :] = m
        n_ge = _all_sublanes(jnp.add, n_ge)
        excess = jnp.max(jnp.where(n_ge > kf, 1.0, 0.0))

        @pl.when(excess > 0.0)
        def _():
            thr1 = thr[0:1, :]

            def tied(kb):
                return jnp.where(key_scr[kb] == thr1, 1.0, 0.0)

            def count_tied(kb, acc):
                return acc + jnp.sum(tied(kb), axis=0, keepdims=True)

            n_tied = lax.fori_loop(0, nkb, count_tied, jnp.zeros((1, t), F32))
            need = kf - (n_ge[0:1, :] - n_tied)

            def rewrite(kb, before):
                e = tied(kb)
                rank = _dot(tri_ref[...], e.astype(BF16)) + before
                k = key_scr[kb]
                keep = jnp.where(k > thr1, 0.0,
                                 jnp.where(k == thr1, jnp.where(rank < need, 0.0, NEG_BIG), NEG_BIG))
                selm_scr[kb] = jnp.where(selm_scr[kb] < 0.0, NEG_BIG, keep)
                return before + jnp.sum(e, axis=0, keepdims=True)

            lax.fori_loop(0, nkb, rewrite, jnp.zeros((1, t), F32))

        def head_rows(hd):
            if isinstance(hd, int):
                return slice(hd * HEAD_DIM, (hd + 1) * HEAD_DIM)
            return pl.ds(pl.multiple_of(hd * HEAD_DIM, HEAD_DIM), HEAD_DIM)

        def logits(hd, buf):
            qt = qat_ref[0, head_rows(hd), :]
            maxes = []
            for kb in range(nkb):
                lg = _dot(ka_ref[0, hd, kb * t:(kb + 1) * t, :], qt) + selm_scr[kb]
                if kb >= nkb - 2:
                    lg = lg + bias_ref[hd, diag - kb]
                buf[kb] = lg
                maxes.append(jnp.max(lg, axis=0, keepdims=True))
            return _tree(jnp.maximum, maxes)

        def weighted_values(hd, buf, m):
            if isinstance(hd, int):
                vrows = slice(hd * V_ROWS, (hd + 1) * V_ROWS)
            else:
                vrows = pl.ds(pl.multiple_of(hd * V_ROWS, 16), V_ROWS)
            acc = None
            for kb in range(nkb):
                p = jnp.exp2(buf[kb] - m).astype(BF16)
                pv = _dot(vat_ref[0, vrows, kb * t:(kb + 1) * t], p)
                acc = pv if acc is None else acc + pv
            out_scr[head_rows(hd), :] = acc[:HEAD_DIM, :] * (1.0 / acc[HEAD_DIM:HEAD_DIM + 1, :])

        def head_pair(j, m_even):
            m_odd = logits(2 * j + 1, logit_b)
            weighted_values(2 * j, logit_a, m_even)
            m_next = logits((2 * j + 2) % N_HEADS, logit_a)
            weighted_values(2 * j + 1, logit_b, m_odd)
            return m_next

        lax.fori_loop(0, N_HEADS // 2, head_pair, logits(0, logit_a), unroll=nkb <= HEADS_UNROLL_BLOCKS)

    i = pl.program_id(1)
    for k in range(n_q_tiles):
        pl.when(i == k)(functools.partial(tile_work, first_tile + k + 1))

    o = out_scr[...]
    o = o * lax.rsqrt(jnp.mean(o * o, axis=0, keepdims=True) + EPS) * gt_ref[...]
    o_ref[0] = o.T.astype(BF16)


def _dsa(qit, wit, ki, qat, ka, vat, bias, g_out, bsz, seq, first_tile, n_q_tiles):
    t = DSA_TILE
    nq = first_tile + n_q_tiles
    keys = nq * t
    gw = GROUP_WIDTH
    topk = min(TOPK_MAX, seq // 4)
    q_fm = pl.BlockSpec((1, gw, t), lambda b, i: (b, 0, i + first_tile))
    gt = jnp.broadcast_to(g_out.reshape(gw, 1), (gw, t))
    r = np.arange(t)
    lower = jnp.asarray(r[None, :] < r[:, None], dtype=BF16)
    return pl.pallas_call(
        functools.partial(_dsa_kernel, topk=topk, first_tile=first_tile, n_q_tiles=n_q_tiles),
        grid=(bsz, n_q_tiles),
        in_specs=[q_fm,
                  pl.BlockSpec((1, N_HEADS, t), lambda b, i: (b, 0, i + first_tile)),
                  pl.BlockSpec((1, keys, IDX_DIM), lambda b, i: (b, 0, 0)),
                  q_fm,
                  pl.BlockSpec((1, N_HEADS, keys, HEAD_DIM), lambda b, i: (b, 0, 0, 0)),
                  pl.BlockSpec((1, N_HEADS * V_ROWS, keys), lambda b, i: (b, 0, 0)),
                  _resident(bias.shape),
                  _resident((gw, t)),
                  _resident((t, t))],
        out_specs=pl.BlockSpec((1, t, gw), lambda b, i: (b, i, 0)),
        out_shape=jax.ShapeDtypeStruct((bsz, n_q_tiles * t, gw), BF16),
        scratch_shapes=[pltpu.VMEM((nq, t, t), I32), pltpu.VMEM((nq, t, t), I16),
                        pltpu.VMEM((nq, t, t), I16), pltpu.VMEM((nq, t, t), F32),
                        pltpu.VMEM((nq, t, t), F32), pltpu.VMEM((nq, t, t), F32),
                        pltpu.VMEM((gw, t), F32)],
        compiler_params=_params(2),
        name="dsa_%d" % first_tile,
    )(qit, wit, ki, qat, ka, vat, bias, gt, lower)


def _neg_log2_1m_sigmoid(z2):
    return jnp.maximum(z2, 0.0) + jnp.log2(1.0 + jnp.exp2(-jnp.abs(z2)))


def _split_bf16(x):
    hi = x.astype(BF16)
    lo = (x - hi.astype(F32)).astype(BF16)
    return jnp.concatenate([hi, lo], axis=1)


def _sb_kernel(q_ref, k_ref, v_ref, tri_ref, g_ref, o_ref, z_scr, lh_scr, cs_scr, r_scr, acc_scr,
               mask_scr):
    t = SB_TILE
    nw = SB_WINDOW
    row = lax.broadcasted_iota(I32, (t, t), 0)
    col = lax.broadcasted_iota(I32, (t, t), 1)
    tiles = []
    for sub in range(SB_PAIR):
        i = pl.program_id(1) * SB_PAIR + sub
        wb = jnp.maximum(i - (nw - 1), 0)
        tiles.append((sub, i, wb, pl.ds(pl.multiple_of(wb * t, t), nw * t)))

    below_diagonal = jnp.where(col < row, 1.0, 0.0)
    for sub, i, wb, wrows in tiles:
        for w in range(nw):
            uniform = jnp.where(w < i - wb, 1.0, 0.0)
            mask_scr[sub * nw + w] = jnp.where(w == i - wb, below_diagonal, uniform)

    def q_rows(sub):
        return slice(sub * t, (sub + 1) * t)

    for sub, i, wb, wrows in tiles:
        for hd in range(N_HEADS):
            p = sub * N_HEADS + hd
            z = _dot_nt(q_ref[0, hd, q_rows(sub), :], k_ref[0, hd, wrows, :])
            z_scr[p] = z
            for w in range(nw):
                m = _neg_log2_1m_sigmoid(z[:, w * t:(w + 1) * t]) * mask_scr[sub * nw + w]
                lh_scr[(p * nw + w) * t:(p * nw + w + 1) * t, :] = _split_bf16(m)
    cs_scr[...] = _dot(lh_scr[...], tri_ref[...])
    for sub, i, wb, wrows in tiles:
        for hd in range(N_HEADS):
            p = sub * N_HEADS + hd
            after = None
            a_blocks = [None] * nw
            for w in reversed(range(nw)):
                cs = cs_scr[(p * nw + w) * t:(p * nw + w + 1) * t, :]
                c = cs[:, :t] if after is None else cs[:, :t] + after
                log2_a = jnp.minimum(z_scr[p, :, w * t:(w + 1) * t] - c, 0.0)
                a = jnp.exp2(log2_a) * mask_scr[sub * nw + w]
                a_blocks[w] = a.astype(BF16)
                after = cs[:, t:] if after is None else after + cs[:, t:]
            acc_scr[p] = _dot(jnp.concatenate(a_blocks, axis=1), v_ref[0, hd, wrows, :])
            r_scr[p] = after

    for sub, i, wb, wrows in tiles:
        def far_block(kb, sub=sub):
            rows = pl.ds(pl.multiple_of(kb * t, t), t)
            rmin = None
            for hd in range(N_HEADS):
                p = sub * N_HEADS + hd
                z = _dot_nt(q_ref[0, hd, q_rows(sub), :], k_ref[0, hd, rows, :])
                cs = _dot(_split_bf16(_neg_log2_1m_sigmoid(z)), tri_ref[...])
                r_old = r_scr[p]
                a = jnp.exp2(z - cs[:, :t] - r_old)
                acc_scr[p] += _dot(a.astype(BF16), v_ref[0, hd, rows, :])
                r_new = r_old + cs[:, t:]
                r_scr[p] = r_new
                rmin = r_new if rmin is None else jnp.minimum(rmin, r_new)
            return jnp.min(rmin)

        def cond(c):
            kb, rmin = c
            return jnp.logical_and(kb >= 0, rmin < SB_DEAD)

        def body(c, far_block=far_block):
            kb, _ = c
            return kb - 1, far_block(kb)

        rmin0 = _tree(jnp.minimum, [r_scr[sub * N_HEADS + hd] for hd in range(N_HEADS)])
        lax.while_loop(cond, body, (wb - 1, jnp.min(rmin0)))
        o = jnp.concatenate([acc_scr[sub * N_HEADS + hd] for hd in range(N_HEADS)], axis=1)
        o_ref[0, q_rows(sub), :] = _rms(o, g_ref[...]).astype(BF16)


def _sb(q, k, v, g_out, bsz, seq):
    t = SB_TILE
    tq = SB_PAIR * t
    assert seq >= SB_WINDOW * t and seq % tq == 0
    j = np.arange(t)[:, None]
    s = np.arange(t)[None, :]
    tri = np.concatenate([(j >= s), np.ones((t, t), bool)], axis=1).astype(np.float32)
    tri = np.concatenate([tri, tri], axis=0)
    q_hm = pl.BlockSpec((1, N_HEADS, tq, HEAD_DIM), lambda b, i: (b, 0, i, 0))
    kv_hm = pl.BlockSpec((1, N_HEADS, seq, HEAD_DIM), lambda b, i: (b, 0, 0, 0))
    n_p = SB_PAIR * N_HEADS
    n_rows = n_p * SB_WINDOW * t
    return pl.pallas_call(
        _sb_kernel,
        grid=(bsz, seq // tq),
        in_specs=[q_hm, kv_hm, kv_hm, _resident((2 * t, 2 * t)),
                  pl.BlockSpec((1, GROUP_WIDTH), lambda b, i: (0, 0))],
        out_specs=pl.BlockSpec((1, tq, GROUP_WIDTH), lambda b, i: (b, i, 0)),
        out_shape=jax.ShapeDtypeStruct((bsz, seq, GROUP_WIDTH), BF16),
        scratch_shapes=[pltpu.VMEM((n_p, t, SB_WINDOW * t), F32),
                        pltpu.VMEM((n_rows, 2 * t), BF16),
                        pltpu.VMEM((n_rows, 2 * t), F32),
                        pltpu.VMEM((n_p, t, t), F32),
                        pltpu.VMEM((n_p, t, HEAD_DIM), F32),
                        pltpu.VMEM((SB_PAIR * SB_WINDOW, t, t), F32)],
        compiler_params=_params(2),
        name="sb",
    )(q, k, v, jnp.asarray(tri, dtype=BF16), g_out)


def kernel(x, c, w_ada, b_ada, g_pre, g_post, w_ffn_gate, w_ffn_up, w_ffn_down, w_in, g_kidx,
           rel_bias, g_out_a, g_out_b, w_out):
    bsz, seq, d = x.shape
    depth = w_ada.shape[0]
    gw = GROUP_WIDTH
    h = x.reshape(bsz * seq, d)
    bias = _bias_tiles(rel_bias, DSA_TILE)
    for l in range(depth):
        mods = _ada(c, w_ada[l], b_ada[l]).reshape(bsz, -1, d)
        ffn_w = [(w_ffn_gate[l, j].astype(BF16), w_ffn_up[l, j].astype(BF16),
                  w_ffn_down[l, j].astype(BF16)) for j in range(2)]
        w = w_in[l]
        o_ki = 4 * gw
        o_wi = o_ki + IDX_DIM
        o_b = o_wi + N_HEADS
        wt = jnp.concatenate([w[:, :gw], w[:, 2 * gw:3 * gw], w[:, 3 * gw:o_ki]], axis=1).T.astype(BF16)
        wka = w[:, gw:2 * gw].astype(BF16)
        wki = w[:, o_ki:o_wi].astype(BF16)
        wwit = jnp.pad(w[:, o_wi:o_b].T, ((0, 16 - N_HEADS), (0, 0))).astype(BF16)
        wb = w[:, o_b:].astype(BF16)
        woa = w_out[l, :gw].astype(BF16)
        wob = w_out[l, gw:].astype(BF16)
        vec = lambda g: g.reshape(1, -1)

        h1, qat, ka, vat, qit, ki, wit, qb, kb, vb = _ffn(
            h, mods, vec(g_pre[l, 0]), vec(g_post[l, 0]), *ffn_w[0], 0, seq,
            proj=(vec(g_pre[l, 1]), vec(g_kidx[l]), bsz, wt, wka, wki, wwit, wb))
        half_tiles = seq // DSA_TILE // 2
        oa_lo = _dsa(qit, wit, ki, qat, ka, vat, bias, g_out_a[l], bsz, seq, 0, half_tiles)
        oa_hi = _dsa(qit, wit, ki, qat, ka, vat, bias, g_out_a[l], bsz, seq, half_tiles, half_tiles)
        ob = _sb(qb, kb, vb, vec(g_out_b[l]), bsz, seq)
        h = _ffn(h1, mods, vec(g_pre[l, 2]), vec(g_post[l, 2]), *ffn_w[1], 2, seq,
                 mix=(oa_lo.reshape(bsz * seq // 2, gw), oa_hi.reshape(bsz * seq // 2, gw),
                      ob.reshape(bsz * seq, gw), woa, wob, vec(g_post[l, 1])))
    return h.reshape(bsz, seq, d)
```

```python
import functools
import math

import numpy as np
import jax
import jax.numpy as jnp
from jax import lax
from jax.experimental import pallas as pl
from jax.experimental.pallas import tpu as pltpu

F32 = jnp.float32
BF16 = jnp.bfloat16
I32 = jnp.int32
I16 = jnp.int16

EPS = 1e-6
HEAD_DIM = 64
N_HEADS = 8
GROUP_WIDTH = N_HEADS * HEAD_DIM
IDX_DIM = 64
TOPK_MAX = 256
N_BUCKETS = 32
MAX_DISTANCE = 128
SUBLANES = 8
V_ROWS = 80
FFN_CHUNK = 256
TOKEN_TILE = 512
MIX_TOKEN_TILE = 1024
DSA_TILE = 256
SCORE_ROWS = 64
HEADS_UNROLL_BLOCKS = 8
SB_TILE = 128
SB_WINDOW = 3
SB_PAIR = 4
LOG2E = math.log2(math.e)
NEG_BIG = -1e30
SB_DEAD = 104.0 * LOG2E
VMEM_LIMIT = 56 * 1024 * 1024
INT_MIN = np.int32(-2 ** 31)
HALF = 2 ** 15
ONE16, ZERO16, MIN16 = np.int16(1), np.int16(0), np.int16(-HALF)


def _dot(a, b):
    return jnp.dot(a, b, preferred_element_type=F32)


def _dot_nt(a, b):
    return lax.dot_general(a, b, (((1,), (1,)), ((), ())), preferred_element_type=F32)


def _rms(x, g):
    return x * lax.rsqrt(jnp.mean(x * x, axis=-1, keepdims=True) + EPS) * g


def _resident(shape):
    n = len(shape)
    return pl.BlockSpec(shape, lambda *_: (0,) * n, pipeline_mode=pl.Buffered(1))


def _params(n_grid):
    return pltpu.CompilerParams(dimension_semantics=("arbitrary",) * n_grid,
                                vmem_limit_bytes=VMEM_LIMIT)


def _ada_kernel(c_ref, w_ref, b_ref, o_ref):
    c = c_ref[...]
    a = c * jax.nn.sigmoid(c)
    w = w_ref[...]
    a_hi = a.astype(BF16)
    a_lo = (a - a_hi.astype(F32)).astype(BF16)
    w_hi = w.astype(BF16)
    w_lo = (w - w_hi.astype(F32)).astype(BF16)
    o_ref[...] = _dot(a_hi, w_hi) + (_dot(a_hi, w_lo) + _dot(a_lo, w_hi)) + b_ref[...]


def _ada(c, w, b):
    bsz, d = c.shape
    n = w.shape[1]
    tn = 1152 if n % 1152 == 0 else n
    return pl.pallas_call(
        _ada_kernel,
        grid=(n // tn,),
        in_specs=[pl.BlockSpec((bsz, d), lambda j: (0, 0)),
                  pl.BlockSpec((d, tn), lambda j: (0, j)),
                  pl.BlockSpec((1, tn), lambda j: (0, j))],
        out_specs=pl.BlockSpec((bsz, tn), lambda j: (0, j)),
        out_shape=jax.ShapeDtypeStruct((bsz, n), F32),
        compiler_params=_params(1),
        name="ada",
    )(c, w, b.reshape(1, n))


def _ffn_kernel(*refs, sub, mix, proj):
    refs = list(refs)
    take = lambda n: [refs.pop(0) for _ in range(n)]
    (h_ref,) = take(1)
    if mix:
        oa_lo_ref, oa_hi_ref, ob_ref, woa_ref, wob_ref, gpm_ref = take(6)
    mods_ref, gpre_ref, gpost_ref, wg_ref, wu_ref, wd_ref = take(6)
    if proj:
        gpre_next_ref, *proj_in = take(7)
    (o_ref,) = take(1)
    if proj:
        proj_out = take(9)
    n_scr, a_scr = refs
    x = h_ref[...]
    if mix:
        first_half = pl.program_id(0) % 2 == 0
        oa = jnp.where(first_half, oa_lo_ref[...], oa_hi_ref[...])
        o = _dot(oa, woa_ref[...]) + _dot(ob_ref[...], wob_ref[...])
        x = x + mods_ref[0, 5:6, :] * _rms(o, gpm_ref[...])
    shift = mods_ref[0, 3 * sub:3 * sub + 1, :]
    scale = mods_ref[0, 3 * sub + 1:3 * sub + 2, :]
    gate = mods_ref[0, 3 * sub + 2:3 * sub + 3, :]
    n = _rms(x, gpre_ref[...]) * (1.0 + scale) + shift
    n_scr[...] = n.astype(BF16)
    fc = FFN_CHUNK
    for c0 in range(0, wg_ref.shape[1], fc):
        g = _dot(n_scr[...], wg_ref[:, c0:c0 + fc])
        u = _dot(n_scr[...], wu_ref[:, c0:c0 + fc])
        a_scr[:, c0:c0 + fc] = (g * jax.nn.sigmoid(g) * u).astype(BF16)
    f = _dot(a_scr[...], wd_ref[...])
    y = x + 0.5 * gate * _rms(f, gpost_ref[...])
    o_ref[...] = y
    if proj:
        shift = mods_ref[0, 3 * (sub + 1):3 * (sub + 1) + 1, :]
        scale = mods_ref[0, 3 * (sub + 1) + 1:3 * (sub + 1) + 2, :]
        n_scr[...] = (_rms(y, gpre_next_ref[...]) * (1.0 + scale) + shift).astype(BF16)
        _project(n_scr, *proj_in, *proj_out)


def _ffn(h, mods, g_pre, g_post, wg, wu, wd, sub, seq, mix=None, proj=None):
    n_tok, d = h.shape
    tm = min(TOKEN_TILE if proj is not None else MIX_TOKEN_TILE, seq)
    tpb = seq // tm
    row = pl.BlockSpec((tm, d), lambda i: (i, 0))
    vec = pl.BlockSpec((1, d), lambda i: (0, 0))
    mod_spec = pl.BlockSpec((1, mods.shape[1], d), lambda i: (i // tpb, 0, 0))
    args, specs = [h], [row]
    out_specs, out_shape = [row], [jax.ShapeDtypeStruct((n_tok, d), F32)]
    if mix is not None:
        oa_lo, oa_hi, ob, woa, wob, g_post_mix = mix
        gw = ob.shape[1]
        assert tpb == 2 and oa_lo.shape[0] * 2 == n_tok
        half = pl.BlockSpec((tm, gw), lambda i: (i, 0))
        per_batch = pl.BlockSpec((tm, gw), lambda i: (i // 2, 0))
        args += [oa_lo, oa_hi, ob, woa, wob, g_post_mix]
        specs += [per_batch, per_batch, half, _resident(woa.shape), _resident(wob.shape), vec]
    args += [mods, g_pre, g_post, wg, wu, wd]
    specs += [mod_spec, vec, vec, _resident(wg.shape), _resident(wu.shape), _resident(wd.shape)]
    if proj is not None:
        g_pre_next, g_kidx, bsz, *weights = proj
        gw = GROUP_WIDTH
        args += [g_pre_next, g_kidx, *weights]
        specs += [vec, pl.BlockSpec((1, IDX_DIM), lambda i: (0, 0))] + [_resident(w.shape) for w in weights]
        hm = pl.BlockSpec((1, N_HEADS, tm, HEAD_DIM), lambda i: (i // tpb, 0, i % tpb, 0))
        hm_shape = jax.ShapeDtypeStruct((bsz, N_HEADS, seq, HEAD_DIM), BF16)
        fm = pl.BlockSpec((1, gw, tm), lambda i: (i // tpb, 0, i % tpb))
        fm_shape = jax.ShapeDtypeStruct((bsz, gw, seq), BF16)
        vfm = pl.BlockSpec((1, N_HEADS * V_ROWS, tm), lambda i: (i // tpb, 0, i % tpb))
        out_specs += [fm, hm, vfm, fm,
                      pl.BlockSpec((1, tm, IDX_DIM), lambda i: (i // tpb, i % tpb, 0)),
                      pl.BlockSpec((1, N_HEADS, tm), lambda i: (i // tpb, 0, i % tpb)),
                      hm, hm, hm]
        out_shape += [fm_shape, hm_shape, jax.ShapeDtypeStruct((bsz, N_HEADS * V_ROWS, seq), BF16),
                      fm_shape,
                      jax.ShapeDtypeStruct((bsz, seq, IDX_DIM), BF16),
                      jax.ShapeDtypeStruct((bsz, N_HEADS, seq), F32),
                      hm_shape, hm_shape, hm_shape]
    out = pl.pallas_call(
        functools.partial(_ffn_kernel, sub=sub, mix=mix is not None, proj=proj is not None),
        grid=(n_tok // tm,),
        in_specs=specs,
        out_specs=out_specs,
        out_shape=out_shape,
        scratch_shapes=[pltpu.VMEM((tm, d), BF16), pltpu.VMEM((tm, wd.shape[0]), BF16)],
        compiler_params=_params(1),
        name="ffn_mix" if mix is not None else "ffn_proj" if proj is not None else "ffn",
    )(*args)
    return out if proj is not None else out[0]


def _project(n_scr, gk_ref, wt_ref, wka_ref, wki_ref, wwit_ref, wb_ref,
             qat_ref, ka_ref, vat_ref, qit_ref, ki_ref, wit_ref, qb_ref, kb_ref, vb_ref):
    gw = GROUP_WIDTH
    qscale = HEAD_DIM ** -0.5

    def heads(w, out_ref, mul):
        p = _dot(n_scr[...], w)
        if mul != 1.0:
            p = p * mul
        for hd in range(N_HEADS):
            out_ref[0, hd] = p[:, hd * HEAD_DIM:(hd + 1) * HEAD_DIM].astype(BF16)

    def feature_major(col, out_ref, mul):
        p = _dot_nt(wt_ref[col * gw:(col + 1) * gw, :], n_scr[...])
        if mul != 1.0:
            p = p * mul
        out_ref[0] = p.astype(BF16)

    feature_major(0, qat_ref, qscale * LOG2E)
    v = _dot_nt(wt_ref[gw:2 * gw, :], n_scr[...])
    pad = V_ROWS - HEAD_DIM
    ones_row = (lax.broadcasted_iota(I32, (pad, v.shape[1]), 0) == 0).astype(BF16)
    for hd in range(N_HEADS):
        vat_ref[0, hd * V_ROWS:hd * V_ROWS + HEAD_DIM, :] = v[hd * HEAD_DIM:(hd + 1) * HEAD_DIM, :].astype(BF16)
        vat_ref[0, hd * V_ROWS + HEAD_DIM:(hd + 1) * V_ROWS, :] = ones_row
    feature_major(2, qit_ref, 1.0)
    heads(wka_ref[...], ka_ref, 1.0)
    heads(wb_ref[:, :gw], qb_ref, qscale * LOG2E)
    heads(wb_ref[:, gw:2 * gw], kb_ref, 1.0)
    heads(wb_ref[:, 2 * gw:], vb_ref, 1.0)
    ki = _dot(n_scr[...], wki_ref[...])
    ki_ref[0] = _rms(ki, gk_ref[...]).astype(BF16)
    wit = _dot_nt(wwit_ref[...], n_scr[...])
    wit_ref[0] = wit[:N_HEADS, :] * ((N_HEADS * IDX_DIM) ** -0.5)


def _t5_bucket_np(n):
    n = np.maximum(n, 0)
    max_exact = N_BUCKETS // 2
    nf = np.maximum(n, 1).astype(np.float32)
    large = max_exact + (np.log(nf / np.float32(max_exact)) / np.float32(math.log(MAX_DISTANCE / max_exact))
                         * np.float32(N_BUCKETS - max_exact)).astype(np.int32)
    large = np.minimum(large, N_BUCKETS - 1)
    return np.where(n < max_exact, n, large).astype(np.int32)


def _bias_kernel(relb_ref, bucket_ref, o_ref):
    hd = pl.program_id(0)
    far = relb_ref[N_BUCKETS - 1, hd]
    for r in range(2):
        bk = bucket_ref[r]
        acc = jnp.zeros(bk.shape, F32)
        for k in range(N_BUCKETS - 1):
            acc = jnp.where(bk == k, (relb_ref[k, hd] - far) * LOG2E, acc)
        o_ref[0, r] = acc


def _bias_tiles(rel_bias, t):
    assert t >= MAX_DISTANCE
    s = np.arange(t)[:, None]
    q = np.arange(t)[None, :]
    buckets = np.stack([_t5_bucket_np(q - s), _t5_bucket_np(t + q - s)]).astype(np.int32)
    return pl.pallas_call(
        _bias_kernel,
        grid=(N_HEADS,),
        in_specs=[pl.BlockSpec(memory_space=pltpu.SMEM),
                  pl.BlockSpec((2, t, t), lambda hd: (0, 0, 0))],
        out_specs=pl.BlockSpec((1, 2, t, t), lambda hd: (hd, 0, 0, 0)),
        out_shape=jax.ShapeDtypeStruct((N_HEADS, 2, t, t), F32),
        compiler_params=_params(1),
        name="bias_tiles",
    )(rel_bias, jnp.asarray(buckets))


def _tree(op, xs):
    xs = list(xs)
    while len(xs) > 1:
        xs = [op(xs[j], xs[j + 1]) if j + 1 < len(xs) else xs[j] for j in range(0, len(xs), 2)]
    return xs[0]


def _row_groups(x):
    return [x[r:r + SUBLANES, :] for r in range(0, x.shape[0], SUBLANES)]


def _all_sublanes(op, x):
    for shift in (4, 2, 1):
        x = op(x, pltpu.roll(x, shift, 0))
    return x


def _dsa_kernel(qit_ref, wit_ref, ki_ref, qat_ref, ka_ref, vat_ref, bias_ref, gt_ref, tri_ref, o_ref,
                key_scr, hi_scr, lo_scr, selm_scr, logit_a, logit_b, out_scr, *, topk, first_tile, n_q_tiles):
    t = DSA_TILE
    kf = float(topk)
    krow = lax.broadcasted_iota(I32, (t, t), 0)
    qcol = lax.broadcasted_iota(I32, (t, t), 1)
    causal = krow <= qcol

    def tile_work(nkb):
        diag = nkb - 1

        wt = wit_ref[0]
        for kb in range(nkb):
            for r0 in range(0, t, SCORE_ROWS):
                kblk = ki_ref[0, kb * t + r0:kb * t + r0 + SCORE_ROWS, :]
                acc = None
                for hd in range(N_HEADS):
                    sc = _dot(kblk, qit_ref[0, hd * IDX_DIM:(hd + 1) * IDX_DIM, :])
                    term = jnp.maximum(sc, 0.0) * wt[hd:hd + 1, :]
                    acc = term if acc is None else acc + term
                acc = acc + 0.0
                if kb == diag:
                    acc = jnp.where(causal[r0:r0 + SCORE_ROWS, :], acc, -jnp.inf)
                bits = pltpu.bitcast(acc, I32)
                key = bits ^ ((bits >> 31) & jnp.int32(0x7FFFFFFF))
                key_scr[kb, r0:r0 + SCORE_ROWS, :] = key
                hi_scr[kb, r0:r0 + SCORE_ROWS, :] = (key >> 16).astype(I16)

        def packed(x):
            return jnp.concatenate([x, x], axis=0).astype(I16)

        def column_total(parts):
            tot = _tree(jnp.add, parts).astype(F32)
            return _all_sublanes(jnp.add, tot[:SUBLANES, :] + tot[SUBLANES:, :])

        def search16(src_scr, need):
            def step(j, ans):
                cand_off = ans | lax.shift_left(jnp.int32(1), 15 - j)
                cand = packed(cand_off - HALF)
                accs = [jnp.zeros((2 * SUBLANES, t), I16) for _ in range(4)]
                n = 0
                for kb in range(nkb):
                    blk = src_scr[kb]
                    for r in range(0, t, 2 * SUBLANES):
                        hit = jnp.where(blk[r:r + 2 * SUBLANES, :] >= cand, ONE16, ZERO16)
                        accs[n % 4] = accs[n % 4] + hit
                        n += 1
                return jnp.where(column_total(accs) >= need, cand_off, ans)
            return lax.fori_loop(0, 16, step, jnp.zeros((SUBLANES, t), I32))

        if nkb * t <= topk:
            thr = jnp.full((SUBLANES, t), INT_MIN, I32)
        else:
            t_hi = search16(hi_scr, kf) - HALF
            t_hi16 = packed(t_hi)
            above = [jnp.zeros((2 * SUBLANES, t), I16) for _ in range(4)]
            n = 0
            for kb in range(nkb):
                for r in range(0, t, 2 * SUBLANES):
                    rows = slice(r, r + 2 * SUBLANES)
                    h = hi_scr[kb, rows, :]
                    lo = ((key_scr[kb, rows, :] & jnp.int32(0xFFFF)) - HALF).astype(I16)
                    lo_scr[kb, rows, :] = jnp.where(h == t_hi16, lo, MIN16)
                    above[n % 4] = above[n % 4] + jnp.where(h > t_hi16, ONE16, ZERO16)
                    n += 1
            t_lo = search16(lo_scr, kf - column_total(above))
            thr = lax.shift_left(t_hi, 16) | t_lo

        n_ge = jnp.zeros((SUBLANES, t), F32)
        for kb in range(nkb):
            blk = key_scr[kb]
            for r, grp in enumerate(_row_groups(blk)):
                rows = slice(r * SUBLANES, (r + 1) * SUBLANES)
                sel = grp >= thr
                n_ge = n_ge + jnp.where(sel, 1.0, 0.0)
                m = jnp.where(sel, 0.0, NEG_BIG)
                if kb == diag:
                    m = jnp.where(causal[rows, :], m, NEG_BIG)
                selm_scr[kb, rows, :] = m
        n_ge = _all_sublanes(jnp.add, n_ge)
        excess = jnp.max(jnp.where(n_ge > kf, 1.0, 0.0))

        @pl.when(excess > 0.0)
        def _():
            thr1 = thr[0:1, :]

            def tied(kb):
                return jnp.where(key_scr[kb] == thr1, 1.0, 0.0)

            def count_tied(kb, acc):
                return acc + jnp.sum(tied(kb), axis=0, keepdims=True)

            n_tied = lax.fori_loop(0, nkb, count_tied, jnp.zeros((1, t), F32))
            need = kf - (n_ge[0:1, :] - n_tied)

            def rewrite(kb, before):
                e = tied(kb)
                rank = _dot(tri_ref[...], e.astype(BF16)) + before
                k = key_scr[kb]
                keep = jnp.where(k > thr1, 0.0,
                                 jnp.where(k == thr1, jnp.where(rank < need, 0.0, NEG_BIG), NEG_BIG))
                selm_scr[kb] = jnp.where(selm_scr[kb] < 0.0, NEG_BIG, keep)
                return before + jnp.sum(e, axis=0, keepdims=True)

            lax.fori_loop(0, nkb, rewrite, jnp.zeros((1, t), F32))

        def head_rows(hd):
            if isinstance(hd, int):
                return slice(hd * HEAD_DIM, (hd + 1) * HEAD_DIM)
            return pl.ds(pl.multiple_of(hd * HEAD_DIM, HEAD_DIM), HEAD_DIM)

        def logits(hd, buf):
            qt = qat_ref[0, head_rows(hd), :]
            maxes = []
            for kb in range(nkb):
                lg = _dot(ka_ref[0, hd, kb * t:(kb + 1) * t, :], qt) + selm_scr[kb]
                if kb >= nkb - 2:
                    lg = lg + bias_ref[hd, diag - kb]
                buf[kb] = lg
                maxes.append(jnp.max(lg, axis=0, keepdims=True))
            return _tree(jnp.maximum, maxes)

        def weighted_values(hd, buf, m):
            if isinstance(hd, int):
                vrows = slice(hd * V_ROWS, (hd + 1) * V_ROWS)
            else:
                vrows = pl.ds(pl.multiple_of(hd * V_ROWS, 16), V_ROWS)
            acc = None
            for kb in range(nkb):
                p = jnp.exp2(buf[kb] - m).astype(BF16)
                pv = _dot(vat_ref[0, vrows, kb * t:(kb + 1) * t], p)
                acc = pv if acc is None else acc + pv
            out_scr[head_rows(hd), :] = acc[:HEAD_DIM, :] * (1.0 / acc[HEAD_DIM:HEAD_DIM + 1, :])

        def head_pair(j, m_even):
            m_odd = logits(2 * j + 1, logit_b)
            weighted_values(2 * j, logit_a, m_even)
            m_next = logits((2 * j + 2) % N_HEADS, logit_a)
            weighted_values(2 * j + 1, logit_b, m_odd)
            return m_next

        lax.fori_loop(0, N_HEADS // 2, head_pair, logits(0, logit_a), unroll=nkb <= HEADS_UNROLL_BLOCKS)

    i = pl.program_id(1)
    for k in range(n_q_tiles):
        pl.when(i == k)(functools.partial(tile_work, first_tile + k + 1))

    o = out_scr[...]
    o = o * lax.rsqrt(jnp.mean(o * o, axis=0, keepdims=True) + EPS) * gt_ref[...]
    o_ref[0] = o.T.astype(BF16)


def _dsa(qit, wit, ki, qat, ka, vat, bias, g_out, bsz, seq, first_tile, n_q_tiles):
    t = DSA_TILE
    nq = first_tile + n_q_tiles
    keys = nq * t
    gw = GROUP_WIDTH
    topk = min(TOPK_MAX, seq // 4)
    q_fm = pl.BlockSpec((1, gw, t), lambda b, i: (b, 0, i + first_tile))
    gt = jnp.broadcast_to(g_out.reshape(gw, 1), (gw, t))
    r = np.arange(t)
    lower = jnp.asarray(r[None, :] < r[:, None], dtype=BF16)
    return pl.pallas_call(
        functools.partial(_dsa_kernel, topk=topk, first_tile=first_tile, n_q_tiles=n_q_tiles),
        grid=(bsz, n_q_tiles),
        in_specs=[q_fm,
                  pl.BlockSpec((1, N_HEADS, t), lambda b, i: (b, 0, i + first_tile)),
                  pl.BlockSpec((1, keys, IDX_DIM), lambda b, i: (b, 0, 0)),
                  q_fm,
                  pl.BlockSpec((1, N_HEADS, keys, HEAD_DIM), lambda b, i: (b, 0, 0, 0)),
                  pl.BlockSpec((1, N_HEADS * V_ROWS, keys), lambda b, i: (b, 0, 0)),
                  _resident(bias.shape),
                  _resident((gw, t)),
                  _resident((t, t))],
        out_specs=pl.BlockSpec((1, t, gw), lambda b, i: (b, i, 0)),
        out_shape=jax.ShapeDtypeStruct((bsz, n_q_tiles * t, gw), BF16),
        scratch_shapes=[pltpu.VMEM((nq, t, t), I32), pltpu.VMEM((nq, t, t), I16),
                        pltpu.VMEM((nq, t, t), I16), pltpu.VMEM((nq, t, t), F32),
                        pltpu.VMEM((nq, t, t), F32), pltpu.VMEM((nq, t, t), F32),
                        pltpu.VMEM((gw, t), F32)],
        compiler_params=_params(2),
        name="dsa_%d" % first_tile,
    )(qit, wit, ki, qat, ka, vat, bias, gt, lower)


def _neg_log2_1m_sigmoid(z2):
    return jnp.maximum(z2, 0.0) + jnp.log2(1.0 + jnp.exp2(-jnp.abs(z2)))


def _split_bf16(x):
    hi = x.astype(BF16)
    lo = (x - hi.astype(F32)).astype(BF16)
    return jnp.concatenate([hi, lo], axis=1)


def _sb_kernel(q_ref, k_ref, v_ref, tri_ref, g_ref, o_ref, z_scr, lh_scr, cs_scr, r_scr, acc_scr):
    t = SB_TILE
    nw = SB_WINDOW
    row = lax.broadcasted_iota(I32, (t, t), 0)
    col = lax.broadcasted_iota(I32, (t, t), 1)
    tiles = []
    for sub in range(SB_PAIR):
        i = pl.program_id(1) * SB_PAIR + sub
        wb = jnp.maximum(i - (nw - 1), 0)
        tiles.append((sub, i, wb, pl.ds(pl.multiple_of(wb * t, t), nw * t)))

    def causal(i, wb, w):
        return (col - row) < (i - wb - w) * t

    def q_rows(sub):
        return slice(sub * t, (sub + 1) * t)

    for sub, i, wb, wrows in tiles:
        for hd in range(N_HEADS):
            p = sub * N_HEADS + hd
            z = _dot_nt(q_ref[0, hd, q_rows(sub), :], k_ref[0, hd, wrows, :])
            z_scr[p] = z
            for w in range(nw):
                m = jnp.where(causal(i, wb, w), _neg_log2_1m_sigmoid(z[:, w * t:(w + 1) * t]), 0.0)
                lh_scr[(p * nw + w) * t:(p * nw + w + 1) * t, :] = _split_bf16(m)
    cs_scr[...] = _dot(lh_scr[...], tri_ref[...])
    for sub, i, wb, wrows in tiles:
        for hd in range(N_HEADS):
            p = sub * N_HEADS + hd
            after = None
            a_blocks = [None] * nw
            for w in reversed(range(nw)):
                cs = cs_scr[(p * nw + w) * t:(p * nw + w + 1) * t, :]
                c = cs[:, :t] if after is None else cs[:, :t] + after
                a = jnp.where(causal(i, wb, w), jnp.exp2(z_scr[p, :, w * t:(w + 1) * t] - c), 0.0)
                a_blocks[w] = a.astype(BF16)
                after = cs[:, t:] if after is None else after + cs[:, t:]
            acc_scr[p] = _dot(jnp.concatenate(a_blocks, axis=1), v_ref[0, hd, wrows, :])
            r_scr[p] = after

    for sub, i, wb, wrows in tiles:
        def far_block(kb, sub=sub):
            rows = pl.ds(pl.multiple_of(kb * t, t), t)
            rmin = None
            for hd in range(N_HEADS):
                p = sub * N_HEADS + hd
                z = _dot_nt(q_ref[0, hd, q_rows(sub), :], k_ref[0, hd, rows, :])
                cs = _dot(_split_bf16(_neg_log2_1m_sigmoid(z)), tri_ref[...])
                r_old = r_scr[p]
                a = jnp.exp2(z - cs[:, :t] - r_old)
                acc_scr[p] += _dot(a.astype(BF16), v_ref[0, hd, rows, :])
                r_new = r_old + cs[:, t:]
                r_scr[p] = r_new
                rmin = r_new if rmin is None else jnp.minimum(rmin, r_new)
            return jnp.min(rmin)

        def cond(c):
            kb, rmin = c
            return jnp.logical_and(kb >= 0, rmin < SB_DEAD)

        def body(c, far_block=far_block):
            kb, _ = c
            return kb - 1, far_block(kb)

        rmin0 = _tree(jnp.minimum, [r_scr[sub * N_HEADS + hd] for hd in range(N_HEADS)])
        lax.while_loop(cond, body, (wb - 1, jnp.min(rmin0)))
        o = jnp.concatenate([acc_scr[sub * N_HEADS + hd] for hd in range(N_HEADS)], axis=1)
        o_ref[0, q_rows(sub), :] = _rms(o, g_ref[...]).astype(BF16)


def _sb(q, k, v, g_out, bsz, seq):
    t = SB_TILE
    tq = SB_PAIR * t
    assert seq >= SB_WINDOW * t and seq % tq == 0
    j = np.arange(t)[:, None]
    s = np.arange(t)[None, :]
    tri = np.concatenate([(j >= s), np.ones((t, t), bool)], axis=1).astype(np.float32)
    tri = np.concatenate([tri, tri], axis=0)
    q_hm = pl.BlockSpec((1, N_HEADS, tq, HEAD_DIM), lambda b, i: (b, 0, i, 0))
    kv_hm = pl.BlockSpec((1, N_HEADS, seq, HEAD_DIM), lambda b, i: (b, 0, 0, 0))
    n_p = SB_PAIR * N_HEADS
    n_rows = n_p * SB_WINDOW * t
    return pl.pallas_call(
        _sb_kernel,
        grid=(bsz, seq // tq),
        in_specs=[q_hm, kv_hm, kv_hm, _resident((2 * t, 2 * t)),
                  pl.BlockSpec((1, GROUP_WIDTH), lambda b, i: (0, 0))],
        out_specs=pl.BlockSpec((1, tq, GROUP_WIDTH), lambda b, i: (b, i, 0)),
        out_shape=jax.ShapeDtypeStruct((bsz, seq, GROUP_WIDTH), BF16),
        scratch_shapes=[pltpu.VMEM((n_p, t, SB_WINDOW * t), F32),
                        pltpu.VMEM((n_rows, 2 * t), BF16),
                        pltpu.VMEM((n_rows, 2 * t), F32),
                        pltpu.VMEM((n_p, t, t), F32),
                        pltpu.VMEM((n_p, t, HEAD_DIM), F32)],
        compiler_params=_params(2),
        name="sb",
    )(q, k, v, jnp.asarray(tri, dtype=BF16), g_out)


def kernel(x, c, w_ada, b_ada, g_pre, g_post, w_ffn_gate, w_ffn_up, w_ffn_down, w_in, g_kidx,
           rel_bias, g_out_a, g_out_b, w_out):
    bsz, seq, d = x.shape
    depth = w_ada.shape[0]
    gw = GROUP_WIDTH
    h = x.reshape(bsz * seq, d)
    bias = _bias_tiles(rel_bias, DSA_TILE)
    for l in range(depth):
        mods = _ada(c, w_ada[l], b_ada[l]).reshape(bsz, -1, d)
        ffn_w = [(w_ffn_gate[l, j].astype(BF16), w_ffn_up[l, j].astype(BF16),
                  w_ffn_down[l, j].astype(BF16)) for j in range(2)]
        w = w_in[l]
        o_ki = 4 * gw
        o_wi = o_ki + IDX_DIM
        o_b = o_wi + N_HEADS
        wt = jnp.concatenate([w[:, :gw], w[:, 2 * gw:3 * gw], w[:, 3 * gw:o_ki]], axis=1).T.astype(BF16)
        wka = w[:, gw:2 * gw].astype(BF16)
        wki = w[:, o_ki:o_wi].astype(BF16)
        wwit = jnp.pad(w[:, o_wi:o_b].T, ((0, 16 - N_HEADS), (0, 0))).astype(BF16)
        wb = w[:, o_b:].astype(BF16)
        woa = w_out[l, :gw].astype(BF16)
        wob = w_out[l, gw:].astype(BF16)
        vec = lambda g: g.reshape(1, -1)

        h1, qat, ka, vat, qit, ki, wit, qb, kb, vb = _ffn(
            h, mods, vec(g_pre[l, 0]), vec(g_post[l, 0]), *ffn_w[0], 0, seq,
            proj=(vec(g_pre[l, 1]), vec(g_kidx[l]), bsz, wt, wka, wki, wwit, wb))
        half_tiles = seq // DSA_TILE // 2
        oa_lo = _dsa(qit, wit, ki, qat, ka, vat, bias, g_out_a[l], bsz, seq, 0, half_tiles)
        oa_hi = _dsa(qit, wit, ki, qat, ka, vat, bias, g_out_a[l], bsz, seq, half_tiles, half_tiles)
        ob = _sb(qb, kb, vb, vec(g_out_b[l]), bsz, seq)
        h = _ffn(h1, mods, vec(g_pre[l, 2]), vec(g_post[l, 2]), *ffn_w[1], 2, seq,
                 mix=(oa_lo.reshape(bsz * seq // 2, gw), oa_hi.reshape(bsz * seq // 2, gw),
                      ob.reshape(bsz * seq, gw), woa, wob, vec(g_post[l, 1])))
    return h.reshape(bsz, seq, d)
```

```python
import functools
import math

import numpy as np
import jax
import jax.numpy as jnp
from jax import lax
from jax.experimental import pallas as pl
from jax.experimental.pallas import tpu as pltpu

F32 = jnp.float32
BF16 = jnp.bfloat16
I32 = jnp.int32
I16 = jnp.int16

EPS = 1e-6
HEAD_DIM = 64
N_HEADS = 8
GROUP_WIDTH = N_HEADS * HEAD_DIM
IDX_DIM = 64
TOPK_MAX = 256
N_BUCKETS = 32
MAX_DISTANCE = 128
SUBLANES = 8
V_ROWS = 80
FFN_CHUNK = 256
TOKEN_TILE = 512
MIX_TOKEN_TILE = 1024
DSA_TILE = 256
SCORE_ROWS = 64
HEADS_UNROLL_BLOCKS = 8
SB_TILE = 128
SB_WINDOW = 3
SB_PAIR = 4
LOG2E = math.log2(math.e)
NEG_BIG = -1e30
SB_DEAD = 104.0 * LOG2E
VMEM_LIMIT = 56 * 1024 * 1024
INT_MIN = np.int32(-2 ** 31)
HALF = 2 ** 15
ONE16, ZERO16, MIN16 = np.int16(1), np.int16(0), np.int16(-HALF)


def _dot(a, b):
    return jnp.dot(a, b, preferred_element_type=F32)


def _dot_nt(a, b):
    return lax.dot_general(a, b, (((1,), (1,)), ((), ())), preferred_element_type=F32)


def _rms(x, g):
    return x * lax.rsqrt(jnp.mean(x * x, axis=-1, keepdims=True) + EPS) * g


def _resident(shape):
    n = len(shape)
    return pl.BlockSpec(shape, lambda *_: (0,) * n, pipeline_mode=pl.Buffered(1))


def _params(n_grid):
    return pltpu.CompilerParams(dimension_semantics=("arbitrary",) * n_grid,
                                vmem_limit_bytes=VMEM_LIMIT)


def _ada_kernel(c_ref, w_ref, b_ref, o_ref):
    c = c_ref[...]
    a = c * jax.nn.sigmoid(c)
    w = w_ref[...]
    a_hi = a.astype(BF16)
    a_lo = (a - a_hi.astype(F32)).astype(BF16)
    w_hi = w.astype(BF16)
    w_lo = (w - w_hi.astype(F32)).astype(BF16)
    o_ref[...] = _dot(a_hi, w_hi) + (_dot(a_hi, w_lo) + _dot(a_lo, w_hi)) + b_ref[...]


def _ada(c, w, b):
    bsz, d = c.shape
    n = w.shape[1]
    tn = 1152 if n % 1152 == 0 else n
    return pl.pallas_call(
        _ada_kernel,
        grid=(n // tn,),
        in_specs=[pl.BlockSpec((bsz, d), lambda j: (0, 0)),
                  pl.BlockSpec((d, tn), lambda j: (0, j)),
                  pl.BlockSpec((1, tn), lambda j: (0, j))],
        out_specs=pl.BlockSpec((bsz, tn), lambda j: (0, j)),
        out_shape=jax.ShapeDtypeStruct((bsz, n), F32),
        compiler_params=_params(1),
        name="ada",
    )(c, w, b.reshape(1, n))


def _ffn_kernel(*refs, sub, mix, proj):
    refs = list(refs)
    take = lambda n: [refs.pop(0) for _ in range(n)]
    (h_ref,) = take(1)
    if mix:
        oa_lo_ref, oa_mid_ref, oa_hi_ref, ob_ref, woa_ref, wob_ref, gpm_ref = take(7)
    mods_ref, gpre_ref, gpost_ref, wg_ref, wu_ref, wd_ref = take(6)
    if proj:
        gpre_next_ref, *proj_in = take(7)
    (o_ref,) = take(1)
    if proj:
        proj_out = take(9)
    n_scr, a_scr = refs
    x = h_ref[...]
    if mix:
        first_half = pl.program_id(0) % 2 == 0
        second = jnp.concatenate([oa_mid_ref[...], oa_hi_ref[...]], axis=0)
        oa = jnp.where(first_half, oa_lo_ref[...], second)
        o = _dot(oa, woa_ref[...]) + _dot(ob_ref[...], wob_ref[...])
        x = x + mods_ref[0, 5:6, :] * _rms(o, gpm_ref[...])
    shift = mods_ref[0, 3 * sub:3 * sub + 1, :]
    scale = mods_ref[0, 3 * sub + 1:3 * sub + 2, :]
    gate = mods_ref[0, 3 * sub + 2:3 * sub + 3, :]
    n = _rms(x, gpre_ref[...]) * (1.0 + scale) + shift
    n_scr[...] = n.astype(BF16)
    fc = FFN_CHUNK
    for c0 in range(0, wg_ref.shape[1], fc):
        g = _dot(n_scr[...], wg_ref[:, c0:c0 + fc])
        u = _dot(n_scr[...], wu_ref[:, c0:c0 + fc])
        a_scr[:, c0:c0 + fc] = (g * jax.nn.sigmoid(g) * u).astype(BF16)
    f = _dot(a_scr[...], wd_ref[...])
    y = x + 0.5 * gate * _rms(f, gpost_ref[...])
    o_ref[...] = y
    if proj:
        shift = mods_ref[0, 3 * (sub + 1):3 * (sub + 1) + 1, :]
        scale = mods_ref[0, 3 * (sub + 1) + 1:3 * (sub + 1) + 2, :]
        n_scr[...] = (_rms(y, gpre_next_ref[...]) * (1.0 + scale) + shift).astype(BF16)
        _project(n_scr, *proj_in, *proj_out)


def _ffn(h, mods, g_pre, g_post, wg, wu, wd, sub, seq, mix=None, proj=None):
    n_tok, d = h.shape
    tm = min(TOKEN_TILE if proj is not None else MIX_TOKEN_TILE, seq)
    tpb = seq // tm
    row = pl.BlockSpec((tm, d), lambda i: (i, 0))
    vec = pl.BlockSpec((1, d), lambda i: (0, 0))
    mod_spec = pl.BlockSpec((1, mods.shape[1], d), lambda i: (i // tpb, 0, 0))
    args, specs = [h], [row]
    out_specs, out_shape = [row], [jax.ShapeDtypeStruct((n_tok, d), F32)]
    if mix is not None:
        oa_lo, oa_mid, oa_hi, ob, woa, wob, g_post_mix = mix
        gw = ob.shape[1]
        assert tpb == 2 and oa_lo.shape[0] * 2 == n_tok and oa_mid.shape[0] * 4 == n_tok
        half = pl.BlockSpec((tm, gw), lambda i: (i, 0))
        per_batch = pl.BlockSpec((tm, gw), lambda i: (i // 2, 0))
        quarter = pl.BlockSpec((tm // 2, gw), lambda i: (i // 2, 0))
        args += [oa_lo, oa_mid, oa_hi, ob, woa, wob, g_post_mix]
        specs += [per_batch, quarter, quarter, half, _resident(woa.shape), _resident(wob.shape), vec]
    args += [mods, g_pre, g_post, wg, wu, wd]
    specs += [mod_spec, vec, vec, _resident(wg.shape), _resident(wu.shape), _resident(wd.shape)]
    if proj is not None:
        g_pre_next, g_kidx, bsz, *weights = proj
        gw = GROUP_WIDTH
        args += [g_pre_next, g_kidx, *weights]
        specs += [vec, pl.BlockSpec((1, IDX_DIM), lambda i: (0, 0))] + [_resident(w.shape) for w in weights]
        hm = pl.BlockSpec((1, N_HEADS, tm, HEAD_DIM), lambda i: (i // tpb, 0, i % tpb, 0))
        hm_shape = jax.ShapeDtypeStruct((bsz, N_HEADS, seq, HEAD_DIM), BF16)
        fm = pl.BlockSpec((1, gw, tm), lambda i: (i // tpb, 0, i % tpb))
        fm_shape = jax.ShapeDtypeStruct((bsz, gw, seq), BF16)
        vfm = pl.BlockSpec((1, N_HEADS * V_ROWS, tm), lambda i: (i // tpb, 0, i % tpb))
        out_specs += [fm, hm, vfm, fm,
                      pl.BlockSpec((1, tm, IDX_DIM), lambda i: (i // tpb, i % tpb, 0)),
                      pl.BlockSpec((1, N_HEADS, tm), lambda i: (i // tpb, 0, i % tpb)),
                      hm, hm, hm]
        out_shape += [fm_shape, hm_shape, jax.ShapeDtypeStruct((bsz, N_HEADS * V_ROWS, seq), BF16),
                      fm_shape,
                      jax.ShapeDtypeStruct((bsz, seq, IDX_DIM), BF16),
                      jax.ShapeDtypeStruct((bsz, N_HEADS, seq), F32),
                      hm_shape, hm_shape, hm_shape]
    out = pl.pallas_call(
        functools.partial(_ffn_kernel, sub=sub, mix=mix is not None, proj=proj is not None),
        grid=(n_tok // tm,),
        in_specs=specs,
        out_specs=out_specs,
        out_shape=out_shape,
        scratch_shapes=[pltpu.VMEM((tm, d), BF16), pltpu.VMEM((tm, wd.shape[0]), BF16)],
        compiler_params=_params(1),
        name="ffn_mix" if mix is not None else "ffn_proj" if proj is not None else "ffn",
    )(*args)
    return out if proj is not None else out[0]


def _project(n_scr, gk_ref, wt_ref, wka_ref, wki_ref, wwit_ref, wb_ref,
             qat_ref, ka_ref, vat_ref, qit_ref, ki_ref, wit_ref, qb_ref, kb_ref, vb_ref):
    gw = GROUP_WIDTH
    qscale = HEAD_DIM ** -0.5

    def heads(w, out_ref, mul):
        p = _dot(n_scr[...], w)
        if mul != 1.0:
            p = p * mul
        for hd in range(N_HEADS):
            out_ref[0, hd] = p[:, hd * HEAD_DIM:(hd + 1) * HEAD_DIM].astype(BF16)

    def feature_major(col, out_ref, mul):
        p = _dot_nt(wt_ref[col * gw:(col + 1) * gw, :], n_scr[...])
        if mul != 1.0:
            p = p * mul
        out_ref[0] = p.astype(BF16)

    feature_major(0, qat_ref, qscale * LOG2E)
    v = _dot_nt(wt_ref[gw:2 * gw, :], n_scr[...])
    pad = V_ROWS - HEAD_DIM
    ones_row = (lax.broadcasted_iota(I32, (pad, v.shape[1]), 0) == 0).astype(BF16)
    for hd in range(N_HEADS):
        vat_ref[0, hd * V_ROWS:hd * V_ROWS + HEAD_DIM, :] = v[hd * HEAD_DIM:(hd + 1) * HEAD_DIM, :].astype(BF16)
        vat_ref[0, hd * V_ROWS + HEAD_DIM:(hd + 1) * V_ROWS, :] = ones_row
    feature_major(2, qit_ref, 1.0)
    heads(wka_ref[...], ka_ref, 1.0)
    heads(wb_ref[:, :gw], qb_ref, qscale * LOG2E)
    heads(wb_ref[:, gw:2 * gw], kb_ref, 1.0)
    heads(wb_ref[:, 2 * gw:], vb_ref, 1.0)
    ki = _dot(n_scr[...], wki_ref[...])
    ki_ref[0] = _rms(ki, gk_ref[...]).astype(BF16)
    wit = _dot_nt(wwit_ref[...], n_scr[...])
    wit_ref[0] = wit[:N_HEADS, :] * ((N_HEADS * IDX_DIM) ** -0.5)


def _t5_bucket_np(n):
    n = np.maximum(n, 0)
    max_exact = N_BUCKETS // 2
    nf = np.maximum(n, 1).astype(np.float32)
    large = max_exact + (np.log(nf / np.float32(max_exact)) / np.float32(math.log(MAX_DISTANCE / max_exact))
                         * np.float32(N_BUCKETS - max_exact)).astype(np.int32)
    large = np.minimum(large, N_BUCKETS - 1)
    return np.where(n < max_exact, n, large).astype(np.int32)


def _bias_kernel(relb_ref, bucket_ref, o_ref):
    hd = pl.program_id(0)
    far = relb_ref[N_BUCKETS - 1, hd]
    for r in range(2):
        bk = bucket_ref[r]
        acc = jnp.zeros(bk.shape, F32)
        for k in range(N_BUCKETS - 1):
            acc = jnp.where(bk == k, (relb_ref[k, hd] - far) * LOG2E, acc)
        o_ref[0, r] = acc


def _bias_tiles(rel_bias, t):
    assert t >= MAX_DISTANCE
    s = np.arange(t)[:, None]
    q = np.arange(t)[None, :]
    buckets = np.stack([_t5_bucket_np(q - s), _t5_bucket_np(t + q - s)]).astype(np.int32)
    return pl.pallas_call(
        _bias_kernel,
        grid=(N_HEADS,),
        in_specs=[pl.BlockSpec(memory_space=pltpu.SMEM),
                  pl.BlockSpec((2, t, t), lambda hd: (0, 0, 0))],
        out_specs=pl.BlockSpec((1, 2, t, t), lambda hd: (hd, 0, 0, 0)),
        out_shape=jax.ShapeDtypeStruct((N_HEADS, 2, t, t), F32),
        compiler_params=_params(1),
        name="bias_tiles",
    )(rel_bias, jnp.asarray(buckets))


def _tree(op, xs):
    xs = list(xs)
    while len(xs) > 1:
        xs = [op(xs[j], xs[j + 1]) if j + 1 < len(xs) else xs[j] for j in range(0, len(xs), 2)]
    return xs[0]


def _row_groups(x):
    return [x[r:r + SUBLANES, :] for r in range(0, x.shape[0], SUBLANES)]


def _all_sublanes(op, x):
    for shift in (4, 2, 1):
        x = op(x, pltpu.roll(x, shift, 0))
    return x


def _dsa_kernel(qit_ref, wit_ref, ki_ref, qat_ref, ka_ref, vat_ref, bias_ref, gt_ref, tri_ref, o_ref,
                key_scr, hi_scr, lo_scr, selm_scr, logit_a, logit_b, out_scr, *, topk, first_tile, n_q_tiles):
    t = DSA_TILE
    kf = float(topk)
    krow = lax.broadcasted_iota(I32, (t, t), 0)
    qcol = lax.broadcasted_iota(I32, (t, t), 1)
    causal = krow <= qcol

    def tile_work(nkb):
        diag = nkb - 1

        wt = wit_ref[0]
        for kb in range(nkb):
            for r0 in range(0, t, SCORE_ROWS):
                kblk = ki_ref[0, kb * t + r0:kb * t + r0 + SCORE_ROWS, :]
                acc = None
                for hd in range(N_HEADS):
                    sc = _dot(kblk, qit_ref[0, hd * IDX_DIM:(hd + 1) * IDX_DIM, :])
                    term = jnp.maximum(sc, 0.0) * wt[hd:hd + 1, :]
                    acc = term if acc is None else acc + term
                acc = acc + 0.0
                if kb == diag:
                    acc = jnp.where(causal[r0:r0 + SCORE_ROWS, :], acc, -jnp.inf)
                bits = pltpu.bitcast(acc, I32)
                key = bits ^ ((bits >> 31) & jnp.int32(0x7FFFFFFF))
                key_scr[kb, r0:r0 + SCORE_ROWS, :] = key
                hi_scr[kb, r0:r0 + SCORE_ROWS, :] = (key >> 16).astype(I16)

        def packed(x):
            return jnp.concatenate([x, x], axis=0).astype(I16)

        def column_total(parts):
            tot = _tree(jnp.add, parts).astype(F32)
            return _all_sublanes(jnp.add, tot[:SUBLANES, :] + tot[SUBLANES:, :])

        def search16(src_scr, need):
            def step(j, ans):
                cand_off = ans | lax.shift_left(jnp.int32(1), 15 - j)
                cand = packed(cand_off - HALF)
                accs = [jnp.zeros((2 * SUBLANES, t), I16) for _ in range(4)]
                n = 0
                for kb in range(nkb):
                    blk = src_scr[kb]
                    for r in range(0, t, 2 * SUBLANES):
                        hit = jnp.where(blk[r:r + 2 * SUBLANES, :] >= cand, ONE16, ZERO16)
                        accs[n % 4] = accs[n % 4] + hit
                        n += 1
                return jnp.where(column_total(accs) >= need, cand_off, ans)
            return lax.fori_loop(0, 16, step, jnp.zeros((SUBLANES, t), I32))

        if nkb * t <= topk:
            thr = jnp.full((SUBLANES, t), INT_MIN, I32)
        else:
            t_hi = search16(hi_scr, kf) - HALF
            t_hi16 = packed(t_hi)
            above = [jnp.zeros((2 * SUBLANES, t), I16) for _ in range(4)]
            n = 0
            for kb in range(nkb):
                for r in range(0, t, 2 * SUBLANES):
                    rows = slice(r, r + 2 * SUBLANES)
                    h = hi_scr[kb, rows, :]
                    lo = ((key_scr[kb, rows, :] & jnp.int32(0xFFFF)) - HALF).astype(I16)
                    lo_scr[kb, rows, :] = jnp.where(h == t_hi16, lo, MIN16)
                    above[n % 4] = above[n % 4] + jnp.where(h > t_hi16, ONE16, ZERO16)
                    n += 1
            t_lo = search16(lo_scr, kf - column_total(above))
            thr = lax.shift_left(t_hi, 16) | t_lo

        n_ge = jnp.zeros((SUBLANES, t), F32)
        for kb in range(nkb):
            blk = key_scr[kb]
            for r, grp in enumerate(_row_groups(blk)):
                rows = slice(r * SUBLANES, (r + 1) * SUBLANES)
                sel = grp >= thr
                n_ge = n_ge + jnp.where(sel, 1.0, 0.0)
                m = jnp.where(sel, 0.0, NEG_BIG)
                if kb == diag:
                    m = jnp.where(causal[rows, :], m, NEG_BIG)
                selm_scr[kb, rows, :] = m
        n_ge = _all_sublanes(jnp.add, n_ge)
        excess = jnp.max(jnp.where(n_ge > kf, 1.0, 0.0))

        @pl.when(excess > 0.0)
        def _():
            thr1 = thr[0:1, :]

            def tied(kb):
                return jnp.where(key_scr[kb] == thr1, 1.0, 0.0)

            def count_tied(kb, acc):
                return acc + jnp.sum(tied(kb), axis=0, keepdims=True)

            n_tied = lax.fori_loop(0, nkb, count_tied, jnp.zeros((1, t), F32))
            need = kf - (n_ge[0:1, :] - n_tied)

            def rewrite(kb, before):
                e = tied(kb)
                rank = _dot(tri_ref[...], e.astype(BF16)) + before
                k = key_scr[kb]
                keep = jnp.where(k > thr1, 0.0,
                                 jnp.where(k == thr1, jnp.where(rank < need, 0.0, NEG_BIG), NEG_BIG))
                selm_scr[kb] = jnp.where(selm_scr[kb] < 0.0, NEG_BIG, keep)
                return before + jnp.sum(e, axis=0, keepdims=True)

            lax.fori_loop(0, nkb, rewrite, jnp.zeros((1, t), F32))

        def head_rows(hd):
            if isinstance(hd, int):
                return slice(hd * HEAD_DIM, (hd + 1) * HEAD_DIM)
            return pl.ds(pl.multiple_of(hd * HEAD_DIM, HEAD_DIM), HEAD_DIM)

        def logits(hd, buf):
            qt = qat_ref[0, head_rows(hd), :]
            maxes = []
            for kb in range(nkb):
                lg = _dot(ka_ref[0, hd, kb * t:(kb + 1) * t, :], qt) + selm_scr[kb]
                if kb >= nkb - 2:
                    lg = lg + bias_ref[hd, diag - kb]
                buf[kb] = lg
                maxes.append(jnp.max(lg, axis=0, keepdims=True))
            return _tree(jnp.maximum, maxes)

        def weighted_values(hd, buf, m):
            if isinstance(hd, int):
                vrows = slice(hd * V_ROWS, (hd + 1) * V_ROWS)
            else:
                vrows = pl.ds(pl.multiple_of(hd * V_ROWS, 16), V_ROWS)
            acc = None
            for kb in range(nkb):
                p = jnp.exp2(buf[kb] - m).astype(BF16)
                pv = _dot(vat_ref[0, vrows, kb * t:(kb + 1) * t], p)
                acc = pv if acc is None else acc + pv
            out_scr[head_rows(hd), :] = acc[:HEAD_DIM, :] * (1.0 / acc[HEAD_DIM:HEAD_DIM + 1, :])

        def head_pair(j, m_even):
            m_odd = logits(2 * j + 1, logit_b)
            weighted_values(2 * j, logit_a, m_even)
            m_next = logits((2 * j + 2) % N_HEADS, logit_a)
            weighted_values(2 * j + 1, logit_b, m_odd)
            return m_next

        lax.fori_loop(0, N_HEADS // 2, head_pair, logits(0, logit_a), unroll=nkb <= HEADS_UNROLL_BLOCKS)

    i = pl.program_id(1)
    for k in range(n_q_tiles):
        pl.when(i == k)(functools.partial(tile_work, first_tile + k + 1))

    o = out_scr[...]
    o = o * lax.rsqrt(jnp.mean(o * o, axis=0, keepdims=True) + EPS) * gt_ref[...]
    o_ref[0] = o.T.astype(BF16)


def _dsa(qit, wit, ki, qat, ka, vat, bias, g_out, bsz, seq, first_tile, n_q_tiles):
    t = DSA_TILE
    nq = first_tile + n_q_tiles
    keys = nq * t
    gw = GROUP_WIDTH
    topk = min(TOPK_MAX, seq // 4)
    q_fm = pl.BlockSpec((1, gw, t), lambda b, i: (b, 0, i + first_tile))
    gt = jnp.broadcast_to(g_out.reshape(gw, 1), (gw, t))
    r = np.arange(t)
    lower = jnp.asarray(r[None, :] < r[:, None], dtype=BF16)
    return pl.pallas_call(
        functools.partial(_dsa_kernel, topk=topk, first_tile=first_tile, n_q_tiles=n_q_tiles),
        grid=(bsz, n_q_tiles),
        in_specs=[q_fm,
                  pl.BlockSpec((1, N_HEADS, t), lambda b, i: (b, 0, i + first_tile)),
                  pl.BlockSpec((1, keys, IDX_DIM), lambda b, i: (b, 0, 0)),
                  q_fm,
                  pl.BlockSpec((1, N_HEADS, keys, HEAD_DIM), lambda b, i: (b, 0, 0, 0)),
                  pl.BlockSpec((1, N_HEADS * V_ROWS, keys), lambda b, i: (b, 0, 0)),
                  _resident(bias.shape),
                  _resident((gw, t)),
                  _resident((t, t))],
        out_specs=pl.BlockSpec((1, t, gw), lambda b, i: (b, i, 0)),
        out_shape=jax.ShapeDtypeStruct((bsz, n_q_tiles * t, gw), BF16),
        scratch_shapes=[pltpu.VMEM((nq, t, t), I32), pltpu.VMEM((nq, t, t), I16),
                        pltpu.VMEM((nq, t, t), I16), pltpu.VMEM((nq, t, t), F32),
                        pltpu.VMEM((nq, t, t), F32), pltpu.VMEM((nq, t, t), F32),
                        pltpu.VMEM((gw, t), F32)],
        compiler_params=_params(2),
        name="dsa_%d" % first_tile,
    )(qit, wit, ki, qat, ka, vat, bias, gt, lower)


def _neg_log2_1m_sigmoid(z2):
    return jnp.maximum(z2, 0.0) + jnp.log2(1.0 + jnp.exp2(-jnp.abs(z2)))


def _split_bf16(x):
    hi = x.astype(BF16)
    lo = (x - hi.astype(F32)).astype(BF16)
    return jnp.concatenate([hi, lo], axis=1)


def _sb_kernel(q_ref, k_ref, v_ref, tri_ref, g_ref, o_ref, z_scr, lh_scr, cs_scr, r_scr, acc_scr):
    t = SB_TILE
    nw = SB_WINDOW
    row = lax.broadcasted_iota(I32, (t, t), 0)
    col = lax.broadcasted_iota(I32, (t, t), 1)
    tiles = []
    for sub in range(SB_PAIR):
        i = pl.program_id(1) * SB_PAIR + sub
        wb = jnp.maximum(i - (nw - 1), 0)
        tiles.append((sub, i, wb, pl.ds(pl.multiple_of(wb * t, t), nw * t)))

    def causal(i, wb, w):
        return (col - row) < (i - wb - w) * t

    def q_rows(sub):
        return slice(sub * t, (sub + 1) * t)

    for sub, i, wb, wrows in tiles:
        for hd in range(N_HEADS):
            p = sub * N_HEADS + hd
            z = _dot_nt(q_ref[0, hd, q_rows(sub), :], k_ref[0, hd, wrows, :])
            z_scr[p] = z
            for w in range(nw):
                m = jnp.where(causal(i, wb, w), _neg_log2_1m_sigmoid(z[:, w * t:(w + 1) * t]), 0.0)
                lh_scr[(p * nw + w) * t:(p * nw + w + 1) * t, :] = _split_bf16(m)
    cs_scr[...] = _dot(lh_scr[...], tri_ref[...])
    for sub, i, wb, wrows in tiles:
        for hd in range(N_HEADS):
            p = sub * N_HEADS + hd
            after = None
            a_blocks = [None] * nw
            for w in reversed(range(nw)):
                cs = cs_scr[(p * nw + w) * t:(p * nw + w + 1) * t, :]
                c = cs[:, :t] if after is None else cs[:, :t] + after
                a = jnp.where(causal(i, wb, w), jnp.exp2(z_scr[p, :, w * t:(w + 1) * t] - c), 0.0)
                a_blocks[w] = a.astype(BF16)
                after = cs[:, t:] if after is None else after + cs[:, t:]
            acc_scr[p] = _dot(jnp.concatenate(a_blocks, axis=1), v_ref[0, hd, wrows, :])
            r_scr[p] = after

    for sub, i, wb, wrows in tiles:
        def far_block(kb, sub=sub):
            rows = pl.ds(pl.multiple_of(kb * t, t), t)
            rmin = None
            for hd in range(N_HEADS):
                p = sub * N_HEADS + hd
                z = _dot_nt(q_ref[0, hd, q_rows(sub), :], k_ref[0, hd, rows, :])
                cs = _dot(_split_bf16(_neg_log2_1m_sigmoid(z)), tri_ref[...])
                r_old = r_scr[p]
                a = jnp.exp2(z - cs[:, :t] - r_old)
                acc_scr[p] += _dot(a.astype(BF16), v_ref[0, hd, rows, :])
                r_new = r_old + cs[:, t:]
                r_scr[p] = r_new
                rmin = r_new if rmin is None else jnp.minimum(rmin, r_new)
            return jnp.min(rmin)

        def cond(c):
            kb, rmin = c
            return jnp.logical_and(kb >= 0, rmin < SB_DEAD)

        def body(c, far_block=far_block):
            kb, _ = c
            return kb - 1, far_block(kb)

        rmin0 = _tree(jnp.minimum, [r_scr[sub * N_HEADS + hd] for hd in range(N_HEADS)])
        lax.while_loop(cond, body, (wb - 1, jnp.min(rmin0)))
        o = jnp.concatenate([acc_scr[sub * N_HEADS + hd] for hd in range(N_HEADS)], axis=1)
        o_ref[0, q_rows(sub), :] = _rms(o, g_ref[...]).astype(BF16)


def _sb(q, k, v, g_out, bsz, seq):
    t = SB_TILE
    tq = SB_PAIR * t
    assert seq >= SB_WINDOW * t and seq % tq == 0
    j = np.arange(t)[:, None]
    s = np.arange(t)[None, :]
    tri = np.concatenate([(j >= s), np.ones((t, t), bool)], axis=1).astype(np.float32)
    tri = np.concatenate([tri, tri], axis=0)
    q_hm = pl.BlockSpec((1, N_HEADS, tq, HEAD_DIM), lambda b, i: (b, 0, i, 0))
    kv_hm = pl.BlockSpec((1, N_HEADS, seq, HEAD_DIM), lambda b, i: (b, 0, 0, 0))
    n_p = SB_PAIR * N_HEADS
    n_rows = n_p * SB_WINDOW * t
    return pl.pallas_call(
        _sb_kernel,
        grid=(bsz, seq // tq),
        in_specs=[q_hm, kv_hm, kv_hm, _resident((2 * t, 2 * t)),
                  pl.BlockSpec((1, GROUP_WIDTH), lambda b, i: (0, 0))],
        out_specs=pl.BlockSpec((1, tq, GROUP_WIDTH), lambda b, i: (b, i, 0)),
        out_shape=jax.ShapeDtypeStruct((bsz, seq, GROUP_WIDTH), BF16),
        scratch_shapes=[pltpu.VMEM((n_p, t, SB_WINDOW * t), F32),
                        pltpu.VMEM((n_rows, 2 * t), BF16),
                        pltpu.VMEM((n_rows, 2 * t), F32),
                        pltpu.VMEM((n_p, t, t), F32),
                        pltpu.VMEM((n_p, t, HEAD_DIM), F32)],
        compiler_params=_params(2),
        name="sb",
    )(q, k, v, jnp.asarray(tri, dtype=BF16), g_out)


def kernel(x, c, w_ada, b_ada, g_pre, g_post, w_ffn_gate, w_ffn_up, w_ffn_down, w_in, g_kidx,
           rel_bias, g_out_a, g_out_b, w_out):
    bsz, seq, d = x.shape
    depth = w_ada.shape[0]
    gw = GROUP_WIDTH
    h = x.reshape(bsz * seq, d)
    bias = _bias_tiles(rel_bias, DSA_TILE)
    for l in range(depth):
        mods = _ada(c, w_ada[l], b_ada[l]).reshape(bsz, -1, d)
        ffn_w = [(w_ffn_gate[l, j].astype(BF16), w_ffn_up[l, j].astype(BF16),
                  w_ffn_down[l, j].astype(BF16)) for j in range(2)]
        w = w_in[l]
        o_ki = 4 * gw
        o_wi = o_ki + IDX_DIM
        o_b = o_wi + N_HEADS
        wt = jnp.concatenate([w[:, :gw], w[:, 2 * gw:3 * gw], w[:, 3 * gw:o_ki]], axis=1).T.astype(BF16)
        wka = w[:, gw:2 * gw].astype(BF16)
        wki = w[:, o_ki:o_wi].astype(BF16)
        wwit = jnp.pad(w[:, o_wi:o_b].T, ((0, 16 - N_HEADS), (0, 0))).astype(BF16)
        wb = w[:, o_b:].astype(BF16)
        woa = w_out[l, :gw].astype(BF16)
        wob = w_out[l, gw:].astype(BF16)
        vec = lambda g: g.reshape(1, -1)

        h1, qat, ka, vat, qit, ki, wit, qb, kb, vb = _ffn(
            h, mods, vec(g_pre[l, 0]), vec(g_post[l, 0]), *ffn_w[0], 0, seq,
            proj=(vec(g_pre[l, 1]), vec(g_kidx[l]), bsz, wt, wka, wki, wwit, wb))
        nt = seq // DSA_TILE
        dsa = functools.partial(_dsa, qit, wit, ki, qat, ka, vat, bias, g_out_a[l], bsz, seq)
        oa_lo, oa_mid, oa_hi = dsa(0, nt // 2), dsa(nt // 2, nt // 4), dsa(3 * nt // 4, nt // 4)
        ob = _sb(qb, kb, vb, vec(g_out_b[l]), bsz, seq)
        h = _ffn(h1, mods, vec(g_pre[l, 2]), vec(g_post[l, 2]), *ffn_w[1], 2, seq,
                 mix=(oa_lo.reshape(bsz * seq // 2, gw), oa_mid.reshape(bsz * seq // 4, gw),
                      oa_hi.reshape(bsz * seq // 4, gw), ob.reshape(bsz * seq, gw), woa, wob,
                      vec(g_post[l, 1])))
    return h.reshape(bsz, seq, d)
```

```python
import functools
import math

import numpy as np
import jax
import jax.numpy as jnp
from jax import lax
from jax.experimental import pallas as pl
from jax.experimental.pallas import tpu as pltpu

F32 = jnp.float32
BF16 = jnp.bfloat16
I32 = jnp.int32
I16 = jnp.int16

EPS = 1e-6
HEAD_DIM = 64
N_HEADS = 8
GROUP_WIDTH = N_HEADS * HEAD_DIM
IDX_DIM = 64
TOPK_MAX = 256
N_BUCKETS = 32
MAX_DISTANCE = 128
SUBLANES = 8
V_ROWS = 80
FFN_CHUNK = 256
TOKEN_TILE = 512
MIX_TOKEN_TILE = 1024
DSA_TILE = 256
SCORE_ROWS = 64
HEADS_UNROLL_BLOCKS = 8
SB_TILE = 128
SB_WINDOW = 3
SB_PAIR = 4
LOG2E = math.log2(math.e)
NEG_BIG = -1e30
SB_DEAD = 104.0 * LOG2E
VMEM_LIMIT = 56 * 1024 * 1024
INT_MIN = np.int32(-2 ** 31)
HALF = 2 ** 15
ONE16, ZERO16, MIN16 = np.int16(1), np.int16(0), np.int16(-HALF)


def _dot(a, b):
    return jnp.dot(a, b, preferred_element_type=F32)


def _dot_nt(a, b):
    return lax.dot_general(a, b, (((1,), (1,)), ((), ())), preferred_element_type=F32)


def _rms(x, g):
    return x * lax.rsqrt(jnp.mean(x * x, axis=-1, keepdims=True) + EPS) * g


def _resident(shape):
    n = len(shape)
    return pl.BlockSpec(shape, lambda *_: (0,) * n, pipeline_mode=pl.Buffered(1))


def _params(n_grid):
    return pltpu.CompilerParams(dimension_semantics=("arbitrary",) * n_grid,
                                vmem_limit_bytes=VMEM_LIMIT)


def _ada_kernel(c_ref, w_ref, b_ref, o_ref):
    c = c_ref[...]
    a = c * jax.nn.sigmoid(c)
    w = w_ref[...]
    a_hi = a.astype(BF16)
    a_lo = (a - a_hi.astype(F32)).astype(BF16)
    w_hi = w.astype(BF16)
    w_lo = (w - w_hi.astype(F32)).astype(BF16)
    o_ref[...] = _dot(a_hi, w_hi) + (_dot(a_hi, w_lo) + _dot(a_lo, w_hi)) + b_ref[...]


def _ada(c, w, b):
    bsz, d = c.shape
    n = w.shape[1]
    tn = 1152 if n % 1152 == 0 else n
    return pl.pallas_call(
        _ada_kernel,
        grid=(n // tn,),
        in_specs=[pl.BlockSpec((bsz, d), lambda j: (0, 0)),
                  pl.BlockSpec((d, tn), lambda j: (0, j)),
                  pl.BlockSpec((1, tn), lambda j: (0, j))],
        out_specs=pl.BlockSpec((bsz, tn), lambda j: (0, j)),
        out_shape=jax.ShapeDtypeStruct((bsz, n), F32),
        compiler_params=_params(1),
        name="ada",
    )(c, w, b.reshape(1, n))


def _ffn_kernel(*refs, sub, mix, proj):
    refs = list(refs)
    take = lambda n: [refs.pop(0) for _ in range(n)]
    (h_ref,) = take(1)
    if mix:
        oa_lo_ref, oa_mid_ref, oa_hi_ref, ob_ref, woa_ref, wob_ref, gpm_ref = take(7)
    mods_ref, gpre_ref, gpost_ref, wg_ref, wu_ref, wd_ref = take(6)
    if proj:
        gpre_next_ref, *proj_in = take(7)
    (o_ref,) = take(1)
    if proj:
        proj_out = take(9)
    n_scr, a_scr = refs
    x = h_ref[...]
    if mix:
        first_half = pl.program_id(0) % 2 == 0
        second = jnp.concatenate([oa_mid_ref[...], oa_hi_ref[...]], axis=0)
        oa = jnp.where(first_half, oa_lo_ref[...], second)
        o = _dot(oa, woa_ref[...]) + _dot(ob_ref[...], wob_ref[...])
        x = x + mods_ref[0, 5:6, :] * _rms(o, gpm_ref[...])
    shift = mods_ref[0, 3 * sub:3 * sub + 1, :]
    scale = mods_ref[0, 3 * sub + 1:3 * sub + 2, :]
    gate = mods_ref[0, 3 * sub + 2:3 * sub + 3, :]
    n = _rms(x, gpre_ref[...]) * (1.0 + scale) + shift
    n_scr[...] = n.astype(BF16)
    fc = FFN_CHUNK
    for c0 in range(0, wg_ref.shape[1], fc):
        g = _dot(n_scr[...], wg_ref[:, c0:c0 + fc])
        u = _dot(n_scr[...], wu_ref[:, c0:c0 + fc])
        a_scr[:, c0:c0 + fc] = (g * jax.nn.sigmoid(g) * u).astype(BF16)
    f = _dot(a_scr[...], wd_ref[...])
    y = x + 0.5 * gate * _rms(f, gpost_ref[...])
    o_ref[...] = y
    if proj:
        shift = mods_ref[0, 3 * (sub + 1):3 * (sub + 1) + 1, :]
        scale = mods_ref[0, 3 * (sub + 1) + 1:3 * (sub + 1) + 2, :]
        n_scr[...] = (_rms(y, gpre_next_ref[...]) * (1.0 + scale) + shift).astype(BF16)
        _project(n_scr, *proj_in, *proj_out)


def _ffn(h, mods, g_pre, g_post, wg, wu, wd, sub, seq, mix=None, proj=None):
    n_tok, d = h.shape
    tm = min(TOKEN_TILE if proj is not None else MIX_TOKEN_TILE, seq)
    tpb = seq // tm
    row = pl.BlockSpec((tm, d), lambda i: (i, 0))
    vec = pl.BlockSpec((1, d), lambda i: (0, 0))
    mod_spec = pl.BlockSpec((1, mods.shape[1], d), lambda i: (i // tpb, 0, 0))
    args, specs = [h], [row]
    out_specs, out_shape = [row], [jax.ShapeDtypeStruct((n_tok, d), F32)]
    if mix is not None:
        oa_lo, oa_mid, oa_hi, ob, woa, wob, g_post_mix = mix
        gw = ob.shape[1]
        assert tpb == 2 and oa_lo.shape[0] * 2 == n_tok and oa_mid.shape[0] * 4 == n_tok
        half = pl.BlockSpec((tm, gw), lambda i: (i, 0))
        per_batch = pl.BlockSpec((tm, gw), lambda i: (i // 2, 0))
        quarter = pl.BlockSpec((tm // 2, gw), lambda i: (i // 2, 0))
        args += [oa_lo, oa_mid, oa_hi, ob, woa, wob, g_post_mix]
        specs += [per_batch, quarter, quarter, half, _resident(woa.shape), _resident(wob.shape), vec]
    args += [mods, g_pre, g_post, wg, wu, wd]
    specs += [mod_spec, vec, vec, _resident(wg.shape), _resident(wu.shape), _resident(wd.shape)]
    if proj is not None:
        g_pre_next, g_kidx, bsz, *weights = proj
        gw = GROUP_WIDTH
        args += [g_pre_next, g_kidx, *weights]
        specs += [vec, pl.BlockSpec((1, IDX_DIM), lambda i: (0, 0))] + [_resident(w.shape) for w in weights]
        hm = pl.BlockSpec((1, N_HEADS, tm, HEAD_DIM), lambda i: (i // tpb, 0, i % tpb, 0))
        hm_shape = jax.ShapeDtypeStruct((bsz, N_HEADS, seq, HEAD_DIM), BF16)
        fm = pl.BlockSpec((1, gw, tm), lambda i: (i // tpb, 0, i % tpb))
        fm_shape = jax.ShapeDtypeStruct((bsz, gw, seq), BF16)
        vfm = pl.BlockSpec((1, N_HEADS * V_ROWS, tm), lambda i: (i // tpb, 0, i % tpb))
        out_specs += [fm, hm, vfm, fm,
                      pl.BlockSpec((1, tm, IDX_DIM), lambda i: (i // tpb, i % tpb, 0)),
                      pl.BlockSpec((1, N_HEADS, tm), lambda i: (i // tpb, 0, i % tpb)),
                      hm, hm, hm]
        out_shape += [fm_shape, hm_shape, jax.ShapeDtypeStruct((bsz, N_HEADS * V_ROWS, seq), BF16),
                      fm_shape,
                      jax.ShapeDtypeStruct((bsz, seq, IDX_DIM), BF16),
                      jax.ShapeDtypeStruct((bsz, N_HEADS, seq), F32),
                      hm_shape, hm_shape, hm_shape]
    out = pl.pallas_call(
        functools.partial(_ffn_kernel, sub=sub, mix=mix is not None, proj=proj is not None),
        grid=(n_tok // tm,),
        in_specs=specs,
        out_specs=out_specs,
        out_shape=out_shape,
        scratch_shapes=[pltpu.VMEM((tm, d), BF16), pltpu.VMEM((tm, wd.shape[0]), BF16)],
        compiler_params=_params(1),
        name="ffn_mix" if mix is not None else "ffn_proj" if proj is not None else "ffn",
    )(*args)
    return out if proj is not None else out[0]


def _project(n_scr, gk_ref, wt_ref, wka_ref, wki_ref, wwit_ref, wb_ref,
             qat_ref, ka_ref, vat_ref, qit_ref, ki_ref, wit_ref, qb_ref, kb_ref, vb_ref):
    gw = GROUP_WIDTH
    qscale = HEAD_DIM ** -0.5

    def heads(w, out_ref, mul):
        p = _dot(n_scr[...], w)
        if mul != 1.0:
            p = p * mul
        for hd in range(N_HEADS):
            out_ref[0, hd] = p[:, hd * HEAD_DIM:(hd + 1) * HEAD_DIM].astype(BF16)

    def feature_major(col, out_ref, mul):
        p = _dot_nt(wt_ref[col * gw:(col + 1) * gw, :], n_scr[...])
        if mul != 1.0:
            p = p * mul
        out_ref[0] = p.astype(BF16)

    feature_major(0, qat_ref, qscale * LOG2E)
    v = _dot_nt(wt_ref[gw:2 * gw, :], n_scr[...])
    pad = V_ROWS - HEAD_DIM
    ones_row = (lax.broadcasted_iota(I32, (pad, v.shape[1]), 0) == 0).astype(BF16)
    for hd in range(N_HEADS):
        vat_ref[0, hd * V_ROWS:hd * V_ROWS + HEAD_DIM, :] = v[hd * HEAD_DIM:(hd + 1) * HEAD_DIM, :].astype(BF16)
        vat_ref[0, hd * V_ROWS + HEAD_DIM:(hd + 1) * V_ROWS, :] = ones_row
    feature_major(2, qit_ref, 1.0)
    heads(wka_ref[...], ka_ref, 1.0)
    heads(wb_ref[:, :gw], qb_ref, qscale * LOG2E)
    heads(wb_ref[:, gw:2 * gw], kb_ref, 1.0)
    heads(wb_ref[:, 2 * gw:], vb_ref, 1.0)
    ki = _dot(n_scr[...], wki_ref[...])
    ki_ref[0] = _rms(ki, gk_ref[...]).astype(BF16)
    wit = _dot_nt(wwit_ref[...], n_scr[...])
    wit_ref[0] = wit[:N_HEADS, :] * ((N_HEADS * IDX_DIM) ** -0.5)


def _t5_bucket_np(n):
    n = np.maximum(n, 0)
    max_exact = N_BUCKETS // 2
    nf = np.maximum(n, 1).astype(np.float32)
    large = max_exact + (np.log(nf / np.float32(max_exact)) / np.float32(math.log(MAX_DISTANCE / max_exact))
                         * np.float32(N_BUCKETS - max_exact)).astype(np.int32)
    large = np.minimum(large, N_BUCKETS - 1)
    return np.where(n < max_exact, n, large).astype(np.int32)


def _bias_kernel(relb_ref, bucket_ref, o_ref):
    hd = pl.program_id(0)
    far = relb_ref[N_BUCKETS - 1, hd]
    for r in range(2):
        bk = bucket_ref[r]
        acc = jnp.zeros(bk.shape, F32)
        for k in range(N_BUCKETS - 1):
            acc = jnp.where(bk == k, (relb_ref[k, hd] - far) * LOG2E, acc)
        o_ref[0, r] = acc


def _bias_tiles(rel_bias, t):
    assert t >= MAX_DISTANCE
    s = np.arange(t)[:, None]
    q = np.arange(t)[None, :]
    buckets = np.stack([_t5_bucket_np(q - s), _t5_bucket_np(t + q - s)]).astype(np.int32)
    return pl.pallas_call(
        _bias_kernel,
        grid=(N_HEADS,),
        in_specs=[pl.BlockSpec(memory_space=pltpu.SMEM),
                  pl.BlockSpec((2, t, t), lambda hd: (0, 0, 0))],
        out_specs=pl.BlockSpec((1, 2, t, t), lambda hd: (hd, 0, 0, 0)),
        out_shape=jax.ShapeDtypeStruct((N_HEADS, 2, t, t), F32),
        compiler_params=_params(1),
        name="bias_tiles",
    )(rel_bias, jnp.asarray(buckets))


def _tree(op, xs):
    xs = list(xs)
    while len(xs) > 1:
        xs = [op(xs[j], xs[j + 1]) if j + 1 < len(xs) else xs[j] for j in range(0, len(xs), 2)]
    return xs[0]


def _row_groups(x):
    return [x[r:r + SUBLANES, :] for r in range(0, x.shape[0], SUBLANES)]


def _all_sublanes(op, x):
    for shift in (4, 2, 1):
        x = op(x, pltpu.roll(x, shift, 0))
    return x


def _dsa_kernel(qit_ref, wit_ref, ki_ref, qat_ref, ka_ref, vat_ref, bias_ref, gt_ref, tri_ref, o_ref,
                key_scr, hi_scr, lo_scr, selm_scr, logit_a, logit_b, out_scr, *, topk, first_tile, n_q_tiles):
    t = DSA_TILE
    kf = float(topk)
    krow = lax.broadcasted_iota(I32, (t, t), 0)
    qcol = lax.broadcasted_iota(I32, (t, t), 1)
    causal = krow <= qcol

    def tile_work(nkb):
        diag = nkb - 1

        wt = wit_ref[0]
        for kb in range(nkb):
            for r0 in range(0, t, SCORE_ROWS):
                kblk = ki_ref[0, kb * t + r0:kb * t + r0 + SCORE_ROWS, :]
                acc = None
                for hd in range(N_HEADS):
                    sc = _dot(kblk, qit_ref[0, hd * IDX_DIM:(hd + 1) * IDX_DIM, :])
                    term = jnp.maximum(sc, 0.0) * wt[hd:hd + 1, :]
                    acc = term if acc is None else acc + term
                acc = acc + 0.0
                if kb == diag:
                    acc = jnp.where(causal[r0:r0 + SCORE_ROWS, :], acc, -jnp.inf)
                bits = pltpu.bitcast(acc, I32)
                key = bits ^ ((bits >> 31) & jnp.int32(0x7FFFFFFF))
                key_scr[kb, r0:r0 + SCORE_ROWS, :] = key
                hi_scr[kb, r0:r0 + SCORE_ROWS, :] = (key >> 16).astype(I16)

        def packed(x):
            return jnp.concatenate([x, x], axis=0).astype(I16)

        def column_total(parts):
            tot = _tree(jnp.add, parts).astype(F32)
            return _all_sublanes(jnp.add, tot[:SUBLANES, :] + tot[SUBLANES:, :])

        def search16(src_scr, need):
            def step(j, ans):
                cand_off = ans | lax.shift_left(jnp.int32(1), 15 - j)
                cand = packed(cand_off - HALF)
                accs = [jnp.zeros((2 * SUBLANES, t), I16) for _ in range(4)]
                n = 0
                for kb in range(nkb):
                    blk = src_scr[kb]
                    for r in range(0, t, 2 * SUBLANES):
                        hit = jnp.where(blk[r:r + 2 * SUBLANES, :] >= cand, ONE16, ZERO16)
                        accs[n % 4] = accs[n % 4] + hit
                        n += 1
                return jnp.where(column_total(accs) >= need, cand_off, ans)
            return lax.fori_loop(0, 16, step, jnp.zeros((SUBLANES, t), I32))

        if nkb * t <= topk:
            thr = jnp.full((SUBLANES, t), INT_MIN, I32)
        else:
            t_hi = search16(hi_scr, kf) - HALF
            t_hi16 = packed(t_hi)
            above = [jnp.zeros((2 * SUBLANES, t), I16) for _ in range(4)]
            n = 0
            for kb in range(nkb):
                for r in range(0, t, 2 * SUBLANES):
                    rows = slice(r, r + 2 * SUBLANES)
                    h = hi_scr[kb, rows, :]
                    lo = ((key_scr[kb, rows, :] & jnp.int32(0xFFFF)) - HALF).astype(I16)
                    lo_scr[kb, rows, :] = jnp.where(h == t_hi16, lo, MIN16)
                    above[n % 4] = above[n % 4] + jnp.where(h > t_hi16, ONE16, ZERO16)
                    n += 1
            t_lo = search16(lo_scr, kf - column_total(above))
            thr = lax.shift_left(t_hi, 16) | t_lo

        n_ge = jnp.zeros((SUBLANES, t), F32)
        for kb in range(nkb):
            blk = key_scr[kb]
            for r, grp in enumerate(_row_groups(blk)):
                rows = slice(r * SUBLANES, (r + 1) * SUBLANES)
                sel = grp >= thr
                n_ge = n_ge + jnp.where(sel, 1.0, 0.0)
                m = jnp.where(sel, 0.0, NEG_BIG)
                if kb == diag:
                    m = jnp.where(causal[rows, :], m, NEG_BIG)
                selm_scr[kb, rows, :] = m
        n_ge = _all_sublanes(jnp.add, n_ge)
        excess = jnp.max(jnp.where(n_ge > kf, 1.0, 0.0))

        @pl.when(excess > 0.0)
        def _():
            thr1 = thr[0:1, :]

            def tied(kb):
                return jnp.where(key_scr[kb] == thr1, 1.0, 0.0)

            def count_tied(kb, acc):
                return acc + jnp.sum(tied(kb), axis=0, keepdims=True)

            n_tied = lax.fori_loop(0, nkb, count_tied, jnp.zeros((1, t), F32))
            need = kf - (n_ge[0:1, :] - n_tied)

            def rewrite(kb, before):
                e = tied(kb)
                rank = _dot(tri_ref[...], e.astype(BF16)) + before
                k = key_scr[kb]
                keep = jnp.where(k > thr1, 0.0,
                                 jnp.where(k == thr1, jnp.where(rank < need, 0.0, NEG_BIG), NEG_BIG))
                selm_scr[kb] = jnp.where(selm_scr[kb] < 0.0, NEG_BIG, keep)
                return before + jnp.sum(e, axis=0, keepdims=True)

            lax.fori_loop(0, nkb, rewrite, jnp.zeros((1, t), F32))

        def head_rows(hd):
            if isinstance(hd, int):
                return slice(hd * HEAD_DIM, (hd + 1) * HEAD_DIM)
            return pl.ds(pl.multiple_of(hd * HEAD_DIM, HEAD_DIM), HEAD_DIM)

        def logits(hd, buf):
            qt = qat_ref[0, head_rows(hd), :]
            maxes = []
            for kb in range(nkb):
                lg = _dot(ka_ref[0, hd, kb * t:(kb + 1) * t, :], qt) + selm_scr[kb]
                if kb >= nkb - 2:
                    lg = lg + bias_ref[hd, diag - kb]
                buf[kb] = lg
                maxes.append(jnp.max(lg, axis=0, keepdims=True))
            return _tree(jnp.maximum, maxes)

        def weighted_values(hd, buf, m):
            if isinstance(hd, int):
                vrows = slice(hd * V_ROWS, (hd + 1) * V_ROWS)
            else:
                vrows = pl.ds(pl.multiple_of(hd * V_ROWS, 16), V_ROWS)
            acc = None
            for kb in range(nkb):
                p = jnp.exp2(buf[kb] - m).astype(BF16)
                pv = _dot(vat_ref[0, vrows, kb * t:(kb + 1) * t], p)
                acc = pv if acc is None else acc + pv
            out_scr[head_rows(hd), :] = acc[:HEAD_DIM, :] * (1.0 / acc[HEAD_DIM:HEAD_DIM + 1, :])

        def head_pair(j, m_even):
            m_odd = logits(2 * j + 1, logit_b)
            weighted_values(2 * j, logit_a, m_even)
            m_next = logits((2 * j + 2) % N_HEADS, logit_a)
            weighted_values(2 * j + 1, logit_b, m_odd)
            return m_next

        lax.fori_loop(0, N_HEADS // 2, head_pair, logits(0, logit_a), unroll=nkb <= HEADS_UNROLL_BLOCKS)

    i = pl.program_id(1)
    for k in range(n_q_tiles):
        pl.when(i == k)(functools.partial(tile_work, first_tile + k + 1))

    o = out_scr[...]
    o = o * lax.rsqrt(jnp.mean(o * o, axis=0, keepdims=True) + EPS) * gt_ref[...]
    o_ref[0] = o.T.astype(BF16)


def _dsa(qit, wit, ki, qat, ka, vat, bias, g_out, bsz, seq, first_tile, n_q_tiles):
    t = DSA_TILE
    nq = first_tile + n_q_tiles
    keys = nq * t
    gw = GROUP_WIDTH
    topk = min(TOPK_MAX, seq // 4)
    q_fm = pl.BlockSpec((1, gw, t), lambda b, i: (b, 0, i + first_tile))
    gt = jnp.broadcast_to(g_out.reshape(gw, 1), (gw, t))
    r = np.arange(t)
    lower = jnp.asarray(r[None, :] < r[:, None], dtype=BF16)
    return pl.pallas_call(
        functools.partial(_dsa_kernel, topk=topk, first_tile=first_tile, n_q_tiles=n_q_tiles),
        grid=(bsz, n_q_tiles),
        in_specs=[q_fm,
                  pl.BlockSpec((1, N_HEADS, t), lambda b, i: (b, 0, i + first_tile)),
                  pl.BlockSpec((1, keys, IDX_DIM), lambda b, i: (b, 0, 0)),
                  q_fm,
                  pl.BlockSpec((1, N_HEADS, keys, HEAD_DIM), lambda b, i: (b, 0, 0, 0)),
                  pl.BlockSpec((1, N_HEADS * V_ROWS, keys), lambda b, i: (b, 0, 0)),
                  _resident(bias.shape),
                  _resident((gw, t)),
                  _resident((t, t))],
        out_specs=pl.BlockSpec((1, t, gw), lambda b, i: (b, i, 0)),
        out_shape=jax.ShapeDtypeStruct((bsz, n_q_tiles * t, gw), BF16),
        scratch_shapes=[pltpu.VMEM((nq, t, t), I32), pltpu.VMEM((nq, t, t), I16),
                        pltpu.VMEM((nq, t, t), I16), pltpu.VMEM((nq, t, t), F32),
                        pltpu.VMEM((nq, t, t), F32), pltpu.VMEM((nq, t, t), F32),
                        pltpu.VMEM((gw, t), F32)],
        compiler_params=_params(2),
        name="dsa_%d" % first_tile,
    )(qit, wit, ki, qat, ka, vat, bias, gt, lower)


def _neg_log2_1m_sigmoid(z2):
    return jnp.maximum(z2, 0.0) + jnp.log2(1.0 + jnp.exp2(-jnp.abs(z2)))


def _split_bf16(x):
    hi = x.astype(BF16)
    lo = (x - hi.astype(F32)).astype(BF16)
    return jnp.concatenate([hi, lo], axis=1)


def _sb_kernel(q_ref, k_ref, v_ref, tri_ref, g_ref, o_ref, z_scr, lh_scr, cs_scr, r_scr, acc_scr):
    t = SB_TILE
    nw = SB_WINDOW
    row = lax.broadcasted_iota(I32, (t, t), 0)
    col = lax.broadcasted_iota(I32, (t, t), 1)
    tiles = []
    for sub in range(SB_PAIR):
        i = pl.program_id(1) * SB_PAIR + sub
        wb = jnp.maximum(i - (nw - 1), 0)
        tiles.append((sub, i, wb, pl.ds(pl.multiple_of(wb * t, t), nw * t)))

    def causal(i, wb, w):
        return (col - row) < (i - wb - w) * t

    def q_rows(sub):
        return slice(sub * t, (sub + 1) * t)

    for sub, i, wb, wrows in tiles:
        for hd in range(N_HEADS):
            p = sub * N_HEADS + hd
            z = _dot_nt(q_ref[0, hd, q_rows(sub), :], k_ref[0, hd, wrows, :])
            z_scr[p] = z
            for w in range(nw):
                m = jnp.where(causal(i, wb, w), _neg_log2_1m_sigmoid(z[:, w * t:(w + 1) * t]), 0.0)
                lh_scr[(p * nw + w) * t:(p * nw + w + 1) * t, :] = _split_bf16(m)
    cs_scr[...] = _dot(lh_scr[...], tri_ref[...])
    for sub, i, wb, wrows in tiles:
        for hd in range(N_HEADS):
            p = sub * N_HEADS + hd
            after = None
            a_blocks = [None] * nw
            for w in reversed(range(nw)):
                cs = cs_scr[(p * nw + w) * t:(p * nw + w + 1) * t, :]
                total = jnp.broadcast_to(cs[:, 0:1], (t, t))
                c = cs if after is None else cs + after
                a = jnp.where(causal(i, wb, w), jnp.exp2(z_scr[p, :, w * t:(w + 1) * t] - c), 0.0)
                a_blocks[w] = a.astype(BF16)
                after = total if after is None else after + total
            acc_scr[p] = _dot(jnp.concatenate(a_blocks, axis=1), v_ref[0, hd, wrows, :])
            r_scr[p] = after

    for sub, i, wb, wrows in tiles:
        def far_block(kb, sub=sub):
            rows = pl.ds(pl.multiple_of(kb * t, t), t)
            rmin = None
            for hd in range(N_HEADS):
                p = sub * N_HEADS + hd
                z = _dot_nt(q_ref[0, hd, q_rows(sub), :], k_ref[0, hd, rows, :])
                cs = _dot(_split_bf16(_neg_log2_1m_sigmoid(z)), tri_ref[...])
                r_old = r_scr[p]
                a = jnp.exp2(z - cs - r_old)
                acc_scr[p] += _dot(a.astype(BF16), v_ref[0, hd, rows, :])
                r_new = r_old + jnp.broadcast_to(cs[:, 0:1], (t, t))
                r_scr[p] = r_new
                rmin = r_new if rmin is None else jnp.minimum(rmin, r_new)
            return jnp.min(rmin)

        def cond(c):
            kb, rmin = c
            return jnp.logical_and(kb >= 0, rmin < SB_DEAD)

        def body(c, far_block=far_block):
            kb, _ = c
            return kb - 1, far_block(kb)

        rmin0 = _tree(jnp.minimum, [r_scr[sub * N_HEADS + hd] for hd in range(N_HEADS)])
        lax.while_loop(cond, body, (wb - 1, jnp.min(rmin0)))
        o = jnp.concatenate([acc_scr[sub * N_HEADS + hd] for hd in range(N_HEADS)], axis=1)
        o_ref[0, q_rows(sub), :] = _rms(o, g_ref[...]).astype(BF16)


def _sb(q, k, v, g_out, bsz, seq):
    t = SB_TILE
    tq = SB_PAIR * t
    assert seq >= SB_WINDOW * t and seq % tq == 0
    j = np.arange(t)[:, None]
    s = np.arange(t)[None, :]
    tri = (j >= s).astype(np.float32)
    tri = np.concatenate([tri, tri], axis=0)
    q_hm = pl.BlockSpec((1, N_HEADS, tq, HEAD_DIM), lambda b, i: (b, 0, i, 0))
    kv_hm = pl.BlockSpec((1, N_HEADS, seq, HEAD_DIM), lambda b, i: (b, 0, 0, 0))
    n_p = SB_PAIR * N_HEADS
    n_rows = n_p * SB_WINDOW * t
    return pl.pallas_call(
        _sb_kernel,
        grid=(bsz, seq // tq),
        in_specs=[q_hm, kv_hm, kv_hm, _resident((2 * t, t)),
                  pl.BlockSpec((1, GROUP_WIDTH), lambda b, i: (0, 0))],
        out_specs=pl.BlockSpec((1, tq, GROUP_WIDTH), lambda b, i: (b, i, 0)),
        out_shape=jax.ShapeDtypeStruct((bsz, seq, GROUP_WIDTH), BF16),
        scratch_shapes=[pltpu.VMEM((n_p, t, SB_WINDOW * t), F32),
                        pltpu.VMEM((n_rows, 2 * t), BF16),
                        pltpu.VMEM((n_rows, t), F32),
                        pltpu.VMEM((n_p, t, t), F32),
                        pltpu.VMEM((n_p, t, HEAD_DIM), F32)],
        compiler_params=_params(2),
        name="sb",
    )(q, k, v, jnp.asarray(tri, dtype=BF16), g_out)


def kernel(x, c, w_ada, b_ada, g_pre, g_post, w_ffn_gate, w_ffn_up, w_ffn_down, w_in, g_kidx,
           rel_bias, g_out_a, g_out_b, w_out):
    bsz, seq, d = x.shape
    depth = w_ada.shape[0]
    gw = GROUP_WIDTH
    h = x.reshape(bsz * seq, d)
    bias = _bias_tiles(rel_bias, DSA_TILE)
    for l in range(depth):
        mods = _ada(c, w_ada[l], b_ada[l]).reshape(bsz, -1, d)
        ffn_w = [(w_ffn_gate[l, j].astype(BF16), w_ffn_up[l, j].astype(BF16),
                  w_ffn_down[l, j].astype(BF16)) for j in range(2)]
        w = w_in[l]
        o_ki = 4 * gw
        o_wi = o_ki + IDX_DIM
        o_b = o_wi + N_HEADS
        wt = jnp.concatenate([w[:, :gw], w[:, 2 * gw:3 * gw], w[:, 3 * gw:o_ki]], axis=1).T.astype(BF16)
        wka = w[:, gw:2 * gw].astype(BF16)
        wki = w[:, o_ki:o_wi].astype(BF16)
        wwit = jnp.pad(w[:, o_wi:o_b].T, ((0, 16 - N_HEADS), (0, 0))).astype(BF16)
        wb = w[:, o_b:].astype(BF16)
        woa = w_out[l, :gw].astype(BF16)
        wob = w_out[l, gw:].astype(BF16)
        vec = lambda g: g.reshape(1, -1)

        h1, qat, ka, vat, qit, ki, wit, qb, kb, vb = _ffn(
            h, mods, vec(g_pre[l, 0]), vec(g_post[l, 0]), *ffn_w[0], 0, seq,
            proj=(vec(g_pre[l, 1]), vec(g_kidx[l]), bsz, wt, wka, wki, wwit, wb))
        nt = seq // DSA_TILE
        dsa = functools.partial(_dsa, qit, wit, ki, qat, ka, vat, bias, g_out_a[l], bsz, seq)
        oa_lo, oa_mid, oa_hi = dsa(0, nt // 2), dsa(nt // 2, nt // 4), dsa(3 * nt // 4, nt // 4)
        ob = _sb(qb, kb, vb, vec(g_out_b[l]), bsz, seq)
        h = _ffn(h1, mods, vec(g_pre[l, 2]), vec(g_post[l, 2]), *ffn_w[1], 2, seq,
                 mix=(oa_lo.reshape(bsz * seq // 2, gw), oa_mid.reshape(bsz * seq // 4, gw),
                      oa_hi.reshape(bsz * seq // 4, gw), ob.reshape(bsz * seq, gw), woa, wob,
                      vec(g_post[l, 1])))
    return h.reshape(bsz, seq, d)
```

```python
import functools
import math

import numpy as np
import jax
import jax.numpy as jnp
from jax import lax
from jax.experimental import pallas as pl
from jax.experimental.pallas import tpu as pltpu

F32 = jnp.float32
BF16 = jnp.bfloat16
I32 = jnp.int32
I16 = jnp.int16

EPS = 1e-6
HEAD_DIM = 64
N_HEADS = 8
GROUP_WIDTH = N_HEADS * HEAD_DIM
IDX_DIM = 64
TOPK_MAX = 256
N_BUCKETS = 32
MAX_DISTANCE = 128
SUBLANES = 8
V_ROWS = 80
FFN_CHUNK = 256
TOKEN_TILE = 512
MIX_TOKEN_TILE = 1024
DSA_TILE = 256
SCORE_ROWS = 64
HEADS_UNROLL_BLOCKS = 8
SB_TILE = 128
SB_WINDOW = 3
SB_PAIR = 4
LOG2E = math.log2(math.e)
NEG_BIG = -1e30
SB_DEAD = 104.0 * LOG2E
VMEM_LIMIT = 56 * 1024 * 1024
INT_MIN = np.int32(-2 ** 31)
HALF = 2 ** 15
ONE16, ZERO16, MIN16 = np.int16(1), np.int16(0), np.int16(-HALF)


def _dot(a, b):
    return jnp.dot(a, b, preferred_element_type=F32)


def _dot_nt(a, b):
    return lax.dot_general(a, b, (((1,), (1,)), ((), ())), preferred_element_type=F32)


def _rms(x, g):
    return x * lax.rsqrt(jnp.mean(x * x, axis=-1, keepdims=True) + EPS) * g


def _resident(shape):
    n = len(shape)
    return pl.BlockSpec(shape, lambda *_: (0,) * n, pipeline_mode=pl.Buffered(1))


def _params(n_grid):
    return pltpu.CompilerParams(dimension_semantics=("arbitrary",) * n_grid,
                                vmem_limit_bytes=VMEM_LIMIT)


def _ada_kernel(c_ref, w_ref, b_ref, o_ref):
    c = c_ref[...]
    a = c * jax.nn.sigmoid(c)
    w = w_ref[...]
    a_hi = a.astype(BF16)
    a_lo = (a - a_hi.astype(F32)).astype(BF16)
    w_hi = w.astype(BF16)
    w_lo = (w - w_hi.astype(F32)).astype(BF16)
    o_ref[...] = _dot(a_hi, w_hi) + (_dot(a_hi, w_lo) + _dot(a_lo, w_hi)) + b_ref[...]


def _ada(c, w, b):
    bsz, d = c.shape
    n = w.shape[1]
    tn = 1152 if n % 1152 == 0 else n
    return pl.pallas_call(
        _ada_kernel,
        grid=(n // tn,),
        in_specs=[pl.BlockSpec((bsz, d), lambda j: (0, 0)),
                  pl.BlockSpec((d, tn), lambda j: (0, j)),
                  pl.BlockSpec((1, tn), lambda j: (0, j))],
        out_specs=pl.BlockSpec((bsz, tn), lambda j: (0, j)),
        out_shape=jax.ShapeDtypeStruct((bsz, n), F32),
        compiler_params=_params(1),
        name="ada",
    )(c, w, b.reshape(1, n))


def _ffn_kernel(*refs, sub, mix, proj):
    refs = list(refs)
    take = lambda n: [refs.pop(0) for _ in range(n)]
    (h_ref,) = take(1)
    if mix:
        oa_lo_ref, oa_mid_ref, oa_hi_ref, ob_ref, woa_ref, wob_ref, gpm_ref = take(7)
    mods_ref, gpre_ref, gpost_ref, wg_ref, wu_ref, wd_ref = take(6)
    if proj:
        gpre_next_ref, *proj_in = take(7)
    (o_ref,) = take(1)
    if proj:
        proj_out = take(9)
    n_scr, a_scr = refs
    x = h_ref[...]
    if mix:
        first_half = pl.program_id(0) % 2 == 0
        second = jnp.concatenate([oa_mid_ref[...], oa_hi_ref[...]], axis=0)
        oa = jnp.where(first_half, oa_lo_ref[...], second)
        o = _dot(oa, woa_ref[...]) + _dot(ob_ref[...], wob_ref[...])
        x = x + mods_ref[0, 5:6, :] * _rms(o, gpm_ref[...])
    shift = mods_ref[0, 3 * sub:3 * sub + 1, :]
    scale = mods_ref[0, 3 * sub + 1:3 * sub + 2, :]
    gate = mods_ref[0, 3 * sub + 2:3 * sub + 3, :]
    n = _rms(x, gpre_ref[...]) * (1.0 + scale) + shift
    n_scr[...] = n.astype(BF16)
    fc = FFN_CHUNK
    for c0 in range(0, wg_ref.shape[1], fc):
        g = _dot(n_scr[...], wg_ref[:, c0:c0 + fc])
        u = _dot(n_scr[...], wu_ref[:, c0:c0 + fc])
        a_scr[:, c0:c0 + fc] = (g * jax.nn.sigmoid(g) * u).astype(BF16)
    f = _dot(a_scr[...], wd_ref[...])
    y = x + 0.5 * gate * _rms(f, gpost_ref[...])
    o_ref[...] = y
    if proj:
        shift = mods_ref[0, 3 * (sub + 1):3 * (sub + 1) + 1, :]
        scale = mods_ref[0, 3 * (sub + 1) + 1:3 * (sub + 1) + 2, :]
        n_scr[...] = (_rms(y, gpre_next_ref[...]) * (1.0 + scale) + shift).astype(BF16)
        _project(n_scr, *proj_in, *proj_out)


def _ffn(h, mods, g_pre, g_post, wg, wu, wd, sub, seq, mix=None, proj=None):
    n_tok, d = h.shape
    tm = min(TOKEN_TILE if proj is not None else MIX_TOKEN_TILE, seq)
    tpb = seq // tm
    row = pl.BlockSpec((tm, d), lambda i: (i, 0))
    vec = pl.BlockSpec((1, d), lambda i: (0, 0))
    mod_spec = pl.BlockSpec((1, mods.shape[1], d), lambda i: (i // tpb, 0, 0))
    args, specs = [h], [row]
    out_specs, out_shape = [row], [jax.ShapeDtypeStruct((n_tok, d), F32)]
    if mix is not None:
        oa_lo, oa_mid, oa_hi, ob, woa, wob, g_post_mix = mix
        gw = ob.shape[1]
        assert tpb == 2 and oa_lo.shape[0] * 2 == n_tok and oa_mid.shape[0] * 4 == n_tok
        half = pl.BlockSpec((tm, gw), lambda i: (i, 0))
        per_batch = pl.BlockSpec((tm, gw), lambda i: (i // 2, 0))
        quarter = pl.BlockSpec((tm // 2, gw), lambda i: (i // 2, 0))
        args += [oa_lo, oa_mid, oa_hi, ob, woa, wob, g_post_mix]
        specs += [per_batch, quarter, quarter, half, _resident(woa.shape), _resident(wob.shape), vec]
    args += [mods, g_pre, g_post, wg, wu, wd]
    specs += [mod_spec, vec, vec, _resident(wg.shape), _resident(wu.shape), _resident(wd.shape)]
    if proj is not None:
        g_pre_next, g_kidx, bsz, *weights = proj
        gw = GROUP_WIDTH
        args += [g_pre_next, g_kidx, *weights]
        specs += [vec, pl.BlockSpec((1, IDX_DIM), lambda i: (0, 0))] + [_resident(w.shape) for w in weights]
        hm = pl.BlockSpec((1, N_HEADS, tm, HEAD_DIM), lambda i: (i // tpb, 0, i % tpb, 0))
        hm_shape = jax.ShapeDtypeStruct((bsz, N_HEADS, seq, HEAD_DIM), BF16)
        fm = pl.BlockSpec((1, gw, tm), lambda i: (i // tpb, 0, i % tpb))
        fm_shape = jax.ShapeDtypeStruct((bsz, gw, seq), BF16)
        vfm = pl.BlockSpec((1, N_HEADS * V_ROWS, tm), lambda i: (i // tpb, 0, i % tpb))
        out_specs += [fm, hm, vfm, fm,
                      pl.BlockSpec((1, tm, IDX_DIM), lambda i: (i // tpb, i % tpb, 0)),
                      pl.BlockSpec((1, N_HEADS, tm), lambda i: (i // tpb, 0, i % tpb)),
                      hm, hm, hm]
        out_shape += [fm_shape, hm_shape, jax.ShapeDtypeStruct((bsz, N_HEADS * V_ROWS, seq), BF16),
                      fm_shape,
                      jax.ShapeDtypeStruct((bsz, seq, IDX_DIM), BF16),
                      jax.ShapeDtypeStruct((bsz, N_HEADS, seq), F32),
                      hm_shape, hm_shape, hm_shape]
    out = pl.pallas_call(
        functools.partial(_ffn_kernel, sub=sub, mix=mix is not None, proj=proj is not None),
        grid=(n_tok // tm,),
        in_specs=specs,
        out_specs=out_specs,
        out_shape=out_shape,
        scratch_shapes=[pltpu.VMEM((tm, d), BF16), pltpu.VMEM((tm, wd.shape[0]), BF16)],
        compiler_params=_params(1),
        name="ffn_mix" if mix is not None else "ffn_proj" if proj is not None else "ffn",
    )(*args)
    return out if proj is not None else out[0]


def _project(n_scr, gk_ref, wt_ref, wka_ref, wki_ref, wwit_ref, wb_ref,
             qat_ref, ka_ref, vat_ref, qit_ref, ki_ref, wit_ref, qb_ref, kb_ref, vb_ref):
    gw = GROUP_WIDTH
    qscale = HEAD_DIM ** -0.5

    def heads(w, out_ref, mul):
        p = _dot(n_scr[...], w)
        if mul != 1.0:
            p = p * mul
        for hd in range(N_HEADS):
            out_ref[0, hd] = p[:, hd * HEAD_DIM:(hd + 1) * HEAD_DIM].astype(BF16)

    def feature_major(col, out_ref, mul):
        p = _dot_nt(wt_ref[col * gw:(col + 1) * gw, :], n_scr[...])
        if mul != 1.0:
            p = p * mul
        out_ref[0] = p.astype(BF16)

    feature_major(0, qat_ref, qscale * LOG2E)
    v = _dot_nt(wt_ref[gw:2 * gw, :], n_scr[...])
    pad = V_ROWS - HEAD_DIM
    ones_row = (lax.broadcasted_iota(I32, (pad, v.shape[1]), 0) == 0).astype(BF16)
    for hd in range(N_HEADS):
        vat_ref[0, hd * V_ROWS:hd * V_ROWS + HEAD_DIM, :] = v[hd * HEAD_DIM:(hd + 1) * HEAD_DIM, :].astype(BF16)
        vat_ref[0, hd * V_ROWS + HEAD_DIM:(hd + 1) * V_ROWS, :] = ones_row
    feature_major(2, qit_ref, 1.0)
    heads(wka_ref[...], ka_ref, 1.0)
    heads(wb_ref[:, :gw], qb_ref, qscale * LOG2E)
    heads(wb_ref[:, gw:2 * gw], kb_ref, 1.0)
    heads(wb_ref[:, 2 * gw:], vb_ref, 1.0)
    ki = _dot(n_scr[...], wki_ref[...])
    ki_ref[0] = _rms(ki, gk_ref[...]).astype(BF16)
    wit = _dot_nt(wwit_ref[...], n_scr[...])
    wit_ref[0] = wit[:N_HEADS, :] * ((N_HEADS * IDX_DIM) ** -0.5)


def _t5_bucket_np(n):
    n = np.maximum(n, 0)
    max_exact = N_BUCKETS // 2
    nf = np.maximum(n, 1).astype(np.float32)
    large = max_exact + (np.log(nf / np.float32(max_exact)) / np.float32(math.log(MAX_DISTANCE / max_exact))
                         * np.float32(N_BUCKETS - max_exact)).astype(np.int32)
    large = np.minimum(large, N_BUCKETS - 1)
    return np.where(n < max_exact, n, large).astype(np.int32)


def _bias_kernel(relb_ref, bucket_ref, o_ref):
    hd = pl.program_id(0)
    far = relb_ref[N_BUCKETS - 1, hd]
    for r in range(2):
        bk = bucket_ref[r]
        acc = jnp.zeros(bk.shape, F32)
        for k in range(N_BUCKETS - 1):
            acc = jnp.where(bk == k, (relb_ref[k, hd] - far) * LOG2E, acc)
        o_ref[0, r] = acc


def _bias_tiles(rel_bias, t):
    assert t >= MAX_DISTANCE
    s = np.arange(t)[:, None]
    q = np.arange(t)[None, :]
    buckets = np.stack([_t5_bucket_np(q - s), _t5_bucket_np(t + q - s)]).astype(np.int32)
    return pl.pallas_call(
        _bias_kernel,
        grid=(N_HEADS,),
        in_specs=[pl.BlockSpec(memory_space=pltpu.SMEM),
                  pl.BlockSpec((2, t, t), lambda hd: (0, 0, 0))],
        out_specs=pl.BlockSpec((1, 2, t, t), lambda hd: (hd, 0, 0, 0)),
        out_shape=jax.ShapeDtypeStruct((N_HEADS, 2, t, t), F32),
        compiler_params=_params(1),
        name="bias_tiles",
    )(rel_bias, jnp.asarray(buckets))


def _tree(op, xs):
    xs = list(xs)
    while len(xs) > 1:
        xs = [op(xs[j], xs[j + 1]) if j + 1 < len(xs) else xs[j] for j in range(0, len(xs), 2)]
    return xs[0]


def _row_groups(x):
    return [x[r:r + SUBLANES, :] for r in range(0, x.shape[0], SUBLANES)]


def _all_sublanes(op, x):
    for shift in (4, 2, 1):
        x = op(x, pltpu.roll(x, shift, 0))
    return x


def _dsa_kernel(qit_ref, wit_ref, ki_ref, qat_ref, ka_ref, vat_ref, bias_ref, gt_ref, tri_ref, o_ref,
                key_scr, hi_scr, lo_scr, selm_scr, logit_a, logit_b, out_scr, *, topk, first_tile, n_q_tiles):
    t = DSA_TILE
    kf = float(topk)
    krow = lax.broadcasted_iota(I32, (t, t), 0)
    qcol = lax.broadcasted_iota(I32, (t, t), 1)
    causal = krow <= qcol

    def tile_work(nkb):
        diag = nkb - 1

        wt = wit_ref[0]
        for kb in range(nkb):
            for r0 in range(0, t, SCORE_ROWS):
                kblk = ki_ref[0, kb * t + r0:kb * t + r0 + SCORE_ROWS, :]
                acc = None
                for hd in range(N_HEADS):
                    sc = _dot(kblk, qit_ref[0, hd * IDX_DIM:(hd + 1) * IDX_DIM, :])
                    term = jnp.maximum(sc, 0.0) * wt[hd:hd + 1, :]
                    acc = term if acc is None else acc + term
                acc = acc + 0.0
                if kb == diag:
                    acc = jnp.where(causal[r0:r0 + SCORE_ROWS, :], acc, -jnp.inf)
                bits = pltpu.bitcast(acc, I32)
                key = bits ^ ((bits >> 31) & jnp.int32(0x7FFFFFFF))
                key_scr[kb, r0:r0 + SCORE_ROWS, :] = key
                hi_scr[kb, r0:r0 + SCORE_ROWS, :] = (key >> 16).astype(I16)

        def packed(x):
            return jnp.concatenate([x, x], axis=0).astype(I16)

        def column_total(parts):
            tot = _tree(jnp.add, parts).astype(F32)
            return _all_sublanes(jnp.add, tot[:SUBLANES, :] + tot[SUBLANES:, :])

        def search16(src_scr, need):
            def step(j, ans):
                cand_off = ans | lax.shift_left(jnp.int32(1), 15 - j)
                cand = packed(cand_off - HALF)
                accs = [jnp.zeros((2 * SUBLANES, t), I16) for _ in range(4)]
                n = 0
                for kb in range(nkb):
                    blk = src_scr[kb]
                    for r in range(0, t, 2 * SUBLANES):
                        hit = jnp.where(blk[r:r + 2 * SUBLANES, :] >= cand, ONE16, ZERO16)
                        accs[n % 4] = accs[n % 4] + hit
                        n += 1
                return jnp.where(column_total(accs) >= need, cand_off, ans)
            return lax.fori_loop(0, 16, step, jnp.zeros((SUBLANES, t), I32))

        if nkb * t <= topk:
            thr = jnp.full((SUBLANES, t), INT_MIN, I32)
        else:
            t_hi = search16(hi_scr, kf) - HALF
            t_hi16 = packed(t_hi)
            above = [jnp.zeros((2 * SUBLANES, t), I16) for _ in range(4)]
            n = 0
            for kb in range(nkb):
                for r in range(0, t, 2 * SUBLANES):
                    rows = slice(r, r + 2 * SUBLANES)
                    h = hi_scr[kb, rows, :]
                    lo = ((key_scr[kb, rows, :] & jnp.int32(0xFFFF)) - HALF).astype(I16)
                    lo_scr[kb, rows, :] = jnp.where(h == t_hi16, lo, MIN16)
                    above[n % 4] = above[n % 4] + jnp.where(h > t_hi16, ONE16, ZERO16)
                    n += 1
            t_lo = search16(lo_scr, kf - column_total(above))
            thr = lax.shift_left(t_hi, 16) | t_lo

        n_ge = jnp.zeros((SUBLANES, t), F32)
        for kb in range(nkb):
            blk = key_scr[kb]
            for r, grp in enumerate(_row_groups(blk)):
                rows = slice(r * SUBLANES, (r + 1) * SUBLANES)
                sel = grp >= thr
                n_ge = n_ge + jnp.where(sel, 1.0, 0.0)
                m = jnp.where(sel, 0.0, NEG_BIG)
                if kb == diag:
                    m = jnp.where(causal[rows, :], m, NEG_BIG)
                selm_scr[kb, rows, :] = m
        n_ge = _all_sublanes(jnp.add, n_ge)
        excess = jnp.max(jnp.where(n_ge > kf, 1.0, 0.0))

        @pl.when(excess > 0.0)
        def _():
            thr1 = thr[0:1, :]

            def tied(kb):
                return jnp.where(key_scr[kb] == thr1, 1.0, 0.0)

            def count_tied(kb, acc):
                return acc + jnp.sum(tied(kb), axis=0, keepdims=True)

            n_tied = lax.fori_loop(0, nkb, count_tied, jnp.zeros((1, t), F32))
            need = kf - (n_ge[0:1, :] - n_tied)

            def rewrite(kb, before):
                is_tied = key_scr[kb] == thr1
                e = jnp.where(is_tied, 1.0, 0.0)
                rank = _dot(tri_ref[...], e.astype(BF16)) + before
                surplus = jnp.where(is_tied, rank - need, -1.0) >= 0.0
                selm_scr[kb] = jnp.where(surplus, NEG_BIG, selm_scr[kb])
                return before + jnp.sum(e, axis=0, keepdims=True)

            lax.fori_loop(0, nkb, rewrite, jnp.zeros((1, t), F32))

        def head_rows(hd):
            if isinstance(hd, int):
                return slice(hd * HEAD_DIM, (hd + 1) * HEAD_DIM)
            return pl.ds(pl.multiple_of(hd * HEAD_DIM, HEAD_DIM), HEAD_DIM)

        def logits(hd, buf):
            qt = qat_ref[0, head_rows(hd), :]
            maxes = []
            for kb in range(nkb):
                lg = _dot(ka_ref[0, hd, kb * t:(kb + 1) * t, :], qt) + selm_scr[kb]
                if kb >= nkb - 2:
                    lg = lg + bias_ref[hd, diag - kb]
                buf[kb] = lg
                maxes.append(jnp.max(lg, axis=0, keepdims=True))
            return _tree(jnp.maximum, maxes)

        def weighted_values(hd, buf, m):
            if isinstance(hd, int):
                vrows = slice(hd * V_ROWS, (hd + 1) * V_ROWS)
            else:
                vrows = pl.ds(pl.multiple_of(hd * V_ROWS, 16), V_ROWS)
            acc = None
            for kb in range(nkb):
                p = jnp.exp2(buf[kb] - m).astype(BF16)
                pv = _dot(vat_ref[0, vrows, kb * t:(kb + 1) * t], p)
                acc = pv if acc is None else acc + pv
            out_scr[head_rows(hd), :] = acc[:HEAD_DIM, :] * (1.0 / acc[HEAD_DIM:HEAD_DIM + 1, :])

        def head_pair(j, m_even):
            m_odd = logits(2 * j + 1, logit_b)
            weighted_values(2 * j, logit_a, m_even)
            m_next = logits((2 * j + 2) % N_HEADS, logit_a)
            weighted_values(2 * j + 1, logit_b, m_odd)
            return m_next

        lax.fori_loop(0, N_HEADS // 2, head_pair, logits(0, logit_a), unroll=nkb <= HEADS_UNROLL_BLOCKS)

    i = pl.program_id(1)
    for k in range(n_q_tiles):
        pl.when(i == k)(functools.partial(tile_work, first_tile + k + 1))

    o = out_scr[...]
    o = o * lax.rsqrt(jnp.mean(o * o, axis=0, keepdims=True) + EPS) * gt_ref[...]
    o_ref[0] = o.T.astype(BF16)


def _dsa(qit, wit, ki, qat, ka, vat, bias, g_out, bsz, seq, first_tile, n_q_tiles):
    t = DSA_TILE
    nq = first_tile + n_q_tiles
    keys = nq * t
    gw = GROUP_WIDTH
    topk = min(TOPK_MAX, seq // 4)
    q_fm = pl.BlockSpec((1, gw, t), lambda b, i: (b, 0, i + first_tile))
    gt = jnp.broadcast_to(g_out.reshape(gw, 1), (gw, t))
    r = np.arange(t)
    lower = jnp.asarray(r[None, :] < r[:, None], dtype=BF16)
    return pl.pallas_call(
        functools.partial(_dsa_kernel, topk=topk, first_tile=first_tile, n_q_tiles=n_q_tiles),
        grid=(bsz, n_q_tiles),
        in_specs=[q_fm,
                  pl.BlockSpec((1, N_HEADS, t), lambda b, i: (b, 0, i + first_tile)),
                  pl.BlockSpec((1, keys, IDX_DIM), lambda b, i: (b, 0, 0)),
                  q_fm,
                  pl.BlockSpec((1, N_HEADS, keys, HEAD_DIM), lambda b, i: (b, 0, 0, 0)),
                  pl.BlockSpec((1, N_HEADS * V_ROWS, keys), lambda b, i: (b, 0, 0)),
                  _resident(bias.shape),
                  _resident((gw, t)),
                  _resident((t, t))],
        out_specs=pl.BlockSpec((1, t, gw), lambda b, i: (b, i, 0)),
        out_shape=jax.ShapeDtypeStruct((bsz, n_q_tiles * t, gw), BF16),
        scratch_shapes=[pltpu.VMEM((nq, t, t), I32), pltpu.VMEM((nq, t, t), I16),
                        pltpu.VMEM((nq, t, t), I16), pltpu.VMEM((nq, t, t), F32),
                        pltpu.VMEM((nq, t, t), F32), pltpu.VMEM((nq, t, t), F32),
                        pltpu.VMEM((gw, t), F32)],
        compiler_params=_params(2),
        name="dsa_%d" % first_tile,
    )(qit, wit, ki, qat, ka, vat, bias, gt, lower)


def _neg_log2_1m_sigmoid(z2):
    return jnp.maximum(z2, 0.0) + jnp.log2(1.0 + jnp.exp2(-jnp.abs(z2)))


def _split_bf16(x):
    hi = x.astype(BF16)
    lo = (x - hi.astype(F32)).astype(BF16)
    return jnp.concatenate([hi, lo], axis=1)


def _sb_kernel(q_ref, k_ref, v_ref, tri_ref, g_ref, o_ref, z_scr, lh_scr, cs_scr, r_scr, acc_scr):
    t = SB_TILE
    nw = SB_WINDOW
    row = lax.broadcasted_iota(I32, (t, t), 0)
    col = lax.broadcasted_iota(I32, (t, t), 1)
    tiles = []
    for sub in range(SB_PAIR):
        i = pl.program_id(1) * SB_PAIR + sub
        wb = jnp.maximum(i - (nw - 1), 0)
        tiles.append((sub, i, wb, pl.ds(pl.multiple_of(wb * t, t), nw * t)))

    def causal(i, wb, w):
        return (col - row) < (i - wb - w) * t

    def q_rows(sub):
        return slice(sub * t, (sub + 1) * t)

    for sub, i, wb, wrows in tiles:
        for hd in range(N_HEADS):
            p = sub * N_HEADS + hd
            z = _dot_nt(q_ref[0, hd, q_rows(sub), :], k_ref[0, hd, wrows, :])
            z_scr[p] = z
            for w in range(nw):
                m = jnp.where(causal(i, wb, w), _neg_log2_1m_sigmoid(z[:, w * t:(w + 1) * t]), 0.0)
                lh_scr[(p * nw + w) * t:(p * nw + w + 1) * t, :] = _split_bf16(m)
    cs_scr[...] = _dot(lh_scr[...], tri_ref[...])
    window_rmin = {}
    for sub, i, wb, wrows in tiles:
        for hd in range(N_HEADS):
            p = sub * N_HEADS + hd
            after = None
            a_blocks = [None] * nw
            for w in reversed(range(nw)):
                cs = cs_scr[(p * nw + w) * t:(p * nw + w + 1) * t, :]
                total = jnp.broadcast_to(cs[:, 0:1], (t, t))
                c = cs if after is None else cs + after
                a = jnp.where(causal(i, wb, w), jnp.exp2(z_scr[p, :, w * t:(w + 1) * t] - c), 0.0)
                a_blocks[w] = a.astype(BF16)
                after = total if after is None else after + total
            acc_scr[p] = _dot(jnp.concatenate(a_blocks, axis=1), v_ref[0, hd, wrows, :])
            r_scr[p] = after
            window_rmin[sub] = after if hd == 0 else jnp.minimum(window_rmin[sub], after)

    for sub, i, wb, wrows in tiles:
        def far_block(kb, sub=sub):
            rows = pl.ds(pl.multiple_of(kb * t, t), t)
            rmin = None
            for hd in range(N_HEADS):
                p = sub * N_HEADS + hd
                z = _dot_nt(q_ref[0, hd, q_rows(sub), :], k_ref[0, hd, rows, :])
                cs = _dot(_split_bf16(_neg_log2_1m_sigmoid(z)), tri_ref[...])
                r_old = r_scr[p]
                a = jnp.exp2(z - cs - r_old)
                acc_scr[p] += _dot(a.astype(BF16), v_ref[0, hd, rows, :])
                r_new = r_old + jnp.broadcast_to(cs[:, 0:1], (t, t))
                r_scr[p] = r_new
                rmin = r_new if rmin is None else jnp.minimum(rmin, r_new)
            return jnp.min(rmin)

        def cond(c):
            kb, rmin = c
            return jnp.logical_and(kb >= 0, rmin < SB_DEAD)

        def body(c, far_block=far_block):
            kb, _ = c
            return kb - 1, far_block(kb)

        lax.while_loop(cond, body, (wb - 1, jnp.min(window_rmin[sub])))
        o = jnp.concatenate([acc_scr[sub * N_HEADS + hd] for hd in range(N_HEADS)], axis=1)
        o_ref[0, q_rows(sub), :] = _rms(o, g_ref[...]).astype(BF16)


def _sb(q, k, v, g_out, bsz, seq):
    t = SB_TILE
    tq = SB_PAIR * t
    assert seq >= SB_WINDOW * t and seq % tq == 0
    j = np.arange(t)[:, None]
    s = np.arange(t)[None, :]
    tri = (j >= s).astype(np.float32)
    tri = np.concatenate([tri, tri], axis=0)
    q_hm = pl.BlockSpec((1, N_HEADS, tq, HEAD_DIM), lambda b, i: (b, 0, i, 0))
    kv_hm = pl.BlockSpec((1, N_HEADS, seq, HEAD_DIM), lambda b, i: (b, 0, 0, 0))
    n_p = SB_PAIR * N_HEADS
    n_rows = n_p * SB_WINDOW * t
    return pl.pallas_call(
        _sb_kernel,
        grid=(bsz, seq // tq),
        in_specs=[q_hm, kv_hm, kv_hm, _resident((2 * t, t)),
                  pl.BlockSpec((1, GROUP_WIDTH), lambda b, i: (0, 0))],
        out_specs=pl.BlockSpec((1, tq, GROUP_WIDTH), lambda b, i: (b, i, 0)),
        out_shape=jax.ShapeDtypeStruct((bsz, seq, GROUP_WIDTH), BF16),
        scratch_shapes=[pltpu.VMEM((n_p, t, SB_WINDOW * t), F32),
                        pltpu.VMEM((n_rows, 2 * t), BF16),
                        pltpu.VMEM((n_rows, t), F32),
                        pltpu.VMEM((n_p, t, t), F32),
                        pltpu.VMEM((n_p, t, HEAD_DIM), F32)],
        compiler_params=_params(2),
        name="sb",
    )(q, k, v, jnp.asarray(tri, dtype=BF16), g_out)


def kernel(x, c, w_ada, b_ada, g_pre, g_post, w_ffn_gate, w_ffn_up, w_ffn_down, w_in, g_kidx,
           rel_bias, g_out_a, g_out_b, w_out):
    bsz, seq, d = x.shape
    depth = w_ada.shape[0]
    gw = GROUP_WIDTH
    h = x.reshape(bsz * seq, d)
    bias = _bias_tiles(rel_bias, DSA_TILE)
    for l in range(depth):
        mods = _ada(c, w_ada[l], b_ada[l]).reshape(bsz, -1, d)
        ffn_w = [(w_ffn_gate[l, j].astype(BF16), w_ffn_up[l, j].astype(BF16),
                  w_ffn_down[l, j].astype(BF16)) for j in range(2)]
        w = w_in[l]
        o_ki = 4 * gw
        o_wi = o_ki + IDX_DIM
        o_b = o_wi + N_HEADS
        wt = jnp.concatenate([w[:, :gw], w[:, 2 * gw:3 * gw], w[:, 3 * gw:o_ki]], axis=1).T.astype(BF16)
        wka = w[:, gw:2 * gw].astype(BF16)
        wki = w[:, o_ki:o_wi].astype(BF16)
        wwit = jnp.pad(w[:, o_wi:o_b].T, ((0, 16 - N_HEADS), (0, 0))).astype(BF16)
        wb = w[:, o_b:].astype(BF16)
        woa = w_out[l, :gw].astype(BF16)
        wob = w_out[l, gw:].astype(BF16)
        vec = lambda g: g.reshape(1, -1)

        h1, qat, ka, vat, qit, ki, wit, qb, kb, vb = _ffn(
            h, mods, vec(g_pre[l, 0]), vec(g_post[l, 0]), *ffn_w[0], 0, seq,
            proj=(vec(g_pre[l, 1]), vec(g_kidx[l]), bsz, wt, wka, wki, wwit, wb))
        nt = seq // DSA_TILE
        dsa = functools.partial(_dsa, qit, wit, ki, qat, ka, vat, bias, g_out_a[l], bsz, seq)
        oa_lo, oa_mid, oa_hi = dsa(0, nt // 2), dsa(nt // 2, nt // 4), dsa(3 * nt // 4, nt // 4)
        ob = _sb(qb, kb, vb, vec(g_out_b[l]), bsz, seq)
        h = _ffn(h1, mods, vec(g_pre[l, 2]), vec(g_post[l, 2]), *ffn_w[1], 2, seq,
                 mix=(oa_lo.reshape(bsz * seq // 2, gw), oa_mid.reshape(bsz * seq // 4, gw),
                      oa_hi.reshape(bsz * seq // 4, gw), ob.reshape(bsz * seq, gw), woa, wob,
                      vec(g_post[l, 1])))
    return h.reshape(bsz, seq, d)
```

```python
import functools
import math

import numpy as np
import jax
import jax.numpy as jnp
from jax import lax
from jax.experimental import pallas as pl
from jax.experimental.pallas import tpu as pltpu

F32 = jnp.float32
BF16 = jnp.bfloat16
I32 = jnp.int32
I16 = jnp.int16

EPS = 1e-6
HEAD_DIM = 64
N_HEADS = 8
GROUP_WIDTH = N_HEADS * HEAD_DIM
IDX_DIM = 64
TOPK_MAX = 256
N_BUCKETS = 32
MAX_DISTANCE = 128
SUBLANES = 8
V_ROWS = 80
FFN_CHUNK = 256
TOKEN_TILE = 512
MIX_TOKEN_TILE = 1024
DSA_TILE = 256
SCORE_ROWS = 64
HEADS_UNROLL_BLOCKS = 8
SB_TILE = 128
SB_WINDOW = 3
SB_PAIR = 4
LOG2E = math.log2(math.e)
NEG_BIG = -1e30
SB_DEAD = 104.0 * LOG2E
VMEM_LIMIT = 56 * 1024 * 1024
INT_MIN = np.int32(-2 ** 31)
HALF = 2 ** 15
ONE16, ZERO16, MIN16 = np.int16(1), np.int16(0), np.int16(-HALF)


def _dot(a, b):
    return jnp.dot(a, b, preferred_element_type=F32)


def _dot_nt(a, b):
    return lax.dot_general(a, b, (((1,), (1,)), ((), ())), preferred_element_type=F32)


def _rms(x, g):
    return x * lax.rsqrt(jnp.mean(x * x, axis=-1, keepdims=True) + EPS) * g


def _resident(shape):
    n = len(shape)
    return pl.BlockSpec(shape, lambda *_: (0,) * n, pipeline_mode=pl.Buffered(1))


def _params(n_grid):
    return pltpu.CompilerParams(dimension_semantics=("arbitrary",) * n_grid,
                                vmem_limit_bytes=VMEM_LIMIT)


def _ada_kernel(c_ref, w_ref, b_ref, o_ref):
    c = c_ref[...]
    a = c * jax.nn.sigmoid(c)
    w = w_ref[...]
    a_hi = a.astype(BF16)
    a_lo = (a - a_hi.astype(F32)).astype(BF16)
    w_hi = w.astype(BF16)
    w_lo = (w - w_hi.astype(F32)).astype(BF16)
    o_ref[...] = _dot(a_hi, w_hi) + (_dot(a_hi, w_lo) + _dot(a_lo, w_hi)) + b_ref[...]


def _ada(c, w, b):
    bsz, d = c.shape
    n = w.shape[1]
    tn = 1152 if n % 1152 == 0 else n
    return pl.pallas_call(
        _ada_kernel,
        grid=(n // tn,),
        in_specs=[pl.BlockSpec((bsz, d), lambda j: (0, 0)),
                  pl.BlockSpec((d, tn), lambda j: (0, j)),
                  pl.BlockSpec((1, tn), lambda j: (0, j))],
        out_specs=pl.BlockSpec((bsz, tn), lambda j: (0, j)),
        out_shape=jax.ShapeDtypeStruct((bsz, n), F32),
        compiler_params=_params(1),
        name="ada",
    )(c, w, b.reshape(1, n))


def _ffn_kernel(*refs, sub, mix, proj):
    refs = list(refs)
    take = lambda n: [refs.pop(0) for _ in range(n)]
    (h_ref,) = take(1)
    if mix:
        oa_lo_ref, oa_mid_ref, oa_hi_ref, ob_ref, woa_ref, wob_ref, gpm_ref = take(7)
    mods_ref, gpre_ref, gpost_ref, wg_ref, wu_ref, wd_ref = take(6)
    if proj:
        gpre_next_ref, *proj_in = take(7)
    (o_ref,) = take(1)
    if proj:
        proj_out = take(9)
    n_scr, a_scr = refs
    x = h_ref[...]
    if mix:
        first_half = pl.program_id(0) % 2 == 0
        second = jnp.concatenate([oa_mid_ref[...], oa_hi_ref[...]], axis=0)
        oa = jnp.where(first_half, oa_lo_ref[...], second)
        o = _dot(oa, woa_ref[...]) + _dot(ob_ref[...], wob_ref[...])
        x = x + mods_ref[0, 5:6, :] * _rms(o, gpm_ref[...])
    shift = mods_ref[0, 3 * sub:3 * sub + 1, :]
    scale = mods_ref[0, 3 * sub + 1:3 * sub + 2, :]
    gate = mods_ref[0, 3 * sub + 2:3 * sub + 3, :]
    n = _rms(x, gpre_ref[...]) * (1.0 + scale) + shift
    n_scr[...] = n.astype(BF16)
    fc = FFN_CHUNK
    for c0 in range(0, wg_ref.shape[1], fc):
        g = _dot(n_scr[...], wg_ref[:, c0:c0 + fc])
        u = _dot(n_scr[...], wu_ref[:, c0:c0 + fc])
        a_scr[:, c0:c0 + fc] = (g * jax.nn.sigmoid(g) * u).astype(BF16)
    f = _dot(a_scr[...], wd_ref[...])
    y = x + 0.5 * gate * _rms(f, gpost_ref[...])
    o_ref[...] = y
    if proj:
        shift = mods_ref[0, 3 * (sub + 1):3 * (sub + 1) + 1, :]
        scale = mods_ref[0, 3 * (sub + 1) + 1:3 * (sub + 1) + 2, :]
        n_scr[...] = (_rms(y, gpre_next_ref[...]) * (1.0 + scale) + shift).astype(BF16)
        _project(n_scr, *proj_in, *proj_out)


def _ffn(h, mods, g_pre, g_post, wg, wu, wd, sub, seq, mix=None, proj=None):
    n_tok, d = h.shape
    tm = min(TOKEN_TILE if proj is not None else MIX_TOKEN_TILE, seq)
    tpb = seq // tm
    row = pl.BlockSpec((tm, d), lambda i: (i, 0))
    vec = pl.BlockSpec((1, d), lambda i: (0, 0))
    mod_spec = pl.BlockSpec((1, mods.shape[1], d), lambda i: (i // tpb, 0, 0))
    args, specs = [h], [row]
    out_specs, out_shape = [row], [jax.ShapeDtypeStruct((n_tok, d), F32)]
    if mix is not None:
        oa_lo, oa_mid, oa_hi, ob, woa, wob, g_post_mix = mix
        gw = ob.shape[1]
        assert tpb == 2 and oa_lo.shape[0] * 2 == n_tok and oa_mid.shape[0] * 4 == n_tok
        half = pl.BlockSpec((tm, gw), lambda i: (i, 0))
        per_batch = pl.BlockSpec((tm, gw), lambda i: (i // 2, 0))
        quarter = pl.BlockSpec((tm // 2, gw), lambda i: (i // 2, 0))
        args += [oa_lo, oa_mid, oa_hi, ob, woa, wob, g_post_mix]
        specs += [per_batch, quarter, quarter, half, _resident(woa.shape), _resident(wob.shape), vec]
    args += [mods, g_pre, g_post, wg, wu, wd]
    specs += [mod_spec, vec, vec, _resident(wg.shape), _resident(wu.shape), _resident(wd.shape)]
    if proj is not None:
        g_pre_next, g_kidx, bsz, *weights = proj
        gw = GROUP_WIDTH
        args += [g_pre_next, g_kidx, *weights]
        specs += [vec, pl.BlockSpec((1, IDX_DIM), lambda i: (0, 0))] + [_resident(w.shape) for w in weights]
        hm = pl.BlockSpec((1, N_HEADS, tm, HEAD_DIM), lambda i: (i // tpb, 0, i % tpb, 0))
        hm_shape = jax.ShapeDtypeStruct((bsz, N_HEADS, seq, HEAD_DIM), BF16)
        fm = pl.BlockSpec((1, gw, tm), lambda i: (i // tpb, 0, i % tpb))
        fm_shape = jax.ShapeDtypeStruct((bsz, gw, seq), BF16)
        vfm = pl.BlockSpec((1, N_HEADS * V_ROWS, tm), lambda i: (i // tpb, 0, i % tpb))
        out_specs += [fm, hm, vfm, fm,
                      pl.BlockSpec((1, tm, IDX_DIM), lambda i: (i // tpb, i % tpb, 0)),
                      pl.BlockSpec((1, N_HEADS, tm), lambda i: (i // tpb, 0, i % tpb)),
                      hm, hm, hm]
        out_shape += [fm_shape, hm_shape, jax.ShapeDtypeStruct((bsz, N_HEADS * V_ROWS, seq), BF16),
                      fm_shape,
                      jax.ShapeDtypeStruct((bsz, seq, IDX_DIM), BF16),
                      jax.ShapeDtypeStruct((bsz, N_HEADS, seq), F32),
                      hm_shape, hm_shape, hm_shape]
    out = pl.pallas_call(
        functools.partial(_ffn_kernel, sub=sub, mix=mix is not None, proj=proj is not None),
        grid=(n_tok // tm,),
        in_specs=specs,
        out_specs=out_specs,
        out_shape=out_shape,
        scratch_shapes=[pltpu.VMEM((tm, d), BF16), pltpu.VMEM((tm, wd.shape[0]), BF16)],
        compiler_params=_params(1),
        name="ffn_mix" if mix is not None else "ffn_proj" if proj is not None else "ffn",
    )(*args)
    return out if proj is not None else out[0]


def _project(n_scr, gk_ref, wt_ref, wka_ref, wki_ref, wwit_ref, wb_ref,
             qat_ref, ka_ref, vat_ref, qit_ref, ki_ref, wit_ref, qb_ref, kb_ref, vb_ref):
    gw = GROUP_WIDTH
    qscale = HEAD_DIM ** -0.5

    def heads(w, out_ref, mul):
        p = _dot(n_scr[...], w)
        if mul != 1.0:
            p = p * mul
        for hd in range(N_HEADS):
            out_ref[0, hd] = p[:, hd * HEAD_DIM:(hd + 1) * HEAD_DIM].astype(BF16)

    def feature_major(col, out_ref, mul):
        p = _dot_nt(wt_ref[col * gw:(col + 1) * gw, :], n_scr[...])
        if mul != 1.0:
            p = p * mul
        out_ref[0] = p.astype(BF16)

    feature_major(0, qat_ref, qscale * LOG2E)
    v = _dot_nt(wt_ref[gw:2 * gw, :], n_scr[...])
    pad = V_ROWS - HEAD_DIM
    ones_row = (lax.broadcasted_iota(I32, (pad, v.shape[1]), 0) == 0).astype(BF16)
    for hd in range(N_HEADS):
        vat_ref[0, hd * V_ROWS:hd * V_ROWS + HEAD_DIM, :] = v[hd * HEAD_DIM:(hd + 1) * HEAD_DIM, :].astype(BF16)
        vat_ref[0, hd * V_ROWS + HEAD_DIM:(hd + 1) * V_ROWS, :] = ones_row
    feature_major(2, qit_ref, 1.0)
    heads(wka_ref[...], ka_ref, 1.0)
    heads(wb_ref[:, :gw], qb_ref, qscale * LOG2E)
    heads(wb_ref[:, gw:2 * gw], kb_ref, 1.0)
    heads(wb_ref[:, 2 * gw:], vb_ref, 1.0)
    ki = _dot(n_scr[...], wki_ref[...])
    ki_ref[0] = _rms(ki, gk_ref[...]).astype(BF16)
    wit = _dot_nt(wwit_ref[...], n_scr[...])
    wit_ref[0] = wit[:N_HEADS, :] * ((N_HEADS * IDX_DIM) ** -0.5)


def _t5_bucket_np(n):
    n = np.maximum(n, 0)
    max_exact = N_BUCKETS // 2
    nf = np.maximum(n, 1).astype(np.float32)
    large = max_exact + (np.log(nf / np.float32(max_exact)) / np.float32(math.log(MAX_DISTANCE / max_exact))
                         * np.float32(N_BUCKETS - max_exact)).astype(np.int32)
    large = np.minimum(large, N_BUCKETS - 1)
    return np.where(n < max_exact, n, large).astype(np.int32)


def _bias_kernel(relb_ref, bucket_ref, o_ref):
    hd = pl.program_id(0)
    far = relb_ref[N_BUCKETS - 1, hd]
    for r in range(2):
        bk = bucket_ref[r]
        acc = jnp.zeros(bk.shape, F32)
        for k in range(N_BUCKETS - 1):
            acc = jnp.where(bk == k, (relb_ref[k, hd] - far) * LOG2E, acc)
        o_ref[0, r] = acc


def _bias_tiles(rel_bias, t):
    assert t >= MAX_DISTANCE
    s = np.arange(t)[:, None]
    q = np.arange(t)[None, :]
    buckets = np.stack([_t5_bucket_np(q - s), _t5_bucket_np(t + q - s)]).astype(np.int32)
    return pl.pallas_call(
        _bias_kernel,
        grid=(N_HEADS,),
        in_specs=[pl.BlockSpec(memory_space=pltpu.SMEM),
                  pl.BlockSpec((2, t, t), lambda hd: (0, 0, 0))],
        out_specs=pl.BlockSpec((1, 2, t, t), lambda hd: (hd, 0, 0, 0)),
        out_shape=jax.ShapeDtypeStruct((N_HEADS, 2, t, t), F32),
        compiler_params=_params(1),
        name="bias_tiles",
    )(rel_bias, jnp.asarray(buckets))


def _tree(op, xs):
    xs = list(xs)
    while len(xs) > 1:
        xs = [op(xs[j], xs[j + 1]) if j + 1 < len(xs) else xs[j] for j in range(0, len(xs), 2)]
    return xs[0]


def _row_groups(x):
    return [x[r:r + SUBLANES, :] for r in range(0, x.shape[0], SUBLANES)]


def _all_sublanes(op, x):
    for shift in (4, 2, 1):
        x = op(x, pltpu.roll(x, shift, 0))
    return x


def _dsa_kernel(qit_ref, wit_ref, ki_ref, qat_ref, ka_ref, vat_ref, bias_ref, gt_ref, tri_ref, o_ref,
                key_scr, hi_scr, lo_scr, selm_scr, logit_a, logit_b, out_scr, *, topk, first_tile, n_q_tiles):
    t = DSA_TILE
    kf = float(topk)
    krow = lax.broadcasted_iota(I32, (t, t), 0)
    qcol = lax.broadcasted_iota(I32, (t, t), 1)
    causal = krow <= qcol

    def tile_work(nkb):
        diag = nkb - 1

        wt = wit_ref[0]
        for kb in range(nkb):
            for r0 in range(0, t, SCORE_ROWS):
                kblk = ki_ref[0, kb * t + r0:kb * t + r0 + SCORE_ROWS, :]
                acc = None
                for hd in range(N_HEADS):
                    sc = _dot(kblk, qit_ref[0, hd * IDX_DIM:(hd + 1) * IDX_DIM, :])
                    term = jnp.maximum(sc, 0.0) * wt[hd:hd + 1, :]
                    acc = term if acc is None else acc + term
                acc = acc + 0.0
                if kb == diag:
                    acc = jnp.where(causal[r0:r0 + SCORE_ROWS, :], acc, -jnp.inf)
                bits = pltpu.bitcast(acc, I32)
                key = bits ^ ((bits >> 31) & jnp.int32(0x7FFFFFFF))
                key_scr[kb, r0:r0 + SCORE_ROWS, :] = key
                hi_scr[kb, r0:r0 + SCORE_ROWS, :] = (key >> 16).astype(I16)

        def packed(x):
            return jnp.concatenate([x, x], axis=0).astype(I16)

        def column_total(parts):
            tot = _tree(jnp.add, parts).astype(F32)
            return _all_sublanes(jnp.add, tot[:SUBLANES, :] + tot[SUBLANES:, :])

        def search16(src_scr, need):
            def step(j, ans):
                cand_off = ans | lax.shift_left(jnp.int32(1), 15 - j)
                cand = packed(cand_off - HALF)
                accs = [jnp.zeros((2 * SUBLANES, t), I16) for _ in range(4)]
                n = 0
                for kb in range(nkb):
                    blk = src_scr[kb]
                    for r in range(0, t, 2 * SUBLANES):
                        hit = jnp.where(blk[r:r + 2 * SUBLANES, :] >= cand, ONE16, ZERO16)
                        accs[n % 4] = accs[n % 4] + hit
                        n += 1
                return jnp.where(column_total(accs) >= need, cand_off, ans)
            return lax.fori_loop(0, 16, step, jnp.zeros((SUBLANES, t), I32))

        if nkb * t <= topk:
            thr = jnp.full((SUBLANES, t), INT_MIN, I32)
        else:
            t_hi = search16(hi_scr, kf) - HALF
            t_hi16 = packed(t_hi)
            above = [jnp.zeros((2 * SUBLANES, t), I16) for _ in range(4)]
            n = 0
            for kb in range(nkb):
                for r in range(0, t, 2 * SUBLANES):
                    rows = slice(r, r + 2 * SUBLANES)
                    h = hi_scr[kb, rows, :]
                    lo = ((key_scr[kb, rows, :] & jnp.int32(0xFFFF)) - HALF).astype(I16)
                    lo_scr[kb, rows, :] = jnp.where(h == t_hi16, lo, MIN16)
                    above[n % 4] = above[n % 4] + jnp.where(h > t_hi16, ONE16, ZERO16)
                    n += 1
            t_lo = search16(lo_scr, kf - column_total(above))
            thr = lax.shift_left(t_hi, 16) | t_lo

        n_ge_parts = [jnp.zeros((SUBLANES, t), F32) for _ in range(4)]
        for kb in range(nkb):
            blk = key_scr[kb]
            for r, grp in enumerate(_row_groups(blk)):
                rows = slice(r * SUBLANES, (r + 1) * SUBLANES)
                sel = grp >= thr
                n_ge_parts[r % 4] = n_ge_parts[r % 4] + jnp.where(sel, 1.0, 0.0)
                m = jnp.where(sel, 0.0, NEG_BIG)
                if kb == diag:
                    m = jnp.where(causal[rows, :], m, NEG_BIG)
                selm_scr[kb, rows, :] = m
        n_ge = _all_sublanes(jnp.add, _tree(jnp.add, n_ge_parts))
        excess = jnp.max(jnp.where(n_ge > kf, 1.0, 0.0))

        @pl.when(excess > 0.0)
        def _():
            thr1 = thr[0:1, :]

            def tied(kb):
                return jnp.where(key_scr[kb] == thr1, 1.0, 0.0)

            def count_tied(kb, acc):
                return acc + jnp.sum(tied(kb), axis=0, keepdims=True)

            n_tied = lax.fori_loop(0, nkb, count_tied, jnp.zeros((1, t), F32))
            need = kf - (n_ge[0:1, :] - n_tied)

            def rewrite(kb, before):
                is_tied = key_scr[kb] == thr1
                e = jnp.where(is_tied, 1.0, 0.0)
                rank = _dot(tri_ref[...], e.astype(BF16)) + before
                surplus = jnp.where(is_tied, rank - need, -1.0) >= 0.0
                selm_scr[kb] = jnp.where(surplus, NEG_BIG, selm_scr[kb])
                return before + jnp.sum(e, axis=0, keepdims=True)

            lax.fori_loop(0, nkb, rewrite, jnp.zeros((1, t), F32))

        def head_rows(hd):
            if isinstance(hd, int):
                return slice(hd * HEAD_DIM, (hd + 1) * HEAD_DIM)
            return pl.ds(pl.multiple_of(hd * HEAD_DIM, HEAD_DIM), HEAD_DIM)

        def logits(hd, buf):
            qt = qat_ref[0, head_rows(hd), :]
            maxes = []
            for kb in range(nkb):
                lg = _dot(ka_ref[0, hd, kb * t:(kb + 1) * t, :], qt) + selm_scr[kb]
                if kb >= nkb - 2:
                    lg = lg + bias_ref[hd, diag - kb]
                buf[kb] = lg
                maxes.append(jnp.max(lg, axis=0, keepdims=True))
            return _tree(jnp.maximum, maxes)

        def weighted_values(hd, buf, m):
            if isinstance(hd, int):
                vrows = slice(hd * V_ROWS, (hd + 1) * V_ROWS)
            else:
                vrows = pl.ds(pl.multiple_of(hd * V_ROWS, 16), V_ROWS)
            acc = None
            for kb in range(nkb):
                p = jnp.exp2(buf[kb] - m).astype(BF16)
                pv = _dot(vat_ref[0, vrows, kb * t:(kb + 1) * t], p)
                acc = pv if acc is None else acc + pv
            out_scr[head_rows(hd), :] = acc[:HEAD_DIM, :] * (1.0 / acc[HEAD_DIM:HEAD_DIM + 1, :])

        def head_pair(j, m_even):
            m_odd = logits(2 * j + 1, logit_b)
            weighted_values(2 * j, logit_a, m_even)
            m_next = logits((2 * j + 2) % N_HEADS, logit_a)
            weighted_values(2 * j + 1, logit_b, m_odd)
            return m_next

        lax.fori_loop(0, N_HEADS // 2, head_pair, logits(0, logit_a), unroll=nkb <= HEADS_UNROLL_BLOCKS)

    i = pl.program_id(1)
    for k in range(n_q_tiles):
        pl.when(i == k)(functools.partial(tile_work, first_tile + k + 1))

    o = out_scr[...]
    o = o * lax.rsqrt(jnp.mean(o * o, axis=0, keepdims=True) + EPS) * gt_ref[...]
    o_ref[0] = o.T.astype(BF16)


def _dsa(qit, wit, ki, qat, ka, vat, bias, g_out, bsz, seq, first_tile, n_q_tiles):
    t = DSA_TILE
    nq = first_tile + n_q_tiles
    keys = nq * t
    gw = GROUP_WIDTH
    topk = min(TOPK_MAX, seq // 4)
    q_fm = pl.BlockSpec((1, gw, t), lambda b, i: (b, 0, i + first_tile))
    gt = jnp.broadcast_to(g_out.reshape(gw, 1), (gw, t))
    r = np.arange(t)
    lower = jnp.asarray(r[None, :] < r[:, None], dtype=BF16)
    return pl.pallas_call(
        functools.partial(_dsa_kernel, topk=topk, first_tile=first_tile, n_q_tiles=n_q_tiles),
        grid=(bsz, n_q_tiles),
        in_specs=[q_fm,
                  pl.BlockSpec((1, N_HEADS, t), lambda b, i: (b, 0, i + first_tile)),
                  pl.BlockSpec((1, keys, IDX_DIM), lambda b, i: (b, 0, 0)),
                  q_fm,
                  pl.BlockSpec((1, N_HEADS, keys, HEAD_DIM), lambda b, i: (b, 0, 0, 0)),
                  pl.BlockSpec((1, N_HEADS * V_ROWS, keys), lambda b, i: (b, 0, 0)),
                  _resident(bias.shape),
                  _resident((gw, t)),
                  _resident((t, t))],
        out_specs=pl.BlockSpec((1, t, gw), lambda b, i: (b, i, 0)),
        out_shape=jax.ShapeDtypeStruct((bsz, n_q_tiles * t, gw), BF16),
        scratch_shapes=[pltpu.VMEM((nq, t, t), I32), pltpu.VMEM((nq, t, t), I16),
                        pltpu.VMEM((nq, t, t), I16), pltpu.VMEM((nq, t, t), F32),
                        pltpu.VMEM((nq, t, t), F32), pltpu.VMEM((nq, t, t), F32),
                        pltpu.VMEM((gw, t), F32)],
        compiler_params=_params(2),
        name="dsa_%d" % first_tile,
    )(qit, wit, ki, qat, ka, vat, bias, gt, lower)


def _neg_log2_1m_sigmoid(z2):
    return jnp.maximum(z2, 0.0) + jnp.log2(1.0 + jnp.exp2(-jnp.abs(z2)))


def _split_bf16(x):
    hi = x.astype(BF16)
    lo = (x - hi.astype(F32)).astype(BF16)
    return jnp.concatenate([hi, lo], axis=1)


def _sb_kernel(q_ref, k_ref, v_ref, tri_ref, g_ref, o_ref, z_scr, lh_scr, cs_scr, r_scr, acc_scr):
    t = SB_TILE
    nw = SB_WINDOW
    row = lax.broadcasted_iota(I32, (t, t), 0)
    col = lax.broadcasted_iota(I32, (t, t), 1)
    tiles = []
    for sub in range(SB_PAIR):
        i = pl.program_id(1) * SB_PAIR + sub
        wb = jnp.maximum(i - (nw - 1), 0)
        tiles.append((sub, i, wb, pl.ds(pl.multiple_of(wb * t, t), nw * t)))

    def causal(i, wb, w):
        return (col - row) < (i - wb - w) * t

    def q_rows(sub):
        return slice(sub * t, (sub + 1) * t)

    for sub, i, wb, wrows in tiles:
        for hd in range(N_HEADS):
            p = sub * N_HEADS + hd
            z = _dot_nt(q_ref[0, hd, q_rows(sub), :], k_ref[0, hd, wrows, :])
            z_scr[p] = z
            for w in range(nw):
                m = jnp.where(causal(i, wb, w), _neg_log2_1m_sigmoid(z[:, w * t:(w + 1) * t]), 0.0)
                lh_scr[(p * nw + w) * t:(p * nw + w + 1) * t, :] = _split_bf16(m)
    cs_scr[...] = _dot(lh_scr[...], tri_ref[...])
    window_rmin = {}
    for sub, i, wb, wrows in tiles:
        for hd in range(N_HEADS):
            p = sub * N_HEADS + hd
            after = None
            a_blocks = [None] * nw
            for w in reversed(range(nw)):
                cs = cs_scr[(p * nw + w) * t:(p * nw + w + 1) * t, :]
                total = jnp.broadcast_to(cs[:, 0:1], (t, t))
                c = cs if after is None else cs + after
                a = jnp.where(causal(i, wb, w), jnp.exp2(z_scr[p, :, w * t:(w + 1) * t] - c), 0.0)
                a_blocks[w] = a.astype(BF16)
                after = total if after is None else after + total
            acc_scr[p] = _dot(jnp.concatenate(a_blocks, axis=1), v_ref[0, hd, wrows, :])
            r_scr[p] = after
            window_rmin[sub] = after if hd == 0 else jnp.minimum(window_rmin[sub], after)

    for sub, i, wb, wrows in tiles:
        def far_block(kb, sub=sub):
            rows = pl.ds(pl.multiple_of(kb * t, t), t)
            rmin = None
            for hd in range(N_HEADS):
                p = sub * N_HEADS + hd
                z = _dot_nt(q_ref[0, hd, q_rows(sub), :], k_ref[0, hd, rows, :])
                cs = _dot(_split_bf16(_neg_log2_1m_sigmoid(z)), tri_ref[...])
                r_old = r_scr[p]
                a = jnp.exp2(z - cs - r_old)
                acc_scr[p] += _dot(a.astype(BF16), v_ref[0, hd, rows, :])
                r_new = r_old + jnp.broadcast_to(cs[:, 0:1], (t, t))
                r_scr[p] = r_new
                rmin = r_new if rmin is None else jnp.minimum(rmin, r_new)
            return jnp.min(rmin)

        def cond(c):
            kb, rmin = c
            return jnp.logical_and(kb >= 0, rmin < SB_DEAD)

        def body(c, far_block=far_block):
            kb, _ = c
            return kb - 1, far_block(kb)

        lax.while_loop(cond, body, (wb - 1, jnp.min(window_rmin[sub])))
        o = jnp.concatenate([acc_scr[sub * N_HEADS + hd] for hd in range(N_HEADS)], axis=1)
        o_ref[0, q_rows(sub), :] = _rms(o, g_ref[...]).astype(BF16)


def _sb(q, k, v, g_out, bsz, seq):
    t = SB_TILE
    tq = SB_PAIR * t
    assert seq >= SB_WINDOW * t and seq % tq == 0
    j = np.arange(t)[:, None]
    s = np.arange(t)[None, :]
    tri = (j >= s).astype(np.float32)
    tri = np.concatenate([tri, tri], axis=0)
    q_hm = pl.BlockSpec((1, N_HEADS, tq, HEAD_DIM), lambda b, i: (b, 0, i, 0))
    kv_hm = pl.BlockSpec((1, N_HEADS, seq, HEAD_DIM), lambda b, i: (b, 0, 0, 0))
    n_p = SB_PAIR * N_HEADS
    n_rows = n_p * SB_WINDOW * t
    return pl.pallas_call(
        _sb_kernel,
        grid=(bsz, seq // tq),
        in_specs=[q_hm, kv_hm, kv_hm, _resident((2 * t, t)),
                  pl.BlockSpec((1, GROUP_WIDTH), lambda b, i: (0, 0))],
        out_specs=pl.BlockSpec((1, tq, GROUP_WIDTH), lambda b, i: (b, i, 0)),
        out_shape=jax.ShapeDtypeStruct((bsz, seq, GROUP_WIDTH), BF16),
        scratch_shapes=[pltpu.VMEM((n_p, t, SB_WINDOW * t), F32),
                        pltpu.VMEM((n_rows, 2 * t), BF16),
                        pltpu.VMEM((n_rows, t), F32),
                        pltpu.VMEM((n_p, t, t), F32),
                        pltpu.VMEM((n_p, t, HEAD_DIM), F32)],
        compiler_params=_params(2),
        name="sb",
    )(q, k, v, jnp.asarray(tri, dtype=BF16), g_out)


def kernel(x, c, w_ada, b_ada, g_pre, g_post, w_ffn_gate, w_ffn_up, w_ffn_down, w_in, g_kidx,
           rel_bias, g_out_a, g_out_b, w_out):
    bsz, seq, d = x.shape
    depth = w_ada.shape[0]
    gw = GROUP_WIDTH
    h = x.reshape(bsz * seq, d)
    bias = _bias_tiles(rel_bias, DSA_TILE)
    for l in range(depth):
        mods = _ada(c, w_ada[l], b_ada[l]).reshape(bsz, -1, d)
        ffn_w = [(w_ffn_gate[l, j].astype(BF16), w_ffn_up[l, j].astype(BF16),
                  w_ffn_down[l, j].astype(BF16)) for j in range(2)]
        w = w_in[l]
        o_ki = 4 * gw
        o_wi = o_ki + IDX_DIM
        o_b = o_wi + N_HEADS
        wt = jnp.concatenate([w[:, :gw], w[:, 2 * gw:3 * gw], w[:, 3 * gw:o_ki]], axis=1).T.astype(BF16)
        wka = w[:, gw:2 * gw].astype(BF16)
        wki = w[:, o_ki:o_wi].astype(BF16)
        wwit = jnp.pad(w[:, o_wi:o_b].T, ((0, 16 - N_HEADS), (0, 0))).astype(BF16)
        wb = w[:, o_b:].astype(BF16)
        woa = w_out[l, :gw].astype(BF16)
        wob = w_out[l, gw:].astype(BF16)
        vec = lambda g: g.reshape(1, -1)

        h1, qat, ka, vat, qit, ki, wit, qb, kb, vb = _ffn(
            h, mods, vec(g_pre[l, 0]), vec(g_post[l, 0]), *ffn_w[0], 0, seq,
            proj=(vec(g_pre[l, 1]), vec(g_kidx[l]), bsz, wt, wka, wki, wwit, wb))
        nt = seq // DSA_TILE
        dsa = functools.partial(_dsa, qit, wit, ki, qat, ka, vat, bias, g_out_a[l], bsz, seq)
        oa_lo, oa_mid, oa_hi = dsa(0, nt // 2), dsa(nt // 2, nt // 4), dsa(3 * nt // 4, nt // 4)
        ob = _sb(qb, kb, vb, vec(g_out_b[l]), bsz, seq)
        h = _ffn(h1, mods, vec(g_pre[l, 2]), vec(g_post[l, 2]), *ffn_w[1], 2, seq,
                 mix=(oa_lo.reshape(bsz * seq // 2, gw), oa_mid.reshape(bsz * seq // 4, gw),
                      oa_hi.reshape(bsz * seq // 4, gw), ob.reshape(bsz * seq, gw), woa, wob,
                      vec(g_post[l, 1])))
    return h.reshape(bsz, seq, d)
```

```python
import functools
import math

import numpy as np
import jax
import jax.numpy as jnp
from jax import lax
from jax.experimental import pallas as pl
from jax.experimental.pallas import tpu as pltpu

F32 = jnp.float32
BF16 = jnp.bfloat16
I32 = jnp.int32
I16 = jnp.int16

EPS = 1e-6
HEAD_DIM = 64
N_HEADS = 8
GROUP_WIDTH = N_HEADS * HEAD_DIM
IDX_DIM = 64
TOPK_MAX = 256
N_BUCKETS = 32
MAX_DISTANCE = 128
SUBLANES = 8
V_ROWS = 80
FFN_CHUNK = 256
TOKEN_TILE = 512
MIX_TOKEN_TILE = 1024
DSA_TILE = 256
SCORE_ROWS = 64
HEADS_UNROLL_BLOCKS = 8
SB_TILE = 128
SB_WINDOW = 3
SB_PAIR = 4
SB_GROUP = 2
LOG2E = math.log2(math.e)
NEG_BIG = -1e30
SB_DEAD = 104.0 * LOG2E
VMEM_LIMIT = 56 * 1024 * 1024
INT_MIN = np.int32(-2 ** 31)
HALF = 2 ** 15
ONE16, ZERO16, MIN16 = np.int16(1), np.int16(0), np.int16(-HALF)


def _dot(a, b):
    return jnp.dot(a, b, preferred_element_type=F32)


def _dot_nt(a, b):
    return lax.dot_general(a, b, (((1,), (1,)), ((), ())), preferred_element_type=F32)


def _rms(x, g):
    return x * lax.rsqrt(jnp.mean(x * x, axis=-1, keepdims=True) + EPS) * g


def _resident(shape):
    n = len(shape)
    return pl.BlockSpec(shape, lambda *_: (0,) * n, pipeline_mode=pl.Buffered(1))


def _params(n_grid):
    return pltpu.CompilerParams(dimension_semantics=("arbitrary",) * n_grid,
                                vmem_limit_bytes=VMEM_LIMIT)


def _ada_kernel(c_ref, w_ref, b_ref, o_ref):
    c = c_ref[...]
    a = c * jax.nn.sigmoid(c)
    w = w_ref[...]
    a_hi = a.astype(BF16)
    a_lo = (a - a_hi.astype(F32)).astype(BF16)
    w_hi = w.astype(BF16)
    w_lo = (w - w_hi.astype(F32)).astype(BF16)
    o_ref[...] = _dot(a_hi, w_hi) + (_dot(a_hi, w_lo) + _dot(a_lo, w_hi)) + b_ref[...]


def _ada(c, w, b):
    bsz, d = c.shape
    n = w.shape[1]
    tn = 1152 if n % 1152 == 0 else n
    return pl.pallas_call(
        _ada_kernel,
        grid=(n // tn,),
        in_specs=[pl.BlockSpec((bsz, d), lambda j: (0, 0)),
                  pl.BlockSpec((d, tn), lambda j: (0, j)),
                  pl.BlockSpec((1, tn), lambda j: (0, j))],
        out_specs=pl.BlockSpec((bsz, tn), lambda j: (0, j)),
        out_shape=jax.ShapeDtypeStruct((bsz, n), F32),
        compiler_params=_params(1),
        name="ada",
    )(c, w, b.reshape(1, n))


def _ffn_kernel(*refs, sub, mix, proj):
    refs = list(refs)
    take = lambda n: [refs.pop(0) for _ in range(n)]
    (h_ref,) = take(1)
    if mix:
        oa_lo_ref, oa_mid_ref, oa_hi_ref, ob_ref, woa_ref, wob_ref, gpm_ref = take(7)
    mods_ref, gpre_ref, gpost_ref, wg_ref, wu_ref, wd_ref = take(6)
    if proj:
        gpre_next_ref, *proj_in = take(7)
    (o_ref,) = take(1)
    if proj:
        proj_out = take(9)
    n_scr, a_scr = refs
    x = h_ref[...]
    if mix:
        first_half = pl.program_id(0) % 2 == 0
        second = jnp.concatenate([oa_mid_ref[...], oa_hi_ref[...]], axis=0)
        oa = jnp.where(first_half, oa_lo_ref[...], second)
        o = _dot(oa, woa_ref[...]) + _dot(ob_ref[...], wob_ref[...])
        x = x + mods_ref[0, 5:6, :] * _rms(o, gpm_ref[...])
    shift = mods_ref[0, 3 * sub:3 * sub + 1, :]
    scale = mods_ref[0, 3 * sub + 1:3 * sub + 2, :]
    gate = mods_ref[0, 3 * sub + 2:3 * sub + 3, :]
    n = _rms(x, gpre_ref[...]) * (1.0 + scale) + shift
    n_scr[...] = n.astype(BF16)
    fc = FFN_CHUNK
    for c0 in range(0, wg_ref.shape[1], fc):
        g = _dot(n_scr[...], wg_ref[:, c0:c0 + fc])
        u = _dot(n_scr[...], wu_ref[:, c0:c0 + fc])
        a_scr[:, c0:c0 + fc] = (g * jax.nn.sigmoid(g) * u).astype(BF16)
    f = _dot(a_scr[...], wd_ref[...])
    y = x + 0.5 * gate * _rms(f, gpost_ref[...])
    o_ref[...] = y
    if proj:
        shift = mods_ref[0, 3 * (sub + 1):3 * (sub + 1) + 1, :]
        scale = mods_ref[0, 3 * (sub + 1) + 1:3 * (sub + 1) + 2, :]
        n_scr[...] = (_rms(y, gpre_next_ref[...]) * (1.0 + scale) + shift).astype(BF16)
        _project(n_scr, *proj_in, *proj_out)


def _ffn(h, mods, g_pre, g_post, wg, wu, wd, sub, seq, mix=None, proj=None):
    n_tok, d = h.shape
    tm = min(TOKEN_TILE if proj is not None else MIX_TOKEN_TILE, seq)
    tpb = seq // tm
    row = pl.BlockSpec((tm, d), lambda i: (i, 0))
    vec = pl.BlockSpec((1, d), lambda i: (0, 0))
    mod_spec = pl.BlockSpec((1, mods.shape[1], d), lambda i: (i // tpb, 0, 0))
    args, specs = [h], [row]
    out_specs, out_shape = [row], [jax.ShapeDtypeStruct((n_tok, d), F32)]
    if mix is not None:
        oa_lo, oa_mid, oa_hi, ob, woa, wob, g_post_mix = mix
        gw = ob.shape[1]
        assert tpb == 2 and oa_lo.shape[0] * 2 == n_tok and oa_mid.shape[0] * 4 == n_tok
        half = pl.BlockSpec((tm, gw), lambda i: (i, 0))
        per_batch = pl.BlockSpec((tm, gw), lambda i: (i // 2, 0))
        quarter = pl.BlockSpec((tm // 2, gw), lambda i: (i // 2, 0))
        args += [oa_lo, oa_mid, oa_hi, ob, woa, wob, g_post_mix]
        specs += [per_batch, quarter, quarter, half, _resident(woa.shape), _resident(wob.shape), vec]
    args += [mods, g_pre, g_post, wg, wu, wd]
    specs += [mod_spec, vec, vec, _resident(wg.shape), _resident(wu.shape), _resident(wd.shape)]
    if proj is not None:
        g_pre_next, g_kidx, bsz, *weights = proj
        gw = GROUP_WIDTH
        args += [g_pre_next, g_kidx, *weights]
        specs += [vec, pl.BlockSpec((1, IDX_DIM), lambda i: (0, 0))] + [_resident(w.shape) for w in weights]
        hm = pl.BlockSpec((1, N_HEADS, tm, HEAD_DIM), lambda i: (i // tpb, 0, i % tpb, 0))
        hm_shape = jax.ShapeDtypeStruct((bsz, N_HEADS, seq, HEAD_DIM), BF16)
        fm = pl.BlockSpec((1, gw, tm), lambda i: (i // tpb, 0, i % tpb))
        fm_shape = jax.ShapeDtypeStruct((bsz, gw, seq), BF16)
        vfm = pl.BlockSpec((1, N_HEADS * V_ROWS, tm), lambda i: (i // tpb, 0, i % tpb))
        out_specs += [fm, hm, vfm, fm,
                      pl.BlockSpec((1, tm, IDX_DIM), lambda i: (i // tpb, i % tpb, 0)),
                      pl.BlockSpec((1, N_HEADS, tm), lambda i: (i // tpb, 0, i % tpb)),
                      hm, hm, hm]
        out_shape += [fm_shape, hm_shape, jax.ShapeDtypeStruct((bsz, N_HEADS * V_ROWS, seq), BF16),
                      fm_shape,
                      jax.ShapeDtypeStruct((bsz, seq, IDX_DIM), BF16),
                      jax.ShapeDtypeStruct((bsz, N_HEADS, seq), F32),
                      hm_shape, hm_shape, hm_shape]
    out = pl.pallas_call(
        functools.partial(_ffn_kernel, sub=sub, mix=mix is not None, proj=proj is not None),
        grid=(n_tok // tm,),
        in_specs=specs,
        out_specs=out_specs,
        out_shape=out_shape,
        scratch_shapes=[pltpu.VMEM((tm, d), BF16), pltpu.VMEM((tm, wd.shape[0]), BF16)],
        compiler_params=_params(1),
        name="ffn_mix" if mix is not None else "ffn_proj" if proj is not None else "ffn",
    )(*args)
    return out if proj is not None else out[0]


def _project(n_scr, gk_ref, wt_ref, wka_ref, wki_ref, wwit_ref, wb_ref,
             qat_ref, ka_ref, vat_ref, qit_ref, ki_ref, wit_ref, qb_ref, kb_ref, vb_ref):
    gw = GROUP_WIDTH
    qscale = HEAD_DIM ** -0.5

    def heads(w, out_ref, mul):
        p = _dot(n_scr[...], w)
        if mul != 1.0:
            p = p * mul
        for hd in range(N_HEADS):
            out_ref[0, hd] = p[:, hd * HEAD_DIM:(hd + 1) * HEAD_DIM].astype(BF16)

    def feature_major(col, out_ref, mul):
        p = _dot_nt(wt_ref[col * gw:(col + 1) * gw, :], n_scr[...])
        if mul != 1.0:
            p = p * mul
        out_ref[0] = p.astype(BF16)

    feature_major(0, qat_ref, qscale * LOG2E)
    v = _dot_nt(wt_ref[gw:2 * gw, :], n_scr[...])
    pad = V_ROWS - HEAD_DIM
    ones_row = (lax.broadcasted_iota(I32, (pad, v.shape[1]), 0) == 0).astype(BF16)
    for hd in range(N_HEADS):
        vat_ref[0, hd * V_ROWS:hd * V_ROWS + HEAD_DIM, :] = v[hd * HEAD_DIM:(hd + 1) * HEAD_DIM, :].astype(BF16)
        vat_ref[0, hd * V_ROWS + HEAD_DIM:(hd + 1) * V_ROWS, :] = ones_row
    feature_major(2, qit_ref, 1.0)
    heads(wka_ref[...], ka_ref, 1.0)
    heads(wb_ref[:, :gw], qb_ref, qscale * LOG2E)
    heads(wb_ref[:, gw:2 * gw], kb_ref, 1.0)
    heads(wb_ref[:, 2 * gw:], vb_ref, 1.0)
    ki = _dot(n_scr[...], wki_ref[...])
    ki_ref[0] = _rms(ki, gk_ref[...]).astype(BF16)
    wit = _dot_nt(wwit_ref[...], n_scr[...])
    wit_ref[0] = wit[:N_HEADS, :] * ((N_HEADS * IDX_DIM) ** -0.5)


def _t5_bucket_np(n):
    n = np.maximum(n, 0)
    max_exact = N_BUCKETS // 2
    nf = np.maximum(n, 1).astype(np.float32)
    large = max_exact + (np.log(nf / np.float32(max_exact)) / np.float32(math.log(MAX_DISTANCE / max_exact))
                         * np.float32(N_BUCKETS - max_exact)).astype(np.int32)
    large = np.minimum(large, N_BUCKETS - 1)
    return np.where(n < max_exact, n, large).astype(np.int32)


def _bias_kernel(relb_ref, bucket_ref, o_ref):
    hd = pl.program_id(0)
    far = relb_ref[N_BUCKETS - 1, hd]
    for r in range(2):
        bk = bucket_ref[r]
        acc = jnp.zeros(bk.shape, F32)
        for k in range(N_BUCKETS - 1):
            acc = jnp.where(bk == k, (relb_ref[k, hd] - far) * LOG2E, acc)
        o_ref[0, r] = acc


def _bias_tiles(rel_bias, t):
    assert t >= MAX_DISTANCE
    s = np.arange(t)[:, None]
    q = np.arange(t)[None, :]
    buckets = np.stack([_t5_bucket_np(q - s), _t5_bucket_np(t + q - s)]).astype(np.int32)
    return pl.pallas_call(
        _bias_kernel,
        grid=(N_HEADS,),
        in_specs=[pl.BlockSpec(memory_space=pltpu.SMEM),
                  pl.BlockSpec((2, t, t), lambda hd: (0, 0, 0))],
        out_specs=pl.BlockSpec((1, 2, t, t), lambda hd: (hd, 0, 0, 0)),
        out_shape=jax.ShapeDtypeStruct((N_HEADS, 2, t, t), F32),
        compiler_params=_params(1),
        name="bias_tiles",
    )(rel_bias, jnp.asarray(buckets))


def _tree(op, xs):
    xs = list(xs)
    while len(xs) > 1:
        xs = [op(xs[j], xs[j + 1]) if j + 1 < len(xs) else xs[j] for j in range(0, len(xs), 2)]
    return xs[0]


def _row_groups(x):
    return [x[r:r + SUBLANES, :] for r in range(0, x.shape[0], SUBLANES)]


def _all_sublanes(op, x):
    for shift in (4, 2, 1):
        x = op(x, pltpu.roll(x, shift, 0))
    return x


def _dsa_kernel(qit_ref, wit_ref, ki_ref, qat_ref, ka_ref, vat_ref, bias_ref, gt_ref, tri_ref, o_ref,
                key_scr, hi_scr, lo_scr, selm_scr, logit_a, logit_b, out_scr, *, topk, first_tile, n_q_tiles):
    t = DSA_TILE
    kf = float(topk)
    krow = lax.broadcasted_iota(I32, (t, t), 0)
    qcol = lax.broadcasted_iota(I32, (t, t), 1)
    causal = krow <= qcol

    def tile_work(nkb):
        diag = nkb - 1

        wt = wit_ref[0]
        for kb in range(nkb):
            for r0 in range(0, t, SCORE_ROWS):
                kblk = ki_ref[0, kb * t + r0:kb * t + r0 + SCORE_ROWS, :]
                acc = None
                for hd in range(N_HEADS):
                    sc = _dot(kblk, qit_ref[0, hd * IDX_DIM:(hd + 1) * IDX_DIM, :])
                    term = jnp.maximum(sc, 0.0) * wt[hd:hd + 1, :]
                    acc = term if acc is None else acc + term
                acc = acc + 0.0
                if kb == diag:
                    acc = jnp.where(causal[r0:r0 + SCORE_ROWS, :], acc, -jnp.inf)
                bits = pltpu.bitcast(acc, I32)
                key = bits ^ ((bits >> 31) & jnp.int32(0x7FFFFFFF))
                key_scr[kb, r0:r0 + SCORE_ROWS, :] = key
                hi_scr[kb, r0:r0 + SCORE_ROWS, :] = (key >> 16).astype(I16)

        def packed(x):
            return jnp.concatenate([x, x], axis=0).astype(I16)

        def column_total(parts):
            tot = _tree(jnp.add, parts).astype(F32)
            return _all_sublanes(jnp.add, tot[:SUBLANES, :] + tot[SUBLANES:, :])

        def search16(src_scr, need):
            def step(j, ans):
                cand_off = ans | lax.shift_left(jnp.int32(1), 15 - j)
                cand = packed(cand_off - HALF)
                accs = [jnp.zeros((2 * SUBLANES, t), I16) for _ in range(4)]
                n = 0
                for kb in range(nkb):
                    blk = src_scr[kb]
                    for r in range(0, t, 2 * SUBLANES):
                        hit = jnp.where(blk[r:r + 2 * SUBLANES, :] >= cand, ONE16, ZERO16)
                        accs[n % 4] = accs[n % 4] + hit
                        n += 1
                return jnp.where(column_total(accs) >= need, cand_off, ans)
            return lax.fori_loop(0, 16, step, jnp.zeros((SUBLANES, t), I32))

        if nkb * t <= topk:
            thr = jnp.full((SUBLANES, t), INT_MIN, I32)
        else:
            t_hi = search16(hi_scr, kf) - HALF
            t_hi16 = packed(t_hi)
            above = [jnp.zeros((2 * SUBLANES, t), I16) for _ in range(4)]
            n = 0
            for kb in range(nkb):
                for r in range(0, t, 2 * SUBLANES):
                    rows = slice(r, r + 2 * SUBLANES)
                    h = hi_scr[kb, rows, :]
                    lo = ((key_scr[kb, rows, :] & jnp.int32(0xFFFF)) - HALF).astype(I16)
                    lo_scr[kb, rows, :] = jnp.where(h == t_hi16, lo, MIN16)
                    above[n % 4] = above[n % 4] + jnp.where(h > t_hi16, ONE16, ZERO16)
                    n += 1
            t_lo = search16(lo_scr, kf - column_total(above))
            thr = lax.shift_left(t_hi, 16) | t_lo

        n_ge = jnp.zeros((SUBLANES, t), F32)
        for kb in range(nkb):
            blk = key_scr[kb]
            for r, grp in enumerate(_row_groups(blk)):
                rows = slice(r * SUBLANES, (r + 1) * SUBLANES)
                sel = grp >= thr
                n_ge = n_ge + jnp.where(sel, 1.0, 0.0)
                m = jnp.where(sel, 0.0, NEG_BIG)
                if kb == diag:
                    m = jnp.where(causal[rows, :], m, NEG_BIG)
                selm_scr[kb, rows, :] = m
        n_ge = _all_sublanes(jnp.add, n_ge)
        excess = jnp.max(jnp.where(n_ge > kf, 1.0, 0.0))

        @pl.when(excess > 0.0)
        def _():
            thr1 = thr[0:1, :]

            def tied(kb):
                return jnp.where(key_scr[kb] == thr1, 1.0, 0.0)

            def count_tied(kb, acc):
                return acc + jnp.sum(tied(kb), axis=0, keepdims=True)

            n_tied = lax.fori_loop(0, nkb, count_tied, jnp.zeros((1, t), F32))
            need = kf - (n_ge[0:1, :] - n_tied)

            def rewrite(kb, before):
                is_tied = key_scr[kb] == thr1
                e = jnp.where(is_tied, 1.0, 0.0)
                rank = _dot(tri_ref[...], e.astype(BF16)) + before
                surplus = jnp.where(is_tied, rank - need, -1.0) >= 0.0
                selm_scr[kb] = jnp.where(surplus, NEG_BIG, selm_scr[kb])
                return before + jnp.sum(e, axis=0, keepdims=True)

            lax.fori_loop(0, nkb, rewrite, jnp.zeros((1, t), F32))

        def head_rows(hd):
            if isinstance(hd, int):
                return slice(hd * HEAD_DIM, (hd + 1) * HEAD_DIM)
            return pl.ds(pl.multiple_of(hd * HEAD_DIM, HEAD_DIM), HEAD_DIM)

        def logits(hd, buf):
            qt = qat_ref[0, head_rows(hd), :]
            maxes = []
            for kb in range(nkb):
                lg = _dot(ka_ref[0, hd, kb * t:(kb + 1) * t, :], qt) + selm_scr[kb]
                if kb >= nkb - 2:
                    lg = lg + bias_ref[hd, diag - kb]
                buf[kb] = lg
                maxes.append(jnp.max(lg, axis=0, keepdims=True))
            return _tree(jnp.maximum, maxes)

        def weighted_values(hd, buf, m):
            if isinstance(hd, int):
                vrows = slice(hd * V_ROWS, (hd + 1) * V_ROWS)
            else:
                vrows = pl.ds(pl.multiple_of(hd * V_ROWS, 16), V_ROWS)
            acc = None
            for kb in range(nkb):
                p = jnp.exp2(buf[kb] - m).astype(BF16)
                pv = _dot(vat_ref[0, vrows, kb * t:(kb + 1) * t], p)
                acc = pv if acc is None else acc + pv
            out_scr[head_rows(hd), :] = acc[:HEAD_DIM, :] * (1.0 / acc[HEAD_DIM:HEAD_DIM + 1, :])

        def head_pair(j, m_even):
            m_odd = logits(2 * j + 1, logit_b)
            weighted_values(2 * j, logit_a, m_even)
            m_next = logits((2 * j + 2) % N_HEADS, logit_a)
            weighted_values(2 * j + 1, logit_b, m_odd)
            return m_next

        lax.fori_loop(0, N_HEADS // 2, head_pair, logits(0, logit_a), unroll=nkb <= HEADS_UNROLL_BLOCKS)

    i = pl.program_id(1)
    for k in range(n_q_tiles):
        pl.when(i == k)(functools.partial(tile_work, first_tile + k + 1))

    o = out_scr[...]
    o = o * lax.rsqrt(jnp.mean(o * o, axis=0, keepdims=True) + EPS) * gt_ref[...]
    o_ref[0] = o.T.astype(BF16)


def _dsa(qit, wit, ki, qat, ka, vat, bias, g_out, bsz, seq, first_tile, n_q_tiles):
    t = DSA_TILE
    nq = first_tile + n_q_tiles
    keys = nq * t
    gw = GROUP_WIDTH
    topk = min(TOPK_MAX, seq // 4)
    q_fm = pl.BlockSpec((1, gw, t), lambda b, i: (b, 0, i + first_tile))
    gt = jnp.broadcast_to(g_out.reshape(gw, 1), (gw, t))
    r = np.arange(t)
    lower = jnp.asarray(r[None, :] < r[:, None], dtype=BF16)
    return pl.pallas_call(
        functools.partial(_dsa_kernel, topk=topk, first_tile=first_tile, n_q_tiles=n_q_tiles),
        grid=(bsz, n_q_tiles),
        in_specs=[q_fm,
                  pl.BlockSpec((1, N_HEADS, t), lambda b, i: (b, 0, i + first_tile)),
                  pl.BlockSpec((1, keys, IDX_DIM), lambda b, i: (b, 0, 0)),
                  q_fm,
                  pl.BlockSpec((1, N_HEADS, keys, HEAD_DIM), lambda b, i: (b, 0, 0, 0)),
                  pl.BlockSpec((1, N_HEADS * V_ROWS, keys), lambda b, i: (b, 0, 0)),
                  _resident(bias.shape),
                  _resident((gw, t)),
                  _resident((t, t))],
        out_specs=pl.BlockSpec((1, t, gw), lambda b, i: (b, i, 0)),
        out_shape=jax.ShapeDtypeStruct((bsz, n_q_tiles * t, gw), BF16),
        scratch_shapes=[pltpu.VMEM((nq, t, t), I32), pltpu.VMEM((nq, t, t), I16),
                        pltpu.VMEM((nq, t, t), I16), pltpu.VMEM((nq, t, t), F32),
                        pltpu.VMEM((nq, t, t), F32), pltpu.VMEM((nq, t, t), F32),
                        pltpu.VMEM((gw, t), F32)],
        compiler_params=_params(2),
        name="dsa_%d" % first_tile,
    )(qit, wit, ki, qat, ka, vat, bias, gt, lower)


def _neg_log2_1m_sigmoid(z2):
    return jnp.maximum(z2, 0.0) + jnp.log2(1.0 + jnp.exp2(-jnp.abs(z2)))


def _split_bf16(x):
    hi = x.astype(BF16)
    lo = (x - hi.astype(F32)).astype(BF16)
    return jnp.concatenate([hi, lo], axis=1)


def _sb_kernel(q_ref, k_ref, v_ref, tri_ref, g_ref, o_ref, z_scr, lh_scr, cs_scr, r_scr, acc_scr):
    t = SB_TILE
    nw = SB_WINDOW
    row = lax.broadcasted_iota(I32, (t, t), 0)
    col = lax.broadcasted_iota(I32, (t, t), 1)
    tiles = []
    for sub in range(SB_PAIR):
        i = pl.program_id(1) * SB_PAIR + sub
        wb = jnp.maximum(i - (nw - 1), 0)
        tiles.append((sub, i, wb, pl.ds(pl.multiple_of(wb * t, t), nw * t)))

    def causal(i, wb, w):
        return (col - row) < (i - wb - w) * t

    def q_rows(sub):
        return slice(sub * t, (sub + 1) * t)

    window_rmin = {}
    for g0 in range(0, SB_PAIR, SB_GROUP):
        group = tiles[g0:g0 + SB_GROUP]
        for sub, i, wb, wrows in group:
            for hd in range(N_HEADS):
                p = sub * N_HEADS + hd
                pg = (sub - g0) * N_HEADS + hd
                z = _dot_nt(q_ref[0, hd, q_rows(sub), :], k_ref[0, hd, wrows, :])
                z_scr[p] = z
                for w in range(nw):
                    m = jnp.where(causal(i, wb, w), _neg_log2_1m_sigmoid(z[:, w * t:(w + 1) * t]), 0.0)
                    lh_scr[(pg * nw + w) * t:(pg * nw + w + 1) * t, :] = _split_bf16(m)
        cs_scr[...] = _dot(lh_scr[...], tri_ref[...])
        for sub, i, wb, wrows in group:
            for hd in range(N_HEADS):
                p = sub * N_HEADS + hd
                pg = (sub - g0) * N_HEADS + hd
                after = None
                a_blocks = [None] * nw
                for w in reversed(range(nw)):
                    cs = cs_scr[(pg * nw + w) * t:(pg * nw + w + 1) * t, :]
                    total = jnp.broadcast_to(cs[:, 0:1], (t, t))
                    c = cs if after is None else cs + after
                    a = jnp.where(causal(i, wb, w), jnp.exp2(z_scr[p, :, w * t:(w + 1) * t] - c), 0.0)
                    a_blocks[w] = a.astype(BF16)
                    after = total if after is None else after + total
                acc_scr[p] = _dot(jnp.concatenate(a_blocks, axis=1), v_ref[0, hd, wrows, :])
                r_scr[p] = after
                window_rmin[sub] = after if hd == 0 else jnp.minimum(window_rmin[sub], after)

    for sub, i, wb, wrows in tiles:
        def far_block(kb, sub=sub):
            rows = pl.ds(pl.multiple_of(kb * t, t), t)
            rmin = None
            for hd in range(N_HEADS):
                p = sub * N_HEADS + hd
                z = _dot_nt(q_ref[0, hd, q_rows(sub), :], k_ref[0, hd, rows, :])
                cs = _dot(_split_bf16(_neg_log2_1m_sigmoid(z)), tri_ref[...])
                r_old = r_scr[p]
                a = jnp.exp2(z - cs - r_old)
                acc_scr[p] += _dot(a.astype(BF16), v_ref[0, hd, rows, :])
                r_new = r_old + jnp.broadcast_to(cs[:, 0:1], (t, t))
                r_scr[p] = r_new
                rmin = r_new if rmin is None else jnp.minimum(rmin, r_new)
            return jnp.min(rmin)

        def cond(c):
            kb, rmin = c
            return jnp.logical_and(kb >= 0, rmin < SB_DEAD)

        def body(c, far_block=far_block):
            kb, _ = c
            return kb - 1, far_block(kb)

        lax.while_loop(cond, body, (wb - 1, jnp.min(window_rmin[sub])))
        o = jnp.concatenate([acc_scr[sub * N_HEADS + hd] for hd in range(N_HEADS)], axis=1)
        o_ref[0, q_rows(sub), :] = _rms(o, g_ref[...]).astype(BF16)


def _sb(q, k, v, g_out, bsz, seq):
    t = SB_TILE
    tq = SB_PAIR * t
    assert seq >= SB_WINDOW * t and seq % tq == 0
    j = np.arange(t)[:, None]
    s = np.arange(t)[None, :]
    tri = (j >= s).astype(np.float32)
    tri = np.concatenate([tri, tri], axis=0)
    q_hm = pl.BlockSpec((1, N_HEADS, tq, HEAD_DIM), lambda b, i: (b, 0, i, 0))
    kv_hm = pl.BlockSpec((1, N_HEADS, seq, HEAD_DIM), lambda b, i: (b, 0, 0, 0))
    n_p = SB_PAIR * N_HEADS
    n_rows = SB_GROUP * N_HEADS * SB_WINDOW * t
    return pl.pallas_call(
        _sb_kernel,
        grid=(bsz, seq // tq),
        in_specs=[q_hm, kv_hm, kv_hm, _resident((2 * t, t)),
                  pl.BlockSpec((1, GROUP_WIDTH), lambda b, i: (0, 0))],
        out_specs=pl.BlockSpec((1, tq, GROUP_WIDTH), lambda b, i: (b, i, 0)),
        out_shape=jax.ShapeDtypeStruct((bsz, seq, GROUP_WIDTH), BF16),
        scratch_shapes=[pltpu.VMEM((n_p, t, SB_WINDOW * t), F32),
                        pltpu.VMEM((n_rows, 2 * t), BF16),
                        pltpu.VMEM((n_rows, t), F32),
                        pltpu.VMEM((n_p, t, t), F32),
                        pltpu.VMEM((n_p, t, HEAD_DIM), F32)],
        compiler_params=_params(2),
        name="sb",
    )(q, k, v, jnp.asarray(tri, dtype=BF16), g_out)


def kernel(x, c, w_ada, b_ada, g_pre, g_post, w_ffn_gate, w_ffn_up, w_ffn_down, w_in, g_kidx,
           rel_bias, g_out_a, g_out_b, w_out):
    bsz, seq, d = x.shape
    depth = w_ada.shape[0]
    gw = GROUP_WIDTH
    h = x.reshape(bsz * seq, d)
    bias = _bias_tiles(rel_bias, DSA_TILE)
    for l in range(depth):
        mods = _ada(c, w_ada[l], b_ada[l]).reshape(bsz, -1, d)
        ffn_w = [(w_ffn_gate[l, j].astype(BF16), w_ffn_up[l, j].astype(BF16),
                  w_ffn_down[l, j].astype(BF16)) for j in range(2)]
        w = w_in[l]
        o_ki = 4 * gw
        o_wi = o_ki + IDX_DIM
        o_b = o_wi + N_HEADS
        wt = jnp.concatenate([w[:, :gw], w[:, 2 * gw:3 * gw], w[:, 3 * gw:o_ki]], axis=1).T.astype(BF16)
        wka = w[:, gw:2 * gw].astype(BF16)
        wki = w[:, o_ki:o_wi].astype(BF16)
        wwit = jnp.pad(w[:, o_wi:o_b].T, ((0, 16 - N_HEADS), (0, 0))).astype(BF16)
        wb = w[:, o_b:].astype(BF16)
        woa = w_out[l, :gw].astype(BF16)
        wob = w_out[l, gw:].astype(BF16)
        vec = lambda g: g.reshape(1, -1)

        h1, qat, ka, vat, qit, ki, wit, qb, kb, vb = _ffn(
            h, mods, vec(g_pre[l, 0]), vec(g_post[l, 0]), *ffn_w[0], 0, seq,
            proj=(vec(g_pre[l, 1]), vec(g_kidx[l]), bsz, wt, wka, wki, wwit, wb))
        nt = seq // DSA_TILE
        dsa = functools.partial(_dsa, qit, wit, ki, qat, ka, vat, bias, g_out_a[l], bsz, seq)
        oa_lo, oa_mid, oa_hi = dsa(0, nt // 2), dsa(nt // 2, nt // 4), dsa(3 * nt // 4, nt // 4)
        ob = _sb(qb, kb, vb, vec(g_out_b[l]), bsz, seq)
        h = _ffn(h1, mods, vec(g_pre[l, 2]), vec(g_post[l, 2]), *ffn_w[1], 2, seq,
                 mix=(oa_lo.reshape(bsz * seq // 2, gw), oa_mid.reshape(bsz * seq // 4, gw),
                      oa_hi.reshape(bsz * seq // 4, gw), ob.reshape(bsz * seq, gw), woa, wob,
                      vec(g_post[l, 1])))
    return h.reshape(bsz, seq, d)
```
